```python
import math
import jax, jax.numpy as jnp
from jax import lax
import numpy as np

D_MODEL = 2048
BATCH = 1
SEQ = 8192
DEPTH = 2
DEC_BATCH = 128
DEC_SEQ = 1
PAST_LEN = 8192
PAGE_SIZE = 128

EPS = 1e-6
NEG = -1e30
F32 = jnp.float32
D_POOL = 512
POOL_WINDOWS = (2, 4, 8, 16)
POOL_GROUPS = 4
POOL_GROUP_DIM = D_POOL // POOL_GROUPS
POOL_BUF = max(POOL_WINDOWS) - 1
HEAD_DIM = 64
N_Q_HEADS = 24
N_KV_HEADS = 4
Q_PER_KV = N_Q_HEADS // N_KV_HEADS
WINDOW = 128
D_ATTN = N_Q_HEADS * HEAD_DIM
D_KV = N_KV_HEADS * HEAD_DIM
D_IN0 = D_POOL + D_ATTN + 2 * D_KV
D_MIX0 = D_POOL + D_ATTN
GDN_HEADS = 12
GDN_DK = 128
GDN_DV = 128
D_GDN_K = GDN_HEADS * GDN_DK
D_GDN_V = GDN_HEADS * GDN_DV
D_GDN_CONV = 2 * D_GDN_K + D_GDN_V
GDN_CONV = 4
GDN_CHUNK = 64
D_SCONV = 512
SCONV_W = 3
D_IN1 = 2 * D_GDN_K + 2 * D_GDN_V + 2 * GDN_HEADS + 3 * D_SCONV
D_MIX1 = D_GDN_V + D_SCONV
D_FF = 5632
FFN_CONV = 3

kernel_name = 'hybrid_pool_swa_gdn_shortconv_step'


def rmsnorm(x, w):
    xf = x.astype(F32)
    y = xf * lax.rsqrt(jnp.mean(xf * xf, axis=-1, keepdims=True) + EPS)
    return (y * w.astype(F32)).astype(x.dtype)


def l2norm(x):
    return x * lax.rsqrt(jnp.sum(x * x, axis=-1, keepdims=True) + EPS)


def gated_rmsnorm(o, w, gate):
    y = o * lax.rsqrt(jnp.mean(o * o, axis=-1, keepdims=True) + EPS) * w.astype(F32)
    return y * jax.nn.silu(gate.astype(F32))


def causal_dwconv(x_ext, w):
    width = w.shape[0]
    t = x_ext.shape[1] - (width - 1)
    out = w[0] * x_ext[:, :t]
    for j in range(1, width):
        out = out + w[j] * x_ext[:, j:j + t]
    return out


def pool_mix(u_ext, pos, pool_w, pool_scale):
    b, l, _ = u_ext.shape
    t = l - POOL_BUF
    uf = u_ext.astype(F32).reshape(b, l, POOL_GROUPS, POOL_GROUP_DIM)
    cs = jnp.concatenate([jnp.zeros_like(uf[:, :1]), jnp.cumsum(uf, axis=1)], axis=1)
    end = cs[:, POOL_BUF + 1:]
    pooled = []
    for g, w in enumerate(POOL_WINDOWS):
        start = cs[:, POOL_BUF + 1 - w:POOL_BUF + 1 - w + t, g]
        cnt = jnp.minimum(pos + 1, w).astype(F32)[None, :, None]
        pooled.append((end[:, :, g] - start) / cnt)
    diff = jnp.stack(pooled, axis=2) - uf[:, POOL_BUF:]
    y = jnp.einsum('btgc,gcd->btgd', diff, pool_w.astype(F32))
    return (y.reshape(b, t, D_POOL) * pool_scale.astype(F32)).astype(u_ext.dtype)


def sink_softmax(s, sinks):
    sk = sinks.astype(F32).reshape(N_KV_HEADS, Q_PER_KV, 1)
    m = jnp.maximum(jnp.max(s, axis=-1), sk)
    p = jnp.exp(s - m[..., None])
    den = jnp.sum(p, axis=-1) + jnp.exp(sk - m)
    return p / den[..., None]


def swa_prompt(q, k, v, sinks):
    b, t = q.shape[:2]
    nb = t // WINDOW
    qb = q.reshape(b, nb, WINDOW, N_KV_HEADS, Q_PER_KV, HEAD_DIM)
    kb = k.reshape(b, nb, WINDOW, N_KV_HEADS, HEAD_DIM)
    vb = v.reshape(b, nb, WINDOW, N_KV_HEADS, HEAD_DIM)

    def with_prev(a):
        prev = jnp.concatenate([jnp.zeros_like(a[:, :1]), a[:, :-1]], axis=1)
        return jnp.concatenate([prev, a], axis=2)

    k2, v2 = with_prev(kb), with_prev(vb)
    i = jnp.arange(WINDOW)[:, None]
    j = jnp.arange(2 * WINDOW)[None, :]
    rel = i + WINDOW - j
    blk = jnp.arange(nb)[:, None, None]
    valid = (rel >= 0) & (rel <= WINDOW) & ((blk > 0) | (j >= WINDOW))
    s = jnp.einsum('bnqhgd,bnkhd->bnhgqk', qb, k2, preferred_element_type=F32) * HEAD_DIM ** -0.5
    s = jnp.where(valid[None, :, None, None], s, NEG)
    p = sink_softmax(s, sinks)
    o = jnp.einsum('bnhgqk,bnkhd->bnqhgd', p.astype(v.dtype), v2)
    return o.reshape(b, t, D_ATTN)


def swa_sample(q, k_ext, v_ext, sinks):
    b, t = q.shape[:2]
    l = k_ext.shape[1]
    wb = l - t
    qg = q.reshape(b, t, N_KV_HEADS, Q_PER_KV, HEAD_DIM)
    rel = jnp.arange(t)[:, None] + wb - jnp.arange(l)[None, :]
    valid = (rel >= 0) & (rel <= WINDOW)
    s = jnp.einsum('bqhgd,bkhd->bhgqk', qg, k_ext, preferred_element_type=F32) * HEAD_DIM ** -0.5
    s = jnp.where(valid, s, NEG)
    p = sink_softmax(s, sinks)
    o = jnp.einsum('bhgqk,bkhd->bqhgd', p.astype(v_ext.dtype), v_ext)
    return o.reshape(b, t, D_ATTN)


def gdn_chunked(q, k, v, g, beta, s0):
    b, t, h, _ = q.shape
    dv = v.shape[-1]
    c = GDN_CHUNK
    n = t // c

    def chunks(a):
        return a.reshape((b, n, c) + a.shape[2:]).swapaxes(2, 3)

    q, k, v, g, beta = chunks(q), chunks(k), chunks(v), chunks(g), chunks(beta)
    gc = jnp.cumsum(g, axis=-1)
    tri = jnp.tril(jnp.ones((c, c), dtype=bool))
    strict = jnp.tril(jnp.ones((c, c), dtype=bool), -1)
    decay = jnp.exp(jnp.where(tri, gc[..., :, None] - gc[..., None, :], NEG))
    kb = k * beta[..., None]
    lmat = jnp.where(strict, jnp.einsum('bnhid,bnhjd->bnhij', kb, k) * decay, 0.0)
    amat = lmat + jnp.eye(c, dtype=lmat.dtype)
    rhs = jnp.concatenate([v * beta[..., None], kb * jnp.exp(gc)[..., None]], axis=-1)
    sol = lax.linalg.triangular_solve(amat, rhs, left_side=True, lower=True, unit_diagonal=True)
    u, w = sol[..., :dv], sol[..., dv:]
    intra = jnp.where(tri, jnp.einsum('bnhid,bnhjd->bnhij', q, k) * decay, 0.0)
    q_dec = q * jnp.exp(gc)[..., None]
    k_dec = k * jnp.exp(gc[..., -1:] - gc)[..., None]
    g_tot = jnp.exp(gc[..., -1])

    def step(s, xs):
        u_n, w_n, intra_n, qd_n, kd_n, gt_n = xs
        v_new = u_n - jnp.einsum('bhcd,bhde->bhce', w_n, s)
        o_n = jnp.einsum('bhcd,bhde->bhce', qd_n, s) + jnp.einsum('bhij,bhje->bhie', intra_n, v_new)
        s = s * gt_n[..., None, None] + jnp.einsum('bhcd,bhce->bhde', kd_n, v_new)
        return s, o_n

    xs = tuple(jnp.moveaxis(a, 1, 0) for a in (u, w, intra, q_dec, k_dec, g_tot))
    s_fin, o = lax.scan(step, s0, xs)
    o = o.transpose(1, 0, 3, 2, 4).reshape(b, t, h, dv)
    return o, s_fin


def gdn_recurrent(q, k, v, g, beta, s0):
    xs = (q.swapaxes(0, 1), k.swapaxes(0, 1), v.swapaxes(0, 1), g.swapaxes(0, 1), beta.swapaxes(0, 1))

    def step(s, xt):
        qt, kt, vt, gt, bt = xt
        s = s * jnp.exp(gt)[..., None, None]
        kv = jnp.einsum('bhd,bhde->bhe', kt, s)
        s = s + jnp.einsum('bhd,bhe->bhde', kt, (vt - kv) * bt[..., None])
        return s, jnp.einsum('bhd,bhde->bhe', qt, s)

    s_fin, o = lax.scan(step, s0, xs)
    return o.swapaxes(0, 1), s_fin


def conv_ffn(x, buf, norm_w, w_up, w_conv, w_down):
    up = rmsnorm(x, norm_w) @ w_up
    ext = jnp.concatenate([buf.astype(up.dtype), up], axis=1)
    c = causal_dwconv(ext, w_conv)
    hid = jax.nn.silu(c[..., :D_FF]) * c[..., D_FF:]
    return x + hid @ w_down, ext[:, -(FFN_CONV - 1):]


def layer_ab(x, pool_buf, k_buf, v_buf, ffn_buf, pos, is_prompt, wb, params):
    (norm_mix, w_in, pool_w, pool_scale, sinks, w_out, norm_ffn, w_up, w_conv, w_down) = params
    b, t, _ = x.shape
    z = rmsnorm(x, norm_mix) @ w_in
    o_q = D_POOL
    o_k = o_q + D_ATTN
    o_v = o_k + D_KV
    u = z[..., :o_q]
    q = z[..., o_q:o_k].reshape(b, t, N_Q_HEADS, HEAD_DIM)
    k = z[..., o_k:o_v].reshape(b, t, N_KV_HEADS, HEAD_DIM)
    v = z[..., o_v:].reshape(b, t, N_KV_HEADS, HEAD_DIM)
    u_ext = jnp.concatenate([pool_buf.astype(u.dtype), u], axis=1)
    y_pool = pool_mix(u_ext, pos, pool_w, pool_scale)
    if is_prompt:
        k_ext, v_ext = k, v
        y_att = swa_prompt(q, k, v, sinks)
    else:
        k_ext = jnp.concatenate([k_buf.astype(k.dtype), k], axis=1)
        v_ext = jnp.concatenate([v_buf.astype(v.dtype), v], axis=1)
        y_att = swa_sample(q, k_ext, v_ext, sinks)
    x = x + jnp.concatenate([y_pool, y_att], axis=-1) @ w_out
    x, new_ffn = conv_ffn(x, ffn_buf, norm_ffn, w_up, w_conv, w_down)
    return x, (u_ext[:, -POOL_BUF:], k_ext[:, -wb:], v_ext[:, -wb:], new_ffn)


def layer_cd(x, conv_buf, s0, sconv_buf, ffn_buf, is_prompt, params):
    (norm_mix, w_in, gdn_conv, a_log, dt_bias, gdn_norm, sconv_w, w_out,
     norm_ffn, w_up, w_conv, w_down) = params
    b, t, _ = x.shape
    z = rmsnorm(x, norm_mix) @ w_in
    o0 = D_GDN_CONV
    o1 = o0 + D_GDN_V
    o2 = o1 + 2 * GDN_HEADS
    qkv = z[..., :o0]
    zg = z[..., o0:o1].reshape(b, t, GDN_HEADS, GDN_DV)
    b_raw = z[..., o1:o1 + GDN_HEADS].astype(F32)
    a_raw = z[..., o1 + GDN_HEADS:o2].astype(F32)
    sb = z[..., o2:o2 + D_SCONV]
    sc = z[..., o2 + D_SCONV:o2 + 2 * D_SCONV]
    sh = z[..., o2 + 2 * D_SCONV:]
    qkv_ext = jnp.concatenate([conv_buf.astype(qkv.dtype), qkv], axis=1)
    qkv_c = jax.nn.silu(causal_dwconv(qkv_ext, gdn_conv).astype(F32))
    q = l2norm(qkv_c[..., :D_GDN_K].reshape(b, t, GDN_HEADS, GDN_DK)) * GDN_DK ** -0.5
    k = l2norm(qkv_c[..., D_GDN_K:2 * D_GDN_K].reshape(b, t, GDN_HEADS, GDN_DK))
    v = qkv_c[..., 2 * D_GDN_K:].reshape(b, t, GDN_HEADS, GDN_DV)
    beta = jax.nn.sigmoid(b_raw)
    g = -jnp.exp(a_log.astype(F32)) * jax.nn.softplus(a_raw + dt_bias.astype(F32))
    if is_prompt:
        o, s_fin = gdn_chunked(q, k, v, g, beta, s0.astype(F32))
    else:
        o, s_fin = gdn_recurrent(q, k, v, g, beta, s0.astype(F32))
    y_gdn = gated_rmsnorm(o, gdn_norm, zg).reshape(b, t, D_GDN_V).astype(x.dtype)
    u_ext = jnp.concatenate([sconv_buf.astype(sc.dtype), sc * sh], axis=1)
    y_sc = sb * causal_dwconv(u_ext, sconv_w)
    x = x + jnp.concatenate([y_gdn, y_sc], axis=-1) @ w_out
    x, new_ffn = conv_ffn(x, ffn_buf, norm_ffn, w_up, w_conv, w_down)
    return x, (qkv_ext[:, -(GDN_CONV - 1):], s_fin.astype(x.dtype), u_ext[:, -(SCONV_W - 1):], new_ffn)


def trunk(x, bufs, pos, is_prompt, wb, layer_params, final_norm):
    new_state = []
    for layer in range(DEPTH):
        if layer % 2 == 0:
            x, st = layer_ab(x, *bufs[layer], pos, is_prompt, wb, layer_params[layer])
        else:
            x, st = layer_cd(x, *bufs[layer], is_prompt, layer_params[layer])
        new_state.append(st)
    return rmsnorm(x, final_norm), new_state


def setup_inputs(seed: int = 0) -> dict:
    key = jax.random.key(seed)
    keys = list(jax.random.split(key, 48))

    def nxt():
        return keys.pop()

    def normal(shape, scale=1.0):
        return scale * jax.random.normal(nxt(), shape, F32)

    def dense(fan_in, shape):
        return normal(shape, fan_in ** -0.5)

    def gain(n):
        return 1.0 + normal((n,), 0.02)

    wb = min(WINDOW, PAST_LEN)
    dt = jnp.exp(jax.random.uniform(nxt(), (GDN_HEADS,), F32, math.log(1e-3), math.log(1e-1)))
    dt_bias = jnp.log(jnp.expm1(dt))
    a_log = jnp.log(jax.random.uniform(nxt(), (GDN_HEADS,), F32, 1.0, 16.0))
    return {
        'x_prompt': normal((BATCH, SEQ, D_MODEL)),
        'x_sample': normal((DEC_BATCH, DEC_SEQ, D_MODEL)),
        'state_l0_pool': normal((DEC_BATCH, POOL_BUF, D_POOL)),
        'cache_l0_k': normal((DEC_BATCH, wb, N_KV_HEADS, HEAD_DIM)),
        'cache_l0_v': normal((DEC_BATCH, wb, N_KV_HEADS, HEAD_DIM)),
        'state_l0_ffn_conv': normal((DEC_BATCH, FFN_CONV - 1, 2 * D_FF)),
        'state_l1_gdn_conv': normal((DEC_BATCH, GDN_CONV - 1, D_GDN_CONV)),
        'state_l1_gdn_S': normal((DEC_BATCH, GDN_HEADS, GDN_DK, GDN_DV), 0.1),
        'state_l1_sconv': normal((DEC_BATCH, SCONV_W - 1, D_SCONV)),
        'state_l1_ffn_conv': normal((DEC_BATCH, FFN_CONV - 1, 2 * D_FF)),
        'l0_norm_mix': gain(D_MODEL),
        'l0_w_in': dense(D_MODEL, (D_MODEL, D_IN0)),
        'l0_pool_w': dense(POOL_GROUP_DIM, (POOL_GROUPS, POOL_GROUP_DIM, POOL_GROUP_DIM)),
        'l0_pool_scale': 1.0 + normal((D_POOL,), 0.1),
        'l0_sinks': normal((N_Q_HEADS,), 0.5),
        'l0_w_out': dense(D_MIX0, (D_MIX0, D_MODEL)),
        'l0_norm_ffn': gain(D_MODEL),
        'l0_ffn_w_up': dense(D_MODEL, (D_MODEL, 2 * D_FF)),
        'l0_ffn_conv': dense(FFN_CONV, (FFN_CONV, 2 * D_FF)),
        'l0_ffn_w_down': dense(D_FF, (D_FF, D_MODEL)),
        'l1_norm_mix': gain(D_MODEL),
        'l1_w_in': dense(D_MODEL, (D_MODEL, D_IN1)),
        'l1_gdn_conv': dense(GDN_CONV, (GDN_CONV, D_GDN_CONV)),
        'l1_gdn_A_log': a_log,
        'l1_gdn_dt_bias': dt_bias,
        'l1_gdn_norm': gain(GDN_DV),
        'l1_sconv_w': dense(SCONV_W, (SCONV_W, D_SCONV)),
        'l1_w_out': dense(D_MIX1, (D_MIX1, D_MODEL)),
        'l1_norm_ffn': gain(D_MODEL),
        'l1_ffn_w_up': dense(D_MODEL, (D_MODEL, 2 * D_FF)),
        'l1_ffn_conv': dense(FFN_CONV, (FFN_CONV, 2 * D_FF)),
        'l1_ffn_w_down': dense(D_FF, (D_FF, D_MODEL)),
        'final_norm': gain(D_MODEL),
    }


def reference(x_prompt, x_sample, state_l0_pool, cache_l0_k, cache_l0_v, state_l0_ffn_conv,
              state_l1_gdn_conv, state_l1_gdn_S, state_l1_sconv, state_l1_ffn_conv,
              l0_norm_mix, l0_w_in, l0_pool_w, l0_pool_scale, l0_sinks, l0_w_out,
              l0_norm_ffn, l0_ffn_w_up, l0_ffn_conv, l0_ffn_w_down,
              l1_norm_mix, l1_w_in, l1_gdn_conv, l1_gdn_A_log, l1_gdn_dt_bias, l1_gdn_norm,
              l1_sconv_w, l1_w_out, l1_norm_ffn, l1_ffn_w_up, l1_ffn_conv, l1_ffn_w_down,
              final_norm):
    layer_params = (
        (l0_norm_mix, l0_w_in, l0_pool_w, l0_pool_scale, l0_sinks, l0_w_out,
         l0_norm_ffn, l0_ffn_w_up, l0_ffn_conv, l0_ffn_w_down),
        (l1_norm_mix, l1_w_in, l1_gdn_conv, l1_gdn_A_log, l1_gdn_dt_bias, l1_gdn_norm,
         l1_sconv_w, l1_w_out, l1_norm_ffn, l1_ffn_w_up, l1_ffn_conv, l1_ffn_w_down),
    )
    wb = cache_l0_k.shape[1]
    bp, tp = x_prompt.shape[:2]
    dt = x_prompt.dtype
    prompt_bufs = (
        (jnp.zeros((bp, POOL_BUF, D_POOL), dt), None, None,
         jnp.zeros((bp, FFN_CONV - 1, 2 * D_FF), dt)),
        (jnp.zeros((bp, GDN_CONV - 1, D_GDN_CONV), dt),
         jnp.zeros((bp, GDN_HEADS, GDN_DK, GDN_DV), F32),
         jnp.zeros((bp, SCONV_W - 1, D_SCONV), dt),
         jnp.zeros((bp, FFN_CONV - 1, 2 * D_FF), dt)),
    )
    sample_bufs = (
        (state_l0_pool, cache_l0_k, cache_l0_v, state_l0_ffn_conv),
        (state_l1_gdn_conv, state_l1_gdn_S, state_l1_sconv, state_l1_ffn_conv),
    )
    y_prompt, new_p = trunk(x_prompt, prompt_bufs, jnp.arange(tp), True, wb, layer_params, final_norm)
    y_sample, new_s = trunk(x_sample, sample_bufs, PAST_LEN + jnp.arange(x_sample.shape[1]), False,
                            wb, layer_params, final_norm)
    (p_pool, p_k, p_v, p_ffn0), (p_gconv, p_S, p_sconv, p_ffn1) = new_p
    (s_pool, s_k, s_v, s_ffn0), (s_gconv, s_S, s_sconv, s_ffn1) = new_s
    return (y_prompt, y_sample, p_pool, s_pool, p_k, s_k, p_v, s_v, p_ffn0, s_ffn0,
            p_gconv, s_gconv, p_S, s_S, p_sconv, s_sconv, p_ffn1, s_ffn1)
```

```python
import functools

import jax
import jax.numpy as jnp
from jax import lax
from jax.experimental import pallas as pl
from jax.experimental.pallas import tpu as pltpu

F32 = jnp.float32
BF16 = jnp.bfloat16
EPS = 1e-6
NEG = -1e30

LANES = 128
SUBLANES = 8
VMEM_LIMIT_BYTES = 56 * 1024 * 1024

D_MODEL = 2048
D_POOL = 512
POOL_WINDOWS = (2, 4, 8, 16)
POOL_BUF = max(POOL_WINDOWS) - 1
POOL_HIST = 16
HEAD_DIM = 64
N_Q_HEADS = 24
N_KV_HEADS = 4
Q_PER_KV = N_Q_HEADS // N_KV_HEADS
WINDOW = 128
D_ATTN = N_Q_HEADS * HEAD_DIM
D_KV = N_KV_HEADS * HEAD_DIM
D_IN0 = D_POOL + D_ATTN + 2 * D_KV
O_K = D_POOL + D_ATTN
O_V = O_K + D_KV
GDN_HEADS = 12
GDN_DK = 128
GDN_DV = 128
D_GDN_K = GDN_HEADS * GDN_DK
D_GDN_V = GDN_HEADS * GDN_DV
D_GDN_CONV = 2 * D_GDN_K + D_GDN_V
D_GDN_MAIN = D_GDN_CONV + D_GDN_V
GDN_CONV = 4
GDN_CHUNK = 64
D_SCONV = 512
SCONV_W = 3
D_TAIL = 3 * D_SCONV + LANES
D_FF = 5632
FFN_CONV = 3
LANE_BETA = 0
LANE_G = GDN_HEADS
LANE_EG = 2 * GDN_HEADS


def _cparams(*sem):
    return pltpu.CompilerParams(dimension_semantics=sem, vmem_limit_bytes=VMEM_LIMIT_BYTES)


def _dot(a, b):
    return jnp.dot(a, b, preferred_element_type=F32)


def _dot_nt(a, b):
    return lax.dot_general(a, b, (((1,), (1,)), ((), ())), preferred_element_type=F32)


def _dot_tn(a, b):
    return lax.dot_general(a, b, (((0,), (0,)), ((), ())), preferred_element_type=F32)


def _silu(x):
    return x * jax.nn.sigmoid(x)


def _cast_rows(w_ref, wbf_ref, chunk=256):
    k = w_ref.shape[0]
    chunk = min(chunk, k)
    assert k % chunk == 0

    def body(c, carry):
        r = pl.multiple_of(c * chunk, chunk)
        wbf_ref[pl.ds(r, chunk), :] = w_ref[pl.ds(r, chunk), :].astype(BF16)
        return carry

    lax.fori_loop(0, k // chunk, body, 0)


def _shift_rows(u, prev8, k):
    rolled = pltpu.roll(u, k, 0)
    head = pltpu.roll(jnp.concatenate([prev8, u[0:SUBLANES]], axis=0), k, 0)[SUBLANES:2 * SUBLANES]
    return jnp.concatenate([head, rolled[SUBLANES:]], axis=0)


def _rmsnorm_body(x_ref, w_ref, o_ref):
    x = x_ref[...]
    ms = jnp.mean(x * x, axis=-1, keepdims=True)
    o_ref[...] = (x * lax.rsqrt(ms + EPS) * w_ref[...]).astype(o_ref.dtype)


def _rmsnorm(x, w, out_dtype):
    m, d = x.shape
    tm = min(m, 512)
    assert m % tm == 0
    return pl.pallas_call(
        _rmsnorm_body,
        grid=(m // tm,),
        in_specs=[pl.BlockSpec((tm, d), lambda i: (i, 0)), pl.BlockSpec((1, d), lambda i: (0, 0))],
        out_specs=pl.BlockSpec((tm, d), lambda i: (i, 0)),
        out_shape=jax.ShapeDtypeStruct((m, d), out_dtype),
        compiler_params=_cparams("parallel"),
        name="rmsnorm",
    )(x, w.reshape(1, d))


def _mm_body(n_pieces, has_res, *refs):
    x_refs = refs[:n_pieces]
    w_refs = refs[n_pieces:2 * n_pieces]
    pos = 2 * n_pieces
    res_ref = refs[pos] if has_res else None
    pos += int(has_res)
    o_ref = refs[pos]
    wbf_refs = refs[pos + 1:]

    @pl.when(pl.program_id(1) == 0)
    def _():
        for w_ref, wbf_ref in zip(w_refs, wbf_refs):
            _cast_rows(w_ref, wbf_ref)

    acc = None
    for x_ref, wbf_ref in zip(x_refs, wbf_refs):
        d = _dot(x_ref[...], wbf_ref[...])
        acc = d if acc is None else acc + d
    if has_res:
        acc = acc + res_ref[...]
    o_ref[...] = acc.astype(o_ref.dtype)


def _matmul(xs, w, n, *, tn, tm, residual=None, out_dtype=F32, name="matmul"):
    m = xs[0].shape[0]
    tm = min(tm, m)
    assert m % tm == 0 and n % tn == 0
    in_specs, scratch, row = [], [], 0
    for x in xs:
        kp = x.shape[1]
        in_specs.append(pl.BlockSpec((tm, kp), lambda j, i: (i, 0)))
    for x in xs:
        kp = x.shape[1]
        assert row % kp == 0
        in_specs.append(pl.BlockSpec((kp, tn), functools.partial(lambda j, i, rb: (rb, j), rb=row // kp)))
        scratch.append(pltpu.VMEM((kp, tn), BF16))
        row += kp
    args = list(xs) + [w] * len(xs)
    if residual is not None:
        in_specs.append(pl.BlockSpec((tm, tn), lambda j, i: (i, j)))
        args.append(residual)
    return pl.pallas_call(
        functools.partial(_mm_body, len(xs), residual is not None),
        grid=(n // tn, m // tm),
        in_specs=in_specs,
        out_specs=pl.BlockSpec((tm, tn), lambda j, i: (i, j)),
        out_shape=jax.ShapeDtypeStruct((m, n), out_dtype),
        scratch_shapes=scratch,
        compiler_params=_cparams("arbitrary", "arbitrary"),
        name=name,
    )(*args)


def _ffn_up_body(seq_mode, x_ref, wa_ref, wb_ref, cwa_ref, cwb_ref, *rest):
    if seq_mode:
        hid_ref, upa_ref, upb_ref, wa_bf, wb_bf, ca_ref, cb_ref = rest
    else:
        a2_ref, a1_ref, b2_ref, b1_ref, hid_ref, upa_ref, upb_ref, wa_bf, wb_bf = rest
    tm = x_ref.shape[0]

    @pl.when(pl.program_id(1) == 0)
    def _():
        _cast_rows(wa_ref, wa_bf)
        _cast_rows(wb_ref, wb_bf)
        if seq_mode:
            ca_ref[...] = jnp.zeros_like(ca_ref)
            cb_ref[...] = jnp.zeros_like(cb_ref)

    x = x_ref[...]
    ua = _dot(x, wa_bf[...])
    ub = _dot(x, wb_bf[...])
    if seq_mode:
        pa, pb = ca_ref[...], cb_ref[...]
        taps_a = (_shift_rows(ua, pa, 2), _shift_rows(ua, pa, 1))
        taps_b = (_shift_rows(ub, pb, 2), _shift_rows(ub, pb, 1))
        ca_ref[...] = ua[tm - SUBLANES:]
        cb_ref[...] = ub[tm - SUBLANES:]
        upa_ref[...] = ua[tm - SUBLANES:]
        upb_ref[...] = ub[tm - SUBLANES:]
    else:
        taps_a = (a2_ref[...], a1_ref[...])
        taps_b = (b2_ref[...], b1_ref[...])
        upa_ref[...] = ua
        upb_ref[...] = ub
    c_a = cwa_ref[0:1, :] * taps_a[0] + cwa_ref[1:2, :] * taps_a[1] + cwa_ref[2:3, :] * ua
    c_b = cwb_ref[0:1, :] * taps_b[0] + cwb_ref[1:2, :] * taps_b[1] + cwb_ref[2:3, :] * ub
    hid_ref[...] = (_silu(c_a) * c_b).astype(hid_ref.dtype)


def _ffn_up(h, w_up, conv_w, state, *, tn=512, tm=512):
    m, k = h.shape
    seq_mode = state is None
    tm = min(tm, m)
    nb = D_FF // tn
    assert m % tm == 0 and D_FF % tn == 0
    keep = SUBLANES if seq_mode else tm
    in_specs = [
        pl.BlockSpec((tm, k), lambda j, i: (i, 0)),
        pl.BlockSpec((k, tn), lambda j, i: (0, j)),
        pl.BlockSpec((k, tn), lambda j, i: (0, j + nb)),
        pl.BlockSpec((FFN_CONV, tn), lambda j, i: (0, j)),
        pl.BlockSpec((FFN_CONV, tn), lambda j, i: (0, j + nb)),
    ]
    args = [h, w_up, w_up, conv_w, conv_w]
    scratch = [pltpu.VMEM((k, tn), BF16), pltpu.VMEM((k, tn), BF16)]
    if seq_mode:
        scratch += [pltpu.VMEM((SUBLANES, tn), F32), pltpu.VMEM((SUBLANES, tn), F32)]
        keep_rows = SUBLANES
    else:
        assert m == tm
        for off in (0, 2 * nb, nb, 3 * nb):
            in_specs.append(pl.BlockSpec((tm, tn), functools.partial(lambda j, i, o: (i, j + o), o=off)))
            args.append(state)
        keep_rows = m
    keep_spec = pl.BlockSpec((keep, tn), (lambda j, i: (0, j)) if seq_mode else (lambda j, i: (i, j)))
    return pl.pallas_call(
        functools.partial(_ffn_up_body, seq_mode),
        grid=(nb, m // tm),
        in_specs=in_specs,
        out_specs=[pl.BlockSpec((tm, tn), lambda j, i: (i, j)), keep_spec, keep_spec],
        out_shape=[
            jax.ShapeDtypeStruct((m, D_FF), BF16),
            jax.ShapeDtypeStruct((keep_rows, D_FF), F32),
            jax.ShapeDtypeStruct((keep_rows, D_FF), F32),
        ],
        scratch_shapes=scratch,
        compiler_params=_cparams("arbitrary", "arbitrary"),
        name="ffn_up_seq" if seq_mode else "ffn_up_step",
    )(*args)


def _half_lane_pair(x, head_in_high, lo_mask):
    if head_in_high:
        hi = jnp.where(lo_mask, 0.0, x)
        return pltpu.roll(hi, HEAD_DIM, 1), hi
    lo = jnp.where(lo_mask, x, 0.0)
    return lo, pltpu.roll(lo, HEAD_DIM, 1)


def _mixer0_seq_body(sink_ref, z_ref, kvp_ref, up_ref, pw_ref, ps_ref, o_ref):
    n = pl.program_id(0)
    first = n == 0
    w = WINDOW
    lo_mask = lax.broadcasted_iota(jnp.int32, (1, LANES), 1) < HEAD_DIM

    hist = jnp.where(first, 0.0, up_ref[...])
    pos = n * w + lax.broadcasted_iota(jnp.int32, (w, 1), 0)
    for g, win in enumerate(POOL_WINDOWS):
        sl = slice(g * LANES, (g + 1) * LANES)
        e = jnp.concatenate([hist[:, sl], z_ref[:, sl]], axis=0)
        s, sh = e, 1
        while sh < win:
            s = s + pltpu.roll(s, sh, 0)
            sh *= 2
        cnt = jnp.minimum(pos + 1, win).astype(F32)
        diff = s[POOL_HIST:] / cnt - e[POOL_HIST:]
        y = _dot(diff.astype(BF16), pw_ref[g].astype(BF16)) * ps_ref[:, sl]
        o_ref[:, sl] = y.astype(o_ref.dtype)

    row = lax.broadcasted_iota(jnp.int32, (w, 2 * w), 0)
    col = lax.broadcasted_iota(jnp.int32, (w, 2 * w), 1)
    valid = (col >= row) & (col <= row + w) & (jnp.logical_not(first) | (col >= w))
    tiles_per_kv = Q_PER_KV * HEAD_DIM // LANES
    for c in range(D_KV // LANES):
        k2 = jnp.concatenate([kvp_ref[:, c * LANES:(c + 1) * LANES],
                              z_ref[:, O_K + c * LANES:O_K + (c + 1) * LANES]], axis=0)
        v2 = jnp.concatenate([kvp_ref[:, D_KV + c * LANES:D_KV + (c + 1) * LANES],
                              z_ref[:, O_V + c * LANES:O_V + (c + 1) * LANES]], axis=0)
        for p in range(2):
            hk = 2 * c + p
            k_lo, k_hi = (t.astype(BF16) for t in _half_lane_pair(k2, p == 1, lo_mask))
            v_lo, v_hi = (t.astype(BF16) for t in _half_lane_pair(v2, p == 1, lo_mask))
            q0 = D_POOL + hk * Q_PER_KV * HEAD_DIM
            qst = jnp.concatenate(
                [z_ref[:, q0 + a * LANES:q0 + (a + 1) * LANES] for a in range(tiles_per_kv)], axis=0)
            qst = (qst * HEAD_DIM ** -0.5).astype(BF16)
            s_lo = _dot_nt(qst, k_lo)
            s_hi = _dot_nt(qst, k_hi)
            for a in range(tiles_per_kv):
                probs, inv = [], []
                for par, s_all in ((0, s_lo), (1, s_hi)):
                    sk = sink_ref[hk * Q_PER_KV + 2 * a + par]
                    s = jnp.where(valid, s_all[a * w:(a + 1) * w], NEG)
                    mx = jnp.maximum(jnp.max(s, axis=-1, keepdims=True), sk)
                    pr = jnp.exp(s - mx)
                    den = jnp.sum(pr, axis=-1, keepdims=True) + jnp.exp(sk - mx)
                    probs.append(pr.astype(BF16))
                    inv.append(1.0 / den)
                o = _dot(probs[0], v_lo) + _dot(probs[1], v_hi)
                o = o * jnp.where(lo_mask, inv[0], inv[1])
                o_ref[:, q0 + a * LANES:q0 + (a + 1) * LANES] = o.astype(o_ref.dtype)


def _mixer0_seq(z0, pool_w, pool_scale, sinks):
    t = z0.shape[0]
    w = WINDOW
    assert t % w == 0
    hist_blocks = w // POOL_HIST
    return pl.pallas_call(
        _mixer0_seq_body,
        grid=(t // w,),
        in_specs=[
            pl.BlockSpec(memory_space=pltpu.SMEM),
            pl.BlockSpec((w, D_IN0), lambda n: (n, 0)),
            pl.BlockSpec((w, 2 * D_KV), lambda n: (jnp.maximum(n - 1, 0), O_K // (2 * D_KV))),
            pl.BlockSpec((POOL_HIST, D_POOL), lambda n: (jnp.maximum(n * hist_blocks - 1, 0), 0)),
            pl.BlockSpec((len(POOL_WINDOWS), LANES, LANES), lambda n: (0, 0, 0)),
            pl.BlockSpec((1, D_POOL), lambda n: (0, 0)),
        ],
        out_specs=pl.BlockSpec((w, D_POOL + D_ATTN), lambda n: (n, 0)),
        out_shape=jax.ShapeDtypeStruct((t, D_POOL + D_ATTN), BF16),
        compiler_params=_cparams("parallel"),
        name="mixer0_seq",
    )(sinks, z0, z0, z0, pool_w, pool_scale.reshape(1, D_POOL))


def _pool_step_body(*refs):
    hist_refs = refs[:POOL_BUF]
    z_ref, pw_ref, ps_ref, o_ref = refs[POOL_BUF:]
    for g, win in enumerate(POOL_WINDOWS):
        sl = slice(g * LANES, (g + 1) * LANES)
        u = z_ref[:, sl]
        s = u
        for r in range(POOL_BUF - (win - 1), POOL_BUF):
            s = s + hist_refs[r][:, sl]
        diff = s / float(win) - u
        y = _dot(diff.astype(BF16), pw_ref[g].astype(BF16)) * ps_ref[:, sl]
        o_ref[:, sl] = y.astype(o_ref.dtype)


def _pool_step(z0, hist, pool_w, pool_scale):
    b = z0.shape[0]
    in_specs = [pl.BlockSpec((b, D_POOL), functools.partial(lambda i, r: (0, r), r=r)) for r in range(POOL_BUF)]
    in_specs += [
        pl.BlockSpec((b, D_POOL), lambda i: (0, 0)),
        pl.BlockSpec((len(POOL_WINDOWS), LANES, LANES), lambda i: (0, 0, 0)),
        pl.BlockSpec((1, D_POOL), lambda i: (0, 0)),
    ]
    return pl.pallas_call(
        _pool_step_body,
        grid=(1,),
        in_specs=in_specs,
        out_specs=pl.BlockSpec((b, D_POOL), lambda i: (0, 0)),
        out_shape=jax.ShapeDtypeStruct((b, D_POOL), BF16),
        compiler_params=_cparams("arbitrary"),
        name="pool_step",
    )(*([hist] * POOL_BUF), z0, pool_w, pool_scale.reshape(1, D_POOL))


def _attn_step_body(qm_ref, kc_ref, vc_ref, kn_ref, vn_ref, sink_ref, o_ref):
    bs = qm_ref.shape[0]
    sk = sink_ref[...]
    for b in range(bs):
        qm = qm_ref[b] * HEAD_DIM ** -0.5
        kn = kn_ref[b:b + 1, :]
        vn = vn_ref[b:b + 1, :]
        s = _dot_nt(qm.astype(BF16), kc_ref[b].astype(BF16))
        s_new = jnp.sum(qm * kn, axis=-1, keepdims=True)
        mx = jnp.maximum(jnp.maximum(jnp.max(s, axis=-1, keepdims=True), s_new), sk)
        pr = jnp.exp(s - mx)
        pn = jnp.exp(s_new - mx)
        den = jnp.sum(pr, axis=-1, keepdims=True) + pn + jnp.exp(sk - mx)
        o = _dot(pr.astype(BF16), vc_ref[b].astype(BF16)) + pn * vn
        o_ref[b] = o / den


def _attn_step(qm, kc, vc, z0, sinks, *, bs=8):
    b = qm.shape[0]
    assert b % bs == 0 and kc.shape[1] == WINDOW
    return pl.pallas_call(
        _attn_step_body,
        grid=(b // bs,),
        in_specs=[
            pl.BlockSpec((bs, N_Q_HEADS, D_KV), lambda i: (i, 0, 0)),
            pl.BlockSpec((bs, WINDOW, D_KV), lambda i: (i, 0, 0)),
            pl.BlockSpec((bs, WINDOW, D_KV), lambda i: (i, 0, 0)),
            pl.BlockSpec((bs, D_KV), lambda i: (i, O_K // D_KV)),
            pl.BlockSpec((bs, D_KV), lambda i: (i, O_V // D_KV)),
            pl.BlockSpec((N_Q_HEADS, 1), lambda i: (0, 0)),
        ],
        out_specs=pl.BlockSpec((bs, N_Q_HEADS, D_KV), lambda i: (i, 0, 0)),
        out_shape=jax.ShapeDtypeStruct((b, N_Q_HEADS, D_KV), F32),
        compiler_params=_cparams("parallel"),
        name="attn_step",
    )(qm, kc, vc, z0, z0, sinks.reshape(N_Q_HEADS, 1))


def _qkvz_body(seq_mode, x_ref, w_ref, cw_ref, *rest):
    if seq_mode:
        o_ref, raw_ref, w_bf, carry_ref = rest
    else:
        p3_ref, p2_ref, p1_ref, o_ref, raw_ref, w_bf = rest
    tm = x_ref.shape[0]
    j = pl.program_id(0)

    @pl.when(pl.program_id(1) == 0)
    def _():
        _cast_rows(w_ref, w_bf)
        if seq_mode:
            carry_ref[...] = jnp.zeros_like(carry_ref)

    u = _dot(x_ref[...], w_bf[...])
    if seq_mode:
        prev = carry_ref[...]
        taps = tuple(_shift_rows(u, prev, k) for k in (3, 2, 1))
        carry_ref[...] = u[tm - SUBLANES:]
        raw_ref[...] = u[tm - SUBLANES:]
    else:
        taps = (p3_ref[...], p2_ref[...], p1_ref[...])
        raw_ref[...] = u

    def conv_act():
        c = cw_ref[3:4, :] * u
        c = cw_ref[0:1, :] * taps[0] + cw_ref[1:2, :] * taps[1] + cw_ref[2:3, :] * taps[2] + c
        return _silu(c)

    def heads(a):
        return [a[:, h * LANES:(h + 1) * LANES] for h in range(GDN_HEADS)]

    def l2norm(a):
        return a * lax.rsqrt(jnp.sum(a * a, axis=-1, keepdims=True) + EPS)

    @pl.when(j == 0)
    def _():
        for h, a in enumerate(heads(conv_act())):
            o_ref[h] = l2norm(a) * GDN_DK ** -0.5

    @pl.when(j == 1)
    def _():
        for h, a in enumerate(heads(conv_act())):
            o_ref[h] = l2norm(a)

    @pl.when(j == 2)
    def _():
        for h, a in enumerate(heads(conv_act())):
            o_ref[h] = a

    @pl.when(j == 3)
    def _():
        for h, a in enumerate(heads(u)):
            o_ref[h] = a


def _qkvz(h, w_in, conv_w, state, *, tm=512):
    m, k = h.shape
    seq_mode = state is None
    tm = min(tm, m)
    tn = D_GDN_K
    assert D_GDN_K == D_GDN_V and GDN_DK == LANES and m % tm == 0
    nj = D_GDN_MAIN // tn
    conv_blk = lambda j, i: (0, jnp.minimum(j, 2))
    in_specs = [
        pl.BlockSpec((tm, k), lambda j, i: (i, 0)),
        pl.BlockSpec((k, tn), lambda j, i: (0, j)),
        pl.BlockSpec((GDN_CONV, tn), conv_blk),
    ]
    args = [h, w_in, conv_w]
    scratch = [pltpu.VMEM((k, tn), BF16)]
    if seq_mode:
        scratch.append(pltpu.VMEM((SUBLANES, tn), F32))
        keep = SUBLANES
        keep_spec = pl.BlockSpec((keep, tn), lambda j, i: (0, j))
    else:
        assert m == tm
        for r in range(GDN_CONV - 1):
            in_specs.append(pl.BlockSpec(
                (tm, tn), functools.partial(lambda j, i, r: (i, 3 * r + jnp.minimum(j, 2)), r=r)))
            args.append(state)
        keep = m
        keep_spec = pl.BlockSpec((keep, tn), lambda j, i: (i, j))
    return pl.pallas_call(
        functools.partial(_qkvz_body, seq_mode),
        grid=(nj, m // tm),
        in_specs=in_specs,
        out_specs=[pl.BlockSpec((None, GDN_HEADS, tm, LANES), lambda j, i: (j, 0, i, 0)), keep_spec],
        out_shape=[
            jax.ShapeDtypeStruct((nj, GDN_HEADS, m, LANES), F32),
            jax.ShapeDtypeStruct((keep, D_GDN_MAIN), F32),
        ],
        scratch_shapes=scratch,
        compiler_params=_cparams("arbitrary", "arbitrary"),
        name="qkvz_seq" if seq_mode else "qkvz_step",
    )(*args)


def _tail_body(seq_mode, x_ref, w_ref, cw_ref, alog_ref, dtb_ref, *rest):
    if seq_mode:
        ysc_ref, gate_ref, m_ref, w_bf, carry_ref = rest
    else:
        p2_ref, p1_ref, ysc_ref, gate_ref, m_ref, w_bf = rest
    tm = x_ref.shape[0]

    @pl.when(pl.program_id(0) == 0)
    def _():
        _cast_rows(w_ref, w_bf)
        if seq_mode:
            carry_ref[...] = jnp.zeros_like(carry_ref)

    z = _dot(x_ref[...], w_bf[...])
    sb = z[:, 0:D_SCONV]
    mm = z[:, D_SCONV:2 * D_SCONV] * z[:, 2 * D_SCONV:3 * D_SCONV]
    raw = z[:, 3 * D_SCONV:]
    if seq_mode:
        prev = carry_ref[...]
        taps = (_shift_rows(mm, prev, 2), _shift_rows(mm, prev, 1))
        carry_ref[...] = mm[tm - SUBLANES:]
        m_ref[...] = mm[tm - SUBLANES:]
    else:
        taps = (p2_ref[...], p1_ref[...])
        m_ref[...] = mm
    conv = cw_ref[0:1, :] * taps[0] + cw_ref[1:2, :] * taps[1] + cw_ref[2:3, :] * mm
    ysc_ref[...] = (sb * conv).astype(ysc_ref.dtype)

    lane = lax.broadcasted_iota(jnp.int32, raw.shape, 1)
    beta = jax.nn.sigmoid(raw)
    sp = raw + dtb_ref[...]
    softplus = jnp.maximum(sp, 0.0) + jnp.log1p(jnp.exp(-jnp.abs(sp)))
    g = -jnp.exp(alog_ref[...]) * softplus
    eg = pltpu.roll(jnp.exp(g), LANE_EG - LANE_G, 1)
    gate_ref[...] = jnp.where(lane < LANE_G, beta, jnp.where(lane < LANE_EG, g, eg))


def _tail(h, w_tail, conv_w, a_log_row, dt_bias_row, state, *, tm=512):
    m, k = h.shape
    seq_mode = state is None
    tm = min(tm, m)
    assert m % tm == 0
    in_specs = [
        pl.BlockSpec((tm, k), lambda i: (i, 0)),
        pl.BlockSpec((k, D_TAIL), lambda i: (0, 0)),
        pl.BlockSpec((SCONV_W, D_SCONV), lambda i: (0, 0)),
        pl.BlockSpec((1, LANES), lambda i: (0, 0)),
        pl.BlockSpec((1, LANES), lambda i: (0, 0)),
    ]
    args = [h, w_tail, conv_w, a_log_row, dt_bias_row]
    scratch = [pltpu.VMEM((k, D_TAIL), BF16)]
    if seq_mode:
        scratch.append(pltpu.VMEM((SUBLANES, D_SCONV), F32))
        keep = SUBLANES
        keep_spec = pl.BlockSpec((keep, D_SCONV), lambda i: (0, 0))
    else:
        assert m == tm
        for r in range(SCONV_W - 1):
            in_specs.append(pl.BlockSpec((tm, D_SCONV), functools.partial(lambda i, r: (i, r), r=r)))
            args.append(state)
        keep = m
        keep_spec = pl.BlockSpec((keep, D_SCONV), lambda i: (i, 0))
    return pl.pallas_call(
        functools.partial(_tail_body, seq_mode),
        grid=(m // tm,),
        in_specs=in_specs,
        out_specs=[
            pl.BlockSpec((tm, D_SCONV), lambda i: (i, 0)),
            pl.BlockSpec((tm, LANES), lambda i: (i, 0)),
            keep_spec,
        ],
        out_shape=[
            jax.ShapeDtypeStruct((m, D_SCONV), BF16),
            jax.ShapeDtypeStruct((m, LANES), F32),
            jax.ShapeDtypeStruct((keep, D_SCONV), F32),
        ],
        scratch_shapes=scratch,
        compiler_params=_cparams("arbitrary"),
        name="tail_seq" if seq_mode else "tail_step",
    )(*args)


def _gated_norm(o, zg, nw):
    y = o * lax.rsqrt(jnp.mean(o * o, axis=-1, keepdims=True) + EPS) * nw
    return y * _silu(zg)


def _gdn_seq_body(qkvz_ref, gate_ref, nw_ref, y_ref, sfin_ref, s_ref):
    c = GDN_CHUNK
    n = pl.program_id(0)

    @pl.when(n == 0)
    def _():
        s_ref[...] = jnp.zeros_like(s_ref)

    gate = gate_ref[...]
    r = lax.broadcasted_iota(jnp.int32, (c, c), 0)
    cc = lax.broadcasted_iota(jnp.int32, (c, c), 1)
    tri = r >= cc
    strict = r > cc
    ones = jnp.where(tri, 1.0, 0.0).astype(BF16)
    g1 = gate.astype(BF16)
    r1 = gate - g1.astype(F32)
    g2 = r1.astype(BF16)
    g3 = (r1 - g2.astype(F32)).astype(BF16)
    gc = _dot(ones, g1) + _dot(ones, g2) + _dot(ones, g3)
    gct = jnp.concatenate([gc, jnp.zeros((LANES - c, LANES), F32)], axis=0).T
    nw = nw_ref[...]

    for h in range(GDN_HEADS):
        gcol = gc[:, LANE_G + h:LANE_G + h + 1]
        grow = gct[LANE_G + h:LANE_G + h + 1, 0:c]
        bcol = gate[:, LANE_BETA + h:LANE_BETA + h + 1]
        decay = jnp.exp(jnp.where(tri, gcol - grow, NEG))
        q, k, v, zg = qkvz_ref[0, h], qkvz_ref[1, h], qkvz_ref[2, h], qkvz_ref[3, h]
        kb = k * bcol
        eg = jnp.exp(gcol)
        k_b = k.astype(BF16)
        x = jnp.where(strict, -(_dot_nt(kb.astype(BF16), k_b) * decay), 0.0)
        t_off, p = x, x
        for _ in range(c.bit_length() - 2):
            p_b = p.astype(BF16)
            p = _dot(p_b, p_b)
            t_off = t_off + p + _dot(t_off.astype(BF16), p.astype(BF16))
        rhs = jnp.concatenate([v * bcol, kb * eg], axis=1)
        sol = rhs + _dot(t_off.astype(BF16), rhs.astype(BF16))
        u, w = sol[:, :GDN_DV], sol[:, GDN_DV:]
        intra = jnp.where(tri, _dot_nt(q.astype(BF16), k_b) * decay, 0.0)
        glast = gcol[c - 1:c, :]
        k_dec = k * jnp.exp(glast - gcol)
        s = s_ref[h]
        ws = _dot(jnp.concatenate([w, q * eg], axis=0).astype(BF16), s.astype(BF16))
        v_new = u - ws[:c]
        v_new_b = v_new.astype(BF16)
        o = ws[c:] + _dot(intra.astype(BF16), v_new_b)
        s_ref[h] = s * jnp.exp(glast) + _dot_tn(k_dec.astype(BF16), v_new_b)
        y_ref[:, h * GDN_DV:(h + 1) * GDN_DV] = _gated_norm(o, zg, nw).astype(y_ref.dtype)

    @pl.when(n == pl.num_programs(0) - 1)
    def _():
        sfin_ref[...] = s_ref[...]


def _gdn_seq(qkvz, gate, norm_w):
    t = qkvz.shape[2]
    c = GDN_CHUNK
    assert t % c == 0 and c & (c - 1) == 0
    return pl.pallas_call(
        _gdn_seq_body,
        grid=(t // c,),
        in_specs=[
            pl.BlockSpec((4, GDN_HEADS, c, LANES), lambda n: (0, 0, n, 0)),
            pl.BlockSpec((c, LANES), lambda n: (n, 0)),
            pl.BlockSpec((1, GDN_DV), lambda n: (0, 0)),
        ],
        out_specs=[
            pl.BlockSpec((c, D_GDN_V), lambda n: (n, 0)),
            pl.BlockSpec((GDN_HEADS, GDN_DK, GDN_DV), lambda n: (0, 0, 0)),
        ],
        out_shape=[
            jax.ShapeDtypeStruct((t, D_GDN_V), BF16),
            jax.ShapeDtypeStruct((GDN_HEADS, GDN_DK, GDN_DV), F32),
        ],
        scratch_shapes=[pltpu.VMEM((GDN_HEADS, GDN_DK, GDN_DV), F32)],
        compiler_params=_cparams("arbitrary"),
        name="gdn_seq",
    )(qkvz, gate, norm_w.reshape(1, GDN_DV))


def _gdn_step_body(qkvz_ref, gate_ref, nw_ref, s_ref, y_ref, so_ref, beta_ref, eg_ref, yacc_ref):
    h = pl.program_id(0)
    bs = s_ref.shape[0]
    gate = gate_ref[...]
    lane = lax.broadcasted_iota(jnp.int32, gate.shape, 1)

    def pick(l):
        col = jnp.sum(jnp.where(lane == l, gate, 0.0), axis=1, keepdims=True)
        return jnp.broadcast_to(col, gate.shape)

    beta_ref[...] = pick(LANE_BETA + h)
    eg_ref[...] = pick(LANE_EG + h)
    eye = (lax.broadcasted_iota(jnp.int32, (GDN_DK, GDN_DK), 0)
           == lax.broadcasted_iota(jnp.int32, (GDN_DK, GDN_DK), 1))
    nw = nw_ref[...]

    def column(row):
        return jnp.sum(jnp.where(eye, row, 0.0), axis=1, keepdims=True)

    def body(b, carry):
        one = pl.ds(b, 1)
        qcol = column(qkvz_ref[0, one, :])
        kcol = column(qkvz_ref[1, one, :])
        s = s_ref[b] * eg_ref[one, :]
        kv = jnp.sum(s * kcol, axis=0, keepdims=True)
        delta = (qkvz_ref[2, one, :] - kv) * beta_ref[one, :]
        s = s + kcol * delta
        so_ref[b] = s
        o = jnp.sum(s * qcol, axis=0, keepdims=True)
        yacc_ref[one, :] = _gated_norm(o, qkvz_ref[3, one, :], nw)
        return carry

    lax.fori_loop(0, bs, body, 0)
    y_ref[...] = yacc_ref[...].astype(y_ref.dtype)


def _gdn_step(qkvz, gate, norm_w, state, *, bs=32):
    b = qkvz.shape[2]
    bs = min(bs, b)
    assert b % bs == 0
    row_scratch = pltpu.VMEM((bs, LANES), F32)
    return pl.pallas_call(
        _gdn_step_body,
        grid=(GDN_HEADS, b // bs),
        in_specs=[
            pl.BlockSpec((4, None, bs, LANES), lambda h, i: (0, h, i, 0)),
            pl.BlockSpec((bs, LANES), lambda h, i: (i, 0)),
            pl.BlockSpec((1, GDN_DV), lambda h, i: (0, 0)),
            pl.BlockSpec((bs, None, GDN_DK, GDN_DV), lambda h, i: (i, h, 0, 0)),
        ],
        out_specs=[
            pl.BlockSpec((bs, GDN_DV), lambda h, i: (i, h)),
            pl.BlockSpec((bs, None, GDN_DK, GDN_DV), lambda h, i: (i, h, 0, 0)),
        ],
        out_shape=[
            jax.ShapeDtypeStruct((b, D_GDN_V), BF16),
            jax.ShapeDtypeStruct(state.shape, F32),
        ],
        scratch_shapes=[row_scratch, row_scratch, row_scratch],
        compiler_params=_cparams("parallel", "parallel"),
        name="gdn_step",
    )(qkvz, gate, norm_w.reshape(1, GDN_DV), state)


def _gate_param_row(p):
    return jnp.zeros((1, LANES), F32).at[0, LANE_G:LANE_G + GDN_HEADS].set(p.astype(F32))


def _tail_weight(w_in):
    o1 = D_GDN_MAIN
    o2 = o1 + 2 * GDN_HEADS
    pad = jnp.zeros((w_in.shape[0], LANES - 2 * GDN_HEADS), w_in.dtype)
    return jnp.concatenate([w_in[:, o2:], w_in[:, o1:o2], pad], axis=1)


def _ffn(x, state, norm_w, w_up, w_conv, w_down):
    hid, keep_a, keep_b = _ffn_up(_rmsnorm(x, norm_w, BF16), w_up, w_conv, state)
    y = _matmul([hid], w_down, D_MODEL, tn=512, tm=256, residual=x, name="ffn_down")
    return y, jnp.concatenate([keep_a, keep_b], axis=1)


def _layer0_in(x, norm_w, w_in):
    return _matmul([_rmsnorm(x, norm_w, BF16)], w_in, D_IN0, tn=512, tm=512, name="in_proj0")


def _layer1_in(x, norm_w, w_in, w_tail, gdn_conv, sconv_w, a_log_row, dt_bias_row, conv_state, sconv_state):
    h = _rmsnorm(x, norm_w, BF16)
    qkvz, raw = _qkvz(h, w_in, gdn_conv, conv_state)
    ysc, gate, mrows = _tail(h, w_tail, sconv_w, a_log_row, dt_bias_row, sconv_state)
    return qkvz, raw, ysc, gate, mrows


def kernel(x_prompt, x_sample, state_l0_pool, cache_l0_k, cache_l0_v, state_l0_ffn_conv, state_l1_gdn_conv, state_l1_gdn_S, state_l1_sconv, state_l1_ffn_conv, l0_norm_mix, l0_w_in, l0_pool_w, l0_pool_scale, l0_sinks, l0_w_out, l0_norm_ffn, l0_ffn_w_up, l0_ffn_conv, l0_ffn_w_down, l1_norm_mix, l1_w_in, l1_gdn_conv, l1_gdn_A_log, l1_gdn_dt_bias, l1_gdn_norm, l1_sconv_w, l1_w_out, l1_norm_ffn, l1_ffn_w_up, l1_ffn_conv, l1_ffn_w_down, final_norm):
    bp, t, d = x_prompt.shape
    nb, ts = x_sample.shape[:2]
    wb = cache_l0_k.shape[1]
    assert bp == 1 and ts == 1 and d == D_MODEL and wb == WINDOW and t >= WINDOW
    w_tail = _tail_weight(l1_w_in)
    a_log_row = _gate_param_row(l1_gdn_A_log)
    dt_bias_row = _gate_param_row(l1_gdn_dt_bias)
    last = SUBLANES

    xp = x_prompt[0]
    z0 = _layer0_in(xp, l0_norm_mix, l0_w_in)
    mix = _mixer0_seq(z0, l0_pool_w, l0_pool_scale, l0_sinks)
    x1 = _matmul([mix], l0_w_out, D_MODEL, tn=512, tm=512, residual=xp, name="out_proj0")
    x2, ffn0_rows = _ffn(x1, None, l0_norm_ffn, l0_ffn_w_up, l0_ffn_conv, l0_ffn_w_down)
    qkvz, raw, ysc, gate, mrows = _layer1_in(
        x2, l1_norm_mix, l1_w_in, w_tail, l1_gdn_conv, l1_sconv_w, a_log_row, dt_bias_row, None, None)
    ygdn, s_fin = _gdn_seq(qkvz, gate, l1_gdn_norm)
    x3 = _matmul([ygdn, ysc], l1_w_out, D_MODEL, tn=512, tm=512, residual=x2, name="out_proj1")
    x4, ffn1_rows = _ffn(x3, None, l1_norm_ffn, l1_ffn_w_up, l1_ffn_conv, l1_ffn_w_down)
    y_prompt = _rmsnorm(x4, final_norm, F32)[None]
    p_pool = z0[t - POOL_BUF:, :D_POOL][None]
    p_k = z0[t - wb:, O_K:O_V].reshape(1, wb, N_KV_HEADS, HEAD_DIM)
    p_v = z0[t - wb:, O_V:].reshape(1, wb, N_KV_HEADS, HEAD_DIM)
    p_ffn0 = ffn0_rows[last - (FFN_CONV - 1):][None]
    p_gconv = raw[last - (GDN_CONV - 1):, :D_GDN_CONV][None]
    p_s = s_fin[None]
    p_sconv = mrows[last - (SCONV_W - 1):][None]
    p_ffn1 = ffn1_rows[last - (FFN_CONV - 1):][None]

    xs = x_sample[:, 0]
    z0s = _layer0_in(xs, l0_norm_mix, l0_w_in)
    y_pool = _pool_step(z0s, state_l0_pool.reshape(nb, POOL_BUF * D_POOL), l0_pool_w, l0_pool_scale)
    kv_eye = jnp.eye(N_KV_HEADS, dtype=F32)
    q4 = z0s[:, D_POOL:O_K].reshape(nb, N_KV_HEADS, Q_PER_KV, HEAD_DIM)
    qm = jnp.einsum("bhgd,hk->bhgkd", q4, kv_eye).reshape(nb, N_Q_HEADS, D_KV)
    att = _attn_step(qm, cache_l0_k.reshape(nb, wb, D_KV), cache_l0_v.reshape(nb, wb, D_KV), z0s, l0_sinks)
    att = jnp.einsum("bhgkd,hk->bhgd", att.reshape(nb, N_KV_HEADS, Q_PER_KV, N_KV_HEADS, HEAD_DIM), kv_eye)
    mix_s = jnp.concatenate([y_pool, att.reshape(nb, D_ATTN).astype(BF16)], axis=1)
    x1s = _matmul([mix_s], l0_w_out, D_MODEL, tn=512, tm=512, residual=xs, name="out_proj0")
    x2s, ffn0_new = _ffn(x1s, state_l0_ffn_conv.reshape(nb, (FFN_CONV - 1) * 2 * D_FF),
                         l0_norm_ffn, l0_ffn_w_up, l0_ffn_conv, l0_ffn_w_down)
    qkvz_s, raw_s, ysc_s, gate_s, m_s = _layer1_in(
        x2s, l1_norm_mix, l1_w_in, w_tail, l1_gdn_conv, l1_sconv_w, a_log_row, dt_bias_row,
        state_l1_gdn_conv.reshape(nb, (GDN_CONV - 1) * D_GDN_CONV),
        state_l1_sconv.reshape(nb, (SCONV_W - 1) * D_SCONV))
    ygdn_s, s_new = _gdn_step(qkvz_s, gate_s, l1_gdn_norm, state_l1_gdn_S)
    x3s = _matmul([ygdn_s, ysc_s], l1_w_out, D_MODEL, tn=512, tm=512, residual=x2s, name="out_proj1")
    x4s, ffn1_new = _ffn(x3s, state_l1_ffn_conv.reshape(nb, (FFN_CONV - 1) * 2 * D_FF),
                         l1_norm_ffn, l1_ffn_w_up, l1_ffn_conv, l1_ffn_w_down)
    y_sample = _rmsnorm(x4s, final_norm, F32)[:, None]

    def push(state, new_row):
        return jnp.concatenate([state[:, 1:], new_row[:, None].astype(state.dtype)], axis=1)

    s_pool = push(state_l0_pool, z0s[:, :D_POOL])
    s_k = push(cache_l0_k, z0s[:, O_K:O_V].reshape(nb, N_KV_HEADS, HEAD_DIM))
    s_v = push(cache_l0_v, z0s[:, O_V:].reshape(nb, N_KV_HEADS, HEAD_DIM))
    s_ffn0 = push(state_l0_ffn_conv, ffn0_new)
    s_gconv = push(state_l1_gdn_conv, raw_s[:, :D_GDN_CONV])
    s_sconv = push(state_l1_sconv, m_s)
    s_ffn1 = push(state_l1_ffn_conv, ffn1_new)
    return (y_prompt, y_sample, p_pool, s_pool, p_k, s_k, p_v, s_v, p_ffn0, s_ffn0,
            p_gconv, s_gconv, p_s, s_new, p_sconv, s_sconv, p_ffn1, s_ffn1)
```

```python
import functools

import jax
import jax.numpy as jnp
from jax import lax
from jax.experimental import pallas as pl
from jax.experimental.pallas import tpu as pltpu

F32 = jnp.float32
BF16 = jnp.bfloat16
EPS = 1e-6
NEG = -1e30

LANES = 128
SUBLANES = 8
VMEM_LIMIT_BYTES = 56 * 1024 * 1024

D_MODEL = 2048
D_POOL = 512
POOL_WINDOWS = (2, 4, 8, 16)
POOL_BUF = max(POOL_WINDOWS) - 1
POOL_HIST = 16
HEAD_DIM = 64
N_Q_HEADS = 24
N_KV_HEADS = 4
Q_PER_KV = N_Q_HEADS // N_KV_HEADS
WINDOW = 128
D_ATTN = N_Q_HEADS * HEAD_DIM
D_KV = N_KV_HEADS * HEAD_DIM
D_IN0 = D_POOL + D_ATTN + 2 * D_KV
O_K = D_POOL + D_ATTN
O_V = O_K + D_KV
GDN_HEADS = 12
GDN_DK = 128
GDN_DV = 128
D_GDN_K = GDN_HEADS * GDN_DK
D_GDN_V = GDN_HEADS * GDN_DV
D_GDN_CONV = 2 * D_GDN_K + D_GDN_V
D_GDN_MAIN = D_GDN_CONV + D_GDN_V
GDN_CONV = 4
GDN_CHUNK = 64
GDN_STEP_UNROLL = 8
D_SCONV = 512
SCONV_W = 3
D_TAIL = 3 * D_SCONV + LANES
D_FF = 5632
FFN_CONV = 3
LANE_BETA = 0
LANE_G = GDN_HEADS
LANE_EG = 2 * GDN_HEADS


def _cparams(*sem):
    return pltpu.CompilerParams(dimension_semantics=sem, vmem_limit_bytes=VMEM_LIMIT_BYTES)


def _dot(a, b):
    return jnp.dot(a, b, preferred_element_type=F32)


def _dot_nt(a, b):
    return lax.dot_general(a, b, (((1,), (1,)), ((), ())), preferred_element_type=F32)


def _dot_tn(a, b):
    return lax.dot_general(a, b, (((0,), (0,)), ((), ())), preferred_element_type=F32)


def _silu(x):
    return x * jax.nn.sigmoid(x)


def _cast_rows(w_ref, wbf_ref, chunk=256):
    k = w_ref.shape[0]
    chunk = min(chunk, k)
    assert k % chunk == 0

    def body(c, carry):
        r = pl.multiple_of(c * chunk, chunk)
        wbf_ref[pl.ds(r, chunk), :] = w_ref[pl.ds(r, chunk), :].astype(BF16)
        return carry

    lax.fori_loop(0, k // chunk, body, 0)


def _shift_rows(u, prev8, k):
    rolled = pltpu.roll(u, k, 0)
    head = pltpu.roll(jnp.concatenate([prev8, u[0:SUBLANES]], axis=0), k, 0)[SUBLANES:2 * SUBLANES]
    return jnp.concatenate([head, rolled[SUBLANES:]], axis=0)


def _rmsnorm_body(x_ref, w_ref, o_ref):
    x = x_ref[...]
    ms = jnp.mean(x * x, axis=-1, keepdims=True)
    o_ref[...] = (x * lax.rsqrt(ms + EPS) * w_ref[...]).astype(o_ref.dtype)


def _rmsnorm(x, w, out_dtype):
    m, d = x.shape
    tm = min(m, 512)
    assert m % tm == 0
    return pl.pallas_call(
        _rmsnorm_body,
        grid=(m // tm,),
        in_specs=[pl.BlockSpec((tm, d), lambda i: (i, 0)), pl.BlockSpec((1, d), lambda i: (0, 0))],
        out_specs=pl.BlockSpec((tm, d), lambda i: (i, 0)),
        out_shape=jax.ShapeDtypeStruct((m, d), out_dtype),
        compiler_params=_cparams("parallel"),
        name="rmsnorm",
    )(x, w.reshape(1, d))


def _mm_body(n_pieces, has_res, *refs):
    x_refs = refs[:n_pieces]
    w_refs = refs[n_pieces:2 * n_pieces]
    pos = 2 * n_pieces
    res_ref = refs[pos] if has_res else None
    pos += int(has_res)
    o_ref = refs[pos]
    wbf_refs = refs[pos + 1:]

    @pl.when(pl.program_id(1) == 0)
    def _():
        for w_ref, wbf_ref in zip(w_refs, wbf_refs):
            _cast_rows(w_ref, wbf_ref)

    acc = None
    for x_ref, wbf_ref in zip(x_refs, wbf_refs):
        d = _dot(x_ref[...], wbf_ref[...])
        acc = d if acc is None else acc + d
    if has_res:
        acc = acc + res_ref[...]
    o_ref[...] = acc.astype(o_ref.dtype)


def _matmul(xs, w, n, *, tn, tm, residual=None, out_dtype=F32, name="matmul"):
    m = xs[0].shape[0]
    tm = min(tm, m)
    assert m % tm == 0 and n % tn == 0
    in_specs, scratch, row = [], [], 0
    for x in xs:
        kp = x.shape[1]
        in_specs.append(pl.BlockSpec((tm, kp), lambda j, i: (i, 0)))
    for x in xs:
        kp = x.shape[1]
        assert row % kp == 0
        in_specs.append(pl.BlockSpec((kp, tn), functools.partial(lambda j, i, rb: (rb, j), rb=row // kp)))
        scratch.append(pltpu.VMEM((kp, tn), BF16))
        row += kp
    args = list(xs) + [w] * len(xs)
    if residual is not None:
        in_specs.append(pl.BlockSpec((tm, tn), lambda j, i: (i, j)))
        args.append(residual)
    return pl.pallas_call(
        functools.partial(_mm_body, len(xs), residual is not None),
        grid=(n // tn, m // tm),
        in_specs=in_specs,
        out_specs=pl.BlockSpec((tm, tn), lambda j, i: (i, j)),
        out_shape=jax.ShapeDtypeStruct((m, n), out_dtype),
        scratch_shapes=scratch,
        compiler_params=_cparams("arbitrary", "arbitrary"),
        name=name,
    )(*args)


def _ffn_up_body(seq_mode, x_ref, wa_ref, wb_ref, cwa_ref, cwb_ref, *rest):
    if seq_mode:
        hid_ref, upa_ref, upb_ref, wa_bf, wb_bf, ca_ref, cb_ref = rest
    else:
        a2_ref, a1_ref, b2_ref, b1_ref, hid_ref, upa_ref, upb_ref, wa_bf, wb_bf = rest
    tm = x_ref.shape[0]

    @pl.when(pl.program_id(1) == 0)
    def _():
        _cast_rows(wa_ref, wa_bf)
        _cast_rows(wb_ref, wb_bf)
        if seq_mode:
            ca_ref[...] = jnp.zeros_like(ca_ref)
            cb_ref[...] = jnp.zeros_like(cb_ref)

    x = x_ref[...]
    ua = _dot(x, wa_bf[...])
    ub = _dot(x, wb_bf[...])
    if seq_mode:
        pa, pb = ca_ref[...], cb_ref[...]
        taps_a = (_shift_rows(ua, pa, 2), _shift_rows(ua, pa, 1))
        taps_b = (_shift_rows(ub, pb, 2), _shift_rows(ub, pb, 1))
        ca_ref[...] = ua[tm - SUBLANES:]
        cb_ref[...] = ub[tm - SUBLANES:]
        upa_ref[...] = ua[tm - SUBLANES:]
        upb_ref[...] = ub[tm - SUBLANES:]
    else:
        taps_a = (a2_ref[...], a1_ref[...])
        taps_b = (b2_ref[...], b1_ref[...])
        upa_ref[...] = ua
        upb_ref[...] = ub
    c_a = cwa_ref[0:1, :] * taps_a[0] + cwa_ref[1:2, :] * taps_a[1] + cwa_ref[2:3, :] * ua
    c_b = cwb_ref[0:1, :] * taps_b[0] + cwb_ref[1:2, :] * taps_b[1] + cwb_ref[2:3, :] * ub
    hid_ref[...] = (_silu(c_a) * c_b).astype(hid_ref.dtype)


def _ffn_up(h, w_up, conv_w, state, *, tn=512, tm=512):
    m, k = h.shape
    seq_mode = state is None
    tm = min(tm, m)
    nb = D_FF // tn
    assert m % tm == 0 and D_FF % tn == 0
    keep = SUBLANES if seq_mode else tm
    in_specs = [
        pl.BlockSpec((tm, k), lambda j, i: (i, 0)),
        pl.BlockSpec((k, tn), lambda j, i: (0, j)),
        pl.BlockSpec((k, tn), lambda j, i: (0, j + nb)),
        pl.BlockSpec((FFN_CONV, tn), lambda j, i: (0, j)),
        pl.BlockSpec((FFN_CONV, tn), lambda j, i: (0, j + nb)),
    ]
    args = [h, w_up, w_up, conv_w, conv_w]
    scratch = [pltpu.VMEM((k, tn), BF16), pltpu.VMEM((k, tn), BF16)]
    if seq_mode:
        scratch += [pltpu.VMEM((SUBLANES, tn), F32), pltpu.VMEM((SUBLANES, tn), F32)]
        keep_rows = SUBLANES
    else:
        assert m == tm
        for off in (0, 2 * nb, nb, 3 * nb):
            in_specs.append(pl.BlockSpec((tm, tn), functools.partial(lambda j, i, o: (i, j + o), o=off)))
            args.append(state)
        keep_rows = m
    keep_spec = pl.BlockSpec((keep, tn), (lambda j, i: (0, j)) if seq_mode else (lambda j, i: (i, j)))
    return pl.pallas_call(
        functools.partial(_ffn_up_body, seq_mode),
        grid=(nb, m // tm),
        in_specs=in_specs,
        out_specs=[pl.BlockSpec((tm, tn), lambda j, i: (i, j)), keep_spec, keep_spec],
        out_shape=[
            jax.ShapeDtypeStruct((m, D_FF), BF16),
            jax.ShapeDtypeStruct((keep_rows, D_FF), F32),
            jax.ShapeDtypeStruct((keep_rows, D_FF), F32),
        ],
        scratch_shapes=scratch,
        compiler_params=_cparams("arbitrary", "arbitrary"),
        name="ffn_up_seq" if seq_mode else "ffn_up_step",
    )(*args)


def _half_lane_pair(x, head_in_high, lo_mask):
    if head_in_high:
        hi = jnp.where(lo_mask, 0.0, x)
        return pltpu.roll(hi, HEAD_DIM, 1), hi
    lo = jnp.where(lo_mask, x, 0.0)
    return lo, pltpu.roll(lo, HEAD_DIM, 1)


def _mixer0_seq_body(sink_ref, z_ref, kvp_ref, up_ref, pw_ref, ps_ref, o_ref):
    n = pl.program_id(0)
    first = n == 0
    w = WINDOW
    lo_mask = lax.broadcasted_iota(jnp.int32, (1, LANES), 1) < HEAD_DIM

    hist = jnp.where(first, 0.0, up_ref[...])
    pos = n * w + lax.broadcasted_iota(jnp.int32, (w, 1), 0)
    for g, win in enumerate(POOL_WINDOWS):
        sl = slice(g * LANES, (g + 1) * LANES)
        e = jnp.concatenate([hist[:, sl], z_ref[:, sl]], axis=0)
        s, sh = e, 1
        while sh < win:
            s = s + pltpu.roll(s, sh, 0)
            sh *= 2
        cnt = jnp.minimum(pos + 1, win).astype(F32)
        diff = s[POOL_HIST:] / cnt - e[POOL_HIST:]
        y = _dot(diff.astype(BF16), pw_ref[g].astype(BF16)) * ps_ref[:, sl]
        o_ref[:, sl] = y.astype(o_ref.dtype)

    row = lax.broadcasted_iota(jnp.int32, (w, 2 * w), 0)
    col = lax.broadcasted_iota(jnp.int32, (w, 2 * w), 1)
    valid = (col >= row) & (col <= row + w) & (jnp.logical_not(first) | (col >= w))
    tiles_per_kv = Q_PER_KV * HEAD_DIM // LANES
    for c in range(D_KV // LANES):
        k2 = jnp.concatenate([kvp_ref[:, c * LANES:(c + 1) * LANES],
                              z_ref[:, O_K + c * LANES:O_K + (c + 1) * LANES]], axis=0)
        v2 = jnp.concatenate([kvp_ref[:, D_KV + c * LANES:D_KV + (c + 1) * LANES],
                              z_ref[:, O_V + c * LANES:O_V + (c + 1) * LANES]], axis=0)
        for p in range(2):
            hk = 2 * c + p
            k_lo, k_hi = (t.astype(BF16) for t in _half_lane_pair(k2, p == 1, lo_mask))
            v_lo, v_hi = (t.astype(BF16) for t in _half_lane_pair(v2, p == 1, lo_mask))
            q0 = D_POOL + hk * Q_PER_KV * HEAD_DIM
            qst = jnp.concatenate(
                [z_ref[:, q0 + a * LANES:q0 + (a + 1) * LANES] for a in range(tiles_per_kv)], axis=0)
            qst = (qst * HEAD_DIM ** -0.5).astype(BF16)
            s_lo = _dot_nt(qst, k_lo)
            s_hi = _dot_nt(qst, k_hi)
            for a in range(tiles_per_kv):
                probs, inv = [], []
                for par, s_all in ((0, s_lo), (1, s_hi)):
                    sk = sink_ref[hk * Q_PER_KV + 2 * a + par]
                    s = jnp.where(valid, s_all[a * w:(a + 1) * w], NEG)
                    mx = jnp.maximum(jnp.max(s, axis=-1, keepdims=True), sk)
                    pr = jnp.exp(s - mx)
                    den = jnp.sum(pr, axis=-1, keepdims=True) + jnp.exp(sk - mx)
                    probs.append(pr.astype(BF16))
                    inv.append(1.0 / den)
                o = _dot(probs[0], v_lo) + _dot(probs[1], v_hi)
                o = o * jnp.where(lo_mask, inv[0], inv[1])
                o_ref[:, q0 + a * LANES:q0 + (a + 1) * LANES] = o.astype(o_ref.dtype)


def _mixer0_seq(z0, pool_w, pool_scale, sinks):
    t = z0.shape[0]
    w = WINDOW
    assert t % w == 0
    hist_blocks = w // POOL_HIST
    return pl.pallas_call(
        _mixer0_seq_body,
        grid=(t // w,),
        in_specs=[
            pl.BlockSpec(memory_space=pltpu.SMEM),
            pl.BlockSpec((w, D_IN0), lambda n: (n, 0)),
            pl.BlockSpec((w, 2 * D_KV), lambda n: (jnp.maximum(n - 1, 0), O_K // (2 * D_KV))),
            pl.BlockSpec((POOL_HIST, D_POOL), lambda n: (jnp.maximum(n * hist_blocks - 1, 0), 0)),
            pl.BlockSpec((len(POOL_WINDOWS), LANES, LANES), lambda n: (0, 0, 0)),
            pl.BlockSpec((1, D_POOL), lambda n: (0, 0)),
        ],
        out_specs=pl.BlockSpec((w, D_POOL + D_ATTN), lambda n: (n, 0)),
        out_shape=jax.ShapeDtypeStruct((t, D_POOL + D_ATTN), BF16),
        compiler_params=_cparams("parallel"),
        name="mixer0_seq",
    )(sinks, z0, z0, z0, pool_w, pool_scale.reshape(1, D_POOL))


def _pool_step_body(*refs):
    hist_refs = refs[:POOL_BUF]
    z_ref, pw_ref, ps_ref, o_ref = refs[POOL_BUF:]
    for g, win in enumerate(POOL_WINDOWS):
        sl = slice(g * LANES, (g + 1) * LANES)
        u = z_ref[:, sl]
        s = u
        for r in range(POOL_BUF - (win - 1), POOL_BUF):
            s = s + hist_refs[r][:, sl]
        diff = s / float(win) - u
        y = _dot(diff.astype(BF16), pw_ref[g].astype(BF16)) * ps_ref[:, sl]
        o_ref[:, sl] = y.astype(o_ref.dtype)


def _pool_step(z0, hist, pool_w, pool_scale):
    b = z0.shape[0]
    in_specs = [pl.BlockSpec((b, D_POOL), functools.partial(lambda i, r: (0, r), r=r)) for r in range(POOL_BUF)]
    in_specs += [
        pl.BlockSpec((b, D_POOL), lambda i: (0, 0)),
        pl.BlockSpec((len(POOL_WINDOWS), LANES, LANES), lambda i: (0, 0, 0)),
        pl.BlockSpec((1, D_POOL), lambda i: (0, 0)),
    ]
    return pl.pallas_call(
        _pool_step_body,
        grid=(1,),
        in_specs=in_specs,
        out_specs=pl.BlockSpec((b, D_POOL), lambda i: (0, 0)),
        out_shape=jax.ShapeDtypeStruct((b, D_POOL), BF16),
        compiler_params=_cparams("arbitrary"),
        name="pool_step",
    )(*([hist] * POOL_BUF), z0, pool_w, pool_scale.reshape(1, D_POOL))


def _attn_step_body(qm_ref, kc_ref, vc_ref, kn_ref, vn_ref, sink_ref, o_ref):
    bs = qm_ref.shape[0]
    sk = sink_ref[...]
    for b in range(bs):
        qm = qm_ref[b] * HEAD_DIM ** -0.5
        kn = kn_ref[b:b + 1, :]
        vn = vn_ref[b:b + 1, :]
        s = _dot_nt(qm.astype(BF16), kc_ref[b].astype(BF16))
        s_new = jnp.sum(qm * kn, axis=-1, keepdims=True)
        mx = jnp.maximum(jnp.maximum(jnp.max(s, axis=-1, keepdims=True), s_new), sk)
        pr = jnp.exp(s - mx)
        pn = jnp.exp(s_new - mx)
        den = jnp.sum(pr, axis=-1, keepdims=True) + pn + jnp.exp(sk - mx)
        o = _dot(pr.astype(BF16), vc_ref[b].astype(BF16)) + pn * vn
        o_ref[b] = o / den


def _attn_step(qm, kc, vc, z0, sinks, *, bs=8):
    b = qm.shape[0]
    assert b % bs == 0 and kc.shape[1] == WINDOW
    return pl.pallas_call(
        _attn_step_body,
        grid=(b // bs,),
        in_specs=[
            pl.BlockSpec((bs, N_Q_HEADS, D_KV), lambda i: (i, 0, 0)),
            pl.BlockSpec((bs, WINDOW, D_KV), lambda i: (i, 0, 0)),
            pl.BlockSpec((bs, WINDOW, D_KV), lambda i: (i, 0, 0)),
            pl.BlockSpec((bs, D_KV), lambda i: (i, O_K // D_KV)),
            pl.BlockSpec((bs, D_KV), lambda i: (i, O_V // D_KV)),
            pl.BlockSpec((N_Q_HEADS, 1), lambda i: (0, 0)),
        ],
        out_specs=pl.BlockSpec((bs, N_Q_HEADS, D_KV), lambda i: (i, 0, 0)),
        out_shape=jax.ShapeDtypeStruct((b, N_Q_HEADS, D_KV), F32),
        compiler_params=_cparams("parallel"),
        name="attn_step",
    )(qm, kc, vc, z0, z0, sinks.reshape(N_Q_HEADS, 1))


def _qkvz_body(seq_mode, x_ref, w_ref, cw_ref, *rest):
    if seq_mode:
        o_ref, raw_ref, w_bf, carry_ref = rest
    else:
        p3_ref, p2_ref, p1_ref, o_ref, raw_ref, w_bf = rest
    tm = x_ref.shape[0]
    j = pl.program_id(0)

    @pl.when(pl.program_id(1) == 0)
    def _():
        _cast_rows(w_ref, w_bf)
        if seq_mode:
            carry_ref[...] = jnp.zeros_like(carry_ref)

    u = _dot(x_ref[...], w_bf[...])
    if seq_mode:
        prev = carry_ref[...]
        taps = tuple(_shift_rows(u, prev, k) for k in (3, 2, 1))
        carry_ref[...] = u[tm - SUBLANES:]
        raw_ref[...] = u[tm - SUBLANES:]
    else:
        taps = (p3_ref[...], p2_ref[...], p1_ref[...])
        raw_ref[...] = u

    def conv_act():
        c = cw_ref[3:4, :] * u
        c = cw_ref[0:1, :] * taps[0] + cw_ref[1:2, :] * taps[1] + cw_ref[2:3, :] * taps[2] + c
        return _silu(c)

    def heads(a):
        return [a[:, h * LANES:(h + 1) * LANES] for h in range(GDN_HEADS)]

    def l2norm(a):
        return a * lax.rsqrt(jnp.sum(a * a, axis=-1, keepdims=True) + EPS)

    @pl.when(j == 0)
    def _():
        for h, a in enumerate(heads(conv_act())):
            o_ref[h] = l2norm(a) * GDN_DK ** -0.5

    @pl.when(j == 1)
    def _():
        for h, a in enumerate(heads(conv_act())):
            o_ref[h] = l2norm(a)

    @pl.when(j == 2)
    def _():
        for h, a in enumerate(heads(conv_act())):
            o_ref[h] = a

    @pl.when(j == 3)
    def _():
        for h, a in enumerate(heads(u)):
            o_ref[h] = a


def _qkvz(h, w_in, conv_w, state, *, tm=512):
    m, k = h.shape
    seq_mode = state is None
    tm = min(tm, m)
    tn = D_GDN_K
    assert D_GDN_K == D_GDN_V and GDN_DK == LANES and m % tm == 0
    nj = D_GDN_MAIN // tn
    conv_blk = lambda j, i: (0, jnp.minimum(j, 2))
    in_specs = [
        pl.BlockSpec((tm, k), lambda j, i: (i, 0)),
        pl.BlockSpec((k, tn), lambda j, i: (0, j)),
        pl.BlockSpec((GDN_CONV, tn), conv_blk),
    ]
    args = [h, w_in, conv_w]
    scratch = [pltpu.VMEM((k, tn), BF16)]
    if seq_mode:
        scratch.append(pltpu.VMEM((SUBLANES, tn), F32))
        keep = SUBLANES
        keep_spec = pl.BlockSpec((keep, tn), lambda j, i: (0, j))
    else:
        assert m == tm
        for r in range(GDN_CONV - 1):
            in_specs.append(pl.BlockSpec(
                (tm, tn), functools.partial(lambda j, i, r: (i, 3 * r + jnp.minimum(j, 2)), r=r)))
            args.append(state)
        keep = m
        keep_spec = pl.BlockSpec((keep, tn), lambda j, i: (i, j))
    return pl.pallas_call(
        functools.partial(_qkvz_body, seq_mode),
        grid=(nj, m // tm),
        in_specs=in_specs,
        out_specs=[pl.BlockSpec((None, GDN_HEADS, tm, LANES), lambda j, i: (j, 0, i, 0)), keep_spec],
        out_shape=[
            jax.ShapeDtypeStruct((nj, GDN_HEADS, m, LANES), F32),
            jax.ShapeDtypeStruct((keep, D_GDN_MAIN), F32),
        ],
        scratch_shapes=scratch,
        compiler_params=_cparams("arbitrary", "arbitrary"),
        name="qkvz_seq" if seq_mode else "qkvz_step",
    )(*args)


def _tail_body(seq_mode, x_ref, w_ref, cw_ref, alog_ref, dtb_ref, *rest):
    if seq_mode:
        ysc_ref, gate_ref, m_ref, w_bf, carry_ref = rest
    else:
        p2_ref, p1_ref, ysc_ref, gate_ref, m_ref, w_bf = rest
    tm = x_ref.shape[0]

    @pl.when(pl.program_id(0) == 0)
    def _():
        _cast_rows(w_ref, w_bf)
        if seq_mode:
            carry_ref[...] = jnp.zeros_like(carry_ref)

    z = _dot(x_ref[...], w_bf[...])
    sb = z[:, 0:D_SCONV]
    mm = z[:, D_SCONV:2 * D_SCONV] * z[:, 2 * D_SCONV:3 * D_SCONV]
    raw = z[:, 3 * D_SCONV:]
    if seq_mode:
        prev = carry_ref[...]
        taps = (_shift_rows(mm, prev, 2), _shift_rows(mm, prev, 1))
        carry_ref[...] = mm[tm - SUBLANES:]
        m_ref[...] = mm[tm - SUBLANES:]
    else:
        taps = (p2_ref[...], p1_ref[...])
        m_ref[...] = mm
    conv = cw_ref[0:1, :] * taps[0] + cw_ref[1:2, :] * taps[1] + cw_ref[2:3, :] * mm
    ysc_ref[...] = (sb * conv).astype(ysc_ref.dtype)

    lane = lax.broadcasted_iota(jnp.int32, raw.shape, 1)
    beta = jax.nn.sigmoid(raw)
    sp = raw + dtb_ref[...]
    softplus = jnp.maximum(sp, 0.0) + jnp.log1p(jnp.exp(-jnp.abs(sp)))
    g = -jnp.exp(alog_ref[...]) * softplus
    eg = pltpu.roll(jnp.exp(g), LANE_EG - LANE_G, 1)
    gate_ref[...] = jnp.where(lane < LANE_G, beta, jnp.where(lane < LANE_EG, g, eg))


def _tail(h, w_tail, conv_w, a_log_row, dt_bias_row, state, *, tm=512):
    m, k = h.shape
    seq_mode = state is None
    tm = min(tm, m)
    assert m % tm == 0
    in_specs = [
        pl.BlockSpec((tm, k), lambda i: (i, 0)),
        pl.BlockSpec((k, D_TAIL), lambda i: (0, 0)),
        pl.BlockSpec((SCONV_W, D_SCONV), lambda i: (0, 0)),
        pl.BlockSpec((1, LANES), lambda i: (0, 0)),
        pl.BlockSpec((1, LANES), lambda i: (0, 0)),
    ]
    args = [h, w_tail, conv_w, a_log_row, dt_bias_row]
    scratch = [pltpu.VMEM((k, D_TAIL), BF16)]
    if seq_mode:
        scratch.append(pltpu.VMEM((SUBLANES, D_SCONV), F32))
        keep = SUBLANES
        keep_spec = pl.BlockSpec((keep, D_SCONV), lambda i: (0, 0))
    else:
        assert m == tm
        for r in range(SCONV_W - 1):
            in_specs.append(pl.BlockSpec((tm, D_SCONV), functools.partial(lambda i, r: (i, r), r=r)))
            args.append(state)
        keep = m
        keep_spec = pl.BlockSpec((keep, D_SCONV), lambda i: (i, 0))
    return pl.pallas_call(
        functools.partial(_tail_body, seq_mode),
        grid=(m // tm,),
        in_specs=in_specs,
        out_specs=[
            pl.BlockSpec((tm, D_SCONV), lambda i: (i, 0)),
            pl.BlockSpec((tm, LANES), lambda i: (i, 0)),
            keep_spec,
        ],
        out_shape=[
            jax.ShapeDtypeStruct((m, D_SCONV), BF16),
            jax.ShapeDtypeStruct((m, LANES), F32),
            jax.ShapeDtypeStruct((keep, D_SCONV), F32),
        ],
        scratch_shapes=scratch,
        compiler_params=_cparams("arbitrary"),
        name="tail_seq" if seq_mode else "tail_step",
    )(*args)


def _gated_norm(o, zg, nw):
    y = o * lax.rsqrt(jnp.mean(o * o, axis=-1, keepdims=True) + EPS) * nw
    return y * _silu(zg)


def _gdn_seq_body(nc, qkvz_ref, gate_ref, nw_ref, y_ref, sfin_ref, s_ref):
    c = GDN_CHUNK
    n = pl.program_id(0)

    @pl.when(n == 0)
    def _():
        s_ref[...] = jnp.zeros_like(s_ref)

    r = lax.broadcasted_iota(jnp.int32, (c, c), 0)
    cc = lax.broadcasted_iota(jnp.int32, (c, c), 1)
    tri = r >= cc
    strict = r > cc
    ones = jnp.where(tri, 1.0, 0.0).astype(BF16)
    zpad = jnp.zeros((LANES - c, LANES), F32)
    nw = nw_ref[...]
    pairs = [(j, h) for j in range(nc) for h in range(GDN_HEADS)]

    gates, gcs, gcts = [], [], []
    for j in range(nc):
        gate = gate_ref[j * c:(j + 1) * c, :]
        g1 = gate.astype(BF16)
        r1 = gate - g1.astype(F32)
        g2 = r1.astype(BF16)
        g3 = (r1 - g2.astype(F32)).astype(BF16)
        gc = _dot(ones, g1) + _dot(ones, g2) + _dot(ones, g3)
        gates.append(gate)
        gcs.append(gc)
        gcts.append(jnp.concatenate([gc, zpad], axis=0).T)

    def rows(j):
        return slice(j * c, (j + 1) * c)

    gcol = [gcs[j][:, LANE_G + h:LANE_G + h + 1] for j, h in pairs]
    bcol = [gates[j][:, LANE_BETA + h:LANE_BETA + h + 1] for j, h in pairs]
    decay = [jnp.exp(jnp.where(tri, gcol[i] - gcts[j][LANE_G + h:LANE_G + h + 1, 0:c], NEG))
             for i, (j, h) in enumerate(pairs)]
    eg = [jnp.exp(g) for g in gcol]
    k = [qkvz_ref[1, h, rows(j), :] for j, h in pairs]
    kb = [k[i] * bcol[i] for i in range(len(pairs))]
    qd = [qkvz_ref[0, h, rows(j), :] for j, h in pairs]
    kq = [_dot_nt(jnp.concatenate([kb[i], qd[i]], axis=0).astype(BF16), k[i].astype(BF16))
          for i in range(len(pairs))]
    x = [jnp.where(strict, -(kq[i][:c] * decay[i]), 0.0) for i in range(len(pairs))]
    intra = [jnp.where(tri, kq[i][c:] * decay[i], 0.0).astype(BF16) for i in range(len(pairs))]
    x_b = [xi.astype(BF16) for xi in x]
    p = [_dot(xb, xb) for xb in x_b]
    t_off = x
    n_steps = c.bit_length() - 2
    for step in range(n_steps):
        p_b = [pi.astype(BF16) for pi in p]
        if step < n_steps - 1:
            pt = [_dot(jnp.concatenate([p_b[i], t_off[i].astype(BF16)], axis=0), p_b[i]) for i in range(len(pairs))]
            t_off = [t_off[i] + p[i] + pt[i][c:] for i in range(len(pairs))]
            p = [pti[:c] for pti in pt]
        else:
            t_off = [t_off[i] + p[i] + _dot(t_off[i].astype(BF16), p_b[i]) for i in range(len(pairs))]
    rhs = [jnp.concatenate([qkvz_ref[2, h, rows(j), :] * bcol[i], kb[i] * eg[i]], axis=1)
           for i, (j, h) in enumerate(pairs)]
    sol = [rhs[i] + _dot(t_off[i].astype(BF16), rhs[i].astype(BF16)) for i in range(len(pairs))]
    wq = [jnp.concatenate([sol[i][:, GDN_DV:], qd[i] * eg[i]], axis=0).astype(BF16) for i in range(len(pairs))]
    glast = [g[c - 1:c, :] for g in gcol]
    kdt = [jnp.concatenate([k[i] * jnp.exp(glast[i] - gcol[i]), zpad], axis=0).T[:, 0:c].astype(BF16)
           for i in range(len(pairs))]
    ikd = [jnp.concatenate([intra[i], kdt[i]], axis=0) for i in range(len(pairs))]
    g_tot = [jnp.exp(g) for g in glast]

    for j in range(nc):
        idx = [j * GDN_HEADS + h for h in range(GDN_HEADS)]
        s = [s_ref[h] for h in range(GDN_HEADS)]
        ws = [_dot(wq[i], s[h].astype(BF16)) for h, i in enumerate(idx)]
        v_new = [(sol[i][:, :GDN_DV] - ws[h][:c]).astype(BF16) for h, i in enumerate(idx)]
        upd = [_dot(ikd[i], v_new[h]) for h, i in enumerate(idx)]
        for h, i in enumerate(idx):
            s_ref[h] = s[h] * g_tot[i] + upd[h][c:]
            o = ws[h][c:] + upd[h][:c]
            zg = qkvz_ref[3, h, rows(j), :]
            y_ref[rows(j), h * GDN_DV:(h + 1) * GDN_DV] = _gated_norm(o, zg, nw).astype(y_ref.dtype)

    @pl.when(n == pl.num_programs(0) - 1)
    def _():
        sfin_ref[...] = s_ref[...]


def _gdn_seq(qkvz, gate, norm_w, *, nc=2):
    t = qkvz.shape[2]
    c = GDN_CHUNK
    assert t % (nc * c) == 0 and c & (c - 1) == 0
    return pl.pallas_call(
        functools.partial(_gdn_seq_body, nc),
        grid=(t // (nc * c),),
        in_specs=[
            pl.BlockSpec((4, GDN_HEADS, nc * c, LANES), lambda n: (0, 0, n, 0)),
            pl.BlockSpec((nc * c, LANES), lambda n: (n, 0)),
            pl.BlockSpec((1, GDN_DV), lambda n: (0, 0)),
        ],
        out_specs=[
            pl.BlockSpec((nc * c, D_GDN_V), lambda n: (n, 0)),
            pl.BlockSpec((GDN_HEADS, GDN_DK, GDN_DV), lambda n: (0, 0, 0)),
        ],
        out_shape=[
            jax.ShapeDtypeStruct((t, D_GDN_V), BF16),
            jax.ShapeDtypeStruct((GDN_HEADS, GDN_DK, GDN_DV), F32),
        ],
        scratch_shapes=[pltpu.VMEM((GDN_HEADS, GDN_DK, GDN_DV), F32)],
        compiler_params=_cparams("arbitrary"),
        name="gdn_seq",
    )(qkvz, gate, norm_w.reshape(1, GDN_DV))


def _gdn_step_body(qkvz_ref, gate_ref, nw_ref, s_ref, y_ref, so_ref, beta_ref, eg_ref, yacc_ref):
    h = pl.program_id(0)
    bs = s_ref.shape[0]
    gate = gate_ref[...]
    lane = lax.broadcasted_iota(jnp.int32, gate.shape, 1)

    def pick(l):
        col = jnp.sum(jnp.where(lane == l, gate, 0.0), axis=1, keepdims=True)
        return jnp.broadcast_to(col, gate.shape)

    beta_ref[...] = pick(LANE_BETA + h)
    eg_ref[...] = pick(LANE_EG + h)
    eye = (lax.broadcasted_iota(jnp.int32, (GDN_DK, GDN_DK), 0)
           == lax.broadcasted_iota(jnp.int32, (GDN_DK, GDN_DK), 1))
    nw = nw_ref[...]

    def column(row):
        return jnp.sum(jnp.where(eye, row, 0.0), axis=1, keepdims=True)

    def body(b, carry):
        one = pl.ds(b, 1)
        qcol = column(qkvz_ref[0, one, :])
        kcol = column(qkvz_ref[1, one, :])
        s = s_ref[b] * eg_ref[one, :]
        kv = jnp.sum(s * kcol, axis=0, keepdims=True)
        delta = (qkvz_ref[2, one, :] - kv) * beta_ref[one, :]
        s = s + kcol * delta
        so_ref[b] = s
        o = jnp.sum(s * qcol, axis=0, keepdims=True)
        yacc_ref[one, :] = _gated_norm(o, qkvz_ref[3, one, :], nw)
        return carry

    lax.fori_loop(0, bs, body, 0, unroll=min(bs, GDN_STEP_UNROLL))
    y_ref[...] = yacc_ref[...].astype(y_ref.dtype)


def _gdn_step(qkvz, gate, norm_w, state, *, bs=32):
    b = qkvz.shape[2]
    bs = min(bs, b)
    assert b % bs == 0
    row_scratch = pltpu.VMEM((bs, LANES), F32)
    return pl.pallas_call(
        _gdn_step_body,
        grid=(GDN_HEADS, b // bs),
        in_specs=[
            pl.BlockSpec((4, None, bs, LANES), lambda h, i: (0, h, i, 0)),
            pl.BlockSpec((bs, LANES), lambda h, i: (i, 0)),
            pl.BlockSpec((1, GDN_DV), lambda h, i: (0, 0)),
            pl.BlockSpec((bs, None, GDN_DK, GDN_DV), lambda h, i: (i, h, 0, 0)),
        ],
        out_specs=[
            pl.BlockSpec((bs, GDN_DV), lambda h, i: (i, h)),
            pl.BlockSpec((bs, None, GDN_DK, GDN_DV), lambda h, i: (i, h, 0, 0)),
        ],
        out_shape=[
            jax.ShapeDtypeStruct((b, D_GDN_V), BF16),
            jax.ShapeDtypeStruct(state.shape, F32),
        ],
        scratch_shapes=[row_scratch, row_scratch, row_scratch],
        compiler_params=_cparams("parallel", "parallel"),
        name="gdn_step",
    )(qkvz, gate, norm_w.reshape(1, GDN_DV), state)


def _gate_param_row(p):
    return jnp.zeros((1, LANES), F32).at[0, LANE_G:LANE_G + GDN_HEADS].set(p.astype(F32))


def _tail_weight(w_in):
    o1 = D_GDN_MAIN
    o2 = o1 + 2 * GDN_HEADS
    pad = jnp.zeros((w_in.shape[0], LANES - 2 * GDN_HEADS), w_in.dtype)
    return jnp.concatenate([w_in[:, o2:], w_in[:, o1:o2], pad], axis=1)


def _ffn(x, state, norm_w, w_up, w_conv, w_down):
    hid, keep_a, keep_b = _ffn_up(_rmsnorm(x, norm_w, BF16), w_up, w_conv, state)
    y = _matmul([hid], w_down, D_MODEL, tn=512, tm=256, residual=x, name="ffn_down")
    return y, jnp.concatenate([keep_a, keep_b], axis=1)


def _layer0_in(x, norm_w, w_in):
    return _matmul([_rmsnorm(x, norm_w, BF16)], w_in, D_IN0, tn=512, tm=512, name="in_proj0")


def _layer1_in(x, norm_w, w_in, w_tail, gdn_conv, sconv_w, a_log_row, dt_bias_row, conv_state, sconv_state):
    h = _rmsnorm(x, norm_w, BF16)
    qkvz, raw = _qkvz(h, w_in, gdn_conv, conv_state)
    ysc, gate, mrows = _tail(h, w_tail, sconv_w, a_log_row, dt_bias_row, sconv_state)
    return qkvz, raw, ysc, gate, mrows


def kernel(x_prompt, x_sample, state_l0_pool, cache_l0_k, cache_l0_v, state_l0_ffn_conv, state_l1_gdn_conv, state_l1_gdn_S, state_l1_sconv, state_l1_ffn_conv, l0_norm_mix, l0_w_in, l0_pool_w, l0_pool_scale, l0_sinks, l0_w_out, l0_norm_ffn, l0_ffn_w_up, l0_ffn_conv, l0_ffn_w_down, l1_norm_mix, l1_w_in, l1_gdn_conv, l1_gdn_A_log, l1_gdn_dt_bias, l1_gdn_norm, l1_sconv_w, l1_w_out, l1_norm_ffn, l1_ffn_w_up, l1_ffn_conv, l1_ffn_w_down, final_norm):
    bp, t, d = x_prompt.shape
    nb, ts = x_sample.shape[:2]
    wb = cache_l0_k.shape[1]
    assert bp == 1 and ts == 1 and d == D_MODEL and wb == WINDOW and t >= WINDOW
    w_tail = _tail_weight(l1_w_in)
    a_log_row = _gate_param_row(l1_gdn_A_log)
    dt_bias_row = _gate_param_row(l1_gdn_dt_bias)
    last = SUBLANES

    xp = x_prompt[0]
    z0 = _layer0_in(xp, l0_norm_mix, l0_w_in)
    mix = _mixer0_seq(z0, l0_pool_w, l0_pool_scale, l0_sinks)
    x1 = _matmul([mix], l0_w_out, D_MODEL, tn=512, tm=512, residual=xp, name="out_proj0")
    x2, ffn0_rows = _ffn(x1, None, l0_norm_ffn, l0_ffn_w_up, l0_ffn_conv, l0_ffn_w_down)
    qkvz, raw, ysc, gate, mrows = _layer1_in(
        x2, l1_norm_mix, l1_w_in, w_tail, l1_gdn_conv, l1_sconv_w, a_log_row, dt_bias_row, None, None)
    ygdn, s_fin = _gdn_seq(qkvz, gate, l1_gdn_norm)
    x3 = _matmul([ygdn, ysc], l1_w_out, D_MODEL, tn=512, tm=512, residual=x2, name="out_proj1")
    x4, ffn1_rows = _ffn(x3, None, l1_norm_ffn, l1_ffn_w_up, l1_ffn_conv, l1_ffn_w_down)
    y_prompt = _rmsnorm(x4, final_norm, F32)[None]
    p_pool = z0[t - POOL_BUF:, :D_POOL][None]
    p_k = z0[t - wb:, O_K:O_V].reshape(1, wb, N_KV_HEADS, HEAD_DIM)
    p_v = z0[t - wb:, O_V:].reshape(1, wb, N_KV_HEADS, HEAD_DIM)
    p_ffn0 = ffn0_rows[last - (FFN_CONV - 1):][None]
    p_gconv = raw[last - (GDN_CONV - 1):, :D_GDN_CONV][None]
    p_s = s_fin[None]
    p_sconv = mrows[last - (SCONV_W - 1):][None]
    p_ffn1 = ffn1_rows[last - (FFN_CONV - 1):][None]

    xs = x_sample[:, 0]
    z0s = _layer0_in(xs, l0_norm_mix, l0_w_in)
    y_pool = _pool_step(z0s, state_l0_pool.reshape(nb, POOL_BUF * D_POOL), l0_pool_w, l0_pool_scale)
    kv_eye = jnp.eye(N_KV_HEADS, dtype=F32)
    q4 = z0s[:, D_POOL:O_K].reshape(nb, N_KV_HEADS, Q_PER_KV, HEAD_DIM)
    qm = jnp.einsum("bhgd,hk->bhgkd", q4, kv_eye).reshape(nb, N_Q_HEADS, D_KV)
    att = _attn_step(qm, cache_l0_k.reshape(nb, wb, D_KV), cache_l0_v.reshape(nb, wb, D_KV), z0s, l0_sinks)
    att = jnp.einsum("bhgkd,hk->bhgd", att.reshape(nb, N_KV_HEADS, Q_PER_KV, N_KV_HEADS, HEAD_DIM), kv_eye)
    mix_s = jnp.concatenate([y_pool, att.reshape(nb, D_ATTN).astype(BF16)], axis=1)
    x1s = _matmul([mix_s], l0_w_out, D_MODEL, tn=512, tm=512, residual=xs, name="out_proj0")
    x2s, ffn0_new = _ffn(x1s, state_l0_ffn_conv.reshape(nb, (FFN_CONV - 1) * 2 * D_FF),
                         l0_norm_ffn, l0_ffn_w_up, l0_ffn_conv, l0_ffn_w_down)
    qkvz_s, raw_s, ysc_s, gate_s, m_s = _layer1_in(
        x2s, l1_norm_mix, l1_w_in, w_tail, l1_gdn_conv, l1_sconv_w, a_log_row, dt_bias_row,
        state_l1_gdn_conv.reshape(nb, (GDN_CONV - 1) * D_GDN_CONV),
        state_l1_sconv.reshape(nb, (SCONV_W - 1) * D_SCONV))
    ygdn_s, s_new = _gdn_step(qkvz_s, gate_s, l1_gdn_norm, state_l1_gdn_S)
    x3s = _matmul([ygdn_s, ysc_s], l1_w_out, D_MODEL, tn=512, tm=512, residual=x2s, name="out_proj1")
    x4s, ffn1_new = _ffn(x3s, state_l1_ffn_conv.reshape(nb, (FFN_CONV - 1) * 2 * D_FF),
                         l1_norm_ffn, l1_ffn_w_up, l1_ffn_conv, l1_ffn_w_down)
    y_sample = _rmsnorm(x4s, final_norm, F32)[:, None]

    def push(state, new_row):
        return jnp.concatenate([state[:, 1:], new_row[:, None].astype(state.dtype)], axis=1)

    s_pool = push(state_l0_pool, z0s[:, :D_POOL])
    s_k = push(cache_l0_k, z0s[:, O_K:O_V].reshape(nb, N_KV_HEADS, HEAD_DIM))
    s_v = push(cache_l0_v, z0s[:, O_V:].reshape(nb, N_KV_HEADS, HEAD_DIM))
    s_ffn0 = push(state_l0_ffn_conv, ffn0_new)
    s_gconv = push(state_l1_gdn_conv, raw_s[:, :D_GDN_CONV])
    s_sconv = push(state_l1_sconv, m_s)
    s_ffn1 = push(state_l1_ffn_conv, ffn1_new)
    return (y_prompt, y_sample, p_pool, s_pool, p_k, s_k, p_v, s_v, p_ffn0, s_ffn0,
            p_gconv, s_gconv, p_s, s_new, p_sconv, s_sconv, p_ffn1, s_ffn1)
```

```python
import functools

import jax
import jax.numpy as jnp
from jax import lax
from jax.experimental import pallas as pl
from jax.experimental.pallas import tpu as pltpu

F32 = jnp.float32
BF16 = jnp.bfloat16
EPS = 1e-6
NEG = -1e30

LANES = 128
SUBLANES = 8
MXU_COLS = 256
VMEM_LIMIT_BYTES = 60 * 1024 * 1024

D_MODEL = 2048
D_POOL = 512
POOL_WINDOWS = (2, 4, 8, 16)
POOL_BUF = max(POOL_WINDOWS) - 1
POOL_HIST = 16
HEAD_DIM = 64
N_Q_HEADS = 24
N_KV_HEADS = 4
Q_PER_KV = N_Q_HEADS // N_KV_HEADS
WINDOW = 128
D_ATTN = N_Q_HEADS * HEAD_DIM
D_KV = N_KV_HEADS * HEAD_DIM
D_IN0 = D_POOL + D_ATTN + 2 * D_KV
O_K = D_POOL + D_ATTN
O_V = O_K + D_KV
GDN_HEADS = 12
GDN_DK = 128
GDN_DV = 128
D_GDN_K = GDN_HEADS * GDN_DK
D_GDN_V = GDN_HEADS * GDN_DV
D_GDN_CONV = 2 * D_GDN_K + D_GDN_V
D_GDN_MAIN = D_GDN_CONV + D_GDN_V
GDN_CONV = 4
GDN_CHUNK = 64
GDN_STEP_UNROLL = 8
D_SCONV = 512
SCONV_W = 3
D_TAIL = 3 * D_SCONV + LANES
D_FF = 5632
FFN_CONV = 3
LANE_BETA = 0
LANE_G = GDN_HEADS
LANE_EG = 2 * GDN_HEADS


def _cparams(*sem):
    return pltpu.CompilerParams(dimension_semantics=sem, vmem_limit_bytes=VMEM_LIMIT_BYTES)


def _dot(a, b):
    return jnp.dot(a, b, preferred_element_type=F32)


def _dot_nt(a, b):
    return lax.dot_general(a, b, (((1,), (1,)), ((), ())), preferred_element_type=F32)


def _dot_tn(a, b):
    return lax.dot_general(a, b, (((0,), (0,)), ((), ())), preferred_element_type=F32)


def _silu(x):
    return x * jax.nn.sigmoid(x)


def _cast_rows(w_ref, wbf_ref, chunk=256):
    k = w_ref.shape[0]
    chunk = min(chunk, k)
    assert k % chunk == 0

    def body(c, carry):
        r = pl.multiple_of(c * chunk, chunk)
        wbf_ref[pl.ds(r, chunk), :] = w_ref[pl.ds(r, chunk), :].astype(BF16)
        return carry

    lax.fori_loop(0, k // chunk, body, 0)


def _cast_transposed(wt_ref, wbf_ref):
    n = wt_ref.shape[0]
    step = MXU_COLS if n % MXU_COLS == 0 else LANES
    for c in range(0, n, step):
        wbf_ref[:, c:c + step] = wt_ref[c:c + step, :].T.astype(BF16)


def _shift_rows(u, prev8, k):
    rolled = pltpu.roll(u, k, 0)
    head = pltpu.roll(jnp.concatenate([prev8, u[0:SUBLANES]], axis=0), k, 0)[SUBLANES:2 * SUBLANES]
    return jnp.concatenate([head, rolled[SUBLANES:]], axis=0)


def _rmsnorm_body(x_ref, w_ref, o_ref):
    x = x_ref[...]
    ms = jnp.mean(x * x, axis=-1, keepdims=True)
    o_ref[...] = (x * lax.rsqrt(ms + EPS) * w_ref[...]).astype(o_ref.dtype)


def _rmsnorm(x, w, out_dtype):
    m, d = x.shape
    tm = min(m, 512)
    assert m % tm == 0
    return pl.pallas_call(
        _rmsnorm_body,
        grid=(m // tm,),
        in_specs=[pl.BlockSpec((tm, d), lambda i: (i, 0)), pl.BlockSpec((1, d), lambda i: (0, 0))],
        out_specs=pl.BlockSpec((tm, d), lambda i: (i, 0)),
        out_shape=jax.ShapeDtypeStruct((m, d), out_dtype),
        compiler_params=_cparams("parallel"),
        name="rmsnorm",
    )(x, w.reshape(1, d))


def _mm_body(n_pieces, n_stage, mode, *refs):
    x_refs = refs[:n_pieces]
    w_ref = refs[n_pieces]
    pos = n_pieces + 1
    if mode == "plain":
        (o_ref, wbf_ref) = refs[pos:]
    elif mode == "residual":
        (res_ref, nw_ref, o_ref, h_ref, wbf_ref) = refs[pos:]
    else:
        (res_ref, nw_ref, h_ref, wbf_ref) = refs[pos:]
    s = pl.program_id(0)
    ck = w_ref.shape[0]

    @pl.when(s < n_stage)
    def _():
        _cast_rows(w_ref, wbf_ref.at[pl.ds(pl.multiple_of(s * ck, ck), ck), :])

    @pl.when(s >= n_stage)
    def _():
        acc, row = None, 0
        for x_ref in x_refs:
            kp = x_ref.shape[1]
            d = _dot(x_ref[...], wbf_ref[row:row + kp, :])
            acc = d if acc is None else acc + d
            row += kp
        if mode == "plain":
            o_ref[...] = acc
            return
        xn = acc + res_ref[...]
        hn = xn * lax.rsqrt(jnp.mean(xn * xn, axis=-1, keepdims=True) + EPS) * nw_ref[...]
        if mode == "residual":
            o_ref[...] = xn
        h_ref[...] = hn.astype(h_ref.dtype)


def _matmul(xs, w, *, tm, residual=None, norm_w=None, final=False, chunk=256, name="matmul"):
    m = xs[0].shape[0]
    k, n = w.shape
    tm = min(tm, m)
    assert m % tm == 0 and k % chunk == 0 and sum(x.shape[1] for x in xs) == k
    n_stage, nm = k // chunk, m // tm
    mode = "plain" if residual is None else ("final" if final else "residual")

    def row_tile(s):
        return jnp.clip(s - n_stage, 0, nm - 1)

    in_specs = [pl.BlockSpec((tm, x.shape[1]), lambda s: (row_tile(s), 0)) for x in xs]
    in_specs.append(pl.BlockSpec((chunk, n), lambda s: (jnp.minimum(s, n_stage - 1), 0)))
    args = list(xs) + [w]
    tile_spec = pl.BlockSpec((tm, n), lambda s: (row_tile(s), 0))
    if mode == "plain":
        out_specs, out_shape = tile_spec, jax.ShapeDtypeStruct((m, n), F32)
    else:
        in_specs += [tile_spec, pl.BlockSpec((1, n), lambda s: (0, 0))]
        args += [residual, norm_w.reshape(1, n)]
        if mode == "residual":
            out_specs = [tile_spec, tile_spec]
            out_shape = [jax.ShapeDtypeStruct((m, n), F32), jax.ShapeDtypeStruct((m, n), BF16)]
        else:
            out_specs, out_shape = tile_spec, jax.ShapeDtypeStruct((m, n), F32)
    return pl.pallas_call(
        functools.partial(_mm_body, len(xs), n_stage, mode),
        grid=(n_stage + nm,),
        in_specs=in_specs,
        out_specs=out_specs,
        out_shape=out_shape,
        scratch_shapes=[pltpu.VMEM((k, n), BF16)],
        compiler_params=_cparams("arbitrary"),
        name=name,
    )(*args)


def _ffn_up_body(seq_mode, x_ref, wa_ref, wb_ref, cwa_ref, cwb_ref, *rest):
    if seq_mode:
        hid_ref, upa_ref, upb_ref, wa_bf, wb_bf, ca_ref, cb_ref = rest
    else:
        a2_ref, a1_ref, b2_ref, b1_ref, hid_ref, upa_ref, upb_ref, wa_bf, wb_bf = rest
    tm = x_ref.shape[0]

    @pl.when(pl.program_id(1) == 0)
    def _():
        _cast_rows(wa_ref, wa_bf)
        _cast_rows(wb_ref, wb_bf)
        if seq_mode:
            ca_ref[...] = jnp.zeros_like(ca_ref)
            cb_ref[...] = jnp.zeros_like(cb_ref)

    x = x_ref[...]
    ua = _dot(x, wa_bf[...])
    ub = _dot(x, wb_bf[...])
    if seq_mode:
        pa, pb = ca_ref[...], cb_ref[...]
        taps_a = (_shift_rows(ua, pa, 2), _shift_rows(ua, pa, 1))
        taps_b = (_shift_rows(ub, pb, 2), _shift_rows(ub, pb, 1))
        ca_ref[...] = ua[tm - SUBLANES:]
        cb_ref[...] = ub[tm - SUBLANES:]
        upa_ref[...] = ua[tm - SUBLANES:]
        upb_ref[...] = ub[tm - SUBLANES:]
    else:
        taps_a = (a2_ref[...], a1_ref[...])
        taps_b = (b2_ref[...], b1_ref[...])
        upa_ref[...] = ua
        upb_ref[...] = ub
    c_a = cwa_ref[0:1, :] * taps_a[0] + cwa_ref[1:2, :] * taps_a[1] + cwa_ref[2:3, :] * ua
    c_b = cwb_ref[0:1, :] * taps_b[0] + cwb_ref[1:2, :] * taps_b[1] + cwb_ref[2:3, :] * ub
    hid_ref[...] = (_silu(c_a) * c_b).astype(hid_ref.dtype)


def _ffn_up(h, w_up, conv_w, state, *, tn=512, tm=512):
    m, k = h.shape
    seq_mode = state is None
    tm = min(tm, m)
    nb = D_FF // tn
    nm = m // tm
    assert m % tm == 0 and D_FF % tn == 0 and tn % MXU_COLS == 0
    w_specs = [
        pl.BlockSpec((k, tn), lambda j, i: (0, j)),
        pl.BlockSpec((k, tn), lambda j, i: (0, j + nb)),
        pl.BlockSpec((FFN_CONV, tn), lambda j, i: (0, j)),
        pl.BlockSpec((FFN_CONV, tn), lambda j, i: (0, j + nb)),
    ]
    args = [h, w_up, w_up, conv_w, conv_w]
    scratch = [pltpu.VMEM((k, tn), BF16), pltpu.VMEM((k, tn), BF16)]
    if seq_mode:
        body = functools.partial(_ffn_up_body, True)
        grid = (nb, nm)
        in_specs = [pl.BlockSpec((tm, k), lambda j, i: (i, 0))] + w_specs
        scratch += [pltpu.VMEM((SUBLANES, tn), F32), pltpu.VMEM((SUBLANES, tn), F32)]
        keep_rows = SUBLANES
        hid_spec = pl.BlockSpec((tm, tn), lambda j, i: (i, j))
        keep_spec = pl.BlockSpec((SUBLANES, tn), lambda j, i: (0, j))
    else:
        assert m == tm
        body = functools.partial(_ffn_up_body, False)
        grid = (nb, 1)
        in_specs = [pl.BlockSpec((tm, k), lambda j, i: (i, 0))] + w_specs
        for off in (0, 2 * nb, nb, 3 * nb):
            in_specs.append(pl.BlockSpec((tm, tn), functools.partial(lambda j, i, o: (i, j + o), o=off)))
            args.append(state)
        keep_rows = m
        hid_spec = pl.BlockSpec((tm, tn), lambda j, i: (i, j))
        keep_spec = pl.BlockSpec((tm, tn), lambda j, i: (i, j))
    return pl.pallas_call(
        body,
        grid=grid,
        in_specs=in_specs,
        out_specs=[hid_spec, keep_spec, keep_spec],
        out_shape=[
            jax.ShapeDtypeStruct((m, D_FF), BF16),
            jax.ShapeDtypeStruct((keep_rows, D_FF), F32),
            jax.ShapeDtypeStruct((keep_rows, D_FF), F32),
        ],
        scratch_shapes=scratch,
        compiler_params=_cparams("arbitrary", "arbitrary"),
        name="ffn_up_seq" if seq_mode else "ffn_up_step",
    )(*args)


def _half_lane_pair(x, head_in_high, lo_mask):
    if head_in_high:
        hi = jnp.where(lo_mask, 0.0, x)
        return pltpu.roll(hi, HEAD_DIM, 1), hi
    lo = jnp.where(lo_mask, x, 0.0)
    return lo, pltpu.roll(lo, HEAD_DIM, 1)


def _mixer0_seq_body(sink_ref, z_ref, kvp_ref, up_ref, pw_ref, ps_ref, o_ref):
    n = pl.program_id(0)
    first = n == 0
    w = WINDOW
    lo_mask = lax.broadcasted_iota(jnp.int32, (1, LANES), 1) < HEAD_DIM

    hist = jnp.where(first, 0.0, up_ref[...])
    pos = n * w + lax.broadcasted_iota(jnp.int32, (w, 1), 0)
    for g, win in enumerate(POOL_WINDOWS):
        sl = slice(g * LANES, (g + 1) * LANES)
        e = jnp.concatenate([hist[:, sl], z_ref[:, sl]], axis=0)
        s, sh = e, 1
        while sh < win:
            s = s + pltpu.roll(s, sh, 0)
            sh *= 2
        cnt = jnp.minimum(pos + 1, win).astype(F32)
        diff = s[POOL_HIST:] / cnt - e[POOL_HIST:]
        y = _dot(diff.astype(BF16), pw_ref[g].astype(BF16)) * ps_ref[:, sl]
        o_ref[:, sl] = y.astype(o_ref.dtype)

    row = lax.broadcasted_iota(jnp.int32, (w, 2 * w), 0)
    col = lax.broadcasted_iota(jnp.int32, (w, 2 * w), 1)
    valid = (col >= row) & (col <= row + w) & (jnp.logical_not(first) | (col >= w))
    tiles_per_kv = Q_PER_KV * HEAD_DIM // LANES
    for c in range(D_KV // LANES):
        k2 = jnp.concatenate([kvp_ref[:, c * LANES:(c + 1) * LANES],
                              z_ref[:, O_K + c * LANES:O_K + (c + 1) * LANES]], axis=0)
        v2 = jnp.concatenate([kvp_ref[:, D_KV + c * LANES:D_KV + (c + 1) * LANES],
                              z_ref[:, O_V + c * LANES:O_V + (c + 1) * LANES]], axis=0)
        for p in range(2):
            hk = 2 * c + p
            k_lo, k_hi = (t.astype(BF16) for t in _half_lane_pair(k2, p == 1, lo_mask))
            v_lo, v_hi = (t.astype(BF16) for t in _half_lane_pair(v2, p == 1, lo_mask))
            q0 = D_POOL + hk * Q_PER_KV * HEAD_DIM
            qst = jnp.concatenate(
                [z_ref[:, q0 + a * LANES:q0 + (a + 1) * LANES] for a in range(tiles_per_kv)], axis=0)
            qst = (qst * HEAD_DIM ** -0.5).astype(BF16)
            s_lo = _dot_nt(qst, k_lo)
            s_hi = _dot_nt(qst, k_hi)
            for a in range(tiles_per_kv):
                probs, inv = [], []
                for par, s_all in ((0, s_lo), (1, s_hi)):
                    sk = sink_ref[hk * Q_PER_KV + 2 * a + par]
                    s = jnp.where(valid, s_all[a * w:(a + 1) * w], NEG)
                    mx = jnp.maximum(jnp.max(s, axis=-1, keepdims=True), sk)
                    pr = jnp.exp(s - mx)
                    den = jnp.sum(pr, axis=-1, keepdims=True) + jnp.exp(sk - mx)
                    probs.append(pr.astype(BF16))
                    inv.append(1.0 / den)
                o = _dot(probs[0], v_lo) + _dot(probs[1], v_hi)
                o = o * jnp.where(lo_mask, inv[0], inv[1])
                o_ref[:, q0 + a * LANES:q0 + (a + 1) * LANES] = o.astype(o_ref.dtype)


def _mixer0_seq(z0, pool_w, pool_scale, sinks):
    t = z0.shape[0]
    w = WINDOW
    assert t % w == 0
    hist_blocks = w // POOL_HIST
    return pl.pallas_call(
        _mixer0_seq_body,
        grid=(t // w,),
        in_specs=[
            pl.BlockSpec(memory_space=pltpu.SMEM),
            pl.BlockSpec((w, D_IN0), lambda n: (n, 0)),
            pl.BlockSpec((w, 2 * D_KV), lambda n: (jnp.maximum(n - 1, 0), O_K // (2 * D_KV))),
            pl.BlockSpec((POOL_HIST, D_POOL), lambda n: (jnp.maximum(n * hist_blocks - 1, 0), 0)),
            pl.BlockSpec((len(POOL_WINDOWS), LANES, LANES), lambda n: (0, 0, 0)),
            pl.BlockSpec((1, D_POOL), lambda n: (0, 0)),
        ],
        out_specs=pl.BlockSpec((w, D_POOL + D_ATTN), lambda n: (n, 0)),
        out_shape=jax.ShapeDtypeStruct((t, D_POOL + D_ATTN), BF16),
        compiler_params=_cparams("parallel"),
        name="mixer0_seq",
    )(sinks, z0, z0, z0, pool_w, pool_scale.reshape(1, D_POOL))


def _pool_step_body(*refs):
    hist_refs = refs[:POOL_BUF]
    z_ref, pw_ref, ps_ref, o_ref = refs[POOL_BUF:]
    for g, win in enumerate(POOL_WINDOWS):
        sl = slice(g * LANES, (g + 1) * LANES)
        u = z_ref[:, sl]
        s = u
        for r in range(POOL_BUF - (win - 1), POOL_BUF):
            s = s + hist_refs[r][:, sl]
        diff = s / float(win) - u
        y = _dot(diff.astype(BF16), pw_ref[g].astype(BF16)) * ps_ref[:, sl]
        o_ref[:, sl] = y.astype(o_ref.dtype)


def _pool_step(z0, hist, pool_w, pool_scale):
    b = z0.shape[0]
    in_specs = [pl.BlockSpec((b, D_POOL), functools.partial(lambda i, r: (0, r), r=r)) for r in range(POOL_BUF)]
    in_specs += [
        pl.BlockSpec((b, D_POOL), lambda i: (0, 0)),
        pl.BlockSpec((len(POOL_WINDOWS), LANES, LANES), lambda i: (0, 0, 0)),
        pl.BlockSpec((1, D_POOL), lambda i: (0, 0)),
    ]
    return pl.pallas_call(
        _pool_step_body,
        grid=(1,),
        in_specs=in_specs,
        out_specs=pl.BlockSpec((b, D_POOL), lambda i: (0, 0)),
        out_shape=jax.ShapeDtypeStruct((b, D_POOL), BF16),
        compiler_params=_cparams("arbitrary"),
        name="pool_step",
    )(*([hist] * POOL_BUF), z0, pool_w, pool_scale.reshape(1, D_POOL))


def _attn_step_body(qm_ref, kc_ref, vc_ref, kn_ref, vn_ref, sink_ref, o_ref):
    bs = qm_ref.shape[0]
    sk = sink_ref[...]
    for b in range(bs):
        qm = qm_ref[b] * HEAD_DIM ** -0.5
        kn = kn_ref[b:b + 1, :]
        vn = vn_ref[b:b + 1, :]
        s = _dot_nt(qm.astype(BF16), kc_ref[b].astype(BF16))
        s_new = jnp.sum(qm * kn, axis=-1, keepdims=True)
        mx = jnp.maximum(jnp.maximum(jnp.max(s, axis=-1, keepdims=True), s_new), sk)
        pr = jnp.exp(s - mx)
        pn = jnp.exp(s_new - mx)
        den = jnp.sum(pr, axis=-1, keepdims=True) + pn + jnp.exp(sk - mx)
        o = _dot(pr.astype(BF16), vc_ref[b].astype(BF16)) + pn * vn
        o_ref[b] = o / den


def _attn_step(qm, kc, vc, z0, sinks, *, bs=8):
    b = qm.shape[0]
    assert b % bs == 0 and kc.shape[1] == WINDOW
    return pl.pallas_call(
        _attn_step_body,
        grid=(b // bs,),
        in_specs=[
            pl.BlockSpec((bs, N_Q_HEADS, D_KV), lambda i: (i, 0, 0)),
            pl.BlockSpec((bs, WINDOW, D_KV), lambda i: (i, 0, 0)),
            pl.BlockSpec((bs, WINDOW, D_KV), lambda i: (i, 0, 0)),
            pl.BlockSpec((bs, D_KV), lambda i: (i, O_K // D_KV)),
            pl.BlockSpec((bs, D_KV), lambda i: (i, O_V // D_KV)),
            pl.BlockSpec((N_Q_HEADS, 1), lambda i: (0, 0)),
        ],
        out_specs=pl.BlockSpec((bs, N_Q_HEADS, D_KV), lambda i: (i, 0, 0)),
        out_shape=jax.ShapeDtypeStruct((b, N_Q_HEADS, D_KV), F32),
        compiler_params=_cparams("parallel"),
        name="attn_step",
    )(qm, kc, vc, z0, z0, sinks.reshape(N_Q_HEADS, 1))


def _qkvz_body(seq_mode, x_ref, w_ref, cw_ref, *rest):
    if seq_mode:
        o_ref, raw_ref, w_bf, carry_ref = rest
    else:
        p3_ref, p2_ref, p1_ref, o_ref, raw_ref, w_bf = rest
    tm = x_ref.shape[0]
    j = pl.program_id(0)

    @pl.when(pl.program_id(1) == 0)
    def _():
        _cast_transposed(w_ref, w_bf)
        if seq_mode:
            carry_ref[...] = jnp.zeros_like(carry_ref)

    u = _dot(x_ref[...], w_bf[...])
    if seq_mode:
        prev = carry_ref[...]
        taps = tuple(_shift_rows(u, prev, k) for k in (3, 2, 1))
        carry_ref[...] = u[tm - SUBLANES:]
        raw_ref[...] = u[tm - SUBLANES:]
    else:
        taps = (p3_ref[...], p2_ref[...], p1_ref[...])
        raw_ref[...] = u

    def conv_act():
        c = cw_ref[3:4, :] * u
        c = cw_ref[0:1, :] * taps[0] + cw_ref[1:2, :] * taps[1] + cw_ref[2:3, :] * taps[2] + c
        return _silu(c)

    def heads(a):
        return [a[:, h * LANES:(h + 1) * LANES] for h in range(GDN_HEADS)]

    def l2norm(a):
        return a * lax.rsqrt(jnp.sum(a * a, axis=-1, keepdims=True) + EPS)

    @pl.when(j == 0)
    def _():
        for h, a in enumerate(heads(conv_act())):
            o_ref[h] = l2norm(a) * GDN_DK ** -0.5

    @pl.when(j == 1)
    def _():
        for h, a in enumerate(heads(conv_act())):
            o_ref[h] = l2norm(a)

    @pl.when(j == 2)
    def _():
        for h, a in enumerate(heads(conv_act())):
            o_ref[h] = a

    @pl.when(j == 3)
    def _():
        for h, a in enumerate(heads(u)):
            o_ref[h] = a


def _qkvz(h, w_in, conv_w, state, *, tm=512):
    m, k = h.shape
    seq_mode = state is None
    tm = min(tm, m)
    tn = D_GDN_K
    assert D_GDN_K == D_GDN_V and GDN_DK == LANES and m % tm == 0
    nj = D_GDN_MAIN // tn
    conv_blk = lambda j, i: (0, jnp.minimum(j, 2))
    in_specs = [
        pl.BlockSpec((tm, k), lambda j, i: (i, 0)),
        pl.BlockSpec((tn, k), lambda j, i: (j, 0)),
        pl.BlockSpec((GDN_CONV, tn), conv_blk),
    ]
    args = [h, w_in, conv_w]
    scratch = [pltpu.VMEM((k, tn), BF16)]
    if seq_mode:
        scratch.append(pltpu.VMEM((SUBLANES, tn), F32))
        keep = SUBLANES
        keep_spec = pl.BlockSpec((keep, tn), lambda j, i: (0, j))
    else:
        assert m == tm
        for r in range(GDN_CONV - 1):
            in_specs.append(pl.BlockSpec(
                (tm, tn), functools.partial(lambda j, i, r: (i, 3 * r + jnp.minimum(j, 2)), r=r)))
            args.append(state)
        keep = m
        keep_spec = pl.BlockSpec((keep, tn), lambda j, i: (i, j))
    return pl.pallas_call(
        functools.partial(_qkvz_body, seq_mode),
        grid=(nj, m // tm),
        in_specs=in_specs,
        out_specs=[pl.BlockSpec((None, GDN_HEADS, tm, LANES), lambda j, i: (j, 0, i, 0)), keep_spec],
        out_shape=[
            jax.ShapeDtypeStruct((nj, GDN_HEADS, m, LANES), F32),
            jax.ShapeDtypeStruct((keep, D_GDN_MAIN), F32),
        ],
        scratch_shapes=scratch,
        compiler_params=_cparams("arbitrary", "arbitrary"),
        name="qkvz_seq" if seq_mode else "qkvz_step",
    )(*args)


def _tail_body(seq_mode, x_ref, w_ref, cw_ref, alog_ref, dtb_ref, *rest):
    if seq_mode:
        ysc_ref, gate_ref, m_ref, w_bf, carry_ref = rest
    else:
        p2_ref, p1_ref, ysc_ref, gate_ref, m_ref, w_bf = rest
    tm = x_ref.shape[0]

    @pl.when(pl.program_id(0) == 0)
    def _():
        _cast_transposed(w_ref, w_bf)
        if seq_mode:
            carry_ref[...] = jnp.zeros_like(carry_ref)

    z = _dot(x_ref[...], w_bf[...])
    sb = z[:, 0:D_SCONV]
    mm = z[:, D_SCONV:2 * D_SCONV] * z[:, 2 * D_SCONV:3 * D_SCONV]
    raw = z[:, 3 * D_SCONV:]
    if seq_mode:
        prev = carry_ref[...]
        taps = (_shift_rows(mm, prev, 2), _shift_rows(mm, prev, 1))
        carry_ref[...] = mm[tm - SUBLANES:]
        m_ref[...] = mm[tm - SUBLANES:]
    else:
        taps = (p2_ref[...], p1_ref[...])
        m_ref[...] = mm
    conv = cw_ref[0:1, :] * taps[0] + cw_ref[1:2, :] * taps[1] + cw_ref[2:3, :] * mm
    ysc_ref[...] = (sb * conv).astype(ysc_ref.dtype)

    lane = lax.broadcasted_iota(jnp.int32, raw.shape, 1)
    beta = jax.nn.sigmoid(raw)
    sp = raw + dtb_ref[...]
    softplus = jnp.maximum(sp, 0.0) + jnp.log1p(jnp.exp(-jnp.abs(sp)))
    g = -jnp.exp(alog_ref[...]) * softplus
    eg = pltpu.roll(jnp.exp(g), LANE_EG - LANE_G, 1)
    gate_ref[...] = jnp.where(lane < LANE_G, beta, jnp.where(lane < LANE_EG, g, eg))


def _tail(h, w_tail, conv_w, a_log_row, dt_bias_row, state, *, tm=512):
    m, k = h.shape
    seq_mode = state is None
    tm = min(tm, m)
    assert m % tm == 0
    in_specs = [
        pl.BlockSpec((tm, k), lambda i: (i, 0)),
        pl.BlockSpec((D_TAIL, k), lambda i: (0, 0)),
        pl.BlockSpec((SCONV_W, D_SCONV), lambda i: (0, 0)),
        pl.BlockSpec((1, LANES), lambda i: (0, 0)),
        pl.BlockSpec((1, LANES), lambda i: (0, 0)),
    ]
    args = [h, w_tail, conv_w, a_log_row, dt_bias_row]
    scratch = [pltpu.VMEM((k, D_TAIL), BF16)]
    if seq_mode:
        scratch.append(pltpu.VMEM((SUBLANES, D_SCONV), F32))
        keep = SUBLANES
        keep_spec = pl.BlockSpec((keep, D_SCONV), lambda i: (0, 0))
    else:
        assert m == tm
        for r in range(SCONV_W - 1):
            in_specs.append(pl.BlockSpec((tm, D_SCONV), functools.partial(lambda i, r: (i, r), r=r)))
            args.append(state)
        keep = m
        keep_spec = pl.BlockSpec((keep, D_SCONV), lambda i: (i, 0))
    return pl.pallas_call(
        functools.partial(_tail_body, seq_mode),
        grid=(m // tm,),
        in_specs=in_specs,
        out_specs=[
            pl.BlockSpec((tm, D_SCONV), lambda i: (i, 0)),
            pl.BlockSpec((tm, LANES), lambda i: (i, 0)),
            keep_spec,
        ],
        out_shape=[
            jax.ShapeDtypeStruct((m, D_SCONV), BF16),
            jax.ShapeDtypeStruct((m, LANES), F32),
            jax.ShapeDtypeStruct((keep, D_SCONV), F32),
        ],
        scratch_shapes=scratch,
        compiler_params=_cparams("arbitrary"),
        name="tail_seq" if seq_mode else "tail_step",
    )(*args)


def _gated_norm(o, zg, nw):
    y = o * lax.rsqrt(jnp.mean(o * o, axis=-1, keepdims=True) + EPS) * nw
    return y * _silu(zg)


def _gdn_seq_body(nc, qkvz_ref, gate_ref, nw_ref, y_ref, sfin_ref, s_ref):
    c = GDN_CHUNK
    n = pl.program_id(0)

    @pl.when(n == 0)
    def _():
        s_ref[...] = jnp.zeros_like(s_ref)

    r = lax.broadcasted_iota(jnp.int32, (c, c), 0)
    cc = lax.broadcasted_iota(jnp.int32, (c, c), 1)
    tri = r >= cc
    strict = r > cc
    ones = jnp.where(tri, 1.0, 0.0).astype(BF16)
    zpad = jnp.zeros((LANES - c, LANES), F32)
    nw = nw_ref[...]
    pairs = [(j, h) for j in range(nc) for h in range(GDN_HEADS)]

    gates, gcs, gcts = [], [], []
    for j in range(nc):
        gate = gate_ref[j * c:(j + 1) * c, :]
        g1 = gate.astype(BF16)
        r1 = gate - g1.astype(F32)
        g2 = r1.astype(BF16)
        g3 = (r1 - g2.astype(F32)).astype(BF16)
        gc = _dot(ones, g1) + _dot(ones, g2) + _dot(ones, g3)
        gates.append(gate)
        gcs.append(gc)
        gcts.append(jnp.concatenate([gc, zpad], axis=0).T)

    def rows(j):
        return slice(j * c, (j + 1) * c)

    gcol = [gcs[j][:, LANE_G + h:LANE_G + h + 1] for j, h in pairs]
    bcol = [gates[j][:, LANE_BETA + h:LANE_BETA + h + 1] for j, h in pairs]
    decay = [jnp.exp(jnp.where(tri, gcol[i] - gcts[j][LANE_G + h:LANE_G + h + 1, 0:c], NEG))
             for i, (j, h) in enumerate(pairs)]
    eg = [jnp.exp(g) for g in gcol]
    k = [qkvz_ref[1, h, rows(j), :] for j, h in pairs]
    kb = [k[i] * bcol[i] for i in range(len(pairs))]
    qd = [qkvz_ref[0, h, rows(j), :] for j, h in pairs]
    kq = [_dot_nt(jnp.concatenate([kb[i], qd[i]], axis=0).astype(BF16), k[i].astype(BF16))
          for i in range(len(pairs))]
    x = [jnp.where(strict, -(kq[i][:c] * decay[i]), 0.0) for i in range(len(pairs))]
    intra = [jnp.where(tri, kq[i][c:] * decay[i], 0.0).astype(BF16) for i in range(len(pairs))]
    x_b = [xi.astype(BF16) for xi in x]
    p = [_dot(xb, xb) for xb in x_b]
    t_off = x
    n_steps = c.bit_length() - 2
    for step in range(n_steps):
        p_b = [pi.astype(BF16) for pi in p]
        if step < n_steps - 1:
            pt = [_dot(jnp.concatenate([p_b[i], t_off[i].astype(BF16)], axis=0), p_b[i]) for i in range(len(pairs))]
            t_off = [t_off[i] + p[i] + pt[i][c:] for i in range(len(pairs))]
            p = [pti[:c] for pti in pt]
        else:
            t_off = [t_off[i] + p[i] + _dot(t_off[i].astype(BF16), p_b[i]) for i in range(len(pairs))]
    rhs = [jnp.concatenate([qkvz_ref[2, h, rows(j), :] * bcol[i], kb[i] * eg[i]], axis=1)
           for i, (j, h) in enumerate(pairs)]
    sol = [rhs[i] + _dot(t_off[i].astype(BF16), rhs[i].astype(BF16)) for i in range(len(pairs))]
    wq = [jnp.concatenate([sol[i][:, GDN_DV:], qd[i] * eg[i]], axis=0).astype(BF16) for i in range(len(pairs))]
    glast = [g[c - 1:c, :] for g in gcol]
    kdt = [jnp.concatenate([k[i] * jnp.exp(glast[i] - gcol[i]), zpad], axis=0).T[:, 0:c].astype(BF16)
           for i in range(len(pairs))]
    ikd = [jnp.concatenate([intra[i], kdt[i]], axis=0) for i in range(len(pairs))]
    g_tot = [jnp.exp(g) for g in glast]

    for j in range(nc):
        idx = [j * GDN_HEADS + h for h in range(GDN_HEADS)]
        s = [s_ref[h] for h in range(GDN_HEADS)]
        ws = [_dot(wq[i], s[h].astype(BF16)) for h, i in enumerate(idx)]
        v_new = [(sol[i][:, :GDN_DV] - ws[h][:c]).astype(BF16) for h, i in enumerate(idx)]
        upd = [_dot(ikd[i], v_new[h]) for h, i in enumerate(idx)]
        for h, i in enumerate(idx):
            s_ref[h] = s[h] * g_tot[i] + upd[h][c:]
            o = ws[h][c:] + upd[h][:c]
            zg = qkvz_ref[3, h, rows(j), :]
            y_ref[rows(j), h * GDN_DV:(h + 1) * GDN_DV] = _gated_norm(o, zg, nw).astype(y_ref.dtype)

    @pl.when(n == pl.num_programs(0) - 1)
    def _():
        sfin_ref[...] = s_ref[...]


def _gdn_seq(qkvz, gate, norm_w, *, nc=2):
    t = qkvz.shape[2]
    c = GDN_CHUNK
    assert t % (nc * c) == 0 and c & (c - 1) == 0
    return pl.pallas_call(
        functools.partial(_gdn_seq_body, nc),
        grid=(t // (nc * c),),
        in_specs=[
            pl.BlockSpec((4, GDN_HEADS, nc * c, LANES), lambda n: (0, 0, n, 0)),
            pl.BlockSpec((nc * c, LANES), lambda n: (n, 0)),
            pl.BlockSpec((1, GDN_DV), lambda n: (0, 0)),
        ],
        out_specs=[
            pl.BlockSpec((nc * c, D_GDN_V), lambda n: (n, 0)),
            pl.BlockSpec((GDN_HEADS, GDN_DK, GDN_DV), lambda n: (0, 0, 0)),
        ],
        out_shape=[
            jax.ShapeDtypeStruct((t, D_GDN_V), BF16),
            jax.ShapeDtypeStruct((GDN_HEADS, GDN_DK, GDN_DV), F32),
        ],
        scratch_shapes=[pltpu.VMEM((GDN_HEADS, GDN_DK, GDN_DV), F32)],
        compiler_params=_cparams("arbitrary"),
        name="gdn_seq",
    )(qkvz, gate, norm_w.reshape(1, GDN_DV))


def _gdn_step_body(qkvz_ref, gate_ref, nw_ref, s_ref, y_ref, so_ref, beta_ref, eg_ref, yacc_ref):
    h = pl.program_id(0)
    bs = s_ref.shape[0]
    gate = gate_ref[...]
    lane = lax.broadcasted_iota(jnp.int32, gate.shape, 1)

    def pick(l):
        col = jnp.sum(jnp.where(lane == l, gate, 0.0), axis=1, keepdims=True)
        return jnp.broadcast_to(col, gate.shape)

    beta_ref[...] = pick(LANE_BETA + h)
    eg_ref[...] = pick(LANE_EG + h)
    eye = (lax.broadcasted_iota(jnp.int32, (GDN_DK, GDN_DK), 0)
           == lax.broadcasted_iota(jnp.int32, (GDN_DK, GDN_DK), 1))
    nw = nw_ref[...]

    def column(row):
        return jnp.sum(jnp.where(eye, row, 0.0), axis=1, keepdims=True)

    def body(b, carry):
        one = pl.ds(b, 1)
        qcol = column(qkvz_ref[0, one, :])
        kcol = column(qkvz_ref[1, one, :])
        s = s_ref[b] * eg_ref[one, :]
        kv = jnp.sum(s * kcol, axis=0, keepdims=True)
        delta = (qkvz_ref[2, one, :] - kv) * beta_ref[one, :]
        s = s + kcol * delta
        so_ref[b] = s
        o = jnp.sum(s * qcol, axis=0, keepdims=True)
        yacc_ref[one, :] = _gated_norm(o, qkvz_ref[3, one, :], nw)
        return carry

    lax.fori_loop(0, bs, body, 0, unroll=min(bs, GDN_STEP_UNROLL))
    y_ref[...] = yacc_ref[...].astype(y_ref.dtype)


def _gdn_step(qkvz, gate, norm_w, state, *, bs=32):
    b = qkvz.shape[2]
    bs = min(bs, b)
    assert b % bs == 0
    row_scratch = pltpu.VMEM((bs, LANES), F32)
    return pl.pallas_call(
        _gdn_step_body,
        grid=(GDN_HEADS, b // bs),
        in_specs=[
            pl.BlockSpec((4, None, bs, LANES), lambda h, i: (0, h, i, 0)),
            pl.BlockSpec((bs, LANES), lambda h, i: (i, 0)),
            pl.BlockSpec((1, GDN_DV), lambda h, i: (0, 0)),
            pl.BlockSpec((bs, None, GDN_DK, GDN_DV), lambda h, i: (i, h, 0, 0)),
        ],
        out_specs=[
            pl.BlockSpec((bs, GDN_DV), lambda h, i: (i, h)),
            pl.BlockSpec((bs, None, GDN_DK, GDN_DV), lambda h, i: (i, h, 0, 0)),
        ],
        out_shape=[
            jax.ShapeDtypeStruct((b, D_GDN_V), BF16),
            jax.ShapeDtypeStruct(state.shape, F32),
        ],
        scratch_shapes=[row_scratch, row_scratch, row_scratch],
        compiler_params=_cparams("parallel", "parallel"),
        name="gdn_step",
    )(qkvz, gate, norm_w.reshape(1, GDN_DV), state)


def _gate_param_row(p):
    return jnp.zeros((1, LANES), F32).at[0, LANE_G:LANE_G + GDN_HEADS].set(p.astype(F32))


def _tail_weight_t(w_in_t):
    o1 = D_GDN_MAIN
    o2 = o1 + 2 * GDN_HEADS
    pad = jnp.zeros((LANES - 2 * GDN_HEADS, w_in_t.shape[1]), w_in_t.dtype)
    return jnp.concatenate([w_in_t[o2:], w_in_t[o1:o2], pad], axis=0)


def _ffn(h, x, state, w_up, w_conv, w_down, next_norm, *, final=False, name):
    hid, keep_a, keep_b = _ffn_up(h, w_up, w_conv, state)
    out = _matmul([hid], w_down, tm=256, residual=x, norm_w=next_norm, final=final, name=name)
    return out, jnp.concatenate([keep_a, keep_b], axis=1)


def _layer1_in(h, w_in_t, w_tail_t, gdn_conv, sconv_w, a_log_row, dt_bias_row, conv_state, sconv_state):
    qkvz, raw = _qkvz(h, w_in_t, gdn_conv, conv_state)
    ysc, gate, mrows = _tail(h, w_tail_t, sconv_w, a_log_row, dt_bias_row, sconv_state)
    return qkvz, raw, ysc, gate, mrows


def kernel(x_prompt, x_sample, state_l0_pool, cache_l0_k, cache_l0_v, state_l0_ffn_conv, state_l1_gdn_conv, state_l1_gdn_S, state_l1_sconv, state_l1_ffn_conv, l0_norm_mix, l0_w_in, l0_pool_w, l0_pool_scale, l0_sinks, l0_w_out, l0_norm_ffn, l0_ffn_w_up, l0_ffn_conv, l0_ffn_w_down, l1_norm_mix, l1_w_in, l1_gdn_conv, l1_gdn_A_log, l1_gdn_dt_bias, l1_gdn_norm, l1_sconv_w, l1_w_out, l1_norm_ffn, l1_ffn_w_up, l1_ffn_conv, l1_ffn_w_down, final_norm):
    bp, t, d = x_prompt.shape
    nb, ts = x_sample.shape[:2]
    wb = cache_l0_k.shape[1]
    assert bp == 1 and ts == 1 and d == D_MODEL and wb == WINDOW and t >= WINDOW
    w_in1_t = l1_w_in.T
    w_tail_t = _tail_weight_t(w_in1_t)
    a_log_row = _gate_param_row(l1_gdn_A_log)
    dt_bias_row = _gate_param_row(l1_gdn_dt_bias)
    last = SUBLANES

    xp = x_prompt[0]
    z0 = _matmul([_rmsnorm(xp, l0_norm_mix, BF16)], l0_w_in, tm=256, name="in_proj0")
    mix = _mixer0_seq(z0, l0_pool_w, l0_pool_scale, l0_sinks)
    x1, h1 = _matmul([mix], l0_w_out, tm=256, residual=xp, norm_w=l0_norm_ffn, name="out_proj0")
    (x2, h2), ffn0_rows = _ffn(h1, x1, None, l0_ffn_w_up, l0_ffn_conv, l0_ffn_w_down, l1_norm_mix,
                               name="ffn_down0")
    qkvz, raw, ysc, gate, mrows = _layer1_in(
        h2, w_in1_t, w_tail_t, l1_gdn_conv, l1_sconv_w, a_log_row, dt_bias_row, None, None)
    ygdn, s_fin = _gdn_seq(qkvz, gate, l1_gdn_norm)
    x3, h3 = _matmul([ygdn, ysc], l1_w_out, tm=256, residual=x2, norm_w=l1_norm_ffn, name="out_proj1")
    y_prompt, ffn1_rows = _ffn(h3, x3, None, l1_ffn_w_up, l1_ffn_conv, l1_ffn_w_down, final_norm,
                               final=True, name="ffn_down1")
    y_prompt = y_prompt[None]
    p_pool = z0[t - POOL_BUF:, :D_POOL][None]
    p_k = z0[t - wb:, O_K:O_V].reshape(1, wb, N_KV_HEADS, HEAD_DIM)
    p_v = z0[t - wb:, O_V:].reshape(1, wb, N_KV_HEADS, HEAD_DIM)
    p_ffn0 = ffn0_rows[last - (FFN_CONV - 1):][None]
    p_gconv = raw[last - (GDN_CONV - 1):, :D_GDN_CONV][None]
    p_s = s_fin[None]
    p_sconv = mrows[last - (SCONV_W - 1):][None]
    p_ffn1 = ffn1_rows[last - (FFN_CONV - 1):][None]

    xs = x_sample[:, 0]
    z0s = _matmul([_rmsnorm(xs, l0_norm_mix, BF16)], l0_w_in, tm=256, name="in_proj0")
    y_pool = _pool_step(z0s, state_l0_pool.reshape(nb, POOL_BUF * D_POOL), l0_pool_w, l0_pool_scale)
    kv_eye = jnp.eye(N_KV_HEADS, dtype=F32)
    q4 = z0s[:, D_POOL:O_K].reshape(nb, N_KV_HEADS, Q_PER_KV, HEAD_DIM)
    qm = jnp.einsum("bhgd,hk->bhgkd", q4, kv_eye).reshape(nb, N_Q_HEADS, D_KV)
    att = _attn_step(qm, cache_l0_k.reshape(nb, wb, D_KV), cache_l0_v.reshape(nb, wb, D_KV), z0s, l0_sinks)
    att = jnp.einsum("bhgkd,hk->bhgd", att.reshape(nb, N_KV_HEADS, Q_PER_KV, N_KV_HEADS, HEAD_DIM), kv_eye)
    mix_s = jnp.concatenate([y_pool, att.reshape(nb, D_ATTN).astype(BF16)], axis=1)
    x1s, h1s = _matmul([mix_s], l0_w_out, tm=256, residual=xs, norm_w=l0_norm_ffn, name="out_proj0")
    (x2s, h2s), ffn0_new = _ffn(h1s, x1s, state_l0_ffn_conv.reshape(nb, (FFN_CONV - 1) * 2 * D_FF),
                                l0_ffn_w_up, l0_ffn_conv, l0_ffn_w_down, l1_norm_mix, name="ffn_down0")
    qkvz_s, raw_s, ysc_s, gate_s, m_s = _layer1_in(
        h2s, w_in1_t, w_tail_t, l1_gdn_conv, l1_sconv_w, a_log_row, dt_bias_row,
        state_l1_gdn_conv.reshape(nb, (GDN_CONV - 1) * D_GDN_CONV),
        state_l1_sconv.reshape(nb, (SCONV_W - 1) * D_SCONV))
    ygdn_s, s_new = _gdn_step(qkvz_s, gate_s, l1_gdn_norm, state_l1_gdn_S)
    x3s, h3s = _matmul([ygdn_s, ysc_s], l1_w_out, tm=256, residual=x2s, norm_w=l1_norm_ffn, name="out_proj1")
    y_sample, ffn1_new = _ffn(h3s, x3s, state_l1_ffn_conv.reshape(nb, (FFN_CONV - 1) * 2 * D_FF),
                              l1_ffn_w_up, l1_ffn_conv, l1_ffn_w_down, final_norm, final=True,
                              name="ffn_down1")
    y_sample = y_sample[:, None]

    def push(state, new_row):
        return jnp.concatenate([state[:, 1:], new_row[:, None].astype(state.dtype)], axis=1)

    s_pool = push(state_l0_pool, z0s[:, :D_POOL])
    s_k = push(cache_l0_k, z0s[:, O_K:O_V].reshape(nb, N_KV_HEADS, HEAD_DIM))
    s_v = push(cache_l0_v, z0s[:, O_V:].reshape(nb, N_KV_HEADS, HEAD_DIM))
    s_ffn0 = push(state_l0_ffn_conv, ffn0_new)
    s_gconv = push(state_l1_gdn_conv, raw_s[:, :D_GDN_CONV])
    s_sconv = push(state_l1_sconv, m_s)
    s_ffn1 = push(state_l1_ffn_conv, ffn1_new)
    return (y_prompt, y_sample, p_pool, s_pool, p_k, s_k, p_v, s_v, p_ffn0, s_ffn0,
            p_gconv, s_gconv, p_s, s_new, p_sconv, s_sconv, p_ffn1, s_ffn1)
```

```python
import functools

import jax
import jax.numpy as jnp
from jax import lax
from jax.experimental import pallas as pl
from jax.experimental.pallas import tpu as pltpu

F32 = jnp.float32
BF16 = jnp.bfloat16
EPS = 1e-6
NEG = -1e30

LANES = 128
SUBLANES = 8
MXU_COLS = 256
VMEM_LIMIT_BYTES = 60 * 1024 * 1024

D_MODEL = 2048
D_POOL = 512
POOL_WINDOWS = (2, 4, 8, 16)
POOL_BUF = max(POOL_WINDOWS) - 1
POOL_HIST = 16
HEAD_DIM = 64
N_Q_HEADS = 24
N_KV_HEADS = 4
Q_PER_KV = N_Q_HEADS // N_KV_HEADS
WINDOW = 128
D_ATTN = N_Q_HEADS * HEAD_DIM
D_KV = N_KV_HEADS * HEAD_DIM
D_IN0 = D_POOL + D_ATTN + 2 * D_KV
O_K = D_POOL + D_ATTN
O_V = O_K + D_KV
GDN_HEADS = 12
GDN_DK = 128
GDN_DV = 128
D_GDN_K = GDN_HEADS * GDN_DK
D_GDN_V = GDN_HEADS * GDN_DV
D_GDN_CONV = 2 * D_GDN_K + D_GDN_V
D_GDN_MAIN = D_GDN_CONV + D_GDN_V
GDN_CONV = 4
GDN_CHUNK = 64
GDN_STEP_UNROLL = 8
D_SCONV = 512
SCONV_W = 3
D_TAIL = 3 * D_SCONV + LANES
D_FF = 5632
FFN_CONV = 3
LANE_BETA = 0
LANE_G = GDN_HEADS
LANE_EG = 2 * GDN_HEADS


def _cparams(*sem):
    return pltpu.CompilerParams(dimension_semantics=sem, vmem_limit_bytes=VMEM_LIMIT_BYTES)


def _dot(a, b):
    return jnp.dot(a, b, preferred_element_type=F32)


def _dot_nt(a, b):
    return lax.dot_general(a, b, (((1,), (1,)), ((), ())), preferred_element_type=F32)


def _silu(x):
    return x * jax.nn.sigmoid(x)


def _cast_rows(w_ref, wbf_ref, chunk=256):
    k = w_ref.shape[0]
    chunk = min(chunk, k)
    assert k % chunk == 0

    def body(c, carry):
        r = pl.multiple_of(c * chunk, chunk)
        wbf_ref[pl.ds(r, chunk), :] = w_ref[pl.ds(r, chunk), :].astype(BF16)
        return carry

    lax.fori_loop(0, k // chunk, body, 0)


def _cast_transposed(wt_ref, wbf_ref):
    n = wt_ref.shape[0]
    step = MXU_COLS if n % MXU_COLS == 0 else LANES
    for c in range(0, n, step):
        wbf_ref[:, c:c + step] = wt_ref[c:c + step, :].T.astype(BF16)


def _shift_rows(u, prev8, k):
    rolled = pltpu.roll(u, k, 0)
    head = pltpu.roll(jnp.concatenate([prev8, u[0:SUBLANES]], axis=0), k, 0)[SUBLANES:2 * SUBLANES]
    return jnp.concatenate([head, rolled[SUBLANES:]], axis=0)


def _rmsnorm_body(x_ref, w_ref, o_ref):
    x = x_ref[...]
    ms = jnp.mean(x * x, axis=-1, keepdims=True)
    o_ref[...] = (x * lax.rsqrt(ms + EPS) * w_ref[...]).astype(o_ref.dtype)


def _rmsnorm(x, w, out_dtype):
    m, d = x.shape
    tm = min(m, 512)
    assert m % tm == 0
    return pl.pallas_call(
        _rmsnorm_body,
        grid=(m // tm,),
        in_specs=[pl.BlockSpec((tm, d), lambda i: (i, 0)), pl.BlockSpec((1, d), lambda i: (0, 0))],
        out_specs=pl.BlockSpec((tm, d), lambda i: (i, 0)),
        out_shape=jax.ShapeDtypeStruct((m, d), out_dtype),
        compiler_params=_cparams("parallel"),
        name="rmsnorm",
    )(x, w.reshape(1, d))


def _mm_body(n_pieces, n_stage, mode, *refs):
    n_out = {"plain": 1, "residual": 2, "final": 1}[mode]
    n_in = n_pieces + (0 if mode == "plain" else 1)
    set_a, set_b = refs[:n_in], refs[n_in:2 * n_in]
    w_ref = refs[2 * n_in]
    pos = 2 * n_in + 1
    nw_ref = None
    if mode != "plain":
        nw_ref = refs[pos]
        pos += 1
    out_a, out_b = refs[pos:pos + n_out], refs[pos + n_out:pos + 2 * n_out]
    wbf_ref = refs[pos + 2 * n_out]
    s = pl.program_id(0)
    last = pl.num_programs(0) - 1
    ck = w_ref.shape[0]

    @pl.when(s < n_stage)
    def _():
        _cast_rows(w_ref, wbf_ref.at[pl.ds(pl.multiple_of(s * ck, ck), ck), :])

    def rows(ins, outs):
        acc, row = None, 0
        for x_ref in ins[:n_pieces]:
            kp = x_ref.shape[1]
            d = _dot(x_ref[...], wbf_ref[row:row + kp, :])
            acc = d if acc is None else acc + d
            row += kp
        if mode == "plain":
            outs[0][...] = acc
            return
        xn = acc + ins[n_pieces][...]
        hn = xn * lax.rsqrt(jnp.mean(xn * xn, axis=-1, keepdims=True) + EPS) * nw_ref[...]
        if mode == "residual":
            outs[0][...] = xn
        outs[-1][...] = hn.astype(outs[-1].dtype)

    @pl.when((s >= n_stage) & (s < last))
    def _():
        rows(set_a, out_a)

    @pl.when(s == last)
    def _():
        rows(set_b, out_b)


def _matmul(xs, xs2, w, *, tm, residual=None, residual2=None, norm_w=None, final=False, chunk=256,
            name="matmul"):
    m, m2 = xs[0].shape[0], xs2[0].shape[0]
    k, n = w.shape
    tm = min(tm, m)
    assert m % tm == 0 and k % chunk == 0 and sum(x.shape[1] for x in xs) == k
    n_stage, nm = k // chunk, m // tm
    mode = "plain" if residual is None else ("final" if final else "residual")

    def row_tile(s):
        return jnp.clip(s - n_stage, 0, nm - 1)

    tile_a = pl.BlockSpec((tm, n), lambda s: (row_tile(s), 0))
    tile_b = pl.BlockSpec((m2, n), lambda s: (0, 0))
    in_a = [pl.BlockSpec((tm, x.shape[1]), lambda s: (row_tile(s), 0)) for x in xs]
    in_b = [pl.BlockSpec((m2, x.shape[1]), lambda s: (0, 0)) for x in xs2]
    args_a, args_b = list(xs), list(xs2)
    if mode != "plain":
        in_a.append(tile_a)
        in_b.append(tile_b)
        args_a.append(residual)
        args_b.append(residual2)
    in_specs = in_a + in_b + [pl.BlockSpec((chunk, n), lambda s: (jnp.minimum(s, n_stage - 1), 0))]
    args = args_a + args_b + [w]
    if mode != "plain":
        in_specs.append(pl.BlockSpec((1, n), lambda s: (0, 0)))
        args.append(norm_w.reshape(1, n))
    dtypes = [F32, BF16] if mode == "residual" else [F32]
    out_specs = [tile_a] * len(dtypes) + [tile_b] * len(dtypes)
    out_shape = ([jax.ShapeDtypeStruct((m, n), dt) for dt in dtypes]
                 + [jax.ShapeDtypeStruct((m2, n), dt) for dt in dtypes])
    outs = pl.pallas_call(
        functools.partial(_mm_body, len(xs), n_stage, mode),
        grid=(n_stage + nm + 1,),
        in_specs=in_specs,
        out_specs=out_specs,
        out_shape=out_shape,
        scratch_shapes=[pltpu.VMEM((k, n), BF16)],
        compiler_params=_cparams("arbitrary"),
        name=name,
    )(*args)
    half = len(dtypes)
    first, second = outs[:half], outs[half:]
    return (first[0], second[0]) if half == 1 else (tuple(first), tuple(second))


def _ffn_up_body(x_ref, xs_ref, wa_ref, wb_ref, cwa_ref, cwb_ref, a2_ref, a1_ref, b2_ref, b1_ref,
                 hid_ref, upa_ref, upb_ref, hids_ref, upsa_ref, upsb_ref, wa_bf, wb_bf, ca_ref, cb_ref):
    tm = x_ref.shape[0]
    i = pl.program_id(1)
    last = pl.num_programs(1) - 1

    @pl.when(i == 0)
    def _():
        _cast_rows(wa_ref, wa_bf)
        _cast_rows(wb_ref, wb_bf)
        ca_ref[...] = jnp.zeros_like(ca_ref)
        cb_ref[...] = jnp.zeros_like(cb_ref)

    def gate(ua, ub, taps_a, taps_b):
        c_a = cwa_ref[0:1, :] * taps_a[0] + cwa_ref[1:2, :] * taps_a[1] + cwa_ref[2:3, :] * ua
        c_b = cwb_ref[0:1, :] * taps_b[0] + cwb_ref[1:2, :] * taps_b[1] + cwb_ref[2:3, :] * ub
        return (_silu(c_a) * c_b).astype(BF16)

    @pl.when(i < last)
    def _():
        x = x_ref[...]
        ua = _dot(x, wa_bf[...])
        ub = _dot(x, wb_bf[...])
        pa, pb = ca_ref[...], cb_ref[...]
        hid_ref[...] = gate(ua, ub, (_shift_rows(ua, pa, 2), _shift_rows(ua, pa, 1)),
                            (_shift_rows(ub, pb, 2), _shift_rows(ub, pb, 1)))
        ca_ref[...] = ua[tm - SUBLANES:]
        cb_ref[...] = ub[tm - SUBLANES:]
        upa_ref[...] = ua[tm - SUBLANES:]
        upb_ref[...] = ub[tm - SUBLANES:]

    @pl.when(i == last)
    def _():
        x = xs_ref[...]
        ua = _dot(x, wa_bf[...])
        ub = _dot(x, wb_bf[...])
        hids_ref[...] = gate(ua, ub, (a2_ref[...], a1_ref[...]), (b2_ref[...], b1_ref[...]))
        upsa_ref[...] = ua
        upsb_ref[...] = ub


def _ffn_up(h, hs, w_up, conv_w, state_t, *, tn=512, tm=1024):
    m, k = h.shape
    ms = hs.shape[0]
    tm = min(tm, m)
    nb, nm = D_FF // tn, m // tm
    assert m % tm == 0 and D_FF % tn == 0
    tile = lambda j, i: (jnp.minimum(i, nm - 1), j)
    in_specs = [
        pl.BlockSpec((tm, k), lambda j, i: (jnp.minimum(i, nm - 1), 0)),
        pl.BlockSpec((ms, k), lambda j, i: (0, 0)),
        pl.BlockSpec((k, tn), lambda j, i: (0, j)),
        pl.BlockSpec((k, tn), lambda j, i: (0, j + nb)),
        pl.BlockSpec((FFN_CONV, tn), lambda j, i: (0, j)),
        pl.BlockSpec((FFN_CONV, tn), lambda j, i: (0, j + nb)),
        pl.BlockSpec((None, ms, tn), lambda j, i: (0, 0, j)),
        pl.BlockSpec((None, ms, tn), lambda j, i: (1, 0, j)),
        pl.BlockSpec((None, ms, tn), lambda j, i: (0, 0, j + nb)),
        pl.BlockSpec((None, ms, tn), lambda j, i: (1, 0, j + nb)),
    ]
    keep_spec = pl.BlockSpec((SUBLANES, tn), lambda j, i: (0, j))
    step_spec = pl.BlockSpec((ms, tn), lambda j, i: (0, j))
    outs = pl.pallas_call(
        _ffn_up_body,
        grid=(nb, nm + 1),
        in_specs=in_specs,
        out_specs=[pl.BlockSpec((tm, tn), tile), keep_spec, keep_spec, step_spec, step_spec, step_spec],
        out_shape=[
            jax.ShapeDtypeStruct((m, D_FF), BF16),
            jax.ShapeDtypeStruct((SUBLANES, D_FF), F32),
            jax.ShapeDtypeStruct((SUBLANES, D_FF), F32),
            jax.ShapeDtypeStruct((ms, D_FF), BF16),
            jax.ShapeDtypeStruct((ms, D_FF), F32),
            jax.ShapeDtypeStruct((ms, D_FF), F32),
        ],
        scratch_shapes=[pltpu.VMEM((k, tn), BF16), pltpu.VMEM((k, tn), BF16),
                        pltpu.VMEM((SUBLANES, tn), F32), pltpu.VMEM((SUBLANES, tn), F32)],
        compiler_params=_cparams("arbitrary", "arbitrary"),
        name="ffn_up",
    )(h, hs, w_up, w_up, conv_w, conv_w, state_t, state_t, state_t, state_t)
    return outs[:3], outs[3:]


def _half_lane_pair(x, head_in_high, lo_mask):
    if head_in_high:
        hi = jnp.where(lo_mask, 0.0, x)
        return pltpu.roll(hi, HEAD_DIM, 1), hi
    lo = jnp.where(lo_mask, x, 0.0)
    return lo, pltpu.roll(lo, HEAD_DIM, 1)


def _mixer0_seq_body(sink_ref, z_ref, kvp_ref, up_ref, pw_ref, ps_ref, o_ref):
    n = pl.program_id(0)
    first = n == 0
    w = WINDOW
    lo_mask = lax.broadcasted_iota(jnp.int32, (1, LANES), 1) < HEAD_DIM

    hist = jnp.where(first, 0.0, up_ref[...])
    pos = n * w + lax.broadcasted_iota(jnp.int32, (w, 1), 0)
    for g, win in enumerate(POOL_WINDOWS):
        sl = slice(g * LANES, (g + 1) * LANES)
        e = jnp.concatenate([hist[:, sl], z_ref[:, sl]], axis=0)
        s, sh = e, 1
        while sh < win:
            s = s + pltpu.roll(s, sh, 0)
            sh *= 2
        cnt = jnp.minimum(pos + 1, win).astype(F32)
        diff = s[POOL_HIST:] / cnt - e[POOL_HIST:]
        y = _dot(diff.astype(BF16), pw_ref[g].astype(BF16)) * ps_ref[:, sl]
        o_ref[:, sl] = y.astype(o_ref.dtype)

    row = lax.broadcasted_iota(jnp.int32, (w, 2 * w), 0)
    col = lax.broadcasted_iota(jnp.int32, (w, 2 * w), 1)
    valid = (col >= row) & (col <= row + w) & (jnp.logical_not(first) | (col >= w))
    tiles_per_kv = Q_PER_KV * HEAD_DIM // LANES
    for c in range(D_KV // LANES):
        k2 = jnp.concatenate([kvp_ref[:, c * LANES:(c + 1) * LANES],
                              z_ref[:, O_K + c * LANES:O_K + (c + 1) * LANES]], axis=0)
        v2 = jnp.concatenate([kvp_ref[:, D_KV + c * LANES:D_KV + (c + 1) * LANES],
                              z_ref[:, O_V + c * LANES:O_V + (c + 1) * LANES]], axis=0)
        for p in range(2):
            hk = 2 * c + p
            k_lo, k_hi = (t.astype(BF16) for t in _half_lane_pair(k2, p == 1, lo_mask))
            v_lo, v_hi = (t.astype(BF16) for t in _half_lane_pair(v2, p == 1, lo_mask))
            q0 = D_POOL + hk * Q_PER_KV * HEAD_DIM
            qst = jnp.concatenate(
                [z_ref[:, q0 + a * LANES:q0 + (a + 1) * LANES] for a in range(tiles_per_kv)], axis=0)
            qst = (qst * HEAD_DIM ** -0.5).astype(BF16)
            s_lo = _dot_nt(qst, k_lo)
            s_hi = _dot_nt(qst, k_hi)
            for a in range(tiles_per_kv):
                probs, inv = [], []
                for par, s_all in ((0, s_lo), (1, s_hi)):
                    sk = sink_ref[hk * Q_PER_KV + 2 * a + par]
                    s = jnp.where(valid, s_all[a * w:(a + 1) * w], NEG)
                    mx = jnp.maximum(jnp.max(s, axis=-1, keepdims=True), sk)
                    pr = jnp.exp(s - mx)
                    den = jnp.sum(pr, axis=-1, keepdims=True) + jnp.exp(sk - mx)
                    probs.append(pr.astype(BF16))
                    inv.append(1.0 / den)
                o = _dot(probs[0], v_lo) + _dot(probs[1], v_hi)
                o = o * jnp.where(lo_mask, inv[0], inv[1])
                o_ref[:, q0 + a * LANES:q0 + (a + 1) * LANES] = o.astype(o_ref.dtype)


def _mixer0_seq(z0, pool_w, pool_scale, sinks):
    t = z0.shape[0]
    w = WINDOW
    assert t % w == 0
    hist_blocks = w // POOL_HIST
    return pl.pallas_call(
        _mixer0_seq_body,
        grid=(t // w,),
        in_specs=[
            pl.BlockSpec(memory_space=pltpu.SMEM),
            pl.BlockSpec((w, D_IN0), lambda n: (n, 0)),
            pl.BlockSpec((w, 2 * D_KV), lambda n: (jnp.maximum(n - 1, 0), O_K // (2 * D_KV))),
            pl.BlockSpec((POOL_HIST, D_POOL), lambda n: (jnp.maximum(n * hist_blocks - 1, 0), 0)),
            pl.BlockSpec((len(POOL_WINDOWS), LANES, LANES), lambda n: (0, 0, 0)),
            pl.BlockSpec((1, D_POOL), lambda n: (0, 0)),
        ],
        out_specs=pl.BlockSpec((w, D_POOL + D_ATTN), lambda n: (n, 0)),
        out_shape=jax.ShapeDtypeStruct((t, D_POOL + D_ATTN), BF16),
        compiler_params=_cparams("parallel"),
        name="mixer0_seq",
    )(sinks, z0, z0, z0, pool_w, pool_scale.reshape(1, D_POOL))


def _pool_step_body(*refs):
    hist_refs = refs[:POOL_BUF]
    z_ref, pw_ref, ps_ref, o_ref = refs[POOL_BUF:]
    for g, win in enumerate(POOL_WINDOWS):
        sl = slice(g * LANES, (g + 1) * LANES)
        u = z_ref[:, sl]
        s = u
        for r in range(POOL_BUF - (win - 1), POOL_BUF):
            s = s + hist_refs[r][:, sl]
        diff = s / float(win) - u
        y = _dot(diff.astype(BF16), pw_ref[g].astype(BF16)) * ps_ref[:, sl]
        o_ref[:, sl] = y.astype(o_ref.dtype)


def _pool_step(z0, hist_t, pool_w, pool_scale):
    b = z0.shape[0]
    in_specs = [pl.BlockSpec((None, b, D_POOL), functools.partial(lambda i, r: (r, 0, 0), r=r))
                for r in range(POOL_BUF)]
    in_specs += [
        pl.BlockSpec((b, D_POOL), lambda i: (0, 0)),
        pl.BlockSpec((len(POOL_WINDOWS), LANES, LANES), lambda i: (0, 0, 0)),
        pl.BlockSpec((1, D_POOL), lambda i: (0, 0)),
    ]
    return pl.pallas_call(
        _pool_step_body,
        grid=(1,),
        in_specs=in_specs,
        out_specs=pl.BlockSpec((b, D_POOL), lambda i: (0, 0)),
        out_shape=jax.ShapeDtypeStruct((b, D_POOL), BF16),
        compiler_params=_cparams("arbitrary"),
        name="pool_step",
    )(*([hist_t] * POOL_BUF), z0, pool_w, pool_scale.reshape(1, D_POOL))


def _attn_step_body(q_ref, kn_ref, vn_ref, knt_ref, vnt_ref, kt_ref, vt_ref, sink_ref, o_ref, kto_ref, vto_ref):
    bs = q_ref.shape[0]
    newest = lax.broadcasted_iota(jnp.int32, (HEAD_DIM, WINDOW), 1) == WINDOW - 1
    for b in range(bs):
        for hk in range(N_KV_HEADS):
            kt, vt = kt_ref[b, hk], vt_ref[b, hk]
            q = q_ref[b, hk] * HEAD_DIM ** -0.5
            kn, vn = kn_ref[b, hk:hk + 1, :], vn_ref[b, hk:hk + 1, :]
            sk = sink_ref[hk * Q_PER_KV:(hk + 1) * Q_PER_KV, :]
            s = _dot(q.astype(BF16), kt.astype(BF16))
            s_new = jnp.sum(q * kn, axis=-1, keepdims=True)
            mx = jnp.maximum(jnp.maximum(jnp.max(s, axis=-1, keepdims=True), s_new), sk)
            pr = jnp.exp(s - mx)
            pn = jnp.exp(s_new - mx)
            den = jnp.sum(pr, axis=-1, keepdims=True) + pn + jnp.exp(sk - mx)
            o = _dot_nt(pr.astype(BF16), vt.astype(BF16)) + pn * vn
            o_ref[b, hk] = o / den
            kto_ref[b, hk] = jnp.where(newest, pltpu.roll(knt_ref[hk], WINDOW - 1 - b, 1),
                                       pltpu.roll(kt, WINDOW - 1, 1))
            vto_ref[b, hk] = jnp.where(newest, pltpu.roll(vnt_ref[hk], WINDOW - 1 - b, 1),
                                       pltpu.roll(vt, WINDOW - 1, 1))


def _attn_step(z0, kt, vt, sinks, *, bs=8):
    b = z0.shape[0]
    assert b % bs == 0 and kt.shape[3] == WINDOW == LANES and bs <= WINDOW
    nblk = b // bs
    q4 = z0[:, D_POOL:O_K].reshape(b, N_KV_HEADS, Q_PER_KV, HEAD_DIM)
    kn = z0[:, O_K:O_V].reshape(b, N_KV_HEADS, HEAD_DIM)
    vn = z0[:, O_V:].reshape(b, N_KV_HEADS, HEAD_DIM)

    def columns(x):
        xt = x.reshape(nblk, bs, N_KV_HEADS, HEAD_DIM).transpose(0, 2, 3, 1)
        return jnp.pad(xt, ((0, 0), (0, 0), (0, 0), (0, WINDOW - bs)))

    cache_spec = pl.BlockSpec((bs, N_KV_HEADS, HEAD_DIM, WINDOW), lambda i: (i, 0, 0, 0))
    col_spec = pl.BlockSpec((None, N_KV_HEADS, HEAD_DIM, WINDOW), lambda i: (i, 0, 0, 0))
    new_spec = pl.BlockSpec((bs, N_KV_HEADS, HEAD_DIM), lambda i: (i, 0, 0))
    q_spec = pl.BlockSpec((bs, N_KV_HEADS, Q_PER_KV, HEAD_DIM), lambda i: (i, 0, 0, 0))
    att, kto, vto = pl.pallas_call(
        _attn_step_body,
        grid=(nblk,),
        in_specs=[q_spec, new_spec, new_spec, col_spec, col_spec, cache_spec, cache_spec,
                  pl.BlockSpec((N_Q_HEADS, 1), lambda i: (0, 0))],
        out_specs=[q_spec, cache_spec, cache_spec],
        out_shape=[jax.ShapeDtypeStruct(q4.shape, F32), jax.ShapeDtypeStruct(kt.shape, F32),
                   jax.ShapeDtypeStruct(vt.shape, F32)],
        compiler_params=_cparams("parallel"),
        name="attn_step",
    )(q4, kn, vn, columns(kn), columns(vn), kt, vt, sinks.reshape(N_Q_HEADS, 1))
    return att.reshape(b, D_ATTN), kto, vto


def _qkvz_body(x_ref, xs_ref, w_ref, cw_ref, p3_ref, p2_ref, p1_ref,
               o_ref, raw_ref, os_ref, raws_ref, w_bf, carry_ref):
    tm = x_ref.shape[0]
    j = pl.program_id(0)
    i = pl.program_id(1)
    last = pl.num_programs(1) - 1

    @pl.when(i == 0)
    def _():
        _cast_transposed(w_ref, w_bf)
        carry_ref[...] = jnp.zeros_like(carry_ref)

    def heads(a):
        return [a[:, h * LANES:(h + 1) * LANES] for h in range(GDN_HEADS)]

    def l2norm(a):
        return a * lax.rsqrt(jnp.sum(a * a, axis=-1, keepdims=True) + EPS)

    def finish(u, taps, out_ref):
        def conv_act():
            t3, t2, t1 = taps()
            c = cw_ref[3:4, :] * u
            c = cw_ref[0:1, :] * t3 + cw_ref[1:2, :] * t2 + cw_ref[2:3, :] * t1 + c
            return _silu(c)

        @pl.when(j == 0)
        def _():
            for h, a in enumerate(heads(conv_act())):
                out_ref[h] = l2norm(a) * GDN_DK ** -0.5

        @pl.when(j == 1)
        def _():
            for h, a in enumerate(heads(conv_act())):
                out_ref[h] = l2norm(a)

        @pl.when(j == 2)
        def _():
            for h, a in enumerate(heads(conv_act())):
                out_ref[h] = a

        @pl.when(j == 3)
        def _():
            for h, a in enumerate(heads(u)):
                out_ref[h] = a

    @pl.when(i < last)
    def _():
        u = _dot(x_ref[...], w_bf[...])
        prev = carry_ref[...]
        carry_ref[...] = u[tm - SUBLANES:]
        raw_ref[...] = u[tm - SUBLANES:]
        finish(u, lambda: tuple(_shift_rows(u, prev, k) for k in (3, 2, 1)), o_ref)

    @pl.when(i == last)
    def _():
        u = _dot(xs_ref[...], w_bf[...])
        raws_ref[...] = u
        finish(u, lambda: (p3_ref[...], p2_ref[...], p1_ref[...]), os_ref)


def _qkvz(h, hs, w_in_t, conv_w, state_t, *, tm=512):
    m, k = h.shape
    ms = hs.shape[0]
    tm = min(tm, m)
    tn = D_GDN_K
    assert D_GDN_K == D_GDN_V and GDN_DK == LANES and m % tm == 0
    nj, nm = D_GDN_MAIN // tn, m // tm
    in_specs = [
        pl.BlockSpec((tm, k), lambda j, i: (jnp.minimum(i, nm - 1), 0)),
        pl.BlockSpec((ms, k), lambda j, i: (0, 0)),
        pl.BlockSpec((tn, k), lambda j, i: (j, 0)),
        pl.BlockSpec((GDN_CONV, tn), lambda j, i: (0, jnp.minimum(j, 2))),
    ]
    for r in range(GDN_CONV - 1):
        in_specs.append(pl.BlockSpec((None, ms, tn), functools.partial(lambda j, i, r: (r, 0, jnp.minimum(j, 2)), r=r)))
    outs = pl.pallas_call(
        _qkvz_body,
        grid=(nj, nm + 1),
        in_specs=in_specs,
        out_specs=[
            pl.BlockSpec((None, GDN_HEADS, tm, LANES), lambda j, i: (j, 0, jnp.minimum(i, nm - 1), 0)),
            pl.BlockSpec((SUBLANES, tn), lambda j, i: (0, j)),
            pl.BlockSpec((None, GDN_HEADS, ms, LANES), lambda j, i: (j, 0, 0, 0)),
            pl.BlockSpec((ms, tn), lambda j, i: (0, j)),
        ],
        out_shape=[
            jax.ShapeDtypeStruct((nj, GDN_HEADS, m, LANES), F32),
            jax.ShapeDtypeStruct((SUBLANES, D_GDN_MAIN), F32),
            jax.ShapeDtypeStruct((nj, GDN_HEADS, ms, LANES), F32),
            jax.ShapeDtypeStruct((ms, D_GDN_MAIN), F32),
        ],
        scratch_shapes=[pltpu.VMEM((k, tn), BF16), pltpu.VMEM((SUBLANES, tn), F32)],
        compiler_params=_cparams("arbitrary", "arbitrary"),
        name="qkvz",
    )(h, hs, w_in_t, conv_w, state_t, state_t, state_t)
    return outs[:2], outs[2:]


def _tail_body(x_ref, xs_ref, w_ref, cw_ref, alog_ref, dtb_ref, p2_ref, p1_ref,
               ysc_ref, gate_ref, m_ref, yscs_ref, gates_ref, ms_ref, w_bf, carry_ref):
    tm = x_ref.shape[0]
    i = pl.program_id(0)
    last = pl.num_programs(0) - 1

    @pl.when(i == 0)
    def _():
        _cast_transposed(w_ref, w_bf)
        carry_ref[...] = jnp.zeros_like(carry_ref)

    def finish(z, mm, taps, y_ref, g_ref):
        conv = cw_ref[0:1, :] * taps[0] + cw_ref[1:2, :] * taps[1] + cw_ref[2:3, :] * mm
        y_ref[...] = (z[:, 0:D_SCONV] * conv).astype(y_ref.dtype)
        raw = z[:, 3 * D_SCONV:]
        lane = lax.broadcasted_iota(jnp.int32, raw.shape, 1)
        beta = jax.nn.sigmoid(raw)
        sp = raw + dtb_ref[...]
        softplus = jnp.maximum(sp, 0.0) + jnp.log1p(jnp.exp(-jnp.abs(sp)))
        g = -jnp.exp(alog_ref[...]) * softplus
        eg = pltpu.roll(jnp.exp(g), LANE_EG - LANE_G, 1)
        g_ref[...] = jnp.where(lane < LANE_G, beta, jnp.where(lane < LANE_EG, g, eg))

    @pl.when(i < last)
    def _():
        z = _dot(x_ref[...], w_bf[...])
        mm = z[:, D_SCONV:2 * D_SCONV] * z[:, 2 * D_SCONV:3 * D_SCONV]
        prev = carry_ref[...]
        carry_ref[...] = mm[tm - SUBLANES:]
        m_ref[...] = mm[tm - SUBLANES:]
        finish(z, mm, (_shift_rows(mm, prev, 2), _shift_rows(mm, prev, 1)), ysc_ref, gate_ref)

    @pl.when(i == last)
    def _():
        z = _dot(xs_ref[...], w_bf[...])
        mm = z[:, D_SCONV:2 * D_SCONV] * z[:, 2 * D_SCONV:3 * D_SCONV]
        ms_ref[...] = mm
        finish(z, mm, (p2_ref[...], p1_ref[...]), yscs_ref, gates_ref)


def _tail(h, hs, w_tail_t, conv_w, a_log_row, dt_bias_row, state_t, *, tm=512):
    m, k = h.shape
    ms = hs.shape[0]
    tm = min(tm, m)
    assert m % tm == 0
    nm = m // tm
    tile = lambda i: (jnp.minimum(i, nm - 1), 0)
    const = lambda i: (0, 0)
    in_specs = [
        pl.BlockSpec((tm, k), tile),
        pl.BlockSpec((ms, k), const),
        pl.BlockSpec((D_TAIL, k), const),
        pl.BlockSpec((SCONV_W, D_SCONV), const),
        pl.BlockSpec((1, LANES), const),
        pl.BlockSpec((1, LANES), const),
        pl.BlockSpec((None, ms, D_SCONV), lambda i: (0, 0, 0)),
        pl.BlockSpec((None, ms, D_SCONV), lambda i: (1, 0, 0)),
    ]
    outs = pl.pallas_call(
        _tail_body,
        grid=(nm + 1,),
        in_specs=in_specs,
        out_specs=[
            pl.BlockSpec((tm, D_SCONV), tile),
            pl.BlockSpec((tm, LANES), tile),
            pl.BlockSpec((SUBLANES, D_SCONV), const),
            pl.BlockSpec((ms, D_SCONV), const),
            pl.BlockSpec((ms, LANES), const),
            pl.BlockSpec((ms, D_SCONV), const),
        ],
        out_shape=[
            jax.ShapeDtypeStruct((m, D_SCONV), BF16),
            jax.ShapeDtypeStruct((m, LANES), F32),
            jax.ShapeDtypeStruct((SUBLANES, D_SCONV), F32),
            jax.ShapeDtypeStruct((ms, D_SCONV), BF16),
            jax.ShapeDtypeStruct((ms, LANES), F32),
            jax.ShapeDtypeStruct((ms, D_SCONV), F32),
        ],
        scratch_shapes=[pltpu.VMEM((k, D_TAIL), BF16), pltpu.VMEM((SUBLANES, D_SCONV), F32)],
        compiler_params=_cparams("arbitrary"),
        name="tail",
    )(h, hs, w_tail_t, conv_w, a_log_row, dt_bias_row, state_t, state_t)
    return outs[:3], outs[3:]


def _gated_norm(o, zg, nw):
    y = o * lax.rsqrt(jnp.mean(o * o, axis=-1, keepdims=True) + EPS) * nw
    return y * _silu(zg)


def _gdn_seq_body(nc, qkvz_ref, gate_ref, nw_ref, y_ref, sfin_ref, s_ref):
    c = GDN_CHUNK
    n = pl.program_id(0)

    @pl.when(n == 0)
    def _():
        s_ref[...] = jnp.zeros_like(s_ref)

    r = lax.broadcasted_iota(jnp.int32, (c, c), 0)
    cc = lax.broadcasted_iota(jnp.int32, (c, c), 1)
    tri = r >= cc
    strict = r > cc
    ones = jnp.where(tri, 1.0, 0.0).astype(BF16)
    zpad = jnp.zeros((LANES - c, LANES), F32)
    nw = nw_ref[...]
    pairs = [(j, h) for j in range(nc) for h in range(GDN_HEADS)]

    gates, gcs, gcts = [], [], []
    for j in range(nc):
        gate = gate_ref[j * c:(j + 1) * c, :]
        g1 = gate.astype(BF16)
        r1 = gate - g1.astype(F32)
        g2 = r1.astype(BF16)
        g3 = (r1 - g2.astype(F32)).astype(BF16)
        gc = _dot(ones, g1) + _dot(ones, g2) + _dot(ones, g3)
        gates.append(gate)
        gcs.append(gc)
        gcts.append(jnp.concatenate([gc, zpad], axis=0).T)

    def rows(j):
        return slice(j * c, (j + 1) * c)

    gcol = [gcs[j][:, LANE_G + h:LANE_G + h + 1] for j, h in pairs]
    bcol = [gates[j][:, LANE_BETA + h:LANE_BETA + h + 1] for j, h in pairs]
    decay = [jnp.exp(jnp.where(tri, gcol[i] - gcts[j][LANE_G + h:LANE_G + h + 1, 0:c], NEG))
             for i, (j, h) in enumerate(pairs)]
    eg = [jnp.exp(g) for g in gcol]
    k = [qkvz_ref[1, h, rows(j), :] for j, h in pairs]
    kb = [k[i] * bcol[i] for i in range(len(pairs))]
    qd = [qkvz_ref[0, h, rows(j), :] for j, h in pairs]
    kq = [_dot_nt(jnp.concatenate([kb[i], qd[i]], axis=0).astype(BF16), k[i].astype(BF16))
          for i in range(len(pairs))]
    x = [jnp.where(strict, -(kq[i][:c] * decay[i]), 0.0) for i in range(len(pairs))]
    intra = [jnp.where(tri, kq[i][c:] * decay[i], 0.0).astype(BF16) for i in range(len(pairs))]
    x_b = [xi.astype(BF16) for xi in x]
    p = [_dot(xb, xb) for xb in x_b]
    t_off = x
    n_steps = c.bit_length() - 2
    for step in range(n_steps):
        p_b = [pi.astype(BF16) for pi in p]
        if step < n_steps - 1:
            pt = [_dot(jnp.concatenate([p_b[i], t_off[i].astype(BF16)], axis=0), p_b[i]) for i in range(len(pairs))]
            t_off = [t_off[i] + p[i] + pt[i][c:] for i in range(len(pairs))]
            p = [pti[:c] for pti in pt]
        else:
            t_off = [t_off[i] + p[i] + _dot(t_off[i].astype(BF16), p_b[i]) for i in range(len(pairs))]
    rhs = [jnp.concatenate([qkvz_ref[2, h, rows(j), :] * bcol[i], kb[i] * eg[i]], axis=1)
           for i, (j, h) in enumerate(pairs)]
    sol = [rhs[i] + _dot(t_off[i].astype(BF16), rhs[i].astype(BF16)) for i in range(len(pairs))]
    wq = [jnp.concatenate([sol[i][:, GDN_DV:], qd[i] * eg[i]], axis=0).astype(BF16) for i in range(len(pairs))]
    glast = [g[c - 1:c, :] for g in gcol]
    kdt = [jnp.concatenate([k[i] * jnp.exp(glast[i] - gcol[i]), zpad], axis=0).T[:, 0:c].astype(BF16)
           for i in range(len(pairs))]
    ikd = [jnp.concatenate([intra[i], kdt[i]], axis=0) for i in range(len(pairs))]
    g_tot = [jnp.exp(g) for g in glast]

    for j in range(nc):
        idx = [j * GDN_HEADS + h for h in range(GDN_HEADS)]
        s = [s_ref[h] for h in range(GDN_HEADS)]
        ws = [_dot(wq[i], s[h].astype(BF16)) for h, i in enumerate(idx)]
        v_new = [(sol[i][:, :GDN_DV] - ws[h][:c]).astype(BF16) for h, i in enumerate(idx)]
        upd = [_dot(ikd[i], v_new[h]) for h, i in enumerate(idx)]
        for h, i in enumerate(idx):
            s_ref[h] = s[h] * g_tot[i] + upd[h][c:]
            o = ws[h][c:] + upd[h][:c]
            zg = qkvz_ref[3, h, rows(j), :]
            y_ref[rows(j), h * GDN_DV:(h + 1) * GDN_DV] = _gated_norm(o, zg, nw).astype(y_ref.dtype)

    @pl.when(n == pl.num_programs(0) - 1)
    def _():
        sfin_ref[...] = s_ref[...]


def _gdn_seq(qkvz, gate, norm_w, *, nc=2):
    t = qkvz.shape[2]
    c = GDN_CHUNK
    assert t % (nc * c) == 0 and c & (c - 1) == 0
    return pl.pallas_call(
        functools.partial(_gdn_seq_body, nc),
        grid=(t // (nc * c),),
        in_specs=[
            pl.BlockSpec((4, GDN_HEADS, nc * c, LANES), lambda n: (0, 0, n, 0)),
            pl.BlockSpec((nc * c, LANES), lambda n: (n, 0)),
            pl.BlockSpec((1, GDN_DV), lambda n: (0, 0)),
        ],
        out_specs=[
            pl.BlockSpec((nc * c, D_GDN_V), lambda n: (n, 0)),
            pl.BlockSpec((GDN_HEADS, GDN_DK, GDN_DV), lambda n: (0, 0, 0)),
        ],
        out_shape=[
            jax.ShapeDtypeStruct((t, D_GDN_V), BF16),
            jax.ShapeDtypeStruct((GDN_HEADS, GDN_DK, GDN_DV), F32),
        ],
        scratch_shapes=[pltpu.VMEM((GDN_HEADS, GDN_DK, GDN_DV), F32)],
        compiler_params=_cparams("arbitrary"),
        name="gdn_seq",
    )(qkvz, gate, norm_w.reshape(1, GDN_DV))


def _gdn_step_body(qkvz_ref, gate_ref, nw_ref, s_ref, y_ref, so_ref, beta_ref, eg_ref, yacc_ref):
    h = pl.program_id(0)
    bs = s_ref.shape[0]
    gate = gate_ref[...]
    lane = lax.broadcasted_iota(jnp.int32, gate.shape, 1)

    def pick(l):
        col = jnp.sum(jnp.where(lane == l, gate, 0.0), axis=1, keepdims=True)
        return jnp.broadcast_to(col, gate.shape)

    beta_ref[...] = pick(LANE_BETA + h)
    eg_ref[...] = pick(LANE_EG + h)
    eye = (lax.broadcasted_iota(jnp.int32, (GDN_DK, GDN_DK), 0)
           == lax.broadcasted_iota(jnp.int32, (GDN_DK, GDN_DK), 1))
    nw = nw_ref[...]

    def column(row):
        return jnp.sum(jnp.where(eye, row, 0.0), axis=1, keepdims=True)

    def body(b, carry):
        one = pl.ds(b, 1)
        qcol = column(qkvz_ref[0, one, :])
        kcol = column(qkvz_ref[1, one, :])
        s = s_ref[b] * eg_ref[one, :]
        kv = jnp.sum(s * kcol, axis=0, keepdims=True)
        delta = (qkvz_ref[2, one, :] - kv) * beta_ref[one, :]
        s = s + kcol * delta
        so_ref[b] = s
        o = jnp.sum(s * qcol, axis=0, keepdims=True)
        yacc_ref[one, :] = _gated_norm(o, qkvz_ref[3, one, :], nw)
        return carry

    lax.fori_loop(0, bs, body, 0, unroll=min(bs, GDN_STEP_UNROLL))
    y_ref[...] = yacc_ref[...].astype(y_ref.dtype)


def _gdn_step(qkvz, gate, norm_w, state, *, bs=32):
    b = qkvz.shape[2]
    bs = min(bs, b)
    assert b % bs == 0
    row_scratch = pltpu.VMEM((bs, LANES), F32)
    return pl.pallas_call(
        _gdn_step_body,
        grid=(GDN_HEADS, b // bs),
        in_specs=[
            pl.BlockSpec((4, None, bs, LANES), lambda h, i: (0, h, i, 0)),
            pl.BlockSpec((bs, LANES), lambda h, i: (i, 0)),
            pl.BlockSpec((1, GDN_DV), lambda h, i: (0, 0)),
            pl.BlockSpec((bs, None, GDN_DK, GDN_DV), lambda h, i: (i, h, 0, 0)),
        ],
        out_specs=[
            pl.BlockSpec((bs, GDN_DV), lambda h, i: (i, h)),
            pl.BlockSpec((bs, None, GDN_DK, GDN_DV), lambda h, i: (i, h, 0, 0)),
        ],
        out_shape=[
            jax.ShapeDtypeStruct((b, D_GDN_V), BF16),
            jax.ShapeDtypeStruct(state.shape, F32),
        ],
        scratch_shapes=[row_scratch, row_scratch, row_scratch],
        compiler_params=_cparams("parallel", "parallel"),
        name="gdn_step",
    )(qkvz, gate, norm_w.reshape(1, GDN_DV), state)


def _gate_param_row(p):
    return jnp.zeros((1, LANES), F32).at[0, LANE_G:LANE_G + GDN_HEADS].set(p.astype(F32))


def _tail_weight_t(w_in_t):
    o1 = D_GDN_MAIN
    o2 = o1 + 2 * GDN_HEADS
    pad = jnp.zeros((LANES - 2 * GDN_HEADS, w_in_t.shape[1]), w_in_t.dtype)
    return jnp.concatenate([w_in_t[o2:], w_in_t[o1:o2], pad], axis=0)


def _ffn(h, hs, x, xs, state_t, w_up, w_conv, w_down, next_norm, *, final=False, name):
    (hid, keep_a, keep_b), (hid_s, up_a, up_b) = _ffn_up(h, hs, w_up, w_conv, state_t)
    out, out_s = _matmul([hid], [hid_s], w_down, tm=256, residual=x, residual2=xs, norm_w=next_norm,
                         final=final, name=name)
    return out, out_s, jnp.concatenate([keep_a, keep_b], axis=1), jnp.concatenate([up_a, up_b], axis=1)


def kernel(x_prompt, x_sample, state_l0_pool, cache_l0_k, cache_l0_v, state_l0_ffn_conv, state_l1_gdn_conv, state_l1_gdn_S, state_l1_sconv, state_l1_ffn_conv, l0_norm_mix, l0_w_in, l0_pool_w, l0_pool_scale, l0_sinks, l0_w_out, l0_norm_ffn, l0_ffn_w_up, l0_ffn_conv, l0_ffn_w_down, l1_norm_mix, l1_w_in, l1_gdn_conv, l1_gdn_A_log, l1_gdn_dt_bias, l1_gdn_norm, l1_sconv_w, l1_w_out, l1_norm_ffn, l1_ffn_w_up, l1_ffn_conv, l1_ffn_w_down, final_norm):
    bp, t, d = x_prompt.shape
    nb, ts = x_sample.shape[:2]
    wb = cache_l0_k.shape[1]
    assert bp == 1 and ts == 1 and d == D_MODEL and wb == WINDOW and t >= WINDOW
    w_in1_t = l1_w_in.T
    w_tail_t = _tail_weight_t(w_in1_t)
    a_log_row = _gate_param_row(l1_gdn_A_log)
    dt_bias_row = _gate_param_row(l1_gdn_dt_bias)
    pool_t = state_l0_pool.transpose(1, 0, 2)
    kt, vt = cache_l0_k.transpose(0, 2, 3, 1), cache_l0_v.transpose(0, 2, 3, 1)
    ffn0_t = state_l0_ffn_conv.transpose(1, 0, 2)
    gconv_t = state_l1_gdn_conv.transpose(1, 0, 2)
    sconv_t = state_l1_sconv.transpose(1, 0, 2)
    ffn1_t = state_l1_ffn_conv.transpose(1, 0, 2)
    last = SUBLANES

    xp, xs = x_prompt[0], x_sample[:, 0]
    z0, z0s = _matmul([_rmsnorm(xp, l0_norm_mix, BF16)], [_rmsnorm(xs, l0_norm_mix, BF16)], l0_w_in,
                      tm=256, name="in_proj0")
    mix = _mixer0_seq(z0, l0_pool_w, l0_pool_scale, l0_sinks)
    y_pool = _pool_step(z0s, pool_t, l0_pool_w, l0_pool_scale)
    att, kt_new, vt_new = _attn_step(z0s, kt, vt, l0_sinks)
    mix_s = jnp.concatenate([y_pool, att.astype(BF16)], axis=1)
    (x1, h1), (x1s, h1s) = _matmul([mix], [mix_s], l0_w_out, tm=256, residual=xp, residual2=xs,
                                   norm_w=l0_norm_ffn, name="out_proj0")
    (x2, h2), (x2s, h2s), ffn0_rows, ffn0_new = _ffn(
        h1, h1s, x1, x1s, ffn0_t, l0_ffn_w_up, l0_ffn_conv, l0_ffn_w_down, l1_norm_mix, name="ffn_down0")
    (qkvz, raw), (qkvz_s, raw_s) = _qkvz(h2, h2s, w_in1_t, l1_gdn_conv, gconv_t)
    (ysc, gate, mrows), (ysc_s, gate_s, m_s) = _tail(
        h2, h2s, w_tail_t, l1_sconv_w, a_log_row, dt_bias_row, sconv_t)
    ygdn, s_fin = _gdn_seq(qkvz, gate, l1_gdn_norm)
    ygdn_s, s_new = _gdn_step(qkvz_s, gate_s, l1_gdn_norm, state_l1_gdn_S)
    (x3, h3), (x3s, h3s) = _matmul([ygdn, ysc], [ygdn_s, ysc_s], l1_w_out, tm=256, residual=x2,
                                   residual2=x2s, norm_w=l1_norm_ffn, name="out_proj1")
    y_prompt, y_sample, ffn1_rows, ffn1_new = _ffn(
        h3, h3s, x3, x3s, ffn1_t, l1_ffn_w_up, l1_ffn_conv, l1_ffn_w_down, final_norm, final=True,
        name="ffn_down1")

    p_pool = z0[t - POOL_BUF:, :D_POOL][None]
    p_k = z0[t - wb:, O_K:O_V].reshape(1, wb, N_KV_HEADS, HEAD_DIM)
    p_v = z0[t - wb:, O_V:].reshape(1, wb, N_KV_HEADS, HEAD_DIM)
    p_ffn0 = ffn0_rows[last - (FFN_CONV - 1):][None]
    p_gconv = raw[last - (GDN_CONV - 1):, :D_GDN_CONV][None]
    p_sconv = mrows[last - (SCONV_W - 1):][None]
    p_ffn1 = ffn1_rows[last - (FFN_CONV - 1):][None]

    def push(state_t, new_row):
        return jnp.concatenate([state_t[1:], new_row[None]], axis=0).transpose(1, 0, 2)

    s_pool = push(pool_t, z0s[:, :D_POOL])
    s_k, s_v = kt_new.transpose(0, 3, 1, 2), vt_new.transpose(0, 3, 1, 2)
    s_ffn0 = push(ffn0_t, ffn0_new)
    s_gconv = push(gconv_t, raw_s[:, :D_GDN_CONV])
    s_sconv = push(sconv_t, m_s)
    s_ffn1 = push(ffn1_t, ffn1_new)
    return (y_prompt[None], y_sample[:, None], p_pool, s_pool, p_k, s_k, p_v, s_v, p_ffn0, s_ffn0,
            p_gconv, s_gconv, s_fin[None], s_new, p_sconv, s_sconv, p_ffn1, s_ffn1)
```

```python
import functools

import jax
import jax.numpy as jnp
from jax import lax
from jax.experimental import pallas as pl
from jax.experimental.pallas import tpu as pltpu

F32 = jnp.float32
BF16 = jnp.bfloat16
EPS = 1e-6
NEG = -1e30

LANES = 128
SUBLANES = 8
MXU_COLS = 256
VMEM_LIMIT_BYTES = 60 * 1024 * 1024

D_MODEL = 2048
D_POOL = 512
POOL_WINDOWS = (2, 4, 8, 16)
POOL_BUF = max(POOL_WINDOWS) - 1
POOL_HIST = 16
HEAD_DIM = 64
N_Q_HEADS = 24
N_KV_HEADS = 4
Q_PER_KV = N_Q_HEADS // N_KV_HEADS
WINDOW = 128
D_ATTN = N_Q_HEADS * HEAD_DIM
D_KV = N_KV_HEADS * HEAD_DIM
D_IN0 = D_POOL + D_ATTN + 2 * D_KV
O_K = D_POOL + D_ATTN
O_V = O_K + D_KV
GDN_HEADS = 12
GDN_DK = 128
GDN_DV = 128
D_GDN_K = GDN_HEADS * GDN_DK
D_GDN_V = GDN_HEADS * GDN_DV
D_GDN_CONV = 2 * D_GDN_K + D_GDN_V
D_GDN_MAIN = D_GDN_CONV + D_GDN_V
GDN_CONV = 4
GDN_CHUNK = 64
GDN_STEP_UNROLL = 16
D_SCONV = 512
SCONV_W = 3
D_TAIL = 3 * D_SCONV + LANES
D_FF = 5632
FFN_CONV = 3
LANE_BETA = 0
LANE_G = GDN_HEADS
LANE_EG = 2 * GDN_HEADS


def _cparams(*sem):
    return pltpu.CompilerParams(dimension_semantics=sem, vmem_limit_bytes=VMEM_LIMIT_BYTES)


def _dot(a, b):
    return jnp.dot(a, b, preferred_element_type=F32)


def _dot_nt(a, b):
    return lax.dot_general(a, b, (((1,), (1,)), ((), ())), preferred_element_type=F32)


def _silu(x):
    return x * jax.nn.sigmoid(x)


def _cast_rows(w_ref, wbf_ref, chunk=256):
    k = w_ref.shape[0]
    chunk = min(chunk, k)
    assert k % chunk == 0

    def body(c, carry):
        r = pl.multiple_of(c * chunk, chunk)
        wbf_ref[pl.ds(r, chunk), :] = w_ref[pl.ds(r, chunk), :].astype(BF16)
        return carry

    lax.fori_loop(0, k // chunk, body, 0)


def _cast_transposed(wt_ref, wbf_ref):
    n = wt_ref.shape[0]
    step = MXU_COLS if n % MXU_COLS == 0 else LANES
    for c in range(0, n, step):
        wbf_ref[:, c:c + step] = wt_ref[c:c + step, :].T.astype(BF16)


def _shift_rows(u, prev8, k):
    rolled = pltpu.roll(u, k, 0)
    head = pltpu.roll(jnp.concatenate([prev8, u[0:SUBLANES]], axis=0), k, 0)[SUBLANES:2 * SUBLANES]
    return jnp.concatenate([head, rolled[SUBLANES:]], axis=0)


def _rmsnorm_body(x_ref, w_ref, o_ref):
    x = x_ref[...]
    ms = jnp.mean(x * x, axis=-1, keepdims=True)
    o_ref[...] = (x * lax.rsqrt(ms + EPS) * w_ref[...]).astype(o_ref.dtype)


def _rmsnorm(x, w, out_dtype):
    m, d = x.shape
    tm = min(m, 512)
    assert m % tm == 0
    return pl.pallas_call(
        _rmsnorm_body,
        grid=(m // tm,),
        in_specs=[pl.BlockSpec((tm, d), lambda i: (i, 0)), pl.BlockSpec((1, d), lambda i: (0, 0))],
        out_specs=pl.BlockSpec((tm, d), lambda i: (i, 0)),
        out_shape=jax.ShapeDtypeStruct((m, d), out_dtype),
        compiler_params=_cparams("parallel"),
        name="rmsnorm",
    )(x, w.reshape(1, d))


def _mm_body(n_pieces, n_stage, mode, *refs):
    n_out = {"plain": 1, "residual": 2, "final": 1}[mode]
    n_in = n_pieces + (0 if mode == "plain" else 1)
    set_a, set_b = refs[:n_in], refs[n_in:2 * n_in]
    w_ref = refs[2 * n_in]
    pos = 2 * n_in + 1
    nw_ref = None
    if mode != "plain":
        nw_ref = refs[pos]
        pos += 1
    out_a, out_b = refs[pos:pos + n_out], refs[pos + n_out:pos + 2 * n_out]
    wbf_ref = refs[pos + 2 * n_out]
    s = pl.program_id(0)
    last = pl.num_programs(0) - 1
    ck = w_ref.shape[0]

    @pl.when(s < n_stage)
    def _():
        _cast_rows(w_ref, wbf_ref.at[pl.ds(pl.multiple_of(s * ck, ck), ck), :])

    def rows(ins, outs):
        acc, row = None, 0
        for x_ref in ins[:n_pieces]:
            kp = x_ref.shape[1]
            d = _dot(x_ref[...], wbf_ref[row:row + kp, :])
            acc = d if acc is None else acc + d
            row += kp
        if mode == "plain":
            outs[0][...] = acc
            return
        xn = acc + ins[n_pieces][...]
        hn = xn * lax.rsqrt(jnp.mean(xn * xn, axis=-1, keepdims=True) + EPS) * nw_ref[...]
        if mode == "residual":
            outs[0][...] = xn
        outs[-1][...] = hn.astype(outs[-1].dtype)

    @pl.when((s >= n_stage) & (s < last))
    def _():
        rows(set_a, out_a)

    @pl.when(s == last)
    def _():
        rows(set_b, out_b)


def _matmul(xs, xs2, w, *, tm, residual=None, residual2=None, norm_w=None, final=False, chunk=256,
            name="matmul"):
    m, m2 = xs[0].shape[0], xs2[0].shape[0]
    k, n = w.shape
    tm = min(tm, m)
    assert m % tm == 0 and k % chunk == 0 and sum(x.shape[1] for x in xs) == k
    n_stage, nm = k // chunk, m // tm
    mode = "plain" if residual is None else ("final" if final else "residual")

    def row_tile(s):
        return jnp.clip(s - n_stage, 0, nm - 1)

    tile_a = pl.BlockSpec((tm, n), lambda s: (row_tile(s), 0))
    tile_b = pl.BlockSpec((m2, n), lambda s: (0, 0))
    in_a = [pl.BlockSpec((tm, x.shape[1]), lambda s: (row_tile(s), 0)) for x in xs]
    in_b = [pl.BlockSpec((m2, x.shape[1]), lambda s: (0, 0)) for x in xs2]
    args_a, args_b = list(xs), list(xs2)
    if mode != "plain":
        in_a.append(tile_a)
        in_b.append(tile_b)
        args_a.append(residual)
        args_b.append(residual2)
    in_specs = in_a + in_b + [pl.BlockSpec((chunk, n), lambda s: (jnp.minimum(s, n_stage - 1), 0))]
    args = args_a + args_b + [w]
    if mode != "plain":
        in_specs.append(pl.BlockSpec((1, n), lambda s: (0, 0)))
        args.append(norm_w.reshape(1, n))
    dtypes = [F32, BF16] if mode == "residual" else [F32]
    out_specs = [tile_a] * len(dtypes) + [tile_b] * len(dtypes)
    out_shape = ([jax.ShapeDtypeStruct((m, n), dt) for dt in dtypes]
                 + [jax.ShapeDtypeStruct((m2, n), dt) for dt in dtypes])
    outs = pl.pallas_call(
        functools.partial(_mm_body, len(xs), n_stage, mode),
        grid=(n_stage + nm + 1,),
        in_specs=in_specs,
        out_specs=out_specs,
        out_shape=out_shape,
        scratch_shapes=[pltpu.VMEM((k, n), BF16)],
        compiler_params=_cparams("arbitrary"),
        name=name,
    )(*args)
    half = len(dtypes)
    first, second = outs[:half], outs[half:]
    return (first[0], second[0]) if half == 1 else (tuple(first), tuple(second))


def _ffn_up_body(x_ref, xs_ref, wa_ref, wb_ref, cwa_ref, cwb_ref, a2_ref, a1_ref, b2_ref, b1_ref,
                 hid_ref, upa_ref, upb_ref, hids_ref, upsa_ref, upsb_ref, wa_bf, wb_bf, ca_ref, cb_ref):
    tm = x_ref.shape[0]
    i = pl.program_id(1)

    @pl.when(i == 0)
    def _():
        _cast_rows(wa_ref, wa_bf)
        _cast_rows(wb_ref, wb_bf)
        ca_ref[...] = jnp.zeros_like(ca_ref)
        cb_ref[...] = jnp.zeros_like(cb_ref)

    def gate(ua, ub, taps_a, taps_b):
        c_a = cwa_ref[0:1, :] * taps_a[0] + cwa_ref[1:2, :] * taps_a[1] + cwa_ref[2:3, :] * ua
        c_b = cwb_ref[0:1, :] * taps_b[0] + cwb_ref[1:2, :] * taps_b[1] + cwb_ref[2:3, :] * ub
        return (_silu(c_a) * c_b).astype(BF16)

    @pl.when(i > 0)
    def _():
        x = x_ref[...]
        ua = _dot(x, wa_bf[...])
        ub = _dot(x, wb_bf[...])
        pa, pb = ca_ref[...], cb_ref[...]
        hid_ref[...] = gate(ua, ub, (_shift_rows(ua, pa, 2), _shift_rows(ua, pa, 1)),
                            (_shift_rows(ub, pb, 2), _shift_rows(ub, pb, 1)))
        ca_ref[...] = ua[tm - SUBLANES:]
        cb_ref[...] = ub[tm - SUBLANES:]
        upa_ref[...] = ua[tm - SUBLANES:]
        upb_ref[...] = ub[tm - SUBLANES:]

    @pl.when(i == 0)
    def _():
        x = xs_ref[...]
        ua = _dot(x, wa_bf[...])
        ub = _dot(x, wb_bf[...])
        hids_ref[...] = gate(ua, ub, (a2_ref[...], a1_ref[...]), (b2_ref[...], b1_ref[...]))
        upsa_ref[...] = ua
        upsb_ref[...] = ub


def _ffn_up(h, hs, w_up, conv_w, state_t, *, tn=512, tm=1024):
    m, k = h.shape
    ms = hs.shape[0]
    tm = min(tm, m)
    nb, nm = D_FF // tn, m // tm
    assert m % tm == 0 and D_FF % tn == 0
    tile = lambda j, i: (jnp.maximum(i - 1, 0), j)
    in_specs = [
        pl.BlockSpec((tm, k), lambda j, i: (jnp.maximum(i - 1, 0), 0)),
        pl.BlockSpec((ms, k), lambda j, i: (0, 0)),
        pl.BlockSpec((k, tn), lambda j, i: (0, j)),
        pl.BlockSpec((k, tn), lambda j, i: (0, j + nb)),
        pl.BlockSpec((FFN_CONV, tn), lambda j, i: (0, j)),
        pl.BlockSpec((FFN_CONV, tn), lambda j, i: (0, j + nb)),
        pl.BlockSpec((None, ms, tn), lambda j, i: (0, 0, j)),
        pl.BlockSpec((None, ms, tn), lambda j, i: (1, 0, j)),
        pl.BlockSpec((None, ms, tn), lambda j, i: (0, 0, j + nb)),
        pl.BlockSpec((None, ms, tn), lambda j, i: (1, 0, j + nb)),
    ]
    keep_spec = pl.BlockSpec((SUBLANES, tn), lambda j, i: (0, j))
    step_spec = pl.BlockSpec((ms, tn), lambda j, i: (0, j))
    outs = pl.pallas_call(
        _ffn_up_body,
        grid=(nb, nm + 1),
        in_specs=in_specs,
        out_specs=[pl.BlockSpec((tm, tn), tile), keep_spec, keep_spec, step_spec, step_spec, step_spec],
        out_shape=[
            jax.ShapeDtypeStruct((m, D_FF), BF16),
            jax.ShapeDtypeStruct((SUBLANES, D_FF), F32),
            jax.ShapeDtypeStruct((SUBLANES, D_FF), F32),
            jax.ShapeDtypeStruct((ms, D_FF), BF16),
            jax.ShapeDtypeStruct((ms, D_FF), F32),
            jax.ShapeDtypeStruct((ms, D_FF), F32),
        ],
        scratch_shapes=[pltpu.VMEM((k, tn), BF16), pltpu.VMEM((k, tn), BF16),
                        pltpu.VMEM((SUBLANES, tn), F32), pltpu.VMEM((SUBLANES, tn), F32)],
        compiler_params=_cparams("arbitrary", "arbitrary"),
        name="ffn_up",
    )(h, hs, w_up, w_up, conv_w, conv_w, state_t, state_t, state_t, state_t)
    return outs[:3], outs[3:]


def _half_lane_pair(x, head_in_high, lo_mask):
    if head_in_high:
        hi = jnp.where(lo_mask, 0.0, x)
        return pltpu.roll(hi, HEAD_DIM, 1), hi
    lo = jnp.where(lo_mask, x, 0.0)
    return lo, pltpu.roll(lo, HEAD_DIM, 1)


def _mixer0_seq_body(sink_ref, z_ref, kvp_ref, up_ref, pw_ref, ps_ref, o_ref):
    n = pl.program_id(0)
    first = n == 0
    w = WINDOW
    lo_mask = lax.broadcasted_iota(jnp.int32, (1, LANES), 1) < HEAD_DIM

    hist = jnp.where(first, 0.0, up_ref[...])
    pos = n * w + lax.broadcasted_iota(jnp.int32, (w, 1), 0)
    for g, win in enumerate(POOL_WINDOWS):
        sl = slice(g * LANES, (g + 1) * LANES)
        e = jnp.concatenate([hist[:, sl], z_ref[:, sl]], axis=0)
        s, sh = e, 1
        while sh < win:
            s = s + pltpu.roll(s, sh, 0)
            sh *= 2
        cnt = jnp.minimum(pos + 1, win).astype(F32)
        diff = s[POOL_HIST:] / cnt - e[POOL_HIST:]
        y = _dot(diff.astype(BF16), pw_ref[g].astype(BF16)) * ps_ref[:, sl]
        o_ref[:, sl] = y.astype(o_ref.dtype)

    row = lax.broadcasted_iota(jnp.int32, (w, 2 * w), 0)
    col = lax.broadcasted_iota(jnp.int32, (w, 2 * w), 1)
    valid = (col >= row) & (col <= row + w) & (jnp.logical_not(first) | (col >= w))
    tiles_per_kv = Q_PER_KV * HEAD_DIM // LANES
    for c in range(D_KV // LANES):
        k2 = jnp.concatenate([kvp_ref[:, c * LANES:(c + 1) * LANES],
                              z_ref[:, O_K + c * LANES:O_K + (c + 1) * LANES]], axis=0)
        v2 = jnp.concatenate([kvp_ref[:, D_KV + c * LANES:D_KV + (c + 1) * LANES],
                              z_ref[:, O_V + c * LANES:O_V + (c + 1) * LANES]], axis=0)
        for p in range(2):
            hk = 2 * c + p
            k_lo, k_hi = (t.astype(BF16) for t in _half_lane_pair(k2, p == 1, lo_mask))
            v_lo, v_hi = (t.astype(BF16) for t in _half_lane_pair(v2, p == 1, lo_mask))
            q0 = D_POOL + hk * Q_PER_KV * HEAD_DIM
            qst = jnp.concatenate(
                [z_ref[:, q0 + a * LANES:q0 + (a + 1) * LANES] for a in range(tiles_per_kv)], axis=0)
            qst = (qst * HEAD_DIM ** -0.5).astype(BF16)
            s_lo = _dot_nt(qst, k_lo)
            s_hi = _dot_nt(qst, k_hi)
            for a in range(tiles_per_kv):
                probs, inv = [], []
                for par, s_all in ((0, s_lo), (1, s_hi)):
                    sk = sink_ref[hk * Q_PER_KV + 2 * a + par]
                    s = jnp.where(valid, s_all[a * w:(a + 1) * w], NEG)
                    mx = jnp.maximum(jnp.max(s, axis=-1, keepdims=True), sk)
                    pr = jnp.exp(s - mx)
                    den = jnp.sum(pr, axis=-1, keepdims=True) + jnp.exp(sk - mx)
                    probs.append(pr.astype(BF16))
                    inv.append(1.0 / den)
                o = _dot(probs[0], v_lo) + _dot(probs[1], v_hi)
                o = o * jnp.where(lo_mask, inv[0], inv[1])
                o_ref[:, q0 + a * LANES:q0 + (a + 1) * LANES] = o.astype(o_ref.dtype)


def _mixer0_seq(z0, pool_w, pool_scale, sinks):
    t = z0.shape[0]
    w = WINDOW
    assert t % w == 0
    hist_blocks = w // POOL_HIST
    return pl.pallas_call(
        _mixer0_seq_body,
        grid=(t // w,),
        in_specs=[
            pl.BlockSpec(memory_space=pltpu.SMEM),
            pl.BlockSpec((w, D_IN0), lambda n: (n, 0)),
            pl.BlockSpec((w, 2 * D_KV), lambda n: (jnp.maximum(n - 1, 0), O_K // (2 * D_KV))),
            pl.BlockSpec((POOL_HIST, D_POOL), lambda n: (jnp.maximum(n * hist_blocks - 1, 0), 0)),
            pl.BlockSpec((len(POOL_WINDOWS), LANES, LANES), lambda n: (0, 0, 0)),
            pl.BlockSpec((1, D_POOL), lambda n: (0, 0)),
        ],
        out_specs=pl.BlockSpec((w, D_POOL + D_ATTN), lambda n: (n, 0)),
        out_shape=jax.ShapeDtypeStruct((t, D_POOL + D_ATTN), BF16),
        compiler_params=_cparams("parallel"),
        name="mixer0_seq",
    )(sinks, z0, z0, z0, pool_w, pool_scale.reshape(1, D_POOL))


def _pool_step_body(*refs):
    hist_refs = refs[:POOL_BUF]
    z_ref, pw_ref, ps_ref, o_ref = refs[POOL_BUF:]
    for g, win in enumerate(POOL_WINDOWS):
        sl = slice(g * LANES, (g + 1) * LANES)
        u = z_ref[:, sl]
        s = u
        for r in range(POOL_BUF - (win - 1), POOL_BUF):
            s = s + hist_refs[r][:, sl]
        diff = s / float(win) - u
        y = _dot(diff.astype(BF16), pw_ref[g].astype(BF16)) * ps_ref[:, sl]
        o_ref[:, sl] = y.astype(o_ref.dtype)


def _pool_step(z0, hist_t, pool_w, pool_scale):
    b = z0.shape[0]
    in_specs = [pl.BlockSpec((None, b, D_POOL), functools.partial(lambda i, r: (r, 0, 0), r=r))
                for r in range(POOL_BUF)]
    in_specs += [
        pl.BlockSpec((b, D_POOL), lambda i: (0, 0)),
        pl.BlockSpec((len(POOL_WINDOWS), LANES, LANES), lambda i: (0, 0, 0)),
        pl.BlockSpec((1, D_POOL), lambda i: (0, 0)),
    ]
    return pl.pallas_call(
        _pool_step_body,
        grid=(1,),
        in_specs=in_specs,
        out_specs=pl.BlockSpec((b, D_POOL), lambda i: (0, 0)),
        out_shape=jax.ShapeDtypeStruct((b, D_POOL), BF16),
        compiler_params=_cparams("arbitrary"),
        name="pool_step",
    )(*([hist_t] * POOL_BUF), z0, pool_w, pool_scale.reshape(1, D_POOL))


def _attn_step_body(q_ref, kn_ref, vn_ref, knt_ref, vnt_ref, kt_ref, vt_ref, sink_ref, o_ref, kto_ref, vto_ref):
    bs = q_ref.shape[0]
    newest = lax.broadcasted_iota(jnp.int32, (HEAD_DIM, WINDOW), 1) == WINDOW - 1
    pairs = [(b, hk) for b in range(bs) for hk in range(N_KV_HEADS)]
    n = range(len(pairs))
    kt = [kt_ref[b, hk] for b, hk in pairs]
    vt = [vt_ref[b, hk] for b, hk in pairs]
    q = [q_ref[b, hk] * HEAD_DIM ** -0.5 for b, hk in pairs]
    sk = [sink_ref[hk * Q_PER_KV:(hk + 1) * Q_PER_KV, :] for b, hk in pairs]
    s = [_dot(q[i].astype(BF16), kt[i].astype(BF16)) for i in n]
    s_new = [jnp.sum(q[i] * kn_ref[b, hk:hk + 1, :], axis=-1, keepdims=True) for i, (b, hk) in enumerate(pairs)]
    mx = [jnp.maximum(jnp.maximum(jnp.max(s[i], axis=-1, keepdims=True), s_new[i]), sk[i]) for i in n]
    pr = [jnp.exp(s[i] - mx[i]) for i in n]
    pn = [jnp.exp(s_new[i] - mx[i]) for i in n]
    den = [jnp.sum(pr[i], axis=-1, keepdims=True) + pn[i] + jnp.exp(sk[i] - mx[i]) for i in n]
    o = [_dot_nt(pr[i].astype(BF16), vt[i].astype(BF16)) for i in n]
    for i, (b, hk) in enumerate(pairs):
        o_ref[b, hk] = (o[i] + pn[i] * vn_ref[b, hk:hk + 1, :]) / den[i]
        kto_ref[b, hk] = jnp.where(newest, pltpu.roll(knt_ref[hk], WINDOW - 1 - b, 1),
                                   pltpu.roll(kt[i], WINDOW - 1, 1))
        vto_ref[b, hk] = jnp.where(newest, pltpu.roll(vnt_ref[hk], WINDOW - 1 - b, 1),
                                   pltpu.roll(vt[i], WINDOW - 1, 1))


def _attn_step(z0, kt, vt, sinks, *, bs=8):
    b = z0.shape[0]
    assert b % bs == 0 and kt.shape[3] == WINDOW == LANES and bs <= WINDOW
    nblk = b // bs
    q4 = z0[:, D_POOL:O_K].reshape(b, N_KV_HEADS, Q_PER_KV, HEAD_DIM)
    kn = z0[:, O_K:O_V].reshape(b, N_KV_HEADS, HEAD_DIM)
    vn = z0[:, O_V:].reshape(b, N_KV_HEADS, HEAD_DIM)

    def columns(x):
        xt = x.reshape(nblk, bs, N_KV_HEADS, HEAD_DIM).transpose(0, 2, 3, 1)
        return jnp.pad(xt, ((0, 0), (0, 0), (0, 0), (0, WINDOW - bs)))

    cache_spec = pl.BlockSpec((bs, N_KV_HEADS, HEAD_DIM, WINDOW), lambda i: (i, 0, 0, 0))
    col_spec = pl.BlockSpec((None, N_KV_HEADS, HEAD_DIM, WINDOW), lambda i: (i, 0, 0, 0))
    new_spec = pl.BlockSpec((bs, N_KV_HEADS, HEAD_DIM), lambda i: (i, 0, 0))
    q_spec = pl.BlockSpec((bs, N_KV_HEADS, Q_PER_KV, HEAD_DIM), lambda i: (i, 0, 0, 0))
    att, kto, vto = pl.pallas_call(
        _attn_step_body,
        grid=(nblk,),
        in_specs=[q_spec, new_spec, new_spec, col_spec, col_spec, cache_spec, cache_spec,
                  pl.BlockSpec((N_Q_HEADS, 1), lambda i: (0, 0))],
        out_specs=[q_spec, cache_spec, cache_spec],
        out_shape=[jax.ShapeDtypeStruct(q4.shape, F32), jax.ShapeDtypeStruct(kt.shape, F32),
                   jax.ShapeDtypeStruct(vt.shape, F32)],
        compiler_params=_cparams("parallel"),
        name="attn_step",
    )(q4, kn, vn, columns(kn), columns(vn), kt, vt, sinks.reshape(N_Q_HEADS, 1))
    return att.reshape(b, D_ATTN), kto, vto


def _qkvz_body(x_ref, xs_ref, w_ref, cw_ref, p3_ref, p2_ref, p1_ref,
               o_ref, raw_ref, os_ref, raws_ref, w_bf, carry_ref):
    tm = x_ref.shape[0]
    j = pl.program_id(0)
    i = pl.program_id(1)

    @pl.when(i == 0)
    def _():
        _cast_transposed(w_ref, w_bf)
        carry_ref[...] = jnp.zeros_like(carry_ref)

    def heads(a):
        return [a[:, h * LANES:(h + 1) * LANES] for h in range(GDN_HEADS)]

    def l2norm(a):
        return a * lax.rsqrt(jnp.sum(a * a, axis=-1, keepdims=True) + EPS)

    def finish(u, taps, out_ref):
        def conv_act():
            t3, t2, t1 = taps()
            c = cw_ref[3:4, :] * u
            c = cw_ref[0:1, :] * t3 + cw_ref[1:2, :] * t2 + cw_ref[2:3, :] * t1 + c
            return _silu(c)

        @pl.when(j == 0)
        def _():
            for h, a in enumerate(heads(conv_act())):
                out_ref[h] = l2norm(a) * GDN_DK ** -0.5

        @pl.when(j == 1)
        def _():
            for h, a in enumerate(heads(conv_act())):
                out_ref[h] = l2norm(a)

        @pl.when(j == 2)
        def _():
            for h, a in enumerate(heads(conv_act())):
                out_ref[h] = a

        @pl.when(j == 3)
        def _():
            for h, a in enumerate(heads(u)):
                out_ref[h] = a

    @pl.when(i > 0)
    def _():
        u = _dot(x_ref[...], w_bf[...])
        prev = carry_ref[...]
        carry_ref[...] = u[tm - SUBLANES:]
        raw_ref[...] = u[tm - SUBLANES:]
        finish(u, lambda: tuple(_shift_rows(u, prev, k) for k in (3, 2, 1)), o_ref)

    @pl.when(i == 0)
    def _():
        u = _dot(xs_ref[...], w_bf[...])
        raws_ref[...] = u
        finish(u, lambda: (p3_ref[...], p2_ref[...], p1_ref[...]), os_ref)


def _qkvz(h, hs, w_in_t, conv_w, state_t, *, tm=512):
    m, k = h.shape
    ms = hs.shape[0]
    tm = min(tm, m)
    tn = D_GDN_K
    assert D_GDN_K == D_GDN_V and GDN_DK == LANES and m % tm == 0
    nj, nm = D_GDN_MAIN // tn, m // tm
    in_specs = [
        pl.BlockSpec((tm, k), lambda j, i: (jnp.maximum(i - 1, 0), 0)),
        pl.BlockSpec((ms, k), lambda j, i: (0, 0)),
        pl.BlockSpec((tn, k), lambda j, i: (j, 0)),
        pl.BlockSpec((GDN_CONV, tn), lambda j, i: (0, jnp.minimum(j, 2))),
    ]
    for r in range(GDN_CONV - 1):
        in_specs.append(pl.BlockSpec((None, ms, tn), functools.partial(lambda j, i, r: (r, 0, jnp.minimum(j, 2)), r=r)))
    outs = pl.pallas_call(
        _qkvz_body,
        grid=(nj, nm + 1),
        in_specs=in_specs,
        out_specs=[
            pl.BlockSpec((None, GDN_HEADS, tm, LANES), lambda j, i: (j, 0, jnp.maximum(i - 1, 0), 0)),
            pl.BlockSpec((SUBLANES, tn), lambda j, i: (0, j)),
            pl.BlockSpec((None, GDN_HEADS, ms, LANES), lambda j, i: (j, 0, 0, 0)),
            pl.BlockSpec((ms, tn), lambda j, i: (0, j)),
        ],
        out_shape=[
            jax.ShapeDtypeStruct((nj, GDN_HEADS, m, LANES), F32),
            jax.ShapeDtypeStruct((SUBLANES, D_GDN_MAIN), F32),
            jax.ShapeDtypeStruct((nj, GDN_HEADS, ms, LANES), F32),
            jax.ShapeDtypeStruct((ms, D_GDN_MAIN), F32),
        ],
        scratch_shapes=[pltpu.VMEM((k, tn), BF16), pltpu.VMEM((SUBLANES, tn), F32)],
        compiler_params=_cparams("arbitrary", "arbitrary"),
        name="qkvz",
    )(h, hs, w_in_t, conv_w, state_t, state_t, state_t)
    return outs[:2], outs[2:]


def _tail_body(x_ref, xs_ref, w_ref, cw_ref, alog_ref, dtb_ref, p2_ref, p1_ref,
               ysc_ref, gate_ref, m_ref, yscs_ref, gates_ref, ms_ref, w_bf, carry_ref):
    tm = x_ref.shape[0]
    i = pl.program_id(0)

    @pl.when(i == 0)
    def _():
        _cast_transposed(w_ref, w_bf)
        carry_ref[...] = jnp.zeros_like(carry_ref)

    def finish(z, mm, taps, y_ref, g_ref):
        conv = cw_ref[0:1, :] * taps[0] + cw_ref[1:2, :] * taps[1] + cw_ref[2:3, :] * mm
        y_ref[...] = (z[:, 0:D_SCONV] * conv).astype(y_ref.dtype)
        raw = z[:, 3 * D_SCONV:]
        lane = lax.broadcasted_iota(jnp.int32, raw.shape, 1)
        beta = jax.nn.sigmoid(raw)
        sp = raw + dtb_ref[...]
        softplus = jnp.maximum(sp, 0.0) + jnp.log1p(jnp.exp(-jnp.abs(sp)))
        g = -jnp.exp(alog_ref[...]) * softplus
        eg = pltpu.roll(jnp.exp(g), LANE_EG - LANE_G, 1)
        g_ref[...] = jnp.where(lane < LANE_G, beta, jnp.where(lane < LANE_EG, g, eg))

    @pl.when(i > 0)
    def _():
        z = _dot(x_ref[...], w_bf[...])
        mm = z[:, D_SCONV:2 * D_SCONV] * z[:, 2 * D_SCONV:3 * D_SCONV]
        prev = carry_ref[...]
        carry_ref[...] = mm[tm - SUBLANES:]
        m_ref[...] = mm[tm - SUBLANES:]
        finish(z, mm, (_shift_rows(mm, prev, 2), _shift_rows(mm, prev, 1)), ysc_ref, gate_ref)

    @pl.when(i == 0)
    def _():
        z = _dot(xs_ref[...], w_bf[...])
        mm = z[:, D_SCONV:2 * D_SCONV] * z[:, 2 * D_SCONV:3 * D_SCONV]
        ms_ref[...] = mm
        finish(z, mm, (p2_ref[...], p1_ref[...]), yscs_ref, gates_ref)


def _tail(h, hs, w_tail_t, conv_w, a_log_row, dt_bias_row, state_t, *, tm=512):
    m, k = h.shape
    ms = hs.shape[0]
    tm = min(tm, m)
    assert m % tm == 0
    nm = m // tm
    tile = lambda i: (jnp.maximum(i - 1, 0), 0)
    const = lambda i: (0, 0)
    in_specs = [
        pl.BlockSpec((tm, k), tile),
        pl.BlockSpec((ms, k), const),
        pl.BlockSpec((D_TAIL, k), const),
        pl.BlockSpec((SCONV_W, D_SCONV), const),
        pl.BlockSpec((1, LANES), const),
        pl.BlockSpec((1, LANES), const),
        pl.BlockSpec((None, ms, D_SCONV), lambda i: (0, 0, 0)),
        pl.BlockSpec((None, ms, D_SCONV), lambda i: (1, 0, 0)),
    ]
    outs = pl.pallas_call(
        _tail_body,
        grid=(nm + 1,),
        in_specs=in_specs,
        out_specs=[
            pl.BlockSpec((tm, D_SCONV), tile),
            pl.BlockSpec((tm, LANES), tile),
            pl.BlockSpec((SUBLANES, D_SCONV), const),
            pl.BlockSpec((ms, D_SCONV), const),
            pl.BlockSpec((ms, LANES), const),
            pl.BlockSpec((ms, D_SCONV), const),
        ],
        out_shape=[
            jax.ShapeDtypeStruct((m, D_SCONV), BF16),
            jax.ShapeDtypeStruct((m, LANES), F32),
            jax.ShapeDtypeStruct((SUBLANES, D_SCONV), F32),
            jax.ShapeDtypeStruct((ms, D_SCONV), BF16),
            jax.ShapeDtypeStruct((ms, LANES), F32),
            jax.ShapeDtypeStruct((ms, D_SCONV), F32),
        ],
        scratch_shapes=[pltpu.VMEM((k, D_TAIL), BF16), pltpu.VMEM((SUBLANES, D_SCONV), F32)],
        compiler_params=_cparams("arbitrary"),
        name="tail",
    )(h, hs, w_tail_t, conv_w, a_log_row, dt_bias_row, state_t, state_t)
    return outs[:3], outs[3:]


def _gated_norm(o, zg, nw):
    y = o * lax.rsqrt(jnp.mean(o * o, axis=-1, keepdims=True) + EPS) * nw
    return y * _silu(zg)


def _gdn_seq_body(nc, qkvz_ref, gate_ref, nw_ref, y_ref, sfin_ref, s_ref):
    c = GDN_CHUNK
    n = pl.program_id(0)

    @pl.when(n == 0)
    def _():
        s_ref[...] = jnp.zeros_like(s_ref)

    r = lax.broadcasted_iota(jnp.int32, (c, c), 0)
    cc = lax.broadcasted_iota(jnp.int32, (c, c), 1)
    tri = r >= cc
    strict = r > cc
    ones = jnp.where(tri, 1.0, 0.0).astype(BF16)
    zpad = jnp.zeros((LANES - c, LANES), F32)
    nw = nw_ref[...]
    pairs = [(j, h) for j in range(nc) for h in range(GDN_HEADS)]

    gates, gcs, gcts = [], [], []
    for j in range(nc):
        gate = gate_ref[j * c:(j + 1) * c, :]
        g1 = gate.astype(BF16)
        r1 = gate - g1.astype(F32)
        g2 = r1.astype(BF16)
        g3 = (r1 - g2.astype(F32)).astype(BF16)
        gc = _dot(ones, g1) + _dot(ones, g2) + _dot(ones, g3)
        gates.append(gate)
        gcs.append(gc)
        gcts.append(jnp.concatenate([gc, zpad], axis=0).T)

    def rows(j):
        return slice(j * c, (j + 1) * c)

    gcol = [gcs[j][:, LANE_G + h:LANE_G + h + 1] for j, h in pairs]
    bcol = [gates[j][:, LANE_BETA + h:LANE_BETA + h + 1] for j, h in pairs]
    decay = [jnp.exp(jnp.where(tri, gcol[i] - gcts[j][LANE_G + h:LANE_G + h + 1, 0:c], NEG))
             for i, (j, h) in enumerate(pairs)]
    eg = [jnp.exp(g) for g in gcol]
    k = [qkvz_ref[1, h, rows(j), :] for j, h in pairs]
    kb = [k[i] * bcol[i] for i in range(len(pairs))]
    qd = [qkvz_ref[0, h, rows(j), :] for j, h in pairs]
    kq = [_dot_nt(jnp.concatenate([kb[i], qd[i]], axis=0).astype(BF16), k[i].astype(BF16))
          for i in range(len(pairs))]
    x = [jnp.where(strict, -(kq[i][:c] * decay[i]), 0.0) for i in range(len(pairs))]
    intra = [jnp.where(tri, kq[i][c:] * decay[i], 0.0).astype(BF16) for i in range(len(pairs))]
    x_b = [xi.astype(BF16) for xi in x]
    p = [_dot(xb, xb) for xb in x_b]
    t_off = x
    n_steps = c.bit_length() - 2
    for step in range(n_steps):
        p_b = [pi.astype(BF16) for pi in p]
        if step < n_steps - 1:
            pt = [_dot(jnp.concatenate([p_b[i], t_off[i].astype(BF16)], axis=0), p_b[i]) for i in range(len(pairs))]
            t_off = [t_off[i] + p[i] + pt[i][c:] for i in range(len(pairs))]
            p = [pti[:c] for pti in pt]
        else:
            t_off = [t_off[i] + p[i] + _dot(t_off[i].astype(BF16), p_b[i]) for i in range(len(pairs))]
    rhs = [jnp.concatenate([qkvz_ref[2, h, rows(j), :] * bcol[i], kb[i] * eg[i]], axis=1)
           for i, (j, h) in enumerate(pairs)]
    sol = [rhs[i] + _dot(t_off[i].astype(BF16), rhs[i].astype(BF16)) for i in range(len(pairs))]
    wq = [jnp.concatenate([sol[i][:, GDN_DV:], qd[i] * eg[i]], axis=0).astype(BF16) for i in range(len(pairs))]
    glast = [g[c - 1:c, :] for g in gcol]
    kdt = [jnp.concatenate([k[i] * jnp.exp(glast[i] - gcol[i]), zpad], axis=0).T[:, 0:c].astype(BF16)
           for i in range(len(pairs))]
    ikd = [jnp.concatenate([intra[i], kdt[i]], axis=0) for i in range(len(pairs))]
    g_tot = [jnp.exp(g) for g in glast]

    for j in range(nc):
        idx = [j * GDN_HEADS + h for h in range(GDN_HEADS)]
        s = [s_ref[h] for h in range(GDN_HEADS)]
        ws = [_dot(wq[i], s[h].astype(BF16)) for h, i in enumerate(idx)]
        v_new = [(sol[i][:, :GDN_DV] - ws[h][:c]).astype(BF16) for h, i in enumerate(idx)]
        upd = [_dot(ikd[i], v_new[h]) for h, i in enumerate(idx)]
        for h, i in enumerate(idx):
            s_ref[h] = s[h] * g_tot[i] + upd[h][c:]
            o = ws[h][c:] + upd[h][:c]
            zg = qkvz_ref[3, h, rows(j), :]
            y_ref[rows(j), h * GDN_DV:(h + 1) * GDN_DV] = _gated_norm(o, zg, nw).astype(y_ref.dtype)

    @pl.when(n == pl.num_programs(0) - 1)
    def _():
        sfin_ref[...] = s_ref[...]


def _gdn_seq(qkvz, gate, norm_w, *, nc=2):
    t = qkvz.shape[2]
    c = GDN_CHUNK
    assert t % (nc * c) == 0 and c & (c - 1) == 0
    return pl.pallas_call(
        functools.partial(_gdn_seq_body, nc),
        grid=(t // (nc * c),),
        in_specs=[
            pl.BlockSpec((4, GDN_HEADS, nc * c, LANES), lambda n: (0, 0, n, 0)),
            pl.BlockSpec((nc * c, LANES), lambda n: (n, 0)),
            pl.BlockSpec((1, GDN_DV), lambda n: (0, 0)),
        ],
        out_specs=[
            pl.BlockSpec((nc * c, D_GDN_V), lambda n: (n, 0)),
            pl.BlockSpec((GDN_HEADS, GDN_DK, GDN_DV), lambda n: (0, 0, 0)),
        ],
        out_shape=[
            jax.ShapeDtypeStruct((t, D_GDN_V), BF16),
            jax.ShapeDtypeStruct((GDN_HEADS, GDN_DK, GDN_DV), F32),
        ],
        scratch_shapes=[pltpu.VMEM((GDN_HEADS, GDN_DK, GDN_DV), F32)],
        compiler_params=_cparams("arbitrary"),
        name="gdn_seq",
    )(qkvz, gate, norm_w.reshape(1, GDN_DV))


def _gdn_step_body(qkvz_ref, gate_ref, nw_ref, s_ref, y_ref, so_ref, beta_ref, eg_ref, yacc_ref):
    h = pl.program_id(0)
    bs = s_ref.shape[0]
    gate = gate_ref[...]
    lane = lax.broadcasted_iota(jnp.int32, gate.shape, 1)

    def pick(l):
        col = jnp.sum(jnp.where(lane == l, gate, 0.0), axis=1, keepdims=True)
        return jnp.broadcast_to(col, gate.shape)

    beta_ref[...] = pick(LANE_BETA + h)
    eg_ref[...] = pick(LANE_EG + h)
    nw = nw_ref[...]
    piece_row = lax.broadcasted_iota(jnp.int32, (2 * SUBLANES, LANES), 0)
    ones = jnp.where(piece_row < 3, 1.0, 0.0).astype(BF16)

    def spread(row):
        p1 = row.astype(BF16).astype(F32)
        r1 = row - p1
        p2 = r1.astype(BF16).astype(F32)
        p3 = r1 - p2
        pieces = jnp.where(piece_row == 0, p1, jnp.where(piece_row == 1, p2, jnp.where(piece_row == 2, p3, 0.0)))
        return lax.dot_general(pieces.astype(BF16), ones, (((0,), (0,)), ((), ())), preferred_element_type=F32)

    def body(b, carry):
        one = pl.ds(b, 1)
        qb = spread(qkvz_ref[0, one, :])
        kb = spread(qkvz_ref[1, one, :])
        s = s_ref[b] * eg_ref[one, :]
        kv = jnp.sum(s * kb, axis=0, keepdims=True)
        delta = (qkvz_ref[2, one, :] - kv) * beta_ref[one, :]
        s = s + kb * delta
        so_ref[b] = s
        o = jnp.sum(s * qb, axis=0, keepdims=True)
        yacc_ref[one, :] = _gated_norm(o, qkvz_ref[3, one, :], nw)
        return carry

    lax.fori_loop(0, bs, body, 0, unroll=min(bs, GDN_STEP_UNROLL))
    y_ref[...] = yacc_ref[...].astype(y_ref.dtype)


def _gdn_step(qkvz, gate, norm_w, state, *, bs=32):
    b = qkvz.shape[2]
    bs = min(bs, b)
    assert b % bs == 0
    row_scratch = pltpu.VMEM((bs, LANES), F32)
    return pl.pallas_call(
        _gdn_step_body,
        grid=(GDN_HEADS, b // bs),
        in_specs=[
            pl.BlockSpec((4, None, bs, LANES), lambda h, i: (0, h, i, 0)),
            pl.BlockSpec((bs, LANES), lambda h, i: (i, 0)),
            pl.BlockSpec((1, GDN_DV), lambda h, i: (0, 0)),
            pl.BlockSpec((bs, None, GDN_DK, GDN_DV), lambda h, i: (i, h, 0, 0)),
        ],
        out_specs=[
            pl.BlockSpec((bs, GDN_DV), lambda h, i: (i, h)),
            pl.BlockSpec((bs, None, GDN_DK, GDN_DV), lambda h, i: (i, h, 0, 0)),
        ],
        out_shape=[
            jax.ShapeDtypeStruct((b, D_GDN_V), BF16),
            jax.ShapeDtypeStruct(state.shape, F32),
        ],
        scratch_shapes=[row_scratch, row_scratch, row_scratch],
        compiler_params=_cparams("parallel", "parallel"),
        name="gdn_step",
    )(qkvz, gate, norm_w.reshape(1, GDN_DV), state)


def _gate_param_row(p):
    return jnp.zeros((1, LANES), F32).at[0, LANE_G:LANE_G + GDN_HEADS].set(p.astype(F32))


def _tail_weight_t(w_in_t):
    o1 = D_GDN_MAIN
    o2 = o1 + 2 * GDN_HEADS
    pad = jnp.zeros((LANES - 2 * GDN_HEADS, w_in_t.shape[1]), w_in_t.dtype)
    return jnp.concatenate([w_in_t[o2:], w_in_t[o1:o2], pad], axis=0)


def _ffn(h, hs, x, xs, state_t, w_up, w_conv, w_down, next_norm, *, final=False, name):
    (hid, keep_a, keep_b), (hid_s, up_a, up_b) = _ffn_up(h, hs, w_up, w_conv, state_t)
    out, out_s = _matmul([hid], [hid_s], w_down, tm=256, residual=x, residual2=xs, norm_w=next_norm,
                         final=final, name=name)
    return out, out_s, jnp.concatenate([keep_a, keep_b], axis=1), jnp.concatenate([up_a, up_b], axis=1)


def kernel(x_prompt, x_sample, state_l0_pool, cache_l0_k, cache_l0_v, state_l0_ffn_conv, state_l1_gdn_conv, state_l1_gdn_S, state_l1_sconv, state_l1_ffn_conv, l0_norm_mix, l0_w_in, l0_pool_w, l0_pool_scale, l0_sinks, l0_w_out, l0_norm_ffn, l0_ffn_w_up, l0_ffn_conv, l0_ffn_w_down, l1_norm_mix, l1_w_in, l1_gdn_conv, l1_gdn_A_log, l1_gdn_dt_bias, l1_gdn_norm, l1_sconv_w, l1_w_out, l1_norm_ffn, l1_ffn_w_up, l1_ffn_conv, l1_ffn_w_down, final_norm):
    bp, t, d = x_prompt.shape
    nb, ts = x_sample.shape[:2]
    wb = cache_l0_k.shape[1]
    assert bp == 1 and ts == 1 and d == D_MODEL and wb == WINDOW and t >= WINDOW
    w_in1_t = l1_w_in.T
    w_tail_t = _tail_weight_t(w_in1_t)
    a_log_row = _gate_param_row(l1_gdn_A_log)
    dt_bias_row = _gate_param_row(l1_gdn_dt_bias)
    pool_t = state_l0_pool.transpose(1, 0, 2)
    kt, vt = cache_l0_k.transpose(0, 2, 3, 1), cache_l0_v.transpose(0, 2, 3, 1)
    ffn0_t = state_l0_ffn_conv.transpose(1, 0, 2)
    gconv_t = state_l1_gdn_conv.transpose(1, 0, 2)
    sconv_t = state_l1_sconv.transpose(1, 0, 2)
    ffn1_t = state_l1_ffn_conv.transpose(1, 0, 2)
    last = SUBLANES

    xp, xs = x_prompt[0], x_sample[:, 0]
    z0, z0s = _matmul([_rmsnorm(xp, l0_norm_mix, BF16)], [_rmsnorm(xs, l0_norm_mix, BF16)], l0_w_in,
                      tm=512, name="in_proj0")
    mix = _mixer0_seq(z0, l0_pool_w, l0_pool_scale, l0_sinks)
    y_pool = _pool_step(z0s, pool_t, l0_pool_w, l0_pool_scale)
    att, kt_new, vt_new = _attn_step(z0s, kt, vt, l0_sinks)
    mix_s = jnp.concatenate([y_pool, att.astype(BF16)], axis=1)
    (x1, h1), (x1s, h1s) = _matmul([mix], [mix_s], l0_w_out, tm=512, residual=xp, residual2=xs,
                                   norm_w=l0_norm_ffn, name="out_proj0")
    (x2, h2), (x2s, h2s), ffn0_rows, ffn0_new = _ffn(
        h1, h1s, x1, x1s, ffn0_t, l0_ffn_w_up, l0_ffn_conv, l0_ffn_w_down, l1_norm_mix, name="ffn_down0")
    (qkvz, raw), (qkvz_s, raw_s) = _qkvz(h2, h2s, w_in1_t, l1_gdn_conv, gconv_t)
    (ysc, gate, mrows), (ysc_s, gate_s, m_s) = _tail(
        h2, h2s, w_tail_t, l1_sconv_w, a_log_row, dt_bias_row, sconv_t)
    ygdn, s_fin = _gdn_seq(qkvz, gate, l1_gdn_norm)
    ygdn_s, s_new = _gdn_step(qkvz_s, gate_s, l1_gdn_norm, state_l1_gdn_S)
    (x3, h3), (x3s, h3s) = _matmul([ygdn, ysc], [ygdn_s, ysc_s], l1_w_out, tm=512, residual=x2,
                                   residual2=x2s, norm_w=l1_norm_ffn, name="out_proj1")
    y_prompt, y_sample, ffn1_rows, ffn1_new = _ffn(
        h3, h3s, x3, x3s, ffn1_t, l1_ffn_w_up, l1_ffn_conv, l1_ffn_w_down, final_norm, final=True,
        name="ffn_down1")

    p_pool = z0[t - POOL_BUF:, :D_POOL][None]
    p_k = z0[t - wb:, O_K:O_V].reshape(1, wb, N_KV_HEADS, HEAD_DIM)
    p_v = z0[t - wb:, O_V:].reshape(1, wb, N_KV_HEADS, HEAD_DIM)
    p_ffn0 = ffn0_rows[last - (FFN_CONV - 1):][None]
    p_gconv = raw[last - (GDN_CONV - 1):, :D_GDN_CONV][None]
    p_sconv = mrows[last - (SCONV_W - 1):][None]
    p_ffn1 = ffn1_rows[last - (FFN_CONV - 1):][None]

    def push(state_t, new_row):
        return jnp.concatenate([state_t[1:], new_row[None]], axis=0).transpose(1, 0, 2)

    s_pool = push(pool_t, z0s[:, :D_POOL])
    s_k, s_v = kt_new.transpose(0, 3, 1, 2), vt_new.transpose(0, 3, 1, 2)
    s_ffn0 = push(ffn0_t, ffn0_new)
    s_gconv = push(gconv_t, raw_s[:, :D_GDN_CONV])
    s_sconv = push(sconv_t, m_s)
    s_ffn1 = push(ffn1_t, ffn1_new)
    return (y_prompt[None], y_sample[:, None], p_pool, s_pool, p_k, s_k, p_v, s_v, p_ffn0, s_ffn0,
            p_gconv, s_gconv, s_fin[None], s_new, p_sconv, s_sconv, p_ffn1, s_ffn1)
```

```python
import functools

import jax
import jax.numpy as jnp
from jax import lax
from jax.experimental import pallas as pl
from jax.experimental.pallas import tpu as pltpu

F32 = jnp.float32
BF16 = jnp.bfloat16
EPS = 1e-6
NEG = -1e30

LANES = 128
SUBLANES = 8
MXU_COLS = 256
VMEM_LIMIT_BYTES = 60 * 1024 * 1024

D_MODEL = 2048
D_POOL = 512
POOL_WINDOWS = (2, 4, 8, 16)
POOL_BUF = max(POOL_WINDOWS) - 1
POOL_HIST = 16
HEAD_DIM = 64
N_Q_HEADS = 24
N_KV_HEADS = 4
Q_PER_KV = N_Q_HEADS // N_KV_HEADS
WINDOW = 128
D_ATTN = N_Q_HEADS * HEAD_DIM
D_KV = N_KV_HEADS * HEAD_DIM
D_IN0 = D_POOL + D_ATTN + 2 * D_KV
O_K = D_POOL + D_ATTN
O_V = O_K + D_KV
GDN_HEADS = 12
GDN_DK = 128
GDN_DV = 128
D_GDN_K = GDN_HEADS * GDN_DK
D_GDN_V = GDN_HEADS * GDN_DV
D_GDN_CONV = 2 * D_GDN_K + D_GDN_V
D_GDN_MAIN = D_GDN_CONV + D_GDN_V
GDN_CONV = 4
GDN_CHUNK = 64
D_SCONV = 512
SCONV_W = 3
D_TAIL = 3 * D_SCONV + LANES
D_FF = 5632
FFN_CONV = 3
LANE_BETA = 0
LANE_G = GDN_HEADS
LANE_EG = 2 * GDN_HEADS


def _cparams(*sem):
    return pltpu.CompilerParams(dimension_semantics=sem, vmem_limit_bytes=VMEM_LIMIT_BYTES)


def _dot(a, b):
    return jnp.dot(a, b, preferred_element_type=F32)


def _dot_nt(a, b):
    return lax.dot_general(a, b, (((1,), (1,)), ((), ())), preferred_element_type=F32)


def _silu(x):
    return x * jax.nn.sigmoid(x)


def _cast_rows(w_ref, wbf_ref, chunk=256):
    k = w_ref.shape[0]
    chunk = min(chunk, k)
    assert k % chunk == 0

    def body(c, carry):
        r = pl.multiple_of(c * chunk, chunk)
        wbf_ref[pl.ds(r, chunk), :] = w_ref[pl.ds(r, chunk), :].astype(BF16)
        return carry

    lax.fori_loop(0, k // chunk, body, 0)


def _cast_transposed(wt_ref, wbf_ref):
    n = wt_ref.shape[0]
    step = MXU_COLS if n % MXU_COLS == 0 else LANES
    for c in range(0, n, step):
        wbf_ref[:, c:c + step] = wt_ref[c:c + step, :].T.astype(BF16)


def _shift_rows(u, prev8, k):
    rolled = pltpu.roll(u, k, 0)
    head = pltpu.roll(jnp.concatenate([prev8, u[0:SUBLANES]], axis=0), k, 0)[SUBLANES:2 * SUBLANES]
    return jnp.concatenate([head, rolled[SUBLANES:]], axis=0)


def _rmsnorm_body(x_ref, w_ref, o_ref):
    x = x_ref[...]
    ms = jnp.mean(x * x, axis=-1, keepdims=True)
    o_ref[...] = (x * lax.rsqrt(ms + EPS) * w_ref[...]).astype(o_ref.dtype)


def _rmsnorm(x, w, out_dtype):
    m, d = x.shape
    tm = min(m, 512)
    assert m % tm == 0
    return pl.pallas_call(
        _rmsnorm_body,
        grid=(m // tm,),
        in_specs=[pl.BlockSpec((tm, d), lambda i: (i, 0)), pl.BlockSpec((1, d), lambda i: (0, 0))],
        out_specs=pl.BlockSpec((tm, d), lambda i: (i, 0)),
        out_shape=jax.ShapeDtypeStruct((m, d), out_dtype),
        compiler_params=_cparams("parallel"),
        name="rmsnorm",
    )(x, w.reshape(1, d))


def _mm_body(n_pieces, n_stage, mode, *refs):
    n_out = {"plain": 1, "residual": 2, "final": 1}[mode]
    n_in = n_pieces + (0 if mode == "plain" else 1)
    set_a, set_b = refs[:n_in], refs[n_in:2 * n_in]
    w_ref = refs[2 * n_in]
    pos = 2 * n_in + 1
    nw_ref = None
    if mode != "plain":
        nw_ref = refs[pos]
        pos += 1
    out_a, out_b = refs[pos:pos + n_out], refs[pos + n_out:pos + 2 * n_out]
    wbf_ref = refs[pos + 2 * n_out]
    s = pl.program_id(0)
    last = pl.num_programs(0) - 1
    ck = w_ref.shape[0]

    @pl.when(s < n_stage)
    def _():
        _cast_rows(w_ref, wbf_ref.at[pl.ds(pl.multiple_of(s * ck, ck), ck), :])

    def rows(ins, outs):
        acc, row = None, 0
        for x_ref in ins[:n_pieces]:
            kp = x_ref.shape[1]
            d = _dot(x_ref[...], wbf_ref[row:row + kp, :])
            acc = d if acc is None else acc + d
            row += kp
        if mode == "plain":
            outs[0][...] = acc
            return
        xn = acc + ins[n_pieces][...]
        hn = xn * lax.rsqrt(jnp.mean(xn * xn, axis=-1, keepdims=True) + EPS) * nw_ref[...]
        if mode == "residual":
            outs[0][...] = xn
        outs[-1][...] = hn.astype(outs[-1].dtype)

    @pl.when((s >= n_stage) & (s < last))
    def _():
        rows(set_a, out_a)

    @pl.when(s == last)
    def _():
        rows(set_b, out_b)


def _matmul(xs, xs2, w, *, tm, residual=None, residual2=None, norm_w=None, final=False, chunk=256,
            name="matmul"):
    m, m2 = xs[0].shape[0], xs2[0].shape[0]
    k, n = w.shape
    tm = min(tm, m)
    assert m % tm == 0 and k % chunk == 0 and sum(x.shape[1] for x in xs) == k
    n_stage, nm = k // chunk, m // tm
    mode = "plain" if residual is None else ("final" if final else "residual")

    def row_tile(s):
        return jnp.clip(s - n_stage, 0, nm - 1)

    tile_a = pl.BlockSpec((tm, n), lambda s: (row_tile(s), 0))
    tile_b = pl.BlockSpec((m2, n), lambda s: (0, 0))
    in_a = [pl.BlockSpec((tm, x.shape[1]), lambda s: (row_tile(s), 0)) for x in xs]
    in_b = [pl.BlockSpec((m2, x.shape[1]), lambda s: (0, 0)) for x in xs2]
    args_a, args_b = list(xs), list(xs2)
    if mode != "plain":
        in_a.append(tile_a)
        in_b.append(tile_b)
        args_a.append(residual)
        args_b.append(residual2)
    in_specs = in_a + in_b + [pl.BlockSpec((chunk, n), lambda s: (jnp.minimum(s, n_stage - 1), 0))]
    args = args_a + args_b + [w]
    if mode != "plain":
        in_specs.append(pl.BlockSpec((1, n), lambda s: (0, 0)))
        args.append(norm_w.reshape(1, n))
    dtypes = [F32, BF16] if mode == "residual" else [F32]
    out_specs = [tile_a] * len(dtypes) + [tile_b] * len(dtypes)
    out_shape = ([jax.ShapeDtypeStruct((m, n), dt) for dt in dtypes]
                 + [jax.ShapeDtypeStruct((m2, n), dt) for dt in dtypes])
    outs = pl.pallas_call(
        functools.partial(_mm_body, len(xs), n_stage, mode),
        grid=(n_stage + nm + 1,),
        in_specs=in_specs,
        out_specs=out_specs,
        out_shape=out_shape,
        scratch_shapes=[pltpu.VMEM((k, n), BF16)],
        compiler_params=_cparams("arbitrary"),
        name=name,
    )(*args)
    half = len(dtypes)
    first, second = outs[:half], outs[half:]
    return (first[0], second[0]) if half == 1 else (tuple(first), tuple(second))


def _ffn_up_body(x_ref, xs_ref, wa_ref, wb_ref, cwa_ref, cwb_ref, a2_ref, a1_ref, b2_ref, b1_ref,
                 hid_ref, upa_ref, upb_ref, hids_ref, upsa_ref, upsb_ref, wa_bf, wb_bf, ca_ref, cb_ref):
    tm = x_ref.shape[0]
    i = pl.program_id(1)

    @pl.when(i == 0)
    def _():
        _cast_rows(wa_ref, wa_bf)
        _cast_rows(wb_ref, wb_bf)
        ca_ref[...] = jnp.zeros_like(ca_ref)
        cb_ref[...] = jnp.zeros_like(cb_ref)

    def gate(ua, ub, taps_a, taps_b):
        c_a = cwa_ref[0:1, :] * taps_a[0] + cwa_ref[1:2, :] * taps_a[1] + cwa_ref[2:3, :] * ua
        c_b = cwb_ref[0:1, :] * taps_b[0] + cwb_ref[1:2, :] * taps_b[1] + cwb_ref[2:3, :] * ub
        return (_silu(c_a) * c_b).astype(BF16)

    @pl.when(i > 0)
    def _():
        x = x_ref[...]
        ua = _dot(x, wa_bf[...])
        ub = _dot(x, wb_bf[...])
        pa, pb = ca_ref[...], cb_ref[...]
        hid_ref[...] = gate(ua, ub, (_shift_rows(ua, pa, 2), _shift_rows(ua, pa, 1)),
                            (_shift_rows(ub, pb, 2), _shift_rows(ub, pb, 1)))
        ca_ref[...] = ua[tm - SUBLANES:]
        cb_ref[...] = ub[tm - SUBLANES:]
        upa_ref[...] = ua[tm - SUBLANES:]
        upb_ref[...] = ub[tm - SUBLANES:]

    @pl.when(i == 0)
    def _():
        x = xs_ref[...]
        ua = _dot(x, wa_bf[...])
        ub = _dot(x, wb_bf[...])
        hids_ref[...] = gate(ua, ub, (a2_ref[...], a1_ref[...]), (b2_ref[...], b1_ref[...]))
        upsa_ref[...] = ua
        upsb_ref[...] = ub


def _ffn_up(h, hs, w_up, conv_w, state_t, *, tn=512, tm=1024):
    m, k = h.shape
    ms = hs.shape[0]
    tm = min(tm, m)
    nb, nm = D_FF // tn, m // tm
    assert m % tm == 0 and D_FF % tn == 0
    tile = lambda j, i: (jnp.maximum(i - 1, 0), j)
    in_specs = [
        pl.BlockSpec((tm, k), lambda j, i: (jnp.maximum(i - 1, 0), 0)),
        pl.BlockSpec((ms, k), lambda j, i: (0, 0)),
        pl.BlockSpec((k, tn), lambda j, i: (0, j)),
        pl.BlockSpec((k, tn), lambda j, i: (0, j + nb)),
        pl.BlockSpec((FFN_CONV, tn), lambda j, i: (0, j)),
        pl.BlockSpec((FFN_CONV, tn), lambda j, i: (0, j + nb)),
        pl.BlockSpec((None, ms, tn), lambda j, i: (0, 0, j)),
        pl.BlockSpec((None, ms, tn), lambda j, i: (1, 0, j)),
        pl.BlockSpec((None, ms, tn), lambda j, i: (0, 0, j + nb)),
        pl.BlockSpec((None, ms, tn), lambda j, i: (1, 0, j + nb)),
    ]
    keep_spec = pl.BlockSpec((SUBLANES, tn), lambda j, i: (0, j))
    step_spec = pl.BlockSpec((ms, tn), lambda j, i: (0, j))
    outs = pl.pallas_call(
        _ffn_up_body,
        grid=(nb, nm + 1),
        in_specs=in_specs,
        out_specs=[pl.BlockSpec((tm, tn), tile), keep_spec, keep_spec, step_spec, step_spec, step_spec],
        out_shape=[
            jax.ShapeDtypeStruct((m, D_FF), BF16),
            jax.ShapeDtypeStruct((SUBLANES, D_FF), F32),
            jax.ShapeDtypeStruct((SUBLANES, D_FF), F32),
            jax.ShapeDtypeStruct((ms, D_FF), BF16),
            jax.ShapeDtypeStruct((ms, D_FF), F32),
            jax.ShapeDtypeStruct((ms, D_FF), F32),
        ],
        scratch_shapes=[pltpu.VMEM((k, tn), BF16), pltpu.VMEM((k, tn), BF16),
                        pltpu.VMEM((SUBLANES, tn), F32), pltpu.VMEM((SUBLANES, tn), F32)],
        compiler_params=_cparams("arbitrary", "arbitrary"),
        name="ffn_up",
    )(h, hs, w_up, w_up, conv_w, conv_w, state_t, state_t, state_t, state_t)
    return outs[:3], outs[3:]


def _half_lane_pair(x, head_in_high, lo_mask):
    if head_in_high:
        hi = jnp.where(lo_mask, 0.0, x)
        return pltpu.roll(hi, HEAD_DIM, 1), hi
    lo = jnp.where(lo_mask, x, 0.0)
    return lo, pltpu.roll(lo, HEAD_DIM, 1)


def _mixer0_seq_body(sink_ref, z_ref, kvp_ref, up_ref, pw_ref, ps_ref, o_ref):
    n = pl.program_id(0)
    first = n == 0
    w = WINDOW
    lo_mask = lax.broadcasted_iota(jnp.int32, (1, LANES), 1) < HEAD_DIM

    hist = jnp.where(first, 0.0, up_ref[...])
    pos = n * w + lax.broadcasted_iota(jnp.int32, (w, 1), 0)
    for g, win in enumerate(POOL_WINDOWS):
        sl = slice(g * LANES, (g + 1) * LANES)
        e = jnp.concatenate([hist[:, sl], z_ref[:, sl]], axis=0)
        s, sh = e, 1
        while sh < win:
            s = s + pltpu.roll(s, sh, 0)
            sh *= 2
        cnt = jnp.minimum(pos + 1, win).astype(F32)
        diff = s[POOL_HIST:] / cnt - e[POOL_HIST:]
        y = _dot(diff.astype(BF16), pw_ref[g].astype(BF16)) * ps_ref[:, sl]
        o_ref[:, sl] = y.astype(o_ref.dtype)

    row = lax.broadcasted_iota(jnp.int32, (w, 2 * w), 0)
    col = lax.broadcasted_iota(jnp.int32, (w, 2 * w), 1)
    valid = (col >= row) & (col <= row + w) & (jnp.logical_not(first) | (col >= w))
    tiles_per_kv = Q_PER_KV * HEAD_DIM // LANES
    for c in range(D_KV // LANES):
        k2 = jnp.concatenate([kvp_ref[:, c * LANES:(c + 1) * LANES],
                              z_ref[:, O_K + c * LANES:O_K + (c + 1) * LANES]], axis=0)
        v2 = jnp.concatenate([kvp_ref[:, D_KV + c * LANES:D_KV + (c + 1) * LANES],
                              z_ref[:, O_V + c * LANES:O_V + (c + 1) * LANES]], axis=0)
        for p in range(2):
            hk = 2 * c + p
            k_lo, k_hi = (t.astype(BF16) for t in _half_lane_pair(k2, p == 1, lo_mask))
            v_lo, v_hi = (t.astype(BF16) for t in _half_lane_pair(v2, p == 1, lo_mask))
            q0 = D_POOL + hk * Q_PER_KV * HEAD_DIM
            qst = jnp.concatenate(
                [z_ref[:, q0 + a * LANES:q0 + (a + 1) * LANES] for a in range(tiles_per_kv)], axis=0)
            qst = (qst * HEAD_DIM ** -0.5).astype(BF16)
            s_lo = _dot_nt(qst, k_lo)
            s_hi = _dot_nt(qst, k_hi)
            for a in range(tiles_per_kv):
                probs, inv = [], []
                for par, s_all in ((0, s_lo), (1, s_hi)):
                    sk = sink_ref[hk * Q_PER_KV + 2 * a + par]
                    s = jnp.where(valid, s_all[a * w:(a + 1) * w], NEG)
                    mx = jnp.maximum(jnp.max(s, axis=-1, keepdims=True), sk)
                    pr = jnp.exp(s - mx)
                    den = jnp.sum(pr, axis=-1, keepdims=True) + jnp.exp(sk - mx)
                    probs.append(pr.astype(BF16))
                    inv.append(1.0 / den)
                o = _dot(probs[0], v_lo) + _dot(probs[1], v_hi)
                o = o * jnp.where(lo_mask, inv[0], inv[1])
                o_ref[:, q0 + a * LANES:q0 + (a + 1) * LANES] = o.astype(o_ref.dtype)


def _mixer0_seq(z0, pool_w, pool_scale, sinks):
    t = z0.shape[0]
    w = WINDOW
    assert t % w == 0
    hist_blocks = w // POOL_HIST
    return pl.pallas_call(
        _mixer0_seq_body,
        grid=(t // w,),
        in_specs=[
            pl.BlockSpec(memory_space=pltpu.SMEM),
            pl.BlockSpec((w, D_IN0), lambda n: (n, 0)),
            pl.BlockSpec((w, 2 * D_KV), lambda n: (jnp.maximum(n - 1, 0), O_K // (2 * D_KV))),
            pl.BlockSpec((POOL_HIST, D_POOL), lambda n: (jnp.maximum(n * hist_blocks - 1, 0), 0)),
            pl.BlockSpec((len(POOL_WINDOWS), LANES, LANES), lambda n: (0, 0, 0)),
            pl.BlockSpec((1, D_POOL), lambda n: (0, 0)),
        ],
        out_specs=pl.BlockSpec((w, D_POOL + D_ATTN), lambda n: (n, 0)),
        out_shape=jax.ShapeDtypeStruct((t, D_POOL + D_ATTN), BF16),
        compiler_params=_cparams("parallel"),
        name="mixer0_seq",
    )(sinks, z0, z0, z0, pool_w, pool_scale.reshape(1, D_POOL))


def _pool_step_body(*refs):
    hist_refs = refs[:POOL_BUF]
    z_ref, pw_ref, ps_ref, o_ref = refs[POOL_BUF:]
    for g, win in enumerate(POOL_WINDOWS):
        sl = slice(g * LANES, (g + 1) * LANES)
        u = z_ref[:, sl]
        s = u
        for r in range(POOL_BUF - (win - 1), POOL_BUF):
            s = s + hist_refs[r][:, sl]
        diff = s / float(win) - u
        y = _dot(diff.astype(BF16), pw_ref[g].astype(BF16)) * ps_ref[:, sl]
        o_ref[:, sl] = y.astype(o_ref.dtype)


def _pool_step(z0, hist_t, pool_w, pool_scale):
    b = z0.shape[0]
    in_specs = [pl.BlockSpec((None, b, D_POOL), functools.partial(lambda i, r: (r, 0, 0), r=r))
                for r in range(POOL_BUF)]
    in_specs += [
        pl.BlockSpec((b, D_POOL), lambda i: (0, 0)),
        pl.BlockSpec((len(POOL_WINDOWS), LANES, LANES), lambda i: (0, 0, 0)),
        pl.BlockSpec((1, D_POOL), lambda i: (0, 0)),
    ]
    return pl.pallas_call(
        _pool_step_body,
        grid=(1,),
        in_specs=in_specs,
        out_specs=pl.BlockSpec((b, D_POOL), lambda i: (0, 0)),
        out_shape=jax.ShapeDtypeStruct((b, D_POOL), BF16),
        compiler_params=_cparams("arbitrary"),
        name="pool_step",
    )(*([hist_t] * POOL_BUF), z0, pool_w, pool_scale.reshape(1, D_POOL))


def _attn_step_body(q_ref, kn_ref, vn_ref, knt_ref, vnt_ref, kt_ref, vt_ref, sink_ref, o_ref, kto_ref, vto_ref):
    bs = q_ref.shape[0]
    newest = lax.broadcasted_iota(jnp.int32, (HEAD_DIM, WINDOW), 1) == WINDOW - 1
    pairs = [(b, hk) for b in range(bs) for hk in range(N_KV_HEADS)]
    n = range(len(pairs))
    kt = [kt_ref[b, hk] for b, hk in pairs]
    vt = [vt_ref[b, hk] for b, hk in pairs]
    q = [q_ref[b, hk] * HEAD_DIM ** -0.5 for b, hk in pairs]
    sk = [sink_ref[hk * Q_PER_KV:(hk + 1) * Q_PER_KV, :] for b, hk in pairs]
    s = [_dot(q[i].astype(BF16), kt[i].astype(BF16)) for i in n]
    s_new = [jnp.sum(q[i] * kn_ref[b, hk:hk + 1, :], axis=-1, keepdims=True) for i, (b, hk) in enumerate(pairs)]
    mx = [jnp.maximum(jnp.maximum(jnp.max(s[i], axis=-1, keepdims=True), s_new[i]), sk[i]) for i in n]
    pr = [jnp.exp(s[i] - mx[i]) for i in n]
    pn = [jnp.exp(s_new[i] - mx[i]) for i in n]
    den = [jnp.sum(pr[i], axis=-1, keepdims=True) + pn[i] + jnp.exp(sk[i] - mx[i]) for i in n]
    o = [_dot_nt(pr[i].astype(BF16), vt[i].astype(BF16)) for i in n]
    for i, (b, hk) in enumerate(pairs):
        o_ref[b, hk] = (o[i] + pn[i] * vn_ref[b, hk:hk + 1, :]) / den[i]
        kto_ref[b, hk] = jnp.where(newest, pltpu.roll(knt_ref[hk], WINDOW - 1 - b, 1),
                                   pltpu.roll(kt[i], WINDOW - 1, 1))
        vto_ref[b, hk] = jnp.where(newest, pltpu.roll(vnt_ref[hk], WINDOW - 1 - b, 1),
                                   pltpu.roll(vt[i], WINDOW - 1, 1))


def _attn_step(z0, kt, vt, sinks, *, bs=8):
    b = z0.shape[0]
    assert b % bs == 0 and kt.shape[3] == WINDOW == LANES and bs <= WINDOW
    nblk = b // bs
    q4 = z0[:, D_POOL:O_K].reshape(b, N_KV_HEADS, Q_PER_KV, HEAD_DIM)
    kn = z0[:, O_K:O_V].reshape(b, N_KV_HEADS, HEAD_DIM)
    vn = z0[:, O_V:].reshape(b, N_KV_HEADS, HEAD_DIM)

    def columns(x):
        xt = x.reshape(nblk, bs, N_KV_HEADS, HEAD_DIM).transpose(0, 2, 3, 1)
        return jnp.pad(xt, ((0, 0), (0, 0), (0, 0), (0, WINDOW - bs)))

    cache_spec = pl.BlockSpec((bs, N_KV_HEADS, HEAD_DIM, WINDOW), lambda i: (i, 0, 0, 0))
    col_spec = pl.BlockSpec((None, N_KV_HEADS, HEAD_DIM, WINDOW), lambda i: (i, 0, 0, 0))
    new_spec = pl.BlockSpec((bs, N_KV_HEADS, HEAD_DIM), lambda i: (i, 0, 0))
    q_spec = pl.BlockSpec((bs, N_KV_HEADS, Q_PER_KV, HEAD_DIM), lambda i: (i, 0, 0, 0))
    att, kto, vto = pl.pallas_call(
        _attn_step_body,
        grid=(nblk,),
        in_specs=[q_spec, new_spec, new_spec, col_spec, col_spec, cache_spec, cache_spec,
                  pl.BlockSpec((N_Q_HEADS, 1), lambda i: (0, 0))],
        out_specs=[q_spec, cache_spec, cache_spec],
        out_shape=[jax.ShapeDtypeStruct(q4.shape, F32), jax.ShapeDtypeStruct(kt.shape, F32),
                   jax.ShapeDtypeStruct(vt.shape, F32)],
        compiler_params=_cparams("parallel"),
        name="attn_step",
    )(q4, kn, vn, columns(kn), columns(vn), kt, vt, sinks.reshape(N_Q_HEADS, 1))
    return att.reshape(b, D_ATTN), kto, vto


def _qkvz_body(x_ref, xs_ref, w_ref, cw_ref, p3_ref, p2_ref, p1_ref,
               o_ref, raw_ref, os_ref, raws_ref, w_bf, carry_ref):
    tm = x_ref.shape[0]
    j = pl.program_id(0)
    i = pl.program_id(1)

    @pl.when(i == 0)
    def _():
        _cast_transposed(w_ref, w_bf)
        carry_ref[...] = jnp.zeros_like(carry_ref)

    def heads(a):
        return [a[:, h * LANES:(h + 1) * LANES] for h in range(GDN_HEADS)]

    def l2norm(a):
        return a * lax.rsqrt(jnp.sum(a * a, axis=-1, keepdims=True) + EPS)

    def finish(u, taps, out_ref):
        def conv_act():
            t3, t2, t1 = taps()
            c = cw_ref[3:4, :] * u
            c = cw_ref[0:1, :] * t3 + cw_ref[1:2, :] * t2 + cw_ref[2:3, :] * t1 + c
            return _silu(c)

        @pl.when(j == 0)
        def _():
            for h, a in enumerate(heads(conv_act())):
                out_ref[h] = l2norm(a) * GDN_DK ** -0.5

        @pl.when(j == 1)
        def _():
            for h, a in enumerate(heads(conv_act())):
                out_ref[h] = l2norm(a)

        @pl.when(j == 2)
        def _():
            for h, a in enumerate(heads(conv_act())):
                out_ref[h] = a

        @pl.when(j == 3)
        def _():
            for h, a in enumerate(heads(u)):
                out_ref[h] = a

    @pl.when(i > 0)
    def _():
        u = _dot(x_ref[...], w_bf[...])
        prev = carry_ref[...]
        carry_ref[...] = u[tm - SUBLANES:]
        raw_ref[...] = u[tm - SUBLANES:]
        finish(u, lambda: tuple(_shift_rows(u, prev, k) for k in (3, 2, 1)), o_ref)

    @pl.when(i == 0)
    def _():
        u = _dot(xs_ref[...], w_bf[...])
        raws_ref[...] = u
        finish(u, lambda: (p3_ref[...], p2_ref[...], p1_ref[...]), os_ref)


def _qkvz(h, hs, w_in_t, conv_w, state_t, *, tm=512):
    m, k = h.shape
    ms = hs.shape[0]
    tm = min(tm, m)
    tn = D_GDN_K
    assert D_GDN_K == D_GDN_V and GDN_DK == LANES and m % tm == 0
    nj, nm = D_GDN_MAIN // tn, m // tm
    in_specs = [
        pl.BlockSpec((tm, k), lambda j, i: (jnp.maximum(i - 1, 0), 0)),
        pl.BlockSpec((ms, k), lambda j, i: (0, 0)),
        pl.BlockSpec((tn, k), lambda j, i: (j, 0)),
        pl.BlockSpec((GDN_CONV, tn), lambda j, i: (0, jnp.minimum(j, 2))),
    ]
    for r in range(GDN_CONV - 1):
        in_specs.append(pl.BlockSpec((None, ms, tn), functools.partial(lambda j, i, r: (r, 0, jnp.minimum(j, 2)), r=r)))
    outs = pl.pallas_call(
        _qkvz_body,
        grid=(nj, nm + 1),
        in_specs=in_specs,
        out_specs=[
            pl.BlockSpec((None, GDN_HEADS, tm, LANES), lambda j, i: (j, 0, jnp.maximum(i - 1, 0), 0)),
            pl.BlockSpec((SUBLANES, tn), lambda j, i: (0, j)),
            pl.BlockSpec((None, GDN_HEADS, ms, LANES), lambda j, i: (j, 0, 0, 0)),
            pl.BlockSpec((ms, tn), lambda j, i: (0, j)),
        ],
        out_shape=[
            jax.ShapeDtypeStruct((nj, GDN_HEADS, m, LANES), F32),
            jax.ShapeDtypeStruct((SUBLANES, D_GDN_MAIN), F32),
            jax.ShapeDtypeStruct((nj, GDN_HEADS, ms, LANES), F32),
            jax.ShapeDtypeStruct((ms, D_GDN_MAIN), F32),
        ],
        scratch_shapes=[pltpu.VMEM((k, tn), BF16), pltpu.VMEM((SUBLANES, tn), F32)],
        compiler_params=_cparams("arbitrary", "arbitrary"),
        name="qkvz",
    )(h, hs, w_in_t, conv_w, state_t, state_t, state_t)
    return outs[:2], outs[2:]


def _tail_body(x_ref, xs_ref, w_ref, cw_ref, alog_ref, dtb_ref, p2_ref, p1_ref,
               ysc_ref, gate_ref, m_ref, yscs_ref, gates_ref, ms_ref, w_bf, carry_ref):
    tm = x_ref.shape[0]
    i = pl.program_id(0)

    @pl.when(i == 0)
    def _():
        _cast_transposed(w_ref, w_bf)
        carry_ref[...] = jnp.zeros_like(carry_ref)

    def finish(z, mm, taps, y_ref, g_ref):
        conv = cw_ref[0:1, :] * taps[0] + cw_ref[1:2, :] * taps[1] + cw_ref[2:3, :] * mm
        y_ref[...] = (z[:, 0:D_SCONV] * conv).astype(y_ref.dtype)
        raw = z[:, 3 * D_SCONV:]
        lane = lax.broadcasted_iota(jnp.int32, raw.shape, 1)
        beta = jax.nn.sigmoid(raw)
        sp = raw + dtb_ref[...]
        softplus = jnp.maximum(sp, 0.0) + jnp.log1p(jnp.exp(-jnp.abs(sp)))
        g = -jnp.exp(alog_ref[...]) * softplus
        eg = pltpu.roll(jnp.exp(g), LANE_EG - LANE_G, 1)
        g_ref[...] = jnp.where(lane < LANE_G, beta, jnp.where(lane < LANE_EG, g, eg))

    @pl.when(i > 0)
    def _():
        z = _dot(x_ref[...], w_bf[...])
        mm = z[:, D_SCONV:2 * D_SCONV] * z[:, 2 * D_SCONV:3 * D_SCONV]
        prev = carry_ref[...]
        carry_ref[...] = mm[tm - SUBLANES:]
        m_ref[...] = mm[tm - SUBLANES:]
        finish(z, mm, (_shift_rows(mm, prev, 2), _shift_rows(mm, prev, 1)), ysc_ref, gate_ref)

    @pl.when(i == 0)
    def _():
        z = _dot(xs_ref[...], w_bf[...])
        mm = z[:, D_SCONV:2 * D_SCONV] * z[:, 2 * D_SCONV:3 * D_SCONV]
        ms_ref[...] = mm
        finish(z, mm, (p2_ref[...], p1_ref[...]), yscs_ref, gates_ref)


def _tail(h, hs, w_tail_t, conv_w, a_log_row, dt_bias_row, state_t, *, tm=512):
    m, k = h.shape
    ms = hs.shape[0]
    tm = min(tm, m)
    assert m % tm == 0
    nm = m // tm
    tile = lambda i: (jnp.maximum(i - 1, 0), 0)
    const = lambda i: (0, 0)
    in_specs = [
        pl.BlockSpec((tm, k), tile),
        pl.BlockSpec((ms, k), const),
        pl.BlockSpec((D_TAIL, k), const),
        pl.BlockSpec((SCONV_W, D_SCONV), const),
        pl.BlockSpec((1, LANES), const),
        pl.BlockSpec((1, LANES), const),
        pl.BlockSpec((None, ms, D_SCONV), lambda i: (0, 0, 0)),
        pl.BlockSpec((None, ms, D_SCONV), lambda i: (1, 0, 0)),
    ]
    outs = pl.pallas_call(
        _tail_body,
        grid=(nm + 1,),
        in_specs=in_specs,
        out_specs=[
            pl.BlockSpec((tm, D_SCONV), tile),
            pl.BlockSpec((tm, LANES), tile),
            pl.BlockSpec((SUBLANES, D_SCONV), const),
            pl.BlockSpec((ms, D_SCONV), const),
            pl.BlockSpec((ms, LANES), const),
            pl.BlockSpec((ms, D_SCONV), const),
        ],
        out_shape=[
            jax.ShapeDtypeStruct((m, D_SCONV), BF16),
            jax.ShapeDtypeStruct((m, LANES), F32),
            jax.ShapeDtypeStruct((SUBLANES, D_SCONV), F32),
            jax.ShapeDtypeStruct((ms, D_SCONV), BF16),
            jax.ShapeDtypeStruct((ms, LANES), F32),
            jax.ShapeDtypeStruct((ms, D_SCONV), F32),
        ],
        scratch_shapes=[pltpu.VMEM((k, D_TAIL), BF16), pltpu.VMEM((SUBLANES, D_SCONV), F32)],
        compiler_params=_cparams("arbitrary"),
        name="tail",
    )(h, hs, w_tail_t, conv_w, a_log_row, dt_bias_row, state_t, state_t)
    return outs[:3], outs[3:]


def _gated_norm(o, zg, nw):
    y = o * lax.rsqrt(jnp.mean(o * o, axis=-1, keepdims=True) + EPS) * nw
    return y * _silu(zg)


def _gdn_body(nc, sb, qkvz_ref, gate_ref, nw_ref, qs_ref, gs_ref, st_ref,
              y_ref, sfin_ref, ys_ref, sto_ref, s_ref, yacc_ref):
    c = GDN_CHUNK
    n = pl.program_id(0)
    step_part = _gdn_step_part(n, sb, qs_ref, gs_ref, nw_ref, st_ref, sto_ref, yacc_ref)

    def tick():
        next(step_part, None)

    @pl.when(n == 0)
    def _():
        s_ref[...] = jnp.zeros_like(s_ref)

    r = lax.broadcasted_iota(jnp.int32, (c, c), 0)
    cc = lax.broadcasted_iota(jnp.int32, (c, c), 1)
    tri = r >= cc
    strict = r > cc
    ones = jnp.where(tri, 1.0, 0.0).astype(BF16)
    zpad = jnp.zeros((LANES - c, LANES), F32)
    nw = nw_ref[...]
    pairs = [(j, h) for j in range(nc) for h in range(GDN_HEADS)]

    gates, gcs, gcts = [], [], []
    for j in range(nc):
        gate = gate_ref[j * c:(j + 1) * c, :]
        g1 = gate.astype(BF16)
        r1 = gate - g1.astype(F32)
        g2 = r1.astype(BF16)
        g3 = (r1 - g2.astype(F32)).astype(BF16)
        gc = _dot(ones, g1) + _dot(ones, g2) + _dot(ones, g3)
        gates.append(gate)
        gcs.append(gc)
        gcts.append(jnp.concatenate([gc, zpad], axis=0).T)

    def rows(j):
        return slice(j * c, (j + 1) * c)

    gcol = [gcs[j][:, LANE_G + h:LANE_G + h + 1] for j, h in pairs]
    bcol = [gates[j][:, LANE_BETA + h:LANE_BETA + h + 1] for j, h in pairs]
    decay = [jnp.exp(jnp.where(tri, gcol[i] - gcts[j][LANE_G + h:LANE_G + h + 1, 0:c], NEG))
             for i, (j, h) in enumerate(pairs)]
    eg = [jnp.exp(g) for g in gcol]
    k = [qkvz_ref[1, h, rows(j), :] for j, h in pairs]
    kb = [k[i] * bcol[i] for i in range(len(pairs))]
    qd = [qkvz_ref[0, h, rows(j), :] for j, h in pairs]
    kq = [_dot_nt(jnp.concatenate([kb[i], qd[i]], axis=0).astype(BF16), k[i].astype(BF16))
          for i in range(len(pairs))]
    tick()
    x = [jnp.where(strict, -(kq[i][:c] * decay[i]), 0.0) for i in range(len(pairs))]
    intra = [jnp.where(tri, kq[i][c:] * decay[i], 0.0).astype(BF16) for i in range(len(pairs))]
    x_b = [xi.astype(BF16) for xi in x]
    p = [_dot(xb, xb) for xb in x_b]
    t_off = x
    n_steps = c.bit_length() - 2
    for step in range(n_steps):
        tick()
        p_b = [pi.astype(BF16) for pi in p]
        if step < n_steps - 1:
            pt = [_dot(jnp.concatenate([p_b[i], t_off[i].astype(BF16)], axis=0), p_b[i]) for i in range(len(pairs))]
            t_off = [t_off[i] + p[i] + pt[i][c:] for i in range(len(pairs))]
            p = [pti[:c] for pti in pt]
        else:
            t_off = [t_off[i] + p[i] + _dot(t_off[i].astype(BF16), p_b[i]) for i in range(len(pairs))]
    tick()
    rhs = [jnp.concatenate([qkvz_ref[2, h, rows(j), :] * bcol[i], kb[i] * eg[i]], axis=1)
           for i, (j, h) in enumerate(pairs)]
    sol = [rhs[i] + _dot(t_off[i].astype(BF16), rhs[i].astype(BF16)) for i in range(len(pairs))]
    wq = [jnp.concatenate([sol[i][:, GDN_DV:], qd[i] * eg[i]], axis=0).astype(BF16) for i in range(len(pairs))]
    glast = [g[c - 1:c, :] for g in gcol]
    kdt = [jnp.concatenate([k[i] * jnp.exp(glast[i] - gcol[i]), zpad], axis=0).T[:, 0:c].astype(BF16)
           for i in range(len(pairs))]
    ikd = [jnp.concatenate([intra[i], kdt[i]], axis=0) for i in range(len(pairs))]
    g_tot = [jnp.exp(g) for g in glast]

    for j in range(nc):
        tick()
        idx = [j * GDN_HEADS + h for h in range(GDN_HEADS)]
        s = [s_ref[h] for h in range(GDN_HEADS)]
        ws = [_dot(wq[i], s[h].astype(BF16)) for h, i in enumerate(idx)]
        v_new = [(sol[i][:, :GDN_DV] - ws[h][:c]).astype(BF16) for h, i in enumerate(idx)]
        upd = [_dot(ikd[i], v_new[h]) for h, i in enumerate(idx)]
        for h, i in enumerate(idx):
            s_ref[h] = s[h] * g_tot[i] + upd[h][c:]
            o = ws[h][c:] + upd[h][:c]
            zg = qkvz_ref[3, h, rows(j), :]
            y_ref[rows(j), h * GDN_DV:(h + 1) * GDN_DV] = _gated_norm(o, zg, nw).astype(y_ref.dtype)

    for _ in step_part:
        pass

    @pl.when(n == pl.num_programs(0) - 1)
    def _():
        sfin_ref[...] = s_ref[...]
        for h in range(GDN_HEADS):
            ys_ref[:, h * GDN_DV:(h + 1) * GDN_DV] = yacc_ref[h].astype(ys_ref.dtype)


def _gdn_step_part(n, sb, qs_ref, gs_ref, nw_ref, st_ref, sto_ref, yacc_ref):
    nw = nw_ref[...]
    lane = lax.broadcasted_iota(jnp.int32, (1, LANES), 1)
    piece_row = lax.broadcasted_iota(jnp.int32, (2 * SUBLANES, LANES), 0)
    ones = jnp.where(piece_row < 3, 1.0, 0.0).astype(BF16)

    def spread(row):
        p1 = row.astype(BF16).astype(F32)
        r1 = row - p1
        p2 = r1.astype(BF16).astype(F32)
        p3 = r1 - p2
        pieces = jnp.where(piece_row == 0, p1, jnp.where(piece_row == 1, p2, jnp.where(piece_row == 2, p3, 0.0)))
        return lax.dot_general(pieces.astype(BF16), ones, (((0,), (0,)), ((), ())), preferred_element_type=F32)

    def pick(row, l):
        return jnp.sum(jnp.where(lane == l, row, 0.0), axis=1, keepdims=True)

    pairs = [(b, h) for b in range(sb) for h in range(GDN_HEADS)]
    m = range(len(pairs))
    rows = [pl.ds(n * sb + b, 1) for b in range(sb)]
    grow = [gs_ref[rows[b], :] for b in range(sb)]
    beta = [pick(grow[b], LANE_BETA + h) for b, h in pairs]
    eg = [pick(grow[b], LANE_EG + h) for b, h in pairs]
    yield
    qb = [spread(qs_ref[0, h, rows[b], :]) for b, h in pairs]
    yield
    kb = [spread(qs_ref[1, h, rows[b], :]) for b, h in pairs]
    yield
    s = [st_ref[b, h] * eg[i] for i, (b, h) in enumerate(pairs)]
    yield
    kv = [jnp.sum(s[i] * kb[i], axis=0, keepdims=True) for i in m]
    yield
    delta = [(qs_ref[2, h, rows[b], :] - kv[i]) * beta[i] for i, (b, h) in enumerate(pairs)]
    s = [s[i] + kb[i] * delta[i] for i in m]
    yield
    o = [jnp.sum(s[i] * qb[i], axis=0, keepdims=True) for i in m]
    yield
    for i, (b, h) in enumerate(pairs):
        sto_ref[b, h] = s[i]
        yacc_ref[h, rows[b], :] = _gated_norm(o[i], qs_ref[3, h, rows[b], :], nw)


def _gdn(qkvz, gate, qkvz_s, gate_s, norm_w, state, *, nc=2):
    t, b = qkvz.shape[2], qkvz_s.shape[2]
    c = GDN_CHUNK
    steps = t // (nc * c)
    assert t % (nc * c) == 0 and c & (c - 1) == 0 and b % steps == 0
    sb = b // steps
    state_spec = pl.BlockSpec((sb, GDN_HEADS, GDN_DK, GDN_DV), lambda n: (n, 0, 0, 0))
    const2 = lambda n: (0, 0)
    return pl.pallas_call(
        functools.partial(_gdn_body, nc, sb),
        grid=(steps,),
        in_specs=[
            pl.BlockSpec((4, GDN_HEADS, nc * c, LANES), lambda n: (0, 0, n, 0)),
            pl.BlockSpec((nc * c, LANES), lambda n: (n, 0)),
            pl.BlockSpec((1, GDN_DV), const2),
            pl.BlockSpec((4, GDN_HEADS, b, LANES), lambda n: (0, 0, 0, 0)),
            pl.BlockSpec((b, LANES), const2),
            state_spec,
        ],
        out_specs=[
            pl.BlockSpec((nc * c, D_GDN_V), lambda n: (n, 0)),
            pl.BlockSpec((GDN_HEADS, GDN_DK, GDN_DV), lambda n: (0, 0, 0)),
            pl.BlockSpec((b, D_GDN_V), const2),
            state_spec,
        ],
        out_shape=[
            jax.ShapeDtypeStruct((t, D_GDN_V), BF16),
            jax.ShapeDtypeStruct((GDN_HEADS, GDN_DK, GDN_DV), F32),
            jax.ShapeDtypeStruct((b, D_GDN_V), BF16),
            jax.ShapeDtypeStruct(state.shape, F32),
        ],
        scratch_shapes=[pltpu.VMEM((GDN_HEADS, GDN_DK, GDN_DV), F32), pltpu.VMEM((GDN_HEADS, b, GDN_DV), F32)],
        compiler_params=_cparams("arbitrary"),
        name="gdn",
    )(qkvz, gate, norm_w.reshape(1, GDN_DV), qkvz_s, gate_s, state)


def _gate_param_row(p):
    return jnp.zeros((1, LANES), F32).at[0, LANE_G:LANE_G + GDN_HEADS].set(p.astype(F32))


def _tail_weight_t(w_in_t):
    o1 = D_GDN_MAIN
    o2 = o1 + 2 * GDN_HEADS
    pad = jnp.zeros((LANES - 2 * GDN_HEADS, w_in_t.shape[1]), w_in_t.dtype)
    return jnp.concatenate([w_in_t[o2:], w_in_t[o1:o2], pad], axis=0)


def _ffn(h, hs, x, xs, state_t, w_up, w_conv, w_down, next_norm, *, final=False, name):
    (hid, keep_a, keep_b), (hid_s, up_a, up_b) = _ffn_up(h, hs, w_up, w_conv, state_t)
    out, out_s = _matmul([hid], [hid_s], w_down, tm=256, residual=x, residual2=xs, norm_w=next_norm,
                         final=final, name=name)
    return out, out_s, jnp.concatenate([keep_a, keep_b], axis=1), jnp.concatenate([up_a, up_b], axis=1)


def kernel(x_prompt, x_sample, state_l0_pool, cache_l0_k, cache_l0_v, state_l0_ffn_conv, state_l1_gdn_conv, state_l1_gdn_S, state_l1_sconv, state_l1_ffn_conv, l0_norm_mix, l0_w_in, l0_pool_w, l0_pool_scale, l0_sinks, l0_w_out, l0_norm_ffn, l0_ffn_w_up, l0_ffn_conv, l0_ffn_w_down, l1_norm_mix, l1_w_in, l1_gdn_conv, l1_gdn_A_log, l1_gdn_dt_bias, l1_gdn_norm, l1_sconv_w, l1_w_out, l1_norm_ffn, l1_ffn_w_up, l1_ffn_conv, l1_ffn_w_down, final_norm):
    bp, t, d = x_prompt.shape
    nb, ts = x_sample.shape[:2]
    wb = cache_l0_k.shape[1]
    assert bp == 1 and ts == 1 and d == D_MODEL and wb == WINDOW and t >= WINDOW
    w_in1_t = l1_w_in.T
    w_tail_t = _tail_weight_t(w_in1_t)
    a_log_row = _gate_param_row(l1_gdn_A_log)
    dt_bias_row = _gate_param_row(l1_gdn_dt_bias)
    pool_t = state_l0_pool.transpose(1, 0, 2)
    kt, vt = cache_l0_k.transpose(0, 2, 3, 1), cache_l0_v.transpose(0, 2, 3, 1)
    ffn0_t = state_l0_ffn_conv.transpose(1, 0, 2)
    gconv_t = state_l1_gdn_conv.transpose(1, 0, 2)
    sconv_t = state_l1_sconv.transpose(1, 0, 2)
    ffn1_t = state_l1_ffn_conv.transpose(1, 0, 2)
    last = SUBLANES

    xp, xs = x_prompt[0], x_sample[:, 0]
    z0, z0s = _matmul([_rmsnorm(xp, l0_norm_mix, BF16)], [_rmsnorm(xs, l0_norm_mix, BF16)], l0_w_in,
                      tm=512, name="in_proj0")
    mix = _mixer0_seq(z0, l0_pool_w, l0_pool_scale, l0_sinks)
    y_pool = _pool_step(z0s, pool_t, l0_pool_w, l0_pool_scale)
    att, kt_new, vt_new = _attn_step(z0s, kt, vt, l0_sinks)
    mix_s = jnp.concatenate([y_pool, att.astype(BF16)], axis=1)
    (x1, h1), (x1s, h1s) = _matmul([mix], [mix_s], l0_w_out, tm=512, residual=xp, residual2=xs,
                                   norm_w=l0_norm_ffn, name="out_proj0")
    (x2, h2), (x2s, h2s), ffn0_rows, ffn0_new = _ffn(
        h1, h1s, x1, x1s, ffn0_t, l0_ffn_w_up, l0_ffn_conv, l0_ffn_w_down, l1_norm_mix, name="ffn_down0")
    (qkvz, raw), (qkvz_s, raw_s) = _qkvz(h2, h2s, w_in1_t, l1_gdn_conv, gconv_t)
    (ysc, gate, mrows), (ysc_s, gate_s, m_s) = _tail(
        h2, h2s, w_tail_t, l1_sconv_w, a_log_row, dt_bias_row, sconv_t)
    ygdn, s_fin, ygdn_s, s_new = _gdn(qkvz, gate, qkvz_s, gate_s, l1_gdn_norm, state_l1_gdn_S)
    (x3, h3), (x3s, h3s) = _matmul([ygdn, ysc], [ygdn_s, ysc_s], l1_w_out, tm=512, residual=x2,
                                   residual2=x2s, norm_w=l1_norm_ffn, name="out_proj1")
    y_prompt, y_sample, ffn1_rows, ffn1_new = _ffn(
        h3, h3s, x3, x3s, ffn1_t, l1_ffn_w_up, l1_ffn_conv, l1_ffn_w_down, final_norm, final=True,
        name="ffn_down1")

    p_pool = z0[t - POOL_BUF:, :D_POOL][None]
    p_k = z0[t - wb:, O_K:O_V].reshape(1, wb, N_KV_HEADS, HEAD_DIM)
    p_v = z0[t - wb:, O_V:].reshape(1, wb, N_KV_HEADS, HEAD_DIM)
    p_ffn0 = ffn0_rows[last - (FFN_CONV - 1):][None]
    p_gconv = raw[last - (GDN_CONV - 1):, :D_GDN_CONV][None]
    p_sconv = mrows[last - (SCONV_W - 1):][None]
    p_ffn1 = ffn1_rows[last - (FFN_CONV - 1):][None]

    def push(state_t, new_row):
        return jnp.concatenate([state_t[1:], new_row[None]], axis=0).transpose(1, 0, 2)

    s_pool = push(pool_t, z0s[:, :D_POOL])
    s_k, s_v = kt_new.transpose(0, 3, 1, 2), vt_new.transpose(0, 3, 1, 2)
    s_ffn0 = push(ffn0_t, ffn0_new)
    s_gconv = push(gconv_t, raw_s[:, :D_GDN_CONV])
    s_sconv = push(sconv_t, m_s)
    s_ffn1 = push(ffn1_t, ffn1_new)
    return (y_prompt[None], y_sample[:, None], p_pool, s_pool, p_k, s_k, p_v, s_v, p_ffn0, s_ffn0,
            p_gconv, s_gconv, s_fin[None], s_new, p_sconv, s_sconv, p_ffn1, s_ffn1)
```

```python
import functools

import jax
import jax.numpy as jnp
from jax import lax
from jax.experimental import pallas as pl
from jax.experimental.pallas import tpu as pltpu

F32 = jnp.float32
BF16 = jnp.bfloat16
EPS = 1e-6
NEG = -1e30

LANES = 128
SUBLANES = 8
MXU_COLS = 256
VMEM_LIMIT_BYTES = 60 * 1024 * 1024

D_MODEL = 2048
D_POOL = 512
POOL_WINDOWS = (2, 4, 8, 16)
POOL_BUF = max(POOL_WINDOWS) - 1
POOL_HIST = 16
HEAD_DIM = 64
N_Q_HEADS = 24
N_KV_HEADS = 4
Q_PER_KV = N_Q_HEADS // N_KV_HEADS
WINDOW = 128
D_ATTN = N_Q_HEADS * HEAD_DIM
D_KV = N_KV_HEADS * HEAD_DIM
D_IN0 = D_POOL + D_ATTN + 2 * D_KV
O_K = D_POOL + D_ATTN
O_V = O_K + D_KV
GDN_HEADS = 12
GDN_DK = 128
GDN_DV = 128
D_GDN_K = GDN_HEADS * GDN_DK
D_GDN_V = GDN_HEADS * GDN_DV
D_GDN_CONV = 2 * D_GDN_K + D_GDN_V
D_GDN_MAIN = D_GDN_CONV + D_GDN_V
GDN_CONV = 4
GDN_CHUNK = 64
D_SCONV = 512
SCONV_W = 3
D_TAIL = 3 * D_SCONV + LANES
D_FF = 5632
FFN_CONV = 3
LANE_BETA = 0
LANE_G = GDN_HEADS
LANE_EG = 2 * GDN_HEADS


def _cparams(*sem):
    return pltpu.CompilerParams(dimension_semantics=sem, vmem_limit_bytes=VMEM_LIMIT_BYTES)


def _dot(a, b):
    return jnp.dot(a, b, preferred_element_type=F32)


def _dot_nt(a, b):
    return lax.dot_general(a, b, (((1,), (1,)), ((), ())), preferred_element_type=F32)


def _silu(x):
    return x * jax.nn.sigmoid(x)


def _cast_rows(w_ref, wbf_ref, chunk=256):
    k = w_ref.shape[0]
    chunk = min(chunk, k)
    assert k % chunk == 0

    def body(c, carry):
        r = pl.multiple_of(c * chunk, chunk)
        wbf_ref[pl.ds(r, chunk), :] = w_ref[pl.ds(r, chunk), :].astype(BF16)
        return carry

    lax.fori_loop(0, k // chunk, body, 0)


def _cast_transposed(wt_ref, wbf_ref):
    n = wt_ref.shape[0]
    step = MXU_COLS if n % MXU_COLS == 0 else LANES
    for c in range(0, n, step):
        wbf_ref[:, c:c + step] = wt_ref[c:c + step, :].T.astype(BF16)


def _shift_rows(u, prev8, k):
    rolled = pltpu.roll(u, k, 0)
    head = pltpu.roll(jnp.concatenate([prev8, u[0:SUBLANES]], axis=0), k, 0)[SUBLANES:2 * SUBLANES]
    return jnp.concatenate([head, rolled[SUBLANES:]], axis=0)


def _rmsnorm_body(x_ref, w_ref, o_ref):
    x = x_ref[...]
    ms = jnp.mean(x * x, axis=-1, keepdims=True)
    o_ref[...] = (x * lax.rsqrt(ms + EPS) * w_ref[...]).astype(o_ref.dtype)


def _rmsnorm(x, w, out_dtype):
    m, d = x.shape
    tm = min(m, 512)
    assert m % tm == 0
    return pl.pallas_call(
        _rmsnorm_body,
        grid=(m // tm,),
        in_specs=[pl.BlockSpec((tm, d), lambda i: (i, 0)), pl.BlockSpec((1, d), lambda i: (0, 0))],
        out_specs=pl.BlockSpec((tm, d), lambda i: (i, 0)),
        out_shape=jax.ShapeDtypeStruct((m, d), out_dtype),
        compiler_params=_cparams("parallel"),
        name="rmsnorm",
    )(x, w.reshape(1, d))


def _mm_body(n_pieces, n_stage, mode, *refs):
    n_out = {"plain": 1, "residual": 2, "final": 1}[mode]
    n_in = n_pieces + (0 if mode == "plain" else 1)
    set_a, set_b = refs[:n_in], refs[n_in:2 * n_in]
    w_ref = refs[2 * n_in]
    pos = 2 * n_in + 1
    nw_ref = None
    if mode != "plain":
        nw_ref = refs[pos]
        pos += 1
    out_a, out_b = refs[pos:pos + n_out], refs[pos + n_out:pos + 2 * n_out]
    wbf_ref = refs[pos + 2 * n_out]
    s = pl.program_id(0)
    last = pl.num_programs(0) - 1
    ck = w_ref.shape[0]

    @pl.when(s < n_stage)
    def _():
        _cast_rows(w_ref, wbf_ref.at[pl.ds(pl.multiple_of(s * ck, ck), ck), :])

    def rows(ins, outs):
        acc, row = None, 0
        for x_ref in ins[:n_pieces]:
            kp = x_ref.shape[1]
            d = _dot(x_ref[...], wbf_ref[row:row + kp, :])
            acc = d if acc is None else acc + d
            row += kp
        if mode == "plain":
            outs[0][...] = acc
            return
        xn = acc + ins[n_pieces][...]
        hn = xn * lax.rsqrt(jnp.mean(xn * xn, axis=-1, keepdims=True) + EPS) * nw_ref[...]
        if mode == "residual":
            outs[0][...] = xn
        outs[-1][...] = hn.astype(outs[-1].dtype)

    @pl.when((s >= n_stage) & (s < last))
    def _():
        rows(set_a, out_a)

    @pl.when(s == last)
    def _():
        rows(set_b, out_b)


def _matmul(xs, xs2, w, *, tm, residual=None, residual2=None, norm_w=None, final=False, chunk=256,
            name="matmul"):
    m, m2 = xs[0].shape[0], xs2[0].shape[0]
    k, n = w.shape
    tm = min(tm, m)
    assert m % tm == 0 and k % chunk == 0 and sum(x.shape[1] for x in xs) == k
    n_stage, nm = k // chunk, m // tm
    mode = "plain" if residual is None else ("final" if final else "residual")

    def row_tile(s):
        return jnp.clip(s - n_stage, 0, nm - 1)

    tile_a = pl.BlockSpec((tm, n), lambda s: (row_tile(s), 0))
    tile_b = pl.BlockSpec((m2, n), lambda s: (0, 0))
    in_a = [pl.BlockSpec((tm, x.shape[1]), lambda s: (row_tile(s), 0)) for x in xs]
    in_b = [pl.BlockSpec((m2, x.shape[1]), lambda s: (0, 0)) for x in xs2]
    args_a, args_b = list(xs), list(xs2)
    if mode != "plain":
        in_a.append(tile_a)
        in_b.append(tile_b)
        args_a.append(residual)
        args_b.append(residual2)
    in_specs = in_a + in_b + [pl.BlockSpec((chunk, n), lambda s: (jnp.minimum(s, n_stage - 1), 0))]
    args = args_a + args_b + [w]
    if mode != "plain":
        in_specs.append(pl.BlockSpec((1, n), lambda s: (0, 0)))
        args.append(norm_w.reshape(1, n))
    dtypes = [F32, BF16] if mode == "residual" else [F32]
    out_specs = [tile_a] * len(dtypes) + [tile_b] * len(dtypes)
    out_shape = ([jax.ShapeDtypeStruct((m, n), dt) for dt in dtypes]
                 + [jax.ShapeDtypeStruct((m2, n), dt) for dt in dtypes])
    outs = pl.pallas_call(
        functools.partial(_mm_body, len(xs), n_stage, mode),
        grid=(n_stage + nm + 1,),
        in_specs=in_specs,
        out_specs=out_specs,
        out_shape=out_shape,
        scratch_shapes=[pltpu.VMEM((k, n), BF16)],
        compiler_params=_cparams("arbitrary"),
        name=name,
    )(*args)
    half = len(dtypes)
    first, second = outs[:half], outs[half:]
    return (first[0], second[0]) if half == 1 else (tuple(first), tuple(second))


def _ffn_up_body(x_ref, xs_ref, wa_ref, wb_ref, cwa_ref, cwb_ref, a2_ref, a1_ref, b2_ref, b1_ref,
                 hid_ref, upa_ref, upb_ref, hids_ref, upsa_ref, upsb_ref, wa_bf, wb_bf, ca_ref, cb_ref):
    tm = x_ref.shape[0]
    i = pl.program_id(1)

    @pl.when(i == 0)
    def _():
        _cast_rows(wa_ref, wa_bf)
        _cast_rows(wb_ref, wb_bf)
        ca_ref[...] = jnp.zeros_like(ca_ref)
        cb_ref[...] = jnp.zeros_like(cb_ref)

    def gate(ua, ub, taps_a, taps_b):
        c_a = cwa_ref[0:1, :] * taps_a[0] + cwa_ref[1:2, :] * taps_a[1] + cwa_ref[2:3, :] * ua
        c_b = cwb_ref[0:1, :] * taps_b[0] + cwb_ref[1:2, :] * taps_b[1] + cwb_ref[2:3, :] * ub
        return (_silu(c_a) * c_b).astype(BF16)

    @pl.when(i > 0)
    def _():
        x = x_ref[...]
        ua = _dot(x, wa_bf[...])
        ub = _dot(x, wb_bf[...])
        pa, pb = ca_ref[...], cb_ref[...]
        hid_ref[...] = gate(ua, ub, (_shift_rows(ua, pa, 2), _shift_rows(ua, pa, 1)),
                            (_shift_rows(ub, pb, 2), _shift_rows(ub, pb, 1)))
        ca_ref[...] = ua[tm - SUBLANES:]
        cb_ref[...] = ub[tm - SUBLANES:]
        upa_ref[...] = ua[tm - SUBLANES:]
        upb_ref[...] = ub[tm - SUBLANES:]

    @pl.when(i == 0)
    def _():
        x = xs_ref[...]
        ua = _dot(x, wa_bf[...])
        ub = _dot(x, wb_bf[...])
        hids_ref[...] = gate(ua, ub, (a2_ref[...], a1_ref[...]), (b2_ref[...], b1_ref[...]))
        upsa_ref[...] = ua
        upsb_ref[...] = ub


def _ffn_up(h, hs, w_up, conv_w, state_t, *, tn=512, tm=1024):
    m, k = h.shape
    ms = hs.shape[0]
    tm = min(tm, m)
    nb, nm = D_FF // tn, m // tm
    assert m % tm == 0 and D_FF % tn == 0
    tile = lambda j, i: (jnp.maximum(i - 1, 0), j)
    in_specs = [
        pl.BlockSpec((tm, k), lambda j, i: (jnp.maximum(i - 1, 0), 0)),
        pl.BlockSpec((ms, k), lambda j, i: (0, 0)),
        pl.BlockSpec((k, tn), lambda j, i: (0, j)),
        pl.BlockSpec((k, tn), lambda j, i: (0, j + nb)),
        pl.BlockSpec((FFN_CONV, tn), lambda j, i: (0, j)),
        pl.BlockSpec((FFN_CONV, tn), lambda j, i: (0, j + nb)),
        pl.BlockSpec((None, ms, tn), lambda j, i: (0, 0, j)),
        pl.BlockSpec((None, ms, tn), lambda j, i: (1, 0, j)),
        pl.BlockSpec((None, ms, tn), lambda j, i: (0, 0, j + nb)),
        pl.BlockSpec((None, ms, tn), lambda j, i: (1, 0, j + nb)),
    ]
    keep_spec = pl.BlockSpec((SUBLANES, tn), lambda j, i: (0, j))
    step_spec = pl.BlockSpec((ms, tn), lambda j, i: (0, j))
    outs = pl.pallas_call(
        _ffn_up_body,
        grid=(nb, nm + 1),
        in_specs=in_specs,
        out_specs=[pl.BlockSpec((tm, tn), tile), keep_spec, keep_spec, step_spec, step_spec, step_spec],
        out_shape=[
            jax.ShapeDtypeStruct((m, D_FF), BF16),
            jax.ShapeDtypeStruct((SUBLANES, D_FF), F32),
            jax.ShapeDtypeStruct((SUBLANES, D_FF), F32),
            jax.ShapeDtypeStruct((ms, D_FF), BF16),
            jax.ShapeDtypeStruct((ms, D_FF), F32),
            jax.ShapeDtypeStruct((ms, D_FF), F32),
        ],
        scratch_shapes=[pltpu.VMEM((k, tn), BF16), pltpu.VMEM((k, tn), BF16),
                        pltpu.VMEM((SUBLANES, tn), F32), pltpu.VMEM((SUBLANES, tn), F32)],
        compiler_params=_cparams("arbitrary", "arbitrary"),
        name="ffn_up",
    )(h, hs, w_up, w_up, conv_w, conv_w, state_t, state_t, state_t, state_t)
    return outs[:3], outs[3:]


def _half_lane_pair(x, head_in_high, lo_mask):
    if head_in_high:
        hi = jnp.where(lo_mask, 0.0, x)
        return pltpu.roll(hi, HEAD_DIM, 1), hi
    lo = jnp.where(lo_mask, x, 0.0)
    return lo, pltpu.roll(lo, HEAD_DIM, 1)


def _mixer0_seq_body(sink_ref, z_ref, kvp_ref, up_ref, pw_ref, ps_ref, o_ref):
    n = pl.program_id(0)
    first = n == 0
    w = WINDOW
    lo_mask = lax.broadcasted_iota(jnp.int32, (1, LANES), 1) < HEAD_DIM

    hist = jnp.where(first, 0.0, up_ref[...])
    pos = n * w + lax.broadcasted_iota(jnp.int32, (w, 1), 0)
    for g, win in enumerate(POOL_WINDOWS):
        sl = slice(g * LANES, (g + 1) * LANES)
        e = jnp.concatenate([hist[:, sl], z_ref[:, sl]], axis=0)
        s, sh = e, 1
        while sh < win:
            s = s + pltpu.roll(s, sh, 0)
            sh *= 2
        cnt = jnp.minimum(pos + 1, win).astype(F32)
        diff = s[POOL_HIST:] / cnt - e[POOL_HIST:]
        y = _dot(diff.astype(BF16), pw_ref[g].astype(BF16)) * ps_ref[:, sl]
        o_ref[:, sl] = y.astype(o_ref.dtype)

    row = lax.broadcasted_iota(jnp.int32, (w, 2 * w), 0)
    col = lax.broadcasted_iota(jnp.int32, (w, 2 * w), 1)
    valid = (col >= row) & (col <= row + w) & (jnp.logical_not(first) | (col >= w))
    tiles_per_kv = Q_PER_KV * HEAD_DIM // LANES
    for c in range(D_KV // LANES):
        k2 = jnp.concatenate([kvp_ref[:, c * LANES:(c + 1) * LANES],
                              z_ref[:, O_K + c * LANES:O_K + (c + 1) * LANES]], axis=0)
        v2 = jnp.concatenate([kvp_ref[:, D_KV + c * LANES:D_KV + (c + 1) * LANES],
                              z_ref[:, O_V + c * LANES:O_V + (c + 1) * LANES]], axis=0)
        for p in range(2):
            hk = 2 * c + p
            k_lo, k_hi = (t.astype(BF16) for t in _half_lane_pair(k2, p == 1, lo_mask))
            v_lo, v_hi = (t.astype(BF16) for t in _half_lane_pair(v2, p == 1, lo_mask))
            q0 = D_POOL + hk * Q_PER_KV * HEAD_DIM
            qst = jnp.concatenate(
                [z_ref[:, q0 + a * LANES:q0 + (a + 1) * LANES] for a in range(tiles_per_kv)], axis=0)
            qst = (qst * HEAD_DIM ** -0.5).astype(BF16)
            s_lo = _dot_nt(qst, k_lo)
            s_hi = _dot_nt(qst, k_hi)
            for a in range(tiles_per_kv):
                probs, inv = [], []
                for par, s_all in ((0, s_lo), (1, s_hi)):
                    sk = sink_ref[hk * Q_PER_KV + 2 * a + par]
                    s = jnp.where(valid, s_all[a * w:(a + 1) * w], NEG)
                    mx = jnp.maximum(jnp.max(s, axis=-1, keepdims=True), sk)
                    pr = jnp.exp(s - mx)
                    den = jnp.sum(pr, axis=-1, keepdims=True) + jnp.exp(sk - mx)
                    probs.append(pr.astype(BF16))
                    inv.append(1.0 / den)
                o = _dot(probs[0], v_lo) + _dot(probs[1], v_hi)
                o = o * jnp.where(lo_mask, inv[0], inv[1])
                o_ref[:, q0 + a * LANES:q0 + (a + 1) * LANES] = o.astype(o_ref.dtype)


def _mixer0_seq(z0, pool_w, pool_scale, sinks):
    t = z0.shape[0]
    w = WINDOW
    assert t % w == 0
    hist_blocks = w // POOL_HIST
    return pl.pallas_call(
        _mixer0_seq_body,
        grid=(t // w,),
        in_specs=[
            pl.BlockSpec(memory_space=pltpu.SMEM),
            pl.BlockSpec((w, D_IN0), lambda n: (n, 0)),
            pl.BlockSpec((w, 2 * D_KV), lambda n: (jnp.maximum(n - 1, 0), O_K // (2 * D_KV))),
            pl.BlockSpec((POOL_HIST, D_POOL), lambda n: (jnp.maximum(n * hist_blocks - 1, 0), 0)),
            pl.BlockSpec((len(POOL_WINDOWS), LANES, LANES), lambda n: (0, 0, 0)),
            pl.BlockSpec((1, D_POOL), lambda n: (0, 0)),
        ],
        out_specs=pl.BlockSpec((w, D_POOL + D_ATTN), lambda n: (n, 0)),
        out_shape=jax.ShapeDtypeStruct((t, D_POOL + D_ATTN), BF16),
        compiler_params=_cparams("parallel"),
        name="mixer0_seq",
    )(sinks, z0, z0, z0, pool_w, pool_scale.reshape(1, D_POOL))


def _pool_step_body(*refs):
    hist_refs = refs[:POOL_BUF]
    z_ref, pw_ref, ps_ref, o_ref = refs[POOL_BUF:]
    for g, win in enumerate(POOL_WINDOWS):
        sl = slice(g * LANES, (g + 1) * LANES)
        u = z_ref[:, sl]
        s = u
        for r in range(POOL_BUF - (win - 1), POOL_BUF):
            s = s + hist_refs[r][:, sl]
        diff = s / float(win) - u
        y = _dot(diff.astype(BF16), pw_ref[g].astype(BF16)) * ps_ref[:, sl]
        o_ref[:, sl] = y.astype(o_ref.dtype)


def _pool_step(z0, hist_t, pool_w, pool_scale):
    b = z0.shape[0]
    in_specs = [pl.BlockSpec((None, b, D_POOL), functools.partial(lambda i, r: (r, 0, 0), r=r))
                for r in range(POOL_BUF)]
    in_specs += [
        pl.BlockSpec((b, D_POOL), lambda i: (0, 0)),
        pl.BlockSpec((len(POOL_WINDOWS), LANES, LANES), lambda i: (0, 0, 0)),
        pl.BlockSpec((1, D_POOL), lambda i: (0, 0)),
    ]
    return pl.pallas_call(
        _pool_step_body,
        grid=(1,),
        in_specs=in_specs,
        out_specs=pl.BlockSpec((b, D_POOL), lambda i: (0, 0)),
        out_shape=jax.ShapeDtypeStruct((b, D_POOL), BF16),
        compiler_params=_cparams("arbitrary"),
        name="pool_step",
    )(*([hist_t] * POOL_BUF), z0, pool_w, pool_scale.reshape(1, D_POOL))


def _attn_step_body(q_ref, kn_ref, vn_ref, knt_ref, vnt_ref, kt_ref, vt_ref, sink_ref, o_ref, kto_ref, vto_ref):
    bs = q_ref.shape[0]
    newest = lax.broadcasted_iota(jnp.int32, (HEAD_DIM, WINDOW), 1) == WINDOW - 1
    pairs = [(b, hk) for b in range(bs) for hk in range(N_KV_HEADS)]
    n = range(len(pairs))
    kt = [kt_ref[b, hk] for b, hk in pairs]
    vt = [vt_ref[b, hk] for b, hk in pairs]
    q = [q_ref[b, hk] * HEAD_DIM ** -0.5 for b, hk in pairs]
    sk = [sink_ref[hk * Q_PER_KV:(hk + 1) * Q_PER_KV, :] for b, hk in pairs]
    s = [_dot(q[i].astype(BF16), kt[i].astype(BF16)) for i in n]
    s_new = [jnp.sum(q[i] * kn_ref[b, hk:hk + 1, :], axis=-1, keepdims=True) for i, (b, hk) in enumerate(pairs)]
    mx = [jnp.maximum(jnp.maximum(jnp.max(s[i], axis=-1, keepdims=True), s_new[i]), sk[i]) for i in n]
    pr = [jnp.exp(s[i] - mx[i]) for i in n]
    pn = [jnp.exp(s_new[i] - mx[i]) for i in n]
    den = [jnp.sum(pr[i], axis=-1, keepdims=True) + pn[i] + jnp.exp(sk[i] - mx[i]) for i in n]
    o = [_dot_nt(pr[i].astype(BF16), vt[i].astype(BF16)) for i in n]
    for i, (b, hk) in enumerate(pairs):
        o_ref[b, hk] = (o[i] + pn[i] * vn_ref[b, hk:hk + 1, :]) / den[i]
        kto_ref[b, hk] = jnp.where(newest, pltpu.roll(knt_ref[hk], WINDOW - 1 - b, 1),
                                   pltpu.roll(kt[i], WINDOW - 1, 1))
        vto_ref[b, hk] = jnp.where(newest, pltpu.roll(vnt_ref[hk], WINDOW - 1 - b, 1),
                                   pltpu.roll(vt[i], WINDOW - 1, 1))


def _attn_step(z0, kt, vt, sinks, *, bs=8):
    b = z0.shape[0]
    assert b % bs == 0 and kt.shape[3] == WINDOW == LANES and bs <= WINDOW
    nblk = b // bs
    q4 = z0[:, D_POOL:O_K].reshape(b, N_KV_HEADS, Q_PER_KV, HEAD_DIM)
    kn = z0[:, O_K:O_V].reshape(b, N_KV_HEADS, HEAD_DIM)
    vn = z0[:, O_V:].reshape(b, N_KV_HEADS, HEAD_DIM)

    def columns(x):
        xt = x.reshape(nblk, bs, N_KV_HEADS, HEAD_DIM).transpose(0, 2, 3, 1)
        return jnp.pad(xt, ((0, 0), (0, 0), (0, 0), (0, WINDOW - bs)))

    cache_spec = pl.BlockSpec((bs, N_KV_HEADS, HEAD_DIM, WINDOW), lambda i: (i, 0, 0, 0))
    col_spec = pl.BlockSpec((None, N_KV_HEADS, HEAD_DIM, WINDOW), lambda i: (i, 0, 0, 0))
    new_spec = pl.BlockSpec((bs, N_KV_HEADS, HEAD_DIM), lambda i: (i, 0, 0))
    q_spec = pl.BlockSpec((bs, N_KV_HEADS, Q_PER_KV, HEAD_DIM), lambda i: (i, 0, 0, 0))
    att, kto, vto = pl.pallas_call(
        _attn_step_body,
        grid=(nblk,),
        in_specs=[q_spec, new_spec, new_spec, col_spec, col_spec, cache_spec, cache_spec,
                  pl.BlockSpec((N_Q_HEADS, 1), lambda i: (0, 0))],
        out_specs=[q_spec, cache_spec, cache_spec],
        out_shape=[jax.ShapeDtypeStruct(q4.shape, F32), jax.ShapeDtypeStruct(kt.shape, F32),
                   jax.ShapeDtypeStruct(vt.shape, F32)],
        compiler_params=_cparams("parallel"),
        name="attn_step",
    )(q4, kn, vn, columns(kn), columns(vn), kt, vt, sinks.reshape(N_Q_HEADS, 1))
    return att.reshape(b, D_ATTN), kto, vto


def _qkvz_body(x_ref, xs_ref, w_ref, cw_ref, p3_ref, p2_ref, p1_ref,
               o_ref, raw_ref, os_ref, raws_ref, w_bf, carry_ref):
    tm = x_ref.shape[0]
    j = pl.program_id(0)
    i = pl.program_id(1)

    @pl.when(i == 0)
    def _():
        _cast_transposed(w_ref, w_bf)
        carry_ref[...] = jnp.zeros_like(carry_ref)

    def heads(a):
        return [a[:, h * LANES:(h + 1) * LANES] for h in range(GDN_HEADS)]

    def l2norm(a):
        return a * lax.rsqrt(jnp.sum(a * a, axis=-1, keepdims=True) + EPS)

    def finish(u, taps, out_ref):
        def conv_act():
            t3, t2, t1 = taps()
            c = cw_ref[3:4, :] * u
            c = cw_ref[0:1, :] * t3 + cw_ref[1:2, :] * t2 + cw_ref[2:3, :] * t1 + c
            return _silu(c)

        @pl.when(j == 0)
        def _():
            for h, a in enumerate(heads(conv_act())):
                out_ref[h] = l2norm(a) * GDN_DK ** -0.5

        @pl.when(j == 1)
        def _():
            for h, a in enumerate(heads(conv_act())):
                out_ref[h] = l2norm(a)

        @pl.when(j == 2)
        def _():
            for h, a in enumerate(heads(conv_act())):
                out_ref[h] = a

        @pl.when(j == 3)
        def _():
            for h, a in enumerate(heads(u)):
                out_ref[h] = a

    @pl.when(i > 0)
    def _():
        u = _dot(x_ref[...], w_bf[...])
        prev = carry_ref[...]
        carry_ref[...] = u[tm - SUBLANES:]
        raw_ref[...] = u[tm - SUBLANES:]
        finish(u, lambda: tuple(_shift_rows(u, prev, k) for k in (3, 2, 1)), o_ref)

    @pl.when(i == 0)
    def _():
        u = _dot(xs_ref[...], w_bf[...])
        raws_ref[...] = u
        finish(u, lambda: (p3_ref[...], p2_ref[...], p1_ref[...]), os_ref)


def _qkvz(h, hs, w_in_t, conv_w, state_t, *, tm=512):
    m, k = h.shape
    ms = hs.shape[0]
    tm = min(tm, m)
    tn = D_GDN_K
    assert D_GDN_K == D_GDN_V and GDN_DK == LANES and m % tm == 0
    nj, nm = D_GDN_MAIN // tn, m // tm
    in_specs = [
        pl.BlockSpec((tm, k), lambda j, i: (jnp.maximum(i - 1, 0), 0)),
        pl.BlockSpec((ms, k), lambda j, i: (0, 0)),
        pl.BlockSpec((tn, k), lambda j, i: (j, 0)),
        pl.BlockSpec((GDN_CONV, tn), lambda j, i: (0, jnp.minimum(j, 2))),
    ]
    for r in range(GDN_CONV - 1):
        in_specs.append(pl.BlockSpec((None, ms, tn), functools.partial(lambda j, i, r: (r, 0, jnp.minimum(j, 2)), r=r)))
    outs = pl.pallas_call(
        _qkvz_body,
        grid=(nj, nm + 1),
        in_specs=in_specs,
        out_specs=[
            pl.BlockSpec((None, GDN_HEADS, tm, LANES), lambda j, i: (j, 0, jnp.maximum(i - 1, 0), 0)),
            pl.BlockSpec((SUBLANES, tn), lambda j, i: (0, j)),
            pl.BlockSpec((None, GDN_HEADS, ms, LANES), lambda j, i: (j, 0, 0, 0)),
            pl.BlockSpec((ms, tn), lambda j, i: (0, j)),
        ],
        out_shape=[
            jax.ShapeDtypeStruct((nj, GDN_HEADS, m, LANES), F32),
            jax.ShapeDtypeStruct((SUBLANES, D_GDN_MAIN), F32),
            jax.ShapeDtypeStruct((nj, GDN_HEADS, ms, LANES), F32),
            jax.ShapeDtypeStruct((ms, D_GDN_MAIN), F32),
        ],
        scratch_shapes=[pltpu.VMEM((k, tn), BF16), pltpu.VMEM((SUBLANES, tn), F32)],
        compiler_params=_cparams("arbitrary", "arbitrary"),
        name="qkvz",
    )(h, hs, w_in_t, conv_w, state_t, state_t, state_t)
    return outs[:2], outs[2:]


def _tail_body(x_ref, xs_ref, wa_ref, wb_ref, cw_ref, alog_ref, dtb_ref, p2_ref, p1_ref,
               ysc_ref, gate_ref, m_ref, yscs_ref, gates_ref, ms_ref, w_bf, carry_ref):
    tm = x_ref.shape[0]
    i = pl.program_id(0)

    @pl.when(i == 0)
    def _():
        n_raw = wb_ref.shape[0]
        step = MXU_COLS
        for c in range(0, 3 * D_SCONV, step):
            r0 = n_raw + c
            if r0 + step <= wa_ref.shape[0]:
                blk = wa_ref[r0:r0 + step, :]
            else:
                blk = jnp.concatenate([wa_ref[r0:, :], wb_ref[...]], axis=0)
            w_bf[:, c:c + step] = blk.T.astype(BF16)
        raw = jnp.concatenate([wa_ref[0:n_raw, :], jnp.zeros((LANES - n_raw, wa_ref.shape[1]), F32)], axis=0)
        w_bf[:, 3 * D_SCONV:] = raw.T.astype(BF16)
        carry_ref[...] = jnp.zeros_like(carry_ref)

    def finish(z, mm, taps, y_ref, g_ref):
        conv = cw_ref[0:1, :] * taps[0] + cw_ref[1:2, :] * taps[1] + cw_ref[2:3, :] * mm
        y_ref[...] = (z[:, 0:D_SCONV] * conv).astype(y_ref.dtype)
        raw = z[:, 3 * D_SCONV:]
        lane = lax.broadcasted_iota(jnp.int32, raw.shape, 1)
        beta = jax.nn.sigmoid(raw)
        sp = raw + dtb_ref[...]
        softplus = jnp.maximum(sp, 0.0) + jnp.log1p(jnp.exp(-jnp.abs(sp)))
        g = -jnp.exp(alog_ref[...]) * softplus
        eg = pltpu.roll(jnp.exp(g), LANE_EG - LANE_G, 1)
        g_ref[...] = jnp.where(lane < LANE_G, beta, jnp.where(lane < LANE_EG, g, eg))

    @pl.when(i > 0)
    def _():
        z = _dot(x_ref[...], w_bf[...])
        mm = z[:, D_SCONV:2 * D_SCONV] * z[:, 2 * D_SCONV:3 * D_SCONV]
        prev = carry_ref[...]
        carry_ref[...] = mm[tm - SUBLANES:]
        m_ref[...] = mm[tm - SUBLANES:]
        finish(z, mm, (_shift_rows(mm, prev, 2), _shift_rows(mm, prev, 1)), ysc_ref, gate_ref)

    @pl.when(i == 0)
    def _():
        z = _dot(xs_ref[...], w_bf[...])
        mm = z[:, D_SCONV:2 * D_SCONV] * z[:, 2 * D_SCONV:3 * D_SCONV]
        ms_ref[...] = mm
        finish(z, mm, (p2_ref[...], p1_ref[...]), yscs_ref, gates_ref)


def _tail(h, hs, w_in_t, conv_w, a_log_row, dt_bias_row, state_t, *, tm=512):
    m, k = h.shape
    ms = hs.shape[0]
    tm = min(tm, m)
    n_raw, n_conv = 2 * GDN_HEADS, 3 * D_SCONV
    assert m % tm == 0 and w_in_t.shape[0] == D_GDN_MAIN + n_raw + n_conv
    assert D_GDN_MAIN % n_conv == 0 and (D_GDN_MAIN + n_conv) % n_raw == 0 and n_raw % SUBLANES == 0
    nm = m // tm
    tile = lambda i: (jnp.maximum(i - 1, 0), 0)
    const = lambda i: (0, 0)
    in_specs = [
        pl.BlockSpec((tm, k), tile),
        pl.BlockSpec((ms, k), const),
        pl.BlockSpec((n_conv, k), lambda i: (D_GDN_MAIN // n_conv, 0)),
        pl.BlockSpec((n_raw, k), lambda i: ((D_GDN_MAIN + n_conv) // n_raw, 0)),
        pl.BlockSpec((SCONV_W, D_SCONV), const),
        pl.BlockSpec((1, LANES), const),
        pl.BlockSpec((1, LANES), const),
        pl.BlockSpec((None, ms, D_SCONV), lambda i: (0, 0, 0)),
        pl.BlockSpec((None, ms, D_SCONV), lambda i: (1, 0, 0)),
    ]
    outs = pl.pallas_call(
        _tail_body,
        grid=(nm + 1,),
        in_specs=in_specs,
        out_specs=[
            pl.BlockSpec((tm, D_SCONV), tile),
            pl.BlockSpec((tm, LANES), tile),
            pl.BlockSpec((SUBLANES, D_SCONV), const),
            pl.BlockSpec((ms, D_SCONV), const),
            pl.BlockSpec((ms, LANES), const),
            pl.BlockSpec((ms, D_SCONV), const),
        ],
        out_shape=[
            jax.ShapeDtypeStruct((m, D_SCONV), BF16),
            jax.ShapeDtypeStruct((m, LANES), F32),
            jax.ShapeDtypeStruct((SUBLANES, D_SCONV), F32),
            jax.ShapeDtypeStruct((ms, D_SCONV), BF16),
            jax.ShapeDtypeStruct((ms, LANES), F32),
            jax.ShapeDtypeStruct((ms, D_SCONV), F32),
        ],
        scratch_shapes=[pltpu.VMEM((k, D_TAIL), BF16), pltpu.VMEM((SUBLANES, D_SCONV), F32)],
        compiler_params=_cparams("arbitrary"),
        name="tail",
    )(h, hs, w_in_t, w_in_t, conv_w, a_log_row, dt_bias_row, state_t, state_t)
    return outs[:3], outs[3:]


def _gated_norm(o, zg, nw):
    y = o * lax.rsqrt(jnp.mean(o * o, axis=-1, keepdims=True) + EPS) * nw
    return y * _silu(zg)


def _gdn_body(nc, sb, qkvz_ref, gate_ref, nw_ref, qs_ref, gs_ref, st_ref,
              y_ref, sfin_ref, ys_ref, sto_ref, s_ref, yacc_ref):
    c = GDN_CHUNK
    n = pl.program_id(0)
    step_part = _gdn_step_part(n, sb, qs_ref, gs_ref, nw_ref, st_ref, sto_ref, yacc_ref)

    def tick():
        next(step_part, None)

    @pl.when(n == 0)
    def _():
        s_ref[...] = jnp.zeros_like(s_ref)

    assert 2 * c == LANES and GDN_HEADS % 2 == 0
    r = lax.broadcasted_iota(jnp.int32, (c, LANES), 0)
    lane = lax.broadcasted_iota(jnp.int32, (c, LANES), 1)
    lo = lane < c
    col = jnp.where(lo, lane, lane - c)
    tri = r >= col
    strict = r > col
    rr = lax.broadcasted_iota(jnp.int32, (c, c), 0)
    ones = jnp.where(rr >= lax.broadcasted_iota(jnp.int32, (c, c), 1), 1.0, 0.0).astype(BF16)
    zpad = jnp.zeros((LANES - c, LANES), F32)
    zrows = jnp.zeros((c, LANES), BF16)
    nw = nw_ref[...]
    duos = [(j, m) for j in range(nc) for m in range(GDN_HEADS // 2)]
    nd = range(len(duos))

    def block_diag(a):
        return jnp.concatenate([jnp.where(lo, a, 0.0), jnp.where(lo, 0.0, a)], axis=0).astype(BF16)

    gates, gcs, gcts = [], [], []
    for j in range(nc):
        gate = gate_ref[j * c:(j + 1) * c, :]
        g1 = gate.astype(BF16)
        r1 = gate - g1.astype(F32)
        g2 = r1.astype(BF16)
        g3 = (r1 - g2.astype(F32)).astype(BF16)
        gc = _dot(ones, g1) + _dot(ones, g2) + _dot(ones, g3)
        gates.append(gate)
        gcs.append(gc)
        gcts.append(jnp.concatenate([gc, zpad], axis=0).T)

    def rows(j):
        return slice(j * c, (j + 1) * c)

    def per_head(fn):
        return [[fn(j, 2 * m + e) for e in range(2)] for j, m in duos]

    gcol = per_head(lambda j, h: gcs[j][:, LANE_G + h:LANE_G + h + 1])
    bcol = per_head(lambda j, h: gates[j][:, LANE_BETA + h:LANE_BETA + h + 1])
    grow = [jnp.concatenate([gcts[j][LANE_G + 2 * m:LANE_G + 2 * m + 1, 0:c],
                             gcts[j][LANE_G + 2 * m + 1:LANE_G + 2 * m + 2, 0:c]], axis=1) for j, m in duos]
    decay = [jnp.exp(jnp.where(tri, jnp.where(lo, gcol[i][0], gcol[i][1]) - grow[i], NEG)) for i in nd]
    eg = [[jnp.exp(g) for g in gcol[i]] for i in nd]
    k = per_head(lambda j, h: qkvz_ref[1, h, rows(j), :])
    qd = per_head(lambda j, h: qkvz_ref[0, h, rows(j), :])
    kb = [[k[i][e] * bcol[i][e] for e in range(2)] for i in nd]
    kq = [_dot_nt(jnp.concatenate([kb[i][0], qd[i][0]], axis=0).astype(BF16),
                  jnp.concatenate([k[i][0].astype(BF16), zrows], axis=0))
          + _dot_nt(jnp.concatenate([kb[i][1], qd[i][1]], axis=0).astype(BF16),
                    jnp.concatenate([zrows, k[i][1].astype(BF16)], axis=0)) for i in nd]
    tick()
    x = [jnp.where(strict, -(kq[i][:c] * decay[i]), 0.0) for i in nd]
    intra = [jnp.where(tri, kq[i][c:] * decay[i], 0.0).astype(BF16) for i in nd]
    p = [_dot(x[i].astype(BF16), block_diag(x[i])) for i in nd]
    t_off = x
    n_steps = c.bit_length() - 2
    for step in range(n_steps):
        tick()
        bd = [block_diag(p[i]) for i in nd]
        if step < n_steps - 1:
            pt = [_dot(jnp.concatenate([p[i], t_off[i]], axis=0).astype(BF16), bd[i]) for i in nd]
            t_off = [t_off[i] + p[i] + pt[i][c:] for i in nd]
            p = [pti[:c] for pti in pt]
        else:
            t_off = [t_off[i] + p[i] + _dot(t_off[i].astype(BF16), bd[i]) for i in nd]
    tick()
    rhs = [[jnp.concatenate([qkvz_ref[2, 2 * m + e, rows(j), :] * bcol[i][e], kb[i][e] * eg[i][e]], axis=1)
            for e in range(2)] for i, (j, m) in enumerate(duos)]
    rhs2 = [jnp.concatenate(rhs[i], axis=0).astype(BF16) for i in nd]
    sol = [[rhs[i][0] + _dot(jnp.where(lo, t_off[i], 0.0).astype(BF16), rhs2[i]),
            rhs[i][1] + _dot(jnp.where(lo, 0.0, t_off[i]).astype(BF16), rhs2[i])] for i in nd]
    wq = [[jnp.concatenate([sol[i][e][:, GDN_DV:], qd[i][e] * eg[i][e]], axis=0).astype(BF16) for e in range(2)]
          for i in nd]
    glast = [[g[c - 1:c, :] for g in gcol[i]] for i in nd]
    kdt = [jnp.concatenate([k[i][e] * jnp.exp(glast[i][e] - gcol[i][e]) for e in range(2)], axis=0).T.astype(BF16)
           for i in nd]
    ikd = [jnp.concatenate([intra[i], kdt[i]], axis=0) for i in nd]
    g_tot = [[jnp.exp(g) for g in glast[i]] for i in nd]

    for j in range(nc):
        tick()
        idx = [j * (GDN_HEADS // 2) + m for m in range(GDN_HEADS // 2)]
        s = [s_ref[h] for h in range(GDN_HEADS)]
        ws = [_dot(wq[idx[h // 2]][h % 2], s[h].astype(BF16)) for h in range(GDN_HEADS)]
        v_new = [(sol[idx[h // 2]][h % 2][:, :GDN_DV] - ws[h][:c]).astype(BF16) for h in range(GDN_HEADS)]
        upd = [_dot(ikd[idx[h // 2]], jnp.concatenate([v_new[h], zrows] if h % 2 == 0 else [zrows, v_new[h]], axis=0))
               for h in range(GDN_HEADS)]
        for h in range(GDN_HEADS):
            s_ref[h] = s[h] * g_tot[idx[h // 2]][h % 2] + upd[h][c:]
            o = ws[h][c:] + upd[h][:c]
            zg = qkvz_ref[3, h, rows(j), :]
            y_ref[rows(j), h * GDN_DV:(h + 1) * GDN_DV] = _gated_norm(o, zg, nw).astype(y_ref.dtype)

    for _ in step_part:
        pass

    @pl.when(n == pl.num_programs(0) - 1)
    def _():
        sfin_ref[...] = s_ref[...]
        for h in range(GDN_HEADS):
            ys_ref[:, h * GDN_DV:(h + 1) * GDN_DV] = yacc_ref[h].astype(ys_ref.dtype)


def _gdn_step_part(n, sb, qs_ref, gs_ref, nw_ref, st_ref, sto_ref, yacc_ref):
    nw = nw_ref[...]
    lane = lax.broadcasted_iota(jnp.int32, (1, LANES), 1)
    piece_row = lax.broadcasted_iota(jnp.int32, (2 * SUBLANES, LANES), 0)
    ones = jnp.where(piece_row < 3, 1.0, 0.0).astype(BF16)

    def spread(row):
        p1 = row.astype(BF16).astype(F32)
        r1 = row - p1
        p2 = r1.astype(BF16).astype(F32)
        p3 = r1 - p2
        pieces = jnp.where(piece_row == 0, p1, jnp.where(piece_row == 1, p2, jnp.where(piece_row == 2, p3, 0.0)))
        return lax.dot_general(pieces.astype(BF16), ones, (((0,), (0,)), ((), ())), preferred_element_type=F32)

    def pick(row, l):
        return jnp.sum(jnp.where(lane == l, row, 0.0), axis=1, keepdims=True)

    pairs = [(b, h) for b in range(sb) for h in range(GDN_HEADS)]
    m = range(len(pairs))
    rows = [pl.ds(n * sb + b, 1) for b in range(sb)]
    grow = [gs_ref[rows[b], :] for b in range(sb)]
    beta = [pick(grow[b], LANE_BETA + h) for b, h in pairs]
    eg = [pick(grow[b], LANE_EG + h) for b, h in pairs]
    yield
    qb = [spread(qs_ref[0, h, rows[b], :]) for b, h in pairs]
    yield
    kb = [spread(qs_ref[1, h, rows[b], :]) for b, h in pairs]
    yield
    s = [st_ref[b, h] * eg[i] for i, (b, h) in enumerate(pairs)]
    yield
    kv = [jnp.sum(s[i] * kb[i], axis=0, keepdims=True) for i in m]
    yield
    delta = [(qs_ref[2, h, rows[b], :] - kv[i]) * beta[i] for i, (b, h) in enumerate(pairs)]
    s = [s[i] + kb[i] * delta[i] for i in m]
    yield
    o = [jnp.sum(s[i] * qb[i], axis=0, keepdims=True) for i in m]
    yield
    for i, (b, h) in enumerate(pairs):
        sto_ref[b, h] = s[i]
        yacc_ref[h, rows[b], :] = _gated_norm(o[i], qs_ref[3, h, rows[b], :], nw)


def _gdn(qkvz, gate, qkvz_s, gate_s, norm_w, state, *, nc=2):
    t, b = qkvz.shape[2], qkvz_s.shape[2]
    c = GDN_CHUNK
    steps = t // (nc * c)
    assert t % (nc * c) == 0 and c & (c - 1) == 0 and b % steps == 0
    sb = b // steps
    state_spec = pl.BlockSpec((sb, GDN_HEADS, GDN_DK, GDN_DV), lambda n: (n, 0, 0, 0))
    const2 = lambda n: (0, 0)
    return pl.pallas_call(
        functools.partial(_gdn_body, nc, sb),
        grid=(steps,),
        in_specs=[
            pl.BlockSpec((4, GDN_HEADS, nc * c, LANES), lambda n: (0, 0, n, 0)),
            pl.BlockSpec((nc * c, LANES), lambda n: (n, 0)),
            pl.BlockSpec((1, GDN_DV), const2),
            pl.BlockSpec((4, GDN_HEADS, b, LANES), lambda n: (0, 0, 0, 0)),
            pl.BlockSpec((b, LANES), const2),
            state_spec,
        ],
        out_specs=[
            pl.BlockSpec((nc * c, D_GDN_V), lambda n: (n, 0)),
            pl.BlockSpec((GDN_HEADS, GDN_DK, GDN_DV), lambda n: (0, 0, 0)),
            pl.BlockSpec((b, D_GDN_V), const2),
            state_spec,
        ],
        out_shape=[
            jax.ShapeDtypeStruct((t, D_GDN_V), BF16),
            jax.ShapeDtypeStruct((GDN_HEADS, GDN_DK, GDN_DV), F32),
            jax.ShapeDtypeStruct((b, D_GDN_V), BF16),
            jax.ShapeDtypeStruct(state.shape, F32),
        ],
        scratch_shapes=[pltpu.VMEM((GDN_HEADS, GDN_DK, GDN_DV), F32), pltpu.VMEM((GDN_HEADS, b, GDN_DV), F32)],
        compiler_params=_cparams("arbitrary"),
        name="gdn",
    )(qkvz, gate, norm_w.reshape(1, GDN_DV), qkvz_s, gate_s, state)


def _gate_param_row(p):
    return jnp.zeros((1, LANES), F32).at[0, LANE_G:LANE_G + GDN_HEADS].set(p.astype(F32))


def _ffn(h, hs, x, xs, state_t, w_up, w_conv, w_down, next_norm, *, final=False, name):
    (hid, keep_a, keep_b), (hid_s, up_a, up_b) = _ffn_up(h, hs, w_up, w_conv, state_t)
    out, out_s = _matmul([hid], [hid_s], w_down, tm=256, residual=x, residual2=xs, norm_w=next_norm,
                         final=final, chunk=512, name=name)
    return out, out_s, jnp.concatenate([keep_a, keep_b], axis=1), jnp.concatenate([up_a, up_b], axis=1)


def kernel(x_prompt, x_sample, state_l0_pool, cache_l0_k, cache_l0_v, state_l0_ffn_conv, state_l1_gdn_conv, state_l1_gdn_S, state_l1_sconv, state_l1_ffn_conv, l0_norm_mix, l0_w_in, l0_pool_w, l0_pool_scale, l0_sinks, l0_w_out, l0_norm_ffn, l0_ffn_w_up, l0_ffn_conv, l0_ffn_w_down, l1_norm_mix, l1_w_in, l1_gdn_conv, l1_gdn_A_log, l1_gdn_dt_bias, l1_gdn_norm, l1_sconv_w, l1_w_out, l1_norm_ffn, l1_ffn_w_up, l1_ffn_conv, l1_ffn_w_down, final_norm):
    bp, t, d = x_prompt.shape
    nb, ts = x_sample.shape[:2]
    wb = cache_l0_k.shape[1]
    assert bp == 1 and ts == 1 and d == D_MODEL and wb == WINDOW and t >= WINDOW
    w_in1_t = l1_w_in.T
    a_log_row = _gate_param_row(l1_gdn_A_log)
    dt_bias_row = _gate_param_row(l1_gdn_dt_bias)
    pool_t = state_l0_pool.transpose(1, 0, 2)
    kt, vt = cache_l0_k.transpose(0, 2, 3, 1), cache_l0_v.transpose(0, 2, 3, 1)
    ffn0_t = state_l0_ffn_conv.transpose(1, 0, 2)
    gconv_t = state_l1_gdn_conv.transpose(1, 0, 2)
    sconv_t = state_l1_sconv.transpose(1, 0, 2)
    ffn1_t = state_l1_ffn_conv.transpose(1, 0, 2)
    last = SUBLANES

    xp, xs = x_prompt[0], x_sample[:, 0]
    z0, z0s = _matmul([_rmsnorm(xp, l0_norm_mix, BF16)], [_rmsnorm(xs, l0_norm_mix, BF16)], l0_w_in,
                      tm=512, name="in_proj0")
    mix = _mixer0_seq(z0, l0_pool_w, l0_pool_scale, l0_sinks)
    y_pool = _pool_step(z0s, pool_t, l0_pool_w, l0_pool_scale)
    att, kt_new, vt_new = _attn_step(z0s, kt, vt, l0_sinks)
    mix_s = jnp.concatenate([y_pool, att.astype(BF16)], axis=1)
    (x1, h1), (x1s, h1s) = _matmul([mix], [mix_s], l0_w_out, tm=512, residual=xp, residual2=xs,
                                   norm_w=l0_norm_ffn, name="out_proj0")
    (x2, h2), (x2s, h2s), ffn0_rows, ffn0_new = _ffn(
        h1, h1s, x1, x1s, ffn0_t, l0_ffn_w_up, l0_ffn_conv, l0_ffn_w_down, l1_norm_mix, name="ffn_down0")
    (qkvz, raw), (qkvz_s, raw_s) = _qkvz(h2, h2s, w_in1_t, l1_gdn_conv, gconv_t)
    (ysc, gate, mrows), (ysc_s, gate_s, m_s) = _tail(
        h2, h2s, w_in1_t, l1_sconv_w, a_log_row, dt_bias_row, sconv_t)
    ygdn, s_fin, ygdn_s, s_new = _gdn(qkvz, gate, qkvz_s, gate_s, l1_gdn_norm, state_l1_gdn_S)
    (x3, h3), (x3s, h3s) = _matmul([ygdn, ysc], [ygdn_s, ysc_s], l1_w_out, tm=512, residual=x2,
                                   residual2=x2s, norm_w=l1_norm_ffn, name="out_proj1")
    y_prompt, y_sample, ffn1_rows, ffn1_new = _ffn(
        h3, h3s, x3, x3s, ffn1_t, l1_ffn_w_up, l1_ffn_conv, l1_ffn_w_down, final_norm, final=True,
        name="ffn_down1")

    p_pool = z0[t - POOL_BUF:, :D_POOL][None]
    p_k = z0[t - wb:, O_K:O_V].reshape(1, wb, N_KV_HEADS, HEAD_DIM)
    p_v = z0[t - wb:, O_V:].reshape(1, wb, N_KV_HEADS, HEAD_DIM)
    p_ffn0 = ffn0_rows[last - (FFN_CONV - 1):][None]
    p_gconv = raw[last - (GDN_CONV - 1):, :D_GDN_CONV][None]
    p_sconv = mrows[last - (SCONV_W - 1):][None]
    p_ffn1 = ffn1_rows[last - (FFN_CONV - 1):][None]

    def push(state_t, new_row):
        return jnp.concatenate([state_t[1:], new_row[None]], axis=0).transpose(1, 0, 2)

    s_pool = push(pool_t, z0s[:, :D_POOL])
    s_k, s_v = kt_new.transpose(0, 3, 1, 2), vt_new.transpose(0, 3, 1, 2)
    s_ffn0 = push(ffn0_t, ffn0_new)
    s_gconv = push(gconv_t, raw_s[:, :D_GDN_CONV])
    s_sconv = push(sconv_t, m_s)
    s_ffn1 = push(ffn1_t, ffn1_new)
    return (y_prompt[None], y_sample[:, None], p_pool, s_pool, p_k, s_k, p_v, s_v, p_ffn0, s_ffn0,
            p_gconv, s_gconv, s_fin[None], s_new, p_sconv, s_sconv, p_ffn1, s_ffn1)
```

```python
import functools

import jax
import jax.numpy as jnp
from jax import lax
from jax.experimental import pallas as pl
from jax.experimental.pallas import tpu as pltpu

F32 = jnp.float32
BF16 = jnp.bfloat16
EPS = 1e-6
NEG = -1e30

LANES = 128
SUBLANES = 8
MXU_COLS = 256
VMEM_LIMIT_BYTES = 60 * 1024 * 1024

ROW_TILE = 512
FFN_UP_ROW_TILE = 1024
FFN_UP_COL_TILE = 512
FFN_DOWN_ROW_TILE = 256
STAGE_ROWS = 256
FFN_DOWN_STAGE_ROWS = 512
GDN_CHUNKS_PER_STEP = 4
ATTN_STEP_SEQS = 8

D_MODEL = 2048
D_POOL = 512
POOL_WINDOWS = (2, 4, 8, 16)
POOL_BUF = max(POOL_WINDOWS) - 1
POOL_HIST = 16
HEAD_DIM = 64
N_Q_HEADS = 24
N_KV_HEADS = 4
Q_PER_KV = N_Q_HEADS // N_KV_HEADS
WINDOW = 128
D_ATTN = N_Q_HEADS * HEAD_DIM
D_KV = N_KV_HEADS * HEAD_DIM
D_IN0 = D_POOL + D_ATTN + 2 * D_KV
O_K = D_POOL + D_ATTN
O_V = O_K + D_KV
GDN_HEADS = 12
GDN_DK = 128
GDN_DV = 128
D_GDN_K = GDN_HEADS * GDN_DK
D_GDN_V = GDN_HEADS * GDN_DV
D_GDN_CONV = 2 * D_GDN_K + D_GDN_V
D_GDN_MAIN = D_GDN_CONV + D_GDN_V
GDN_CONV = 4
GDN_CHUNK = 64
D_SCONV = 512
SCONV_W = 3
D_TAIL = 3 * D_SCONV + LANES
D_FF = 5632
FFN_CONV = 3
LANE_BETA = 0
LANE_G = GDN_HEADS
LANE_EG = 2 * GDN_HEADS


def _cparams(*sem):
    return pltpu.CompilerParams(dimension_semantics=sem, vmem_limit_bytes=VMEM_LIMIT_BYTES)


def _dot(a, b):
    return jnp.dot(a, b, preferred_element_type=F32)


def _dot_nt(a, b):
    return lax.dot_general(a, b, (((1,), (1,)), ((), ())), preferred_element_type=F32)


def _silu(x):
    return x * jax.nn.sigmoid(x)


def _cast_rows(w_ref, wbf_ref, chunk=STAGE_ROWS):
    k = w_ref.shape[0]
    chunk = min(chunk, k)
    assert k % chunk == 0

    def body(c, carry):
        r = pl.multiple_of(c * chunk, chunk)
        wbf_ref[pl.ds(r, chunk), :] = w_ref[pl.ds(r, chunk), :].astype(BF16)
        return carry

    lax.fori_loop(0, k // chunk, body, 0)


def _cast_transposed(wt_ref, wbf_ref):
    n = wt_ref.shape[0]
    step = MXU_COLS if n % MXU_COLS == 0 else LANES
    for c in range(0, n, step):
        wbf_ref[:, c:c + step] = wt_ref[c:c + step, :].T.astype(BF16)


def _shift_rows(u, prev8, k):
    rolled = pltpu.roll(u, k, 0)
    head = pltpu.roll(jnp.concatenate([prev8, u[0:SUBLANES]], axis=0), k, 0)[SUBLANES:2 * SUBLANES]
    return jnp.concatenate([head, rolled[SUBLANES:]], axis=0)


def _rmsnorm_body(x_ref, w_ref, o_ref):
    x = x_ref[...]
    ms = jnp.mean(x * x, axis=-1, keepdims=True)
    o_ref[...] = (x * lax.rsqrt(ms + EPS) * w_ref[...]).astype(o_ref.dtype)


def _rmsnorm(x, w, out_dtype):
    m, d = x.shape
    tm = min(m, ROW_TILE)
    assert m % tm == 0
    return pl.pallas_call(
        _rmsnorm_body,
        grid=(m // tm,),
        in_specs=[pl.BlockSpec((tm, d), lambda i: (i, 0)), pl.BlockSpec((1, d), lambda i: (0, 0))],
        out_specs=pl.BlockSpec((tm, d), lambda i: (i, 0)),
        out_shape=jax.ShapeDtypeStruct((m, d), out_dtype),
        compiler_params=_cparams("parallel"),
        name="rmsnorm",
    )(x, w.reshape(1, d))


def _mm_body(n_pieces, n_stage, mode, *refs):
    n_out = {"plain": 1, "residual": 2, "final": 1}[mode]
    n_in = n_pieces + (0 if mode == "plain" else 1)
    set_a, set_b = refs[:n_in], refs[n_in:2 * n_in]
    w_ref = refs[2 * n_in]
    pos = 2 * n_in + 1
    nw_ref = None
    if mode != "plain":
        nw_ref = refs[pos]
        pos += 1
    out_a, out_b = refs[pos:pos + n_out], refs[pos + n_out:pos + 2 * n_out]
    wbf_ref = refs[pos + 2 * n_out]
    s = pl.program_id(0)
    last = pl.num_programs(0) - 1
    ck = w_ref.shape[0]

    @pl.when(s < n_stage)
    def _():
        _cast_rows(w_ref, wbf_ref.at[pl.ds(pl.multiple_of(s * ck, ck), ck), :])

    def rows(ins, outs):
        acc, row = None, 0
        for x_ref in ins[:n_pieces]:
            kp = x_ref.shape[1]
            d = _dot(x_ref[...], wbf_ref[row:row + kp, :])
            acc = d if acc is None else acc + d
            row += kp
        if mode == "plain":
            outs[0][...] = acc
            return
        xn = acc + ins[n_pieces][...]
        hn = xn * lax.rsqrt(jnp.mean(xn * xn, axis=-1, keepdims=True) + EPS) * nw_ref[...]
        if mode == "residual":
            outs[0][...] = xn
        outs[-1][...] = hn.astype(outs[-1].dtype)

    @pl.when((s >= n_stage) & (s < last))
    def _():
        rows(set_a, out_a)

    @pl.when(s == last)
    def _():
        rows(set_b, out_b)


def _matmul(xs, xs2, w, *, tm=ROW_TILE, residual=None, residual2=None, norm_w=None, final=False, chunk=STAGE_ROWS,
            name="matmul"):
    m, m2 = xs[0].shape[0], xs2[0].shape[0]
    k, n = w.shape
    tm = min(tm, m)
    assert m % tm == 0 and k % chunk == 0 and sum(x.shape[1] for x in xs) == k
    n_stage, nm = k // chunk, m // tm
    mode = "plain" if residual is None else ("final" if final else "residual")

    def row_tile(s):
        return jnp.clip(s - n_stage, 0, nm - 1)

    tile_a = pl.BlockSpec((tm, n), lambda s: (row_tile(s), 0))
    tile_b = pl.BlockSpec((m2, n), lambda s: (0, 0))
    in_a = [pl.BlockSpec((tm, x.shape[1]), lambda s: (row_tile(s), 0)) for x in xs]
    in_b = [pl.BlockSpec((m2, x.shape[1]), lambda s: (0, 0)) for x in xs2]
    args_a, args_b = list(xs), list(xs2)
    if mode != "plain":
        in_a.append(tile_a)
        in_b.append(tile_b)
        args_a.append(residual)
        args_b.append(residual2)
    in_specs = in_a + in_b + [pl.BlockSpec((chunk, n), lambda s: (jnp.minimum(s, n_stage - 1), 0))]
    args = args_a + args_b + [w]
    if mode != "plain":
        in_specs.append(pl.BlockSpec((1, n), lambda s: (0, 0)))
        args.append(norm_w.reshape(1, n))
    dtypes = [F32, BF16] if mode == "residual" else [F32]
    out_specs = [tile_a] * len(dtypes) + [tile_b] * len(dtypes)
    out_shape = ([jax.ShapeDtypeStruct((m, n), dt) for dt in dtypes]
                 + [jax.ShapeDtypeStruct((m2, n), dt) for dt in dtypes])
    outs = pl.pallas_call(
        functools.partial(_mm_body, len(xs), n_stage, mode),
        grid=(n_stage + nm + 1,),
        in_specs=in_specs,
        out_specs=out_specs,
        out_shape=out_shape,
        scratch_shapes=[pltpu.VMEM((k, n), BF16)],
        compiler_params=_cparams("arbitrary"),
        name=name,
    )(*args)
    half = len(dtypes)
    first, second = outs[:half], outs[half:]
    return (first[0], second[0]) if half == 1 else (tuple(first), tuple(second))


def _ffn_up_body(x_ref, xs_ref, wa_ref, wb_ref, cwa_ref, cwb_ref, a2_ref, a1_ref, b2_ref, b1_ref,
                 hid_ref, upa_ref, upb_ref, hids_ref, upsa_ref, upsb_ref, wa_bf, wb_bf, ca_ref, cb_ref):
    tm = x_ref.shape[0]
    i = pl.program_id(1)

    @pl.when(i == 0)
    def _():
        _cast_rows(wa_ref, wa_bf)
        _cast_rows(wb_ref, wb_bf)
        ca_ref[...] = jnp.zeros_like(ca_ref)
        cb_ref[...] = jnp.zeros_like(cb_ref)

    def gate(ua, ub, taps_a, taps_b):
        c_a = cwa_ref[0:1, :] * taps_a[0] + cwa_ref[1:2, :] * taps_a[1] + cwa_ref[2:3, :] * ua
        c_b = cwb_ref[0:1, :] * taps_b[0] + cwb_ref[1:2, :] * taps_b[1] + cwb_ref[2:3, :] * ub
        return (_silu(c_a) * c_b).astype(BF16)

    @pl.when(i > 0)
    def _():
        x = x_ref[...]
        ua = _dot(x, wa_bf[...])
        ub = _dot(x, wb_bf[...])
        pa, pb = ca_ref[...], cb_ref[...]
        hid_ref[...] = gate(ua, ub, (_shift_rows(ua, pa, 2), _shift_rows(ua, pa, 1)),
                            (_shift_rows(ub, pb, 2), _shift_rows(ub, pb, 1)))
        ca_ref[...] = ua[tm - SUBLANES:]
        cb_ref[...] = ub[tm - SUBLANES:]
        upa_ref[...] = ua[tm - SUBLANES:]
        upb_ref[...] = ub[tm - SUBLANES:]

    @pl.when(i == 0)
    def _():
        x = xs_ref[...]
        ua = _dot(x, wa_bf[...])
        ub = _dot(x, wb_bf[...])
        hids_ref[...] = gate(ua, ub, (a2_ref[...], a1_ref[...]), (b2_ref[...], b1_ref[...]))
        upsa_ref[...] = ua
        upsb_ref[...] = ub


def _ffn_up(h, hs, w_up, conv_w, state_t, *, tn=FFN_UP_COL_TILE, tm=FFN_UP_ROW_TILE):
    m, k = h.shape
    ms = hs.shape[0]
    tm = min(tm, m)
    nb, nm = D_FF // tn, m // tm
    assert m % tm == 0 and D_FF % tn == 0
    tile = lambda j, i: (jnp.maximum(i - 1, 0), j)
    in_specs = [
        pl.BlockSpec((tm, k), lambda j, i: (jnp.maximum(i - 1, 0), 0)),
        pl.BlockSpec((ms, k), lambda j, i: (0, 0)),
        pl.BlockSpec((k, tn), lambda j, i: (0, j)),
        pl.BlockSpec((k, tn), lambda j, i: (0, j + nb)),
        pl.BlockSpec((FFN_CONV, tn), lambda j, i: (0, j)),
        pl.BlockSpec((FFN_CONV, tn), lambda j, i: (0, j + nb)),
        pl.BlockSpec((None, ms, tn), lambda j, i: (0, 0, j)),
        pl.BlockSpec((None, ms, tn), lambda j, i: (1, 0, j)),
        pl.BlockSpec((None, ms, tn), lambda j, i: (0, 0, j + nb)),
        pl.BlockSpec((None, ms, tn), lambda j, i: (1, 0, j + nb)),
    ]
    keep_spec = pl.BlockSpec((SUBLANES, tn), lambda j, i: (0, j))
    step_spec = pl.BlockSpec((ms, tn), lambda j, i: (0, j))
    outs = pl.pallas_call(
        _ffn_up_body,
        grid=(nb, nm + 1),
        in_specs=in_specs,
        out_specs=[pl.BlockSpec((tm, tn), tile), keep_spec, keep_spec, step_spec, step_spec, step_spec],
        out_shape=[
            jax.ShapeDtypeStruct((m, D_FF), BF16),
            jax.ShapeDtypeStruct((SUBLANES, D_FF), F32),
            jax.ShapeDtypeStruct((SUBLANES, D_FF), F32),
            jax.ShapeDtypeStruct((ms, D_FF), BF16),
            jax.ShapeDtypeStruct((ms, D_FF), F32),
            jax.ShapeDtypeStruct((ms, D_FF), F32),
        ],
        scratch_shapes=[pltpu.VMEM((k, tn), BF16), pltpu.VMEM((k, tn), BF16),
                        pltpu.VMEM((SUBLANES, tn), F32), pltpu.VMEM((SUBLANES, tn), F32)],
        compiler_params=_cparams("arbitrary", "arbitrary"),
        name="ffn_up",
    )(h, hs, w_up, w_up, conv_w, conv_w, state_t, state_t, state_t, state_t)
    return outs[:3], outs[3:]


def _half_lane_pair(x, head_in_high, lo_mask):
    if head_in_high:
        hi = jnp.where(lo_mask, 0.0, x)
        return pltpu.roll(hi, HEAD_DIM, 1), hi
    lo = jnp.where(lo_mask, x, 0.0)
    return lo, pltpu.roll(lo, HEAD_DIM, 1)


def _mixer0_seq_body(sink_ref, z_ref, kvp_ref, up_ref, pw_ref, ps_ref, o_ref):
    n = pl.program_id(0)
    first = n == 0
    w = WINDOW
    lo_mask = lax.broadcasted_iota(jnp.int32, (1, LANES), 1) < HEAD_DIM

    hist = jnp.where(first, 0.0, up_ref[...])
    pos = n * w + lax.broadcasted_iota(jnp.int32, (w, 1), 0)
    for g, win in enumerate(POOL_WINDOWS):
        sl = slice(g * LANES, (g + 1) * LANES)
        e = jnp.concatenate([hist[:, sl], z_ref[:, sl]], axis=0)
        s, sh = e, 1
        while sh < win:
            s = s + pltpu.roll(s, sh, 0)
            sh *= 2
        cnt = jnp.minimum(pos + 1, win).astype(F32)
        diff = s[POOL_HIST:] / cnt - e[POOL_HIST:]
        y = _dot(diff.astype(BF16), pw_ref[g].astype(BF16)) * ps_ref[:, sl]
        o_ref[:, sl] = y.astype(o_ref.dtype)

    row = lax.broadcasted_iota(jnp.int32, (w, 2 * w), 0)
    col = lax.broadcasted_iota(jnp.int32, (w, 2 * w), 1)
    valid = (col >= row) & (col <= row + w) & (jnp.logical_not(first) | (col >= w))
    tiles_per_kv = Q_PER_KV * HEAD_DIM // LANES
    for c in range(D_KV // LANES):
        k2 = jnp.concatenate([kvp_ref[:, c * LANES:(c + 1) * LANES],
                              z_ref[:, O_K + c * LANES:O_K + (c + 1) * LANES]], axis=0)
        v2 = jnp.concatenate([kvp_ref[:, D_KV + c * LANES:D_KV + (c + 1) * LANES],
                              z_ref[:, O_V + c * LANES:O_V + (c + 1) * LANES]], axis=0)
        for p in range(2):
            hk = 2 * c + p
            k_lo, k_hi = (t.astype(BF16) for t in _half_lane_pair(k2, p == 1, lo_mask))
            v_lo, v_hi = (t.astype(BF16) for t in _half_lane_pair(v2, p == 1, lo_mask))
            q0 = D_POOL + hk * Q_PER_KV * HEAD_DIM
            qst = jnp.concatenate(
                [z_ref[:, q0 + a * LANES:q0 + (a + 1) * LANES] for a in range(tiles_per_kv)], axis=0)
            qst = (qst * HEAD_DIM ** -0.5).astype(BF16)
            s_lo = _dot_nt(qst, k_lo)
            s_hi = _dot_nt(qst, k_hi)
            for a in range(tiles_per_kv):
                probs, inv = [], []
                for par, s_all in ((0, s_lo), (1, s_hi)):
                    sk = sink_ref[hk * Q_PER_KV + 2 * a + par]
                    s = jnp.where(valid, s_all[a * w:(a + 1) * w], NEG)
                    mx = jnp.maximum(jnp.max(s, axis=-1, keepdims=True), sk)
                    pr = jnp.exp(s - mx)
                    den = jnp.sum(pr, axis=-1, keepdims=True) + jnp.exp(sk - mx)
                    probs.append(pr.astype(BF16))
                    inv.append(1.0 / den)
                o = _dot(probs[0], v_lo) + _dot(probs[1], v_hi)
                o = o * jnp.where(lo_mask, inv[0], inv[1])
                o_ref[:, q0 + a * LANES:q0 + (a + 1) * LANES] = o.astype(o_ref.dtype)


def _mixer0_seq(z0, pool_w, pool_scale, sinks):
    t = z0.shape[0]
    w = WINDOW
    assert t % w == 0
    hist_blocks = w // POOL_HIST
    return pl.pallas_call(
        _mixer0_seq_body,
        grid=(t // w,),
        in_specs=[
            pl.BlockSpec(memory_space=pltpu.SMEM),
            pl.BlockSpec((w, D_IN0), lambda n: (n, 0)),
            pl.BlockSpec((w, 2 * D_KV), lambda n: (jnp.maximum(n - 1, 0), O_K // (2 * D_KV))),
            pl.BlockSpec((POOL_HIST, D_POOL), lambda n: (jnp.maximum(n * hist_blocks - 1, 0), 0)),
            pl.BlockSpec((len(POOL_WINDOWS), LANES, LANES), lambda n: (0, 0, 0)),
            pl.BlockSpec((1, D_POOL), lambda n: (0, 0)),
        ],
        out_specs=pl.BlockSpec((w, D_POOL + D_ATTN), lambda n: (n, 0)),
        out_shape=jax.ShapeDtypeStruct((t, D_POOL + D_ATTN), BF16),
        compiler_params=_cparams("parallel"),
        name="mixer0_seq",
    )(sinks, z0, z0, z0, pool_w, pool_scale.reshape(1, D_POOL))


def _pool_step_body(*refs):
    hist_refs = refs[:POOL_BUF]
    z_ref, pw_ref, ps_ref, o_ref = refs[POOL_BUF:]
    for g, win in enumerate(POOL_WINDOWS):
        sl = slice(g * LANES, (g + 1) * LANES)
        u = z_ref[:, sl]
        s = u
        for r in range(POOL_BUF - (win - 1), POOL_BUF):
            s = s + hist_refs[r][:, sl]
        diff = s / float(win) - u
        y = _dot(diff.astype(BF16), pw_ref[g].astype(BF16)) * ps_ref[:, sl]
        o_ref[:, sl] = y.astype(o_ref.dtype)


def _pool_step(z0, hist_t, pool_w, pool_scale):
    b = z0.shape[0]
    in_specs = [pl.BlockSpec((None, b, D_POOL), functools.partial(lambda i, r: (r, 0, 0), r=r))
                for r in range(POOL_BUF)]
    in_specs += [
        pl.BlockSpec((b, D_POOL), lambda i: (0, 0)),
        pl.BlockSpec((len(POOL_WINDOWS), LANES, LANES), lambda i: (0, 0, 0)),
        pl.BlockSpec((1, D_POOL), lambda i: (0, 0)),
    ]
    return pl.pallas_call(
        _pool_step_body,
        grid=(1,),
        in_specs=in_specs,
        out_specs=pl.BlockSpec((b, D_POOL), lambda i: (0, 0)),
        out_shape=jax.ShapeDtypeStruct((b, D_POOL), BF16),
        compiler_params=_cparams("arbitrary"),
        name="pool_step",
    )(*([hist_t] * POOL_BUF), z0, pool_w, pool_scale.reshape(1, D_POOL))


def _attn_step_body(q_ref, kn_ref, vn_ref, knt_ref, vnt_ref, kt_ref, vt_ref, sink_ref, o_ref, kto_ref, vto_ref):
    bs = q_ref.shape[0]
    newest = lax.broadcasted_iota(jnp.int32, (HEAD_DIM, WINDOW), 1) == WINDOW - 1
    pairs = [(b, hk) for b in range(bs) for hk in range(N_KV_HEADS)]
    n = range(len(pairs))
    kt = [kt_ref[b, hk] for b, hk in pairs]
    vt = [vt_ref[b, hk] for b, hk in pairs]
    q = [q_ref[b, hk] * HEAD_DIM ** -0.5 for b, hk in pairs]
    sk = [sink_ref[hk * Q_PER_KV:(hk + 1) * Q_PER_KV, :] for b, hk in pairs]
    s = [_dot(q[i].astype(BF16), kt[i].astype(BF16)) for i in n]
    s_new = [jnp.sum(q[i] * kn_ref[b, hk:hk + 1, :], axis=-1, keepdims=True) for i, (b, hk) in enumerate(pairs)]
    mx = [jnp.maximum(jnp.maximum(jnp.max(s[i], axis=-1, keepdims=True), s_new[i]), sk[i]) for i in n]
    pr = [jnp.exp(s[i] - mx[i]) for i in n]
    pn = [jnp.exp(s_new[i] - mx[i]) for i in n]
    den = [jnp.sum(pr[i], axis=-1, keepdims=True) + pn[i] + jnp.exp(sk[i] - mx[i]) for i in n]
    o = [_dot_nt(pr[i].astype(BF16), vt[i].astype(BF16)) for i in n]
    for i, (b, hk) in enumerate(pairs):
        o_ref[b, hk] = (o[i] + pn[i] * vn_ref[b, hk:hk + 1, :]) / den[i]
        kto_ref[b, hk] = jnp.where(newest, pltpu.roll(knt_ref[hk], WINDOW - 1 - b, 1),
                                   pltpu.roll(kt[i], WINDOW - 1, 1))
        vto_ref[b, hk] = jnp.where(newest, pltpu.roll(vnt_ref[hk], WINDOW - 1 - b, 1),
                                   pltpu.roll(vt[i], WINDOW - 1, 1))


def _attn_step(z0, kt, vt, sinks, *, bs=ATTN_STEP_SEQS):
    b = z0.shape[0]
    assert b % bs == 0 and kt.shape[3] == WINDOW == LANES and bs <= WINDOW
    nblk = b // bs
    q4 = z0[:, D_POOL:O_K].reshape(b, N_KV_HEADS, Q_PER_KV, HEAD_DIM)
    kn = z0[:, O_K:O_V].reshape(b, N_KV_HEADS, HEAD_DIM)
    vn = z0[:, O_V:].reshape(b, N_KV_HEADS, HEAD_DIM)

    def columns(x):
        xt = x.reshape(nblk, bs, N_KV_HEADS, HEAD_DIM).transpose(0, 2, 3, 1)
        return jnp.pad(xt, ((0, 0), (0, 0), (0, 0), (0, WINDOW - bs)))

    cache_spec = pl.BlockSpec((bs, N_KV_HEADS, HEAD_DIM, WINDOW), lambda i: (i, 0, 0, 0))
    col_spec = pl.BlockSpec((None, N_KV_HEADS, HEAD_DIM, WINDOW), lambda i: (i, 0, 0, 0))
    new_spec = pl.BlockSpec((bs, N_KV_HEADS, HEAD_DIM), lambda i: (i, 0, 0))
    q_spec = pl.BlockSpec((bs, N_KV_HEADS, Q_PER_KV, HEAD_DIM), lambda i: (i, 0, 0, 0))
    att, kto, vto = pl.pallas_call(
        _attn_step_body,
        grid=(nblk,),
        in_specs=[q_spec, new_spec, new_spec, col_spec, col_spec, cache_spec, cache_spec,
                  pl.BlockSpec((N_Q_HEADS, 1), lambda i: (0, 0))],
        out_specs=[q_spec, cache_spec, cache_spec],
        out_shape=[jax.ShapeDtypeStruct(q4.shape, F32), jax.ShapeDtypeStruct(kt.shape, F32),
                   jax.ShapeDtypeStruct(vt.shape, F32)],
        compiler_params=_cparams("parallel"),
        name="attn_step",
    )(q4, kn, vn, columns(kn), columns(vn), kt, vt, sinks.reshape(N_Q_HEADS, 1))
    return att.reshape(b, D_ATTN), kto, vto


def _qkvz_body(x_ref, xs_ref, w_ref, cw_ref, p3_ref, p2_ref, p1_ref,
               o_ref, raw_ref, os_ref, raws_ref, w_bf, carry_ref):
    tm = x_ref.shape[0]
    j = pl.program_id(0)
    i = pl.program_id(1)

    @pl.when(i == 0)
    def _():
        _cast_transposed(w_ref, w_bf)
        carry_ref[...] = jnp.zeros_like(carry_ref)

    def heads(a):
        return [a[:, h * LANES:(h + 1) * LANES] for h in range(GDN_HEADS)]

    def l2norm(a):
        return a * lax.rsqrt(jnp.sum(a * a, axis=-1, keepdims=True) + EPS)

    def finish(u, taps, out_ref):
        def conv_act():
            t3, t2, t1 = taps()
            c = cw_ref[3:4, :] * u
            c = cw_ref[0:1, :] * t3 + cw_ref[1:2, :] * t2 + cw_ref[2:3, :] * t1 + c
            return _silu(c)

        @pl.when(j == 0)
        def _():
            for h, a in enumerate(heads(conv_act())):
                out_ref[h] = l2norm(a) * GDN_DK ** -0.5

        @pl.when(j == 1)
        def _():
            for h, a in enumerate(heads(conv_act())):
                out_ref[h] = l2norm(a)

        @pl.when(j == 2)
        def _():
            for h, a in enumerate(heads(conv_act())):
                out_ref[h] = a

        @pl.when(j == 3)
        def _():
            for h, a in enumerate(heads(u)):
                out_ref[h] = a

    @pl.when(i > 0)
    def _():
        u = _dot(x_ref[...], w_bf[...])
        prev = carry_ref[...]
        carry_ref[...] = u[tm - SUBLANES:]
        raw_ref[...] = u[tm - SUBLANES:]
        finish(u, lambda: tuple(_shift_rows(u, prev, k) for k in (3, 2, 1)), o_ref)

    @pl.when(i == 0)
    def _():
        u = _dot(xs_ref[...], w_bf[...])
        raws_ref[...] = u
        finish(u, lambda: (p3_ref[...], p2_ref[...], p1_ref[...]), os_ref)


def _qkvz(h, hs, w_in_t, conv_w, state_t, *, tm=ROW_TILE):
    m, k = h.shape
    ms = hs.shape[0]
    tm = min(tm, m)
    tn = D_GDN_K
    assert D_GDN_K == D_GDN_V and GDN_DK == LANES and m % tm == 0
    nj, nm = D_GDN_MAIN // tn, m // tm
    in_specs = [
        pl.BlockSpec((tm, k), lambda j, i: (jnp.maximum(i - 1, 0), 0)),
        pl.BlockSpec((ms, k), lambda j, i: (0, 0)),
        pl.BlockSpec((tn, k), lambda j, i: (j, 0)),
        pl.BlockSpec((GDN_CONV, tn), lambda j, i: (0, jnp.minimum(j, 2))),
    ]
    for r in range(GDN_CONV - 1):
        in_specs.append(pl.BlockSpec((None, ms, tn), functools.partial(lambda j, i, r: (r, 0, jnp.minimum(j, 2)), r=r)))
    outs = pl.pallas_call(
        _qkvz_body,
        grid=(nj, nm + 1),
        in_specs=in_specs,
        out_specs=[
            pl.BlockSpec((None, GDN_HEADS, tm, LANES), lambda j, i: (j, 0, jnp.maximum(i - 1, 0), 0)),
            pl.BlockSpec((SUBLANES, tn), lambda j, i: (0, j)),
            pl.BlockSpec((None, GDN_HEADS, ms, LANES), lambda j, i: (j, 0, 0, 0)),
            pl.BlockSpec((ms, tn), lambda j, i: (0, j)),
        ],
        out_shape=[
            jax.ShapeDtypeStruct((nj, GDN_HEADS, m, LANES), F32),
            jax.ShapeDtypeStruct((SUBLANES, D_GDN_MAIN), F32),
            jax.ShapeDtypeStruct((nj, GDN_HEADS, ms, LANES), F32),
            jax.ShapeDtypeStruct((ms, D_GDN_MAIN), F32),
        ],
        scratch_shapes=[pltpu.VMEM((k, tn), BF16), pltpu.VMEM((SUBLANES, tn), F32)],
        compiler_params=_cparams("arbitrary", "arbitrary"),
        name="qkvz",
    )(h, hs, w_in_t, conv_w, state_t, state_t, state_t)
    return outs[:2], outs[2:]


def _tail_body(x_ref, xs_ref, wa_ref, wb_ref, cw_ref, alog_ref, dtb_ref, p2_ref, p1_ref,
               ysc_ref, gate_ref, m_ref, yscs_ref, gates_ref, ms_ref, w_bf, carry_ref):
    tm = x_ref.shape[0]
    i = pl.program_id(0)

    @pl.when(i == 0)
    def _():
        n_raw = wb_ref.shape[0]
        step = MXU_COLS
        for c in range(0, 3 * D_SCONV, step):
            r0 = n_raw + c
            if r0 + step <= wa_ref.shape[0]:
                blk = wa_ref[r0:r0 + step, :]
            else:
                blk = jnp.concatenate([wa_ref[r0:, :], wb_ref[...]], axis=0)
            w_bf[:, c:c + step] = blk.T.astype(BF16)
        raw = jnp.concatenate([wa_ref[0:n_raw, :], jnp.zeros((LANES - n_raw, wa_ref.shape[1]), F32)], axis=0)
        w_bf[:, 3 * D_SCONV:] = raw.T.astype(BF16)
        carry_ref[...] = jnp.zeros_like(carry_ref)

    def finish(z, mm, taps, y_ref, g_ref):
        conv = cw_ref[0:1, :] * taps[0] + cw_ref[1:2, :] * taps[1] + cw_ref[2:3, :] * mm
        y_ref[...] = (z[:, 0:D_SCONV] * conv).astype(y_ref.dtype)
        raw = z[:, 3 * D_SCONV:]
        lane = lax.broadcasted_iota(jnp.int32, raw.shape, 1)
        beta = jax.nn.sigmoid(raw)
        sp = raw + dtb_ref[...]
        softplus = jnp.maximum(sp, 0.0) + jnp.log1p(jnp.exp(-jnp.abs(sp)))
        g = -jnp.exp(alog_ref[...]) * softplus
        eg = pltpu.roll(jnp.exp(g), LANE_EG - LANE_G, 1)
        g_ref[...] = jnp.where(lane < LANE_G, beta, jnp.where(lane < LANE_EG, g, eg))

    @pl.when(i > 0)
    def _():
        z = _dot(x_ref[...], w_bf[...])
        mm = z[:, D_SCONV:2 * D_SCONV] * z[:, 2 * D_SCONV:3 * D_SCONV]
        prev = carry_ref[...]
        carry_ref[...] = mm[tm - SUBLANES:]
        m_ref[...] = mm[tm - SUBLANES:]
        finish(z, mm, (_shift_rows(mm, prev, 2), _shift_rows(mm, prev, 1)), ysc_ref, gate_ref)

    @pl.when(i == 0)
    def _():
        z = _dot(xs_ref[...], w_bf[...])
        mm = z[:, D_SCONV:2 * D_SCONV] * z[:, 2 * D_SCONV:3 * D_SCONV]
        ms_ref[...] = mm
        finish(z, mm, (p2_ref[...], p1_ref[...]), yscs_ref, gates_ref)


def _tail(h, hs, w_in_t, conv_w, a_log_row, dt_bias_row, state_t, *, tm=ROW_TILE):
    m, k = h.shape
    ms = hs.shape[0]
    tm = min(tm, m)
    n_raw, n_conv = 2 * GDN_HEADS, 3 * D_SCONV
    assert m % tm == 0 and w_in_t.shape[0] == D_GDN_MAIN + n_raw + n_conv
    assert D_GDN_MAIN % n_conv == 0 and (D_GDN_MAIN + n_conv) % n_raw == 0 and n_raw % SUBLANES == 0
    nm = m // tm
    tile = lambda i: (jnp.maximum(i - 1, 0), 0)
    const = lambda i: (0, 0)
    in_specs = [
        pl.BlockSpec((tm, k), tile),
        pl.BlockSpec((ms, k), const),
        pl.BlockSpec((n_conv, k), lambda i: (D_GDN_MAIN // n_conv, 0)),
        pl.BlockSpec((n_raw, k), lambda i: ((D_GDN_MAIN + n_conv) // n_raw, 0)),
        pl.BlockSpec((SCONV_W, D_SCONV), const),
        pl.BlockSpec((1, LANES), const),
        pl.BlockSpec((1, LANES), const),
        pl.BlockSpec((None, ms, D_SCONV), lambda i: (0, 0, 0)),
        pl.BlockSpec((None, ms, D_SCONV), lambda i: (1, 0, 0)),
    ]
    outs = pl.pallas_call(
        _tail_body,
        grid=(nm + 1,),
        in_specs=in_specs,
        out_specs=[
            pl.BlockSpec((tm, D_SCONV), tile),
            pl.BlockSpec((tm, LANES), tile),
            pl.BlockSpec((SUBLANES, D_SCONV), const),
            pl.BlockSpec((ms, D_SCONV), const),
            pl.BlockSpec((ms, LANES), const),
            pl.BlockSpec((ms, D_SCONV), const),
        ],
        out_shape=[
            jax.ShapeDtypeStruct((m, D_SCONV), BF16),
            jax.ShapeDtypeStruct((m, LANES), F32),
            jax.ShapeDtypeStruct((SUBLANES, D_SCONV), F32),
            jax.ShapeDtypeStruct((ms, D_SCONV), BF16),
            jax.ShapeDtypeStruct((ms, LANES), F32),
            jax.ShapeDtypeStruct((ms, D_SCONV), F32),
        ],
        scratch_shapes=[pltpu.VMEM((k, D_TAIL), BF16), pltpu.VMEM((SUBLANES, D_SCONV), F32)],
        compiler_params=_cparams("arbitrary"),
        name="tail",
    )(h, hs, w_in_t, w_in_t, conv_w, a_log_row, dt_bias_row, state_t, state_t)
    return outs[:3], outs[3:]


def _gated_norm(o, zg, nw):
    y = o * lax.rsqrt(jnp.mean(o * o, axis=-1, keepdims=True) + EPS) * nw
    return y * _silu(zg)


def _gdn_body(nc, sb, qkvz_ref, gate_ref, nw_ref, qs_ref, gs_ref, st_ref,
              y_ref, sfin_ref, ys_ref, sto_ref, s_ref, yacc_ref):
    c = GDN_CHUNK
    n = pl.program_id(0)
    step_part = _gdn_step_part(n, sb, qs_ref, gs_ref, nw_ref, st_ref, sto_ref, yacc_ref)

    def tick():
        next(step_part, None)

    @pl.when(n == 0)
    def _():
        s_ref[...] = jnp.zeros_like(s_ref)

    assert 2 * c == LANES and GDN_HEADS % 2 == 0
    r = lax.broadcasted_iota(jnp.int32, (c, LANES), 0)
    lane = lax.broadcasted_iota(jnp.int32, (c, LANES), 1)
    lo = lane < c
    col = jnp.where(lo, lane, lane - c)
    tri = r >= col
    strict = r > col
    rr = lax.broadcasted_iota(jnp.int32, (c, c), 0)
    ones = jnp.where(rr >= lax.broadcasted_iota(jnp.int32, (c, c), 1), 1.0, 0.0).astype(BF16)
    zpad = jnp.zeros((LANES - c, LANES), F32)
    zrows = jnp.zeros((c, LANES), BF16)
    nw = nw_ref[...]
    duos = [(j, m) for j in range(nc) for m in range(GDN_HEADS // 2)]
    nd = range(len(duos))

    def block_diag(a):
        return jnp.concatenate([jnp.where(lo, a, 0.0), jnp.where(lo, 0.0, a)], axis=0).astype(BF16)

    gates, gcs, gcts = [], [], []
    for j in range(nc):
        gate = gate_ref[j * c:(j + 1) * c, :]
        g1 = gate.astype(BF16)
        r1 = gate - g1.astype(F32)
        g2 = r1.astype(BF16)
        g3 = (r1 - g2.astype(F32)).astype(BF16)
        gc = _dot(ones, g1) + _dot(ones, g2) + _dot(ones, g3)
        gates.append(gate)
        gcs.append(gc)
        gcts.append(jnp.concatenate([gc, zpad], axis=0).T)

    def rows(j):
        return slice(j * c, (j + 1) * c)

    def per_head(fn):
        return [[fn(j, 2 * m + e) for e in range(2)] for j, m in duos]

    gcol = per_head(lambda j, h: gcs[j][:, LANE_G + h:LANE_G + h + 1])
    bcol = per_head(lambda j, h: gates[j][:, LANE_BETA + h:LANE_BETA + h + 1])
    grow = [jnp.concatenate([gcts[j][LANE_G + 2 * m:LANE_G + 2 * m + 1, 0:c],
                             gcts[j][LANE_G + 2 * m + 1:LANE_G + 2 * m + 2, 0:c]], axis=1) for j, m in duos]
    decay = [jnp.exp(jnp.where(tri, jnp.where(lo, gcol[i][0], gcol[i][1]) - grow[i], NEG)) for i in nd]
    eg = [[jnp.exp(g) for g in gcol[i]] for i in nd]
    k = per_head(lambda j, h: qkvz_ref[1, h, rows(j), :])
    qd = per_head(lambda j, h: qkvz_ref[0, h, rows(j), :])
    kb = [[k[i][e] * bcol[i][e] for e in range(2)] for i in nd]
    kq = [_dot_nt(jnp.concatenate([kb[i][0], qd[i][0]], axis=0).astype(BF16),
                  jnp.concatenate([k[i][0].astype(BF16), zrows], axis=0))
          + _dot_nt(jnp.concatenate([kb[i][1], qd[i][1]], axis=0).astype(BF16),
                    jnp.concatenate([zrows, k[i][1].astype(BF16)], axis=0)) for i in nd]
    tick()
    x = [jnp.where(strict, -(kq[i][:c] * decay[i]), 0.0) for i in nd]
    intra = [jnp.where(tri, kq[i][c:] * decay[i], 0.0).astype(BF16) for i in nd]
    p = [_dot(x[i].astype(BF16), block_diag(x[i])) for i in nd]
    t_off = x
    n_steps = c.bit_length() - 2
    for step in range(n_steps):
        tick()
        bd = [block_diag(p[i]) for i in nd]
        if step < n_steps - 1:
            pt = [_dot(jnp.concatenate([p[i], t_off[i]], axis=0).astype(BF16), bd[i]) for i in nd]
            t_off = [t_off[i] + p[i] + pt[i][c:] for i in nd]
            p = [pti[:c] for pti in pt]
        else:
            t_off = [t_off[i] + p[i] + _dot(t_off[i].astype(BF16), bd[i]) for i in nd]
    tick()
    rhs = [[jnp.concatenate([qkvz_ref[2, 2 * m + e, rows(j), :] * bcol[i][e], kb[i][e] * eg[i][e]], axis=1)
            for e in range(2)] for i, (j, m) in enumerate(duos)]
    rhs2 = [jnp.concatenate(rhs[i], axis=0).astype(BF16) for i in nd]
    sol = [[rhs[i][0] + _dot(jnp.where(lo, t_off[i], 0.0).astype(BF16), rhs2[i]),
            rhs[i][1] + _dot(jnp.where(lo, 0.0, t_off[i]).astype(BF16), rhs2[i])] for i in nd]
    wq = [[jnp.concatenate([sol[i][e][:, GDN_DV:], qd[i][e] * eg[i][e]], axis=0).astype(BF16) for e in range(2)]
          for i in nd]
    glast = [[g[c - 1:c, :] for g in gcol[i]] for i in nd]
    kdt = [jnp.concatenate([k[i][e] * jnp.exp(glast[i][e] - gcol[i][e]) for e in range(2)], axis=0).T.astype(BF16)
           for i in nd]
    ikd = [jnp.concatenate([intra[i], kdt[i]], axis=0) for i in nd]
    g_tot = [[jnp.exp(g) for g in glast[i]] for i in nd]

    for j in range(nc):
        tick()
        idx = [j * (GDN_HEADS // 2) + m for m in range(GDN_HEADS // 2)]
        s = [s_ref[h] for h in range(GDN_HEADS)]
        ws = [_dot(wq[idx[h // 2]][h % 2], s[h].astype(BF16)) for h in range(GDN_HEADS)]
        v_new = [(sol[idx[h // 2]][h % 2][:, :GDN_DV] - ws[h][:c]).astype(BF16) for h in range(GDN_HEADS)]
        upd = [_dot(ikd[idx[h // 2]], jnp.concatenate([v_new[h], zrows] if h % 2 == 0 else [zrows, v_new[h]], axis=0))
               for h in range(GDN_HEADS)]
        for h in range(GDN_HEADS):
            s_ref[h] = s[h] * g_tot[idx[h // 2]][h % 2] + upd[h][c:]
            o = ws[h][c:] + upd[h][:c]
            zg = qkvz_ref[3, h, rows(j), :]
            y_ref[rows(j), h * GDN_DV:(h + 1) * GDN_DV] = _gated_norm(o, zg, nw).astype(y_ref.dtype)

    for _ in step_part:
        pass

    @pl.when(n == pl.num_programs(0) - 1)
    def _():
        sfin_ref[...] = s_ref[...]
        for h in range(GDN_HEADS):
            ys_ref[:, h * GDN_DV:(h + 1) * GDN_DV] = yacc_ref[h].astype(ys_ref.dtype)


def _gdn_step_part(n, sb, qs_ref, gs_ref, nw_ref, st_ref, sto_ref, yacc_ref):
    nw = nw_ref[...]
    lane = lax.broadcasted_iota(jnp.int32, (1, LANES), 1)
    piece_row = lax.broadcasted_iota(jnp.int32, (2 * SUBLANES, LANES), 0)
    ones = jnp.where(piece_row < 3, 1.0, 0.0).astype(BF16)

    def spread(row):
        p1 = row.astype(BF16).astype(F32)
        r1 = row - p1
        p2 = r1.astype(BF16).astype(F32)
        p3 = r1 - p2
        pieces = jnp.where(piece_row == 0, p1, jnp.where(piece_row == 1, p2, jnp.where(piece_row == 2, p3, 0.0)))
        return lax.dot_general(pieces.astype(BF16), ones, (((0,), (0,)), ((), ())), preferred_element_type=F32)

    def pick(row, l):
        return jnp.sum(jnp.where(lane == l, row, 0.0), axis=1, keepdims=True)

    pairs = [(b, h) for b in range(sb) for h in range(GDN_HEADS)]
    m = range(len(pairs))
    rows = [pl.ds(n * sb + b, 1) for b in range(sb)]
    grow = [gs_ref[rows[b], :] for b in range(sb)]
    beta = [pick(grow[b], LANE_BETA + h) for b, h in pairs]
    eg = [pick(grow[b], LANE_EG + h) for b, h in pairs]
    yield
    qb = [spread(qs_ref[0, h, rows[b], :]) for b, h in pairs]
    yield
    kb = [spread(qs_ref[1, h, rows[b], :]) for b, h in pairs]
    yield
    s = [st_ref[b, h] * eg[i] for i, (b, h) in enumerate(pairs)]
    yield
    kv = [jnp.sum(s[i] * kb[i], axis=0, keepdims=True) for i in m]
    yield
    delta = [(qs_ref[2, h, rows[b], :] - kv[i]) * beta[i] for i, (b, h) in enumerate(pairs)]
    s = [s[i] + kb[i] * delta[i] for i in m]
    yield
    o = [jnp.sum(s[i] * qb[i], axis=0, keepdims=True) for i in m]
    yield
    for i, (b, h) in enumerate(pairs):
        sto_ref[b, h] = s[i]
        yacc_ref[h, rows[b], :] = _gated_norm(o[i], qs_ref[3, h, rows[b], :], nw)


def _gdn(qkvz, gate, qkvz_s, gate_s, norm_w, state, *, nc=GDN_CHUNKS_PER_STEP):
    t, b = qkvz.shape[2], qkvz_s.shape[2]
    c = GDN_CHUNK
    steps = t // (nc * c)
    assert t % (nc * c) == 0 and c & (c - 1) == 0 and b % steps == 0
    sb = b // steps
    state_spec = pl.BlockSpec((sb, GDN_HEADS, GDN_DK, GDN_DV), lambda n: (n, 0, 0, 0))
    const2 = lambda n: (0, 0)
    return pl.pallas_call(
        functools.partial(_gdn_body, nc, sb),
        grid=(steps,),
        in_specs=[
            pl.BlockSpec((4, GDN_HEADS, nc * c, LANES), lambda n: (0, 0, n, 0)),
            pl.BlockSpec((nc * c, LANES), lambda n: (n, 0)),
            pl.BlockSpec((1, GDN_DV), const2),
            pl.BlockSpec((4, GDN_HEADS, b, LANES), lambda n: (0, 0, 0, 0)),
            pl.BlockSpec((b, LANES), const2),
            state_spec,
        ],
        out_specs=[
            pl.BlockSpec((nc * c, D_GDN_V), lambda n: (n, 0)),
            pl.BlockSpec((GDN_HEADS, GDN_DK, GDN_DV), lambda n: (0, 0, 0)),
            pl.BlockSpec((b, D_GDN_V), const2),
            state_spec,
        ],
        out_shape=[
            jax.ShapeDtypeStruct((t, D_GDN_V), BF16),
            jax.ShapeDtypeStruct((GDN_HEADS, GDN_DK, GDN_DV), F32),
            jax.ShapeDtypeStruct((b, D_GDN_V), BF16),
            jax.ShapeDtypeStruct(state.shape, F32),
        ],
        scratch_shapes=[pltpu.VMEM((GDN_HEADS, GDN_DK, GDN_DV), F32), pltpu.VMEM((GDN_HEADS, b, GDN_DV), F32)],
        compiler_params=_cparams("arbitrary"),
        name="gdn",
    )(qkvz, gate, norm_w.reshape(1, GDN_DV), qkvz_s, gate_s, state)


def _gate_param_row(p):
    return jnp.zeros((1, LANES), F32).at[0, LANE_G:LANE_G + GDN_HEADS].set(p.astype(F32))


def _ffn(h, hs, x, xs, state_t, w_up, w_conv, w_down, next_norm, *, final=False, name):
    (hid, keep_a, keep_b), (hid_s, up_a, up_b) = _ffn_up(h, hs, w_up, w_conv, state_t)
    out, out_s = _matmul([hid], [hid_s], w_down, tm=FFN_DOWN_ROW_TILE, residual=x, residual2=xs,
                         norm_w=next_norm, final=final, chunk=FFN_DOWN_STAGE_ROWS, name=name)
    return out, out_s, jnp.concatenate([keep_a, keep_b], axis=1), jnp.concatenate([up_a, up_b], axis=1)


def kernel(x_prompt, x_sample, state_l0_pool, cache_l0_k, cache_l0_v, state_l0_ffn_conv, state_l1_gdn_conv, state_l1_gdn_S, state_l1_sconv, state_l1_ffn_conv, l0_norm_mix, l0_w_in, l0_pool_w, l0_pool_scale, l0_sinks, l0_w_out, l0_norm_ffn, l0_ffn_w_up, l0_ffn_conv, l0_ffn_w_down, l1_norm_mix, l1_w_in, l1_gdn_conv, l1_gdn_A_log, l1_gdn_dt_bias, l1_gdn_norm, l1_sconv_w, l1_w_out, l1_norm_ffn, l1_ffn_w_up, l1_ffn_conv, l1_ffn_w_down, final_norm):
    bp, t, d = x_prompt.shape
    nb, ts = x_sample.shape[:2]
    wb = cache_l0_k.shape[1]
    assert bp == 1 and ts == 1 and d == D_MODEL and wb == WINDOW and t >= WINDOW
    w_in1_t = l1_w_in.T
    a_log_row = _gate_param_row(l1_gdn_A_log)
    dt_bias_row = _gate_param_row(l1_gdn_dt_bias)
    pool_t = state_l0_pool.transpose(1, 0, 2)
    kt, vt = cache_l0_k.transpose(0, 2, 3, 1), cache_l0_v.transpose(0, 2, 3, 1)
    ffn0_t = state_l0_ffn_conv.transpose(1, 0, 2)
    gconv_t = state_l1_gdn_conv.transpose(1, 0, 2)
    sconv_t = state_l1_sconv.transpose(1, 0, 2)
    ffn1_t = state_l1_ffn_conv.transpose(1, 0, 2)
    last = SUBLANES

    xp, xs = x_prompt[0], x_sample[:, 0]
    z0, z0s = _matmul([_rmsnorm(xp, l0_norm_mix, BF16)], [_rmsnorm(xs, l0_norm_mix, BF16)], l0_w_in,
                      name="in_proj0")
    mix = _mixer0_seq(z0, l0_pool_w, l0_pool_scale, l0_sinks)
    y_pool = _pool_step(z0s, pool_t, l0_pool_w, l0_pool_scale)
    att, kt_new, vt_new = _attn_step(z0s, kt, vt, l0_sinks)
    mix_s = jnp.concatenate([y_pool, att.astype(BF16)], axis=1)
    (x1, h1), (x1s, h1s) = _matmul([mix], [mix_s], l0_w_out, residual=xp, residual2=xs,
                                   norm_w=l0_norm_ffn, name="out_proj0")
    (x2, h2), (x2s, h2s), ffn0_rows, ffn0_new = _ffn(
        h1, h1s, x1, x1s, ffn0_t, l0_ffn_w_up, l0_ffn_conv, l0_ffn_w_down, l1_norm_mix, name="ffn_down0")
    (qkvz, raw), (qkvz_s, raw_s) = _qkvz(h2, h2s, w_in1_t, l1_gdn_conv, gconv_t)
    (ysc, gate, mrows), (ysc_s, gate_s, m_s) = _tail(
        h2, h2s, w_in1_t, l1_sconv_w, a_log_row, dt_bias_row, sconv_t)
    ygdn, s_fin, ygdn_s, s_new = _gdn(qkvz, gate, qkvz_s, gate_s, l1_gdn_norm, state_l1_gdn_S)
    (x3, h3), (x3s, h3s) = _matmul([ygdn, ysc], [ygdn_s, ysc_s], l1_w_out, residual=x2,
                                   residual2=x2s, norm_w=l1_norm_ffn, name="out_proj1")
    y_prompt, y_sample, ffn1_rows, ffn1_new = _ffn(
        h3, h3s, x3, x3s, ffn1_t, l1_ffn_w_up, l1_ffn_conv, l1_ffn_w_down, final_norm, final=True,
        name="ffn_down1")

    p_pool = z0[t - POOL_BUF:, :D_POOL][None]
    p_k = z0[t - wb:, O_K:O_V].reshape(1, wb, N_KV_HEADS, HEAD_DIM)
    p_v = z0[t - wb:, O_V:].reshape(1, wb, N_KV_HEADS, HEAD_DIM)
    p_ffn0 = ffn0_rows[last - (FFN_CONV - 1):][None]
    p_gconv = raw[last - (GDN_CONV - 1):, :D_GDN_CONV][None]
    p_sconv = mrows[last - (SCONV_W - 1):][None]
    p_ffn1 = ffn1_rows[last - (FFN_CONV - 1):][None]

    def push(state_t, new_row):
        return jnp.concatenate([state_t[1:], new_row[None]], axis=0).transpose(1, 0, 2)

    s_pool = push(pool_t, z0s[:, :D_POOL])
    s_k, s_v = kt_new.transpose(0, 3, 1, 2), vt_new.transpose(0, 3, 1, 2)
    s_ffn0 = push(ffn0_t, ffn0_new)
    s_gconv = push(gconv_t, raw_s[:, :D_GDN_CONV])
    s_sconv = push(sconv_t, m_s)
    s_ffn1 = push(ffn1_t, ffn1_new)
    return (y_prompt[None], y_sample[:, None], p_pool, s_pool, p_k, s_k, p_v, s_v, p_ffn0, s_ffn0,
            p_gconv, s_gconv, s_fin[None], s_new, p_sconv, s_sconv, p_ffn1, s_ffn1)
```

```python
import functools

import jax
import jax.numpy as jnp
from jax import lax
from jax.experimental import pallas as pl
from jax.experimental.pallas import tpu as pltpu

F32 = jnp.float32
BF16 = jnp.bfloat16
EPS = 1e-6
NEG = -1e30

LANES = 128
SUBLANES = 8
MXU_COLS = 256
VMEM_LIMIT_BYTES = 60 * 1024 * 1024

ROW_TILE = 512
FFN_UP_ROW_TILE = 1024
FFN_UP_COL_TILE = 512
FFN_DOWN_ROW_TILE = 256
STAGE_ROWS = 256
FFN_DOWN_STAGE_ROWS = 512
GDN_CHUNKS_PER_STEP = 4
ATTN_STEP_SEQS = 8

D_MODEL = 2048
D_POOL = 512
POOL_WINDOWS = (2, 4, 8, 16)
POOL_BUF = max(POOL_WINDOWS) - 1
POOL_HIST = 16
HEAD_DIM = 64
N_Q_HEADS = 24
N_KV_HEADS = 4
Q_PER_KV = N_Q_HEADS // N_KV_HEADS
WINDOW = 128
D_ATTN = N_Q_HEADS * HEAD_DIM
D_KV = N_KV_HEADS * HEAD_DIM
D_IN0 = D_POOL + D_ATTN + 2 * D_KV
O_K = D_POOL + D_ATTN
O_V = O_K + D_KV
GDN_HEADS = 12
GDN_DK = 128
GDN_DV = 128
D_GDN_K = GDN_HEADS * GDN_DK
D_GDN_V = GDN_HEADS * GDN_DV
D_GDN_CONV = 2 * D_GDN_K + D_GDN_V
D_GDN_MAIN = D_GDN_CONV + D_GDN_V
GDN_CONV = 4
GDN_CHUNK = 64
D_SCONV = 512
SCONV_W = 3
D_TAIL = 3 * D_SCONV + LANES
D_FF = 5632
FFN_CONV = 3
LANE_BETA = 0
LANE_G = GDN_HEADS
LANE_EG = 2 * GDN_HEADS


def _cparams(*sem):
    return pltpu.CompilerParams(dimension_semantics=sem, vmem_limit_bytes=VMEM_LIMIT_BYTES)


def _dot(a, b):
    return jnp.dot(a, b, preferred_element_type=F32)


def _dot_nt(a, b):
    return lax.dot_general(a, b, (((1,), (1,)), ((), ())), preferred_element_type=F32)


def _silu(x):
    return x * jax.nn.sigmoid(x)


def _cast_rows(w_ref, wbf_ref, chunk=STAGE_ROWS):
    k = w_ref.shape[0]
    chunk = min(chunk, k)
    assert k % chunk == 0

    def body(c, carry):
        r = pl.multiple_of(c * chunk, chunk)
        wbf_ref[pl.ds(r, chunk), :] = w_ref[pl.ds(r, chunk), :].astype(BF16)
        return carry

    lax.fori_loop(0, k // chunk, body, 0)


def _cast_transposed(wt_ref, wbf_ref):
    n = wt_ref.shape[0]
    step = MXU_COLS if n % MXU_COLS == 0 else LANES
    for c in range(0, n, step):
        wbf_ref[:, c:c + step] = wt_ref[c:c + step, :].T.astype(BF16)


def _store_slabs(slab_ref, u):
    m = u.shape[0]
    for c in range(slab_ref.shape[0]):
        slab_ref[c, SUBLANES:SUBLANES + m, :] = u[:, c * LANES:(c + 1) * LANES]


def _tap(slab_ref, c, m, delay):
    return slab_ref[c, pl.ds(SUBLANES - delay, m), :]


def _roll_history(slab_ref, m):
    last = [slab_ref[c, m:m + SUBLANES, :] for c in range(slab_ref.shape[0])]
    for c, rows in enumerate(last):
        slab_ref[c, 0:SUBLANES, :] = rows
    return jnp.concatenate(last, axis=1)


def _rmsnorm_body(x_ref, w_ref, o_ref):
    x = x_ref[...]
    ms = jnp.mean(x * x, axis=-1, keepdims=True)
    o_ref[...] = (x * lax.rsqrt(ms + EPS) * w_ref[...]).astype(o_ref.dtype)


def _rmsnorm(x, w, out_dtype):
    m, d = x.shape
    tm = min(m, ROW_TILE)
    assert m % tm == 0
    return pl.pallas_call(
        _rmsnorm_body,
        grid=(m // tm,),
        in_specs=[pl.BlockSpec((tm, d), lambda i: (i, 0)), pl.BlockSpec((1, d), lambda i: (0, 0))],
        out_specs=pl.BlockSpec((tm, d), lambda i: (i, 0)),
        out_shape=jax.ShapeDtypeStruct((m, d), out_dtype),
        compiler_params=_cparams("parallel"),
        name="rmsnorm",
    )(x, w.reshape(1, d))


def _mm_body(n_pieces, n_stage, mode, *refs):
    n_out = {"plain": 1, "residual": 2, "final": 1}[mode]
    n_in = n_pieces + (0 if mode == "plain" else 1)
    set_a, set_b = refs[:n_in], refs[n_in:2 * n_in]
    w_ref = refs[2 * n_in]
    pos = 2 * n_in + 1
    nw_ref = None
    if mode != "plain":
        nw_ref = refs[pos]
        pos += 1
    out_a, out_b = refs[pos:pos + n_out], refs[pos + n_out:pos + 2 * n_out]
    wbf_ref = refs[pos + 2 * n_out]
    s = pl.program_id(0)
    last = pl.num_programs(0) - 1
    ck = w_ref.shape[0]

    @pl.when(s < n_stage)
    def _():
        _cast_rows(w_ref, wbf_ref.at[pl.ds(pl.multiple_of(s * ck, ck), ck), :])

    def rows(ins, outs):
        acc, row = None, 0
        for x_ref in ins[:n_pieces]:
            kp = x_ref.shape[1]
            d = _dot(x_ref[...], wbf_ref[row:row + kp, :])
            acc = d if acc is None else acc + d
            row += kp
        if mode == "plain":
            outs[0][...] = acc
            return
        xn = acc + ins[n_pieces][...]
        hn = xn * lax.rsqrt(jnp.mean(xn * xn, axis=-1, keepdims=True) + EPS) * nw_ref[...]
        if mode == "residual":
            outs[0][...] = xn
        outs[-1][...] = hn.astype(outs[-1].dtype)

    @pl.when((s >= n_stage) & (s < last))
    def _():
        rows(set_a, out_a)

    @pl.when(s == last)
    def _():
        rows(set_b, out_b)


def _matmul(xs, xs2, w, *, tm=ROW_TILE, residual=None, residual2=None, norm_w=None, final=False, chunk=STAGE_ROWS,
            name="matmul"):
    m, m2 = xs[0].shape[0], xs2[0].shape[0]
    k, n = w.shape
    tm = min(tm, m)
    assert m % tm == 0 and k % chunk == 0 and sum(x.shape[1] for x in xs) == k
    n_stage, nm = k // chunk, m // tm
    mode = "plain" if residual is None else ("final" if final else "residual")

    def row_tile(s):
        return jnp.clip(s - n_stage, 0, nm - 1)

    tile_a = pl.BlockSpec((tm, n), lambda s: (row_tile(s), 0))
    tile_b = pl.BlockSpec((m2, n), lambda s: (0, 0))
    in_a = [pl.BlockSpec((tm, x.shape[1]), lambda s: (row_tile(s), 0)) for x in xs]
    in_b = [pl.BlockSpec((m2, x.shape[1]), lambda s: (0, 0)) for x in xs2]
    args_a, args_b = list(xs), list(xs2)
    if mode != "plain":
        in_a.append(tile_a)
        in_b.append(tile_b)
        args_a.append(residual)
        args_b.append(residual2)
    in_specs = in_a + in_b + [pl.BlockSpec((chunk, n), lambda s: (jnp.minimum(s, n_stage - 1), 0))]
    args = args_a + args_b + [w]
    if mode != "plain":
        in_specs.append(pl.BlockSpec((1, n), lambda s: (0, 0)))
        args.append(norm_w.reshape(1, n))
    dtypes = [F32, BF16] if mode == "residual" else [F32]
    out_specs = [tile_a] * len(dtypes) + [tile_b] * len(dtypes)
    out_shape = ([jax.ShapeDtypeStruct((m, n), dt) for dt in dtypes]
                 + [jax.ShapeDtypeStruct((m2, n), dt) for dt in dtypes])
    outs = pl.pallas_call(
        functools.partial(_mm_body, len(xs), n_stage, mode),
        grid=(n_stage + nm + 1,),
        in_specs=in_specs,
        out_specs=out_specs,
        out_shape=out_shape,
        scratch_shapes=[pltpu.VMEM((k, n), BF16)],
        compiler_params=_cparams("arbitrary"),
        name=name,
    )(*args)
    half = len(dtypes)
    first, second = outs[:half], outs[half:]
    return (first[0], second[0]) if half == 1 else (tuple(first), tuple(second))


def _ffn_up_body(x_ref, xs_ref, wa_ref, wb_ref, cwa_ref, cwb_ref, a2_ref, a1_ref, b2_ref, b1_ref,
                 hid_ref, upa_ref, upb_ref, hids_ref, upsa_ref, upsb_ref, wa_bf, wb_bf, ca_ref, cb_ref):
    tm = x_ref.shape[0]
    i = pl.program_id(1)

    @pl.when(i == 0)
    def _():
        _cast_rows(wa_ref, wa_bf)
        _cast_rows(wb_ref, wb_bf)
        ca_ref[:, 0:SUBLANES, :] = jnp.zeros((ca_ref.shape[0], SUBLANES, LANES), F32)
        cb_ref[:, 0:SUBLANES, :] = jnp.zeros((cb_ref.shape[0], SUBLANES, LANES), F32)

    def gate(ua, ub, taps_a, taps_b):
        c_a = cwa_ref[0:1, :] * taps_a[0] + cwa_ref[1:2, :] * taps_a[1] + cwa_ref[2:3, :] * ua
        c_b = cwb_ref[0:1, :] * taps_b[0] + cwb_ref[1:2, :] * taps_b[1] + cwb_ref[2:3, :] * ub
        return (_silu(c_a) * c_b).astype(BF16)

    @pl.when(i > 0)
    def _():
        x = x_ref[...]
        _store_slabs(ca_ref, _dot(x, wa_bf[...]))
        _store_slabs(cb_ref, _dot(x, wb_bf[...]))
        for c in range(ca_ref.shape[0]):
            cols = slice(c * LANES, (c + 1) * LANES)
            c_a = (cwa_ref[0:1, cols] * _tap(ca_ref, c, tm, 2) + cwa_ref[1:2, cols] * _tap(ca_ref, c, tm, 1)
                   + cwa_ref[2:3, cols] * _tap(ca_ref, c, tm, 0))
            c_b = (cwb_ref[0:1, cols] * _tap(cb_ref, c, tm, 2) + cwb_ref[1:2, cols] * _tap(cb_ref, c, tm, 1)
                   + cwb_ref[2:3, cols] * _tap(cb_ref, c, tm, 0))
            hid_ref[:, cols] = (_silu(c_a) * c_b).astype(BF16)
        upa_ref[...] = _roll_history(ca_ref, tm)
        upb_ref[...] = _roll_history(cb_ref, tm)

    @pl.when(i == 0)
    def _():
        x = xs_ref[...]
        ua = _dot(x, wa_bf[...])
        ub = _dot(x, wb_bf[...])
        hids_ref[...] = gate(ua, ub, (a2_ref[...], a1_ref[...]), (b2_ref[...], b1_ref[...]))
        upsa_ref[...] = ua
        upsb_ref[...] = ub


def _ffn_up(h, hs, w_up, conv_w, state_t, *, tn=FFN_UP_COL_TILE, tm=FFN_UP_ROW_TILE):
    m, k = h.shape
    ms = hs.shape[0]
    tm = min(tm, m)
    nb, nm = D_FF // tn, m // tm
    assert m % tm == 0 and D_FF % tn == 0
    tile = lambda j, i: (jnp.maximum(i - 1, 0), j)
    in_specs = [
        pl.BlockSpec((tm, k), lambda j, i: (jnp.maximum(i - 1, 0), 0)),
        pl.BlockSpec((ms, k), lambda j, i: (0, 0)),
        pl.BlockSpec((k, tn), lambda j, i: (0, j)),
        pl.BlockSpec((k, tn), lambda j, i: (0, j + nb)),
        pl.BlockSpec((FFN_CONV, tn), lambda j, i: (0, j)),
        pl.BlockSpec((FFN_CONV, tn), lambda j, i: (0, j + nb)),
        pl.BlockSpec((None, ms, tn), lambda j, i: (0, 0, j)),
        pl.BlockSpec((None, ms, tn), lambda j, i: (1, 0, j)),
        pl.BlockSpec((None, ms, tn), lambda j, i: (0, 0, j + nb)),
        pl.BlockSpec((None, ms, tn), lambda j, i: (1, 0, j + nb)),
    ]
    keep_spec = pl.BlockSpec((SUBLANES, tn), lambda j, i: (0, j))
    step_spec = pl.BlockSpec((ms, tn), lambda j, i: (0, j))
    outs = pl.pallas_call(
        _ffn_up_body,
        grid=(nb, nm + 1),
        in_specs=in_specs,
        out_specs=[pl.BlockSpec((tm, tn), tile), keep_spec, keep_spec, step_spec, step_spec, step_spec],
        out_shape=[
            jax.ShapeDtypeStruct((m, D_FF), BF16),
            jax.ShapeDtypeStruct((SUBLANES, D_FF), F32),
            jax.ShapeDtypeStruct((SUBLANES, D_FF), F32),
            jax.ShapeDtypeStruct((ms, D_FF), BF16),
            jax.ShapeDtypeStruct((ms, D_FF), F32),
            jax.ShapeDtypeStruct((ms, D_FF), F32),
        ],
        scratch_shapes=[pltpu.VMEM((k, tn), BF16), pltpu.VMEM((k, tn), BF16),
                        pltpu.VMEM((tn // LANES, SUBLANES + tm, LANES), F32),
                        pltpu.VMEM((tn // LANES, SUBLANES + tm, LANES), F32)],
        compiler_params=_cparams("arbitrary", "arbitrary"),
        name="ffn_up",
    )(h, hs, w_up, w_up, conv_w, conv_w, state_t, state_t, state_t, state_t)
    return outs[:3], outs[3:]


def _half_lane_pair(x, head_in_high, lo_mask):
    if head_in_high:
        hi = jnp.where(lo_mask, 0.0, x)
        return pltpu.roll(hi, HEAD_DIM, 1), hi
    lo = jnp.where(lo_mask, x, 0.0)
    return lo, pltpu.roll(lo, HEAD_DIM, 1)


def _mixer0_seq_body(sink_ref, z_ref, kvp_ref, up_ref, pw_ref, ps_ref, o_ref):
    n = pl.program_id(0)
    first = n == 0
    w = WINDOW
    lo_mask = lax.broadcasted_iota(jnp.int32, (1, LANES), 1) < HEAD_DIM

    hist = jnp.where(first, 0.0, up_ref[...])
    pos = n * w + lax.broadcasted_iota(jnp.int32, (w, 1), 0)
    for g, win in enumerate(POOL_WINDOWS):
        sl = slice(g * LANES, (g + 1) * LANES)
        e = jnp.concatenate([hist[:, sl], z_ref[:, sl]], axis=0)
        s, sh = e, 1
        while sh < win:
            s = s + pltpu.roll(s, sh, 0)
            sh *= 2
        cnt = jnp.minimum(pos + 1, win).astype(F32)
        diff = s[POOL_HIST:] / cnt - e[POOL_HIST:]
        y = _dot(diff.astype(BF16), pw_ref[g].astype(BF16)) * ps_ref[:, sl]
        o_ref[:, sl] = y.astype(o_ref.dtype)

    row = lax.broadcasted_iota(jnp.int32, (w, 2 * w), 0)
    col = lax.broadcasted_iota(jnp.int32, (w, 2 * w), 1)
    valid = (col >= row) & (col <= row + w) & (jnp.logical_not(first) | (col >= w))
    tiles_per_kv = Q_PER_KV * HEAD_DIM // LANES
    for c in range(D_KV // LANES):
        k2 = jnp.concatenate([kvp_ref[:, c * LANES:(c + 1) * LANES],
                              z_ref[:, O_K + c * LANES:O_K + (c + 1) * LANES]], axis=0)
        v2 = jnp.concatenate([kvp_ref[:, D_KV + c * LANES:D_KV + (c + 1) * LANES],
                              z_ref[:, O_V + c * LANES:O_V + (c + 1) * LANES]], axis=0)
        for p in range(2):
            hk = 2 * c + p
            k_lo, k_hi = (t.astype(BF16) for t in _half_lane_pair(k2, p == 1, lo_mask))
            v_lo, v_hi = (t.astype(BF16) for t in _half_lane_pair(v2, p == 1, lo_mask))
            q0 = D_POOL + hk * Q_PER_KV * HEAD_DIM
            qst = jnp.concatenate(
                [z_ref[:, q0 + a * LANES:q0 + (a + 1) * LANES] for a in range(tiles_per_kv)], axis=0)
            qst = (qst * HEAD_DIM ** -0.5).astype(BF16)
            s_lo = _dot_nt(qst, k_lo)
            s_hi = _dot_nt(qst, k_hi)
            for a in range(tiles_per_kv):
                probs, inv = [], []
                for par, s_all in ((0, s_lo), (1, s_hi)):
                    sk = sink_ref[hk * Q_PER_KV + 2 * a + par]
                    s = jnp.where(valid, s_all[a * w:(a + 1) * w], NEG)
                    mx = jnp.maximum(jnp.max(s, axis=-1, keepdims=True), sk)
                    pr = jnp.exp(s - mx)
                    den = jnp.sum(pr, axis=-1, keepdims=True) + jnp.exp(sk - mx)
                    probs.append(pr.astype(BF16))
                    inv.append(1.0 / den)
                o = _dot(probs[0], v_lo) + _dot(probs[1], v_hi)
                o = o * jnp.where(lo_mask, inv[0], inv[1])
                o_ref[:, q0 + a * LANES:q0 + (a + 1) * LANES] = o.astype(o_ref.dtype)


def _mixer0_seq(z0, pool_w, pool_scale, sinks):
    t = z0.shape[0]
    w = WINDOW
    assert t % w == 0
    hist_blocks = w // POOL_HIST
    return pl.pallas_call(
        _mixer0_seq_body,
        grid=(t // w,),
        in_specs=[
            pl.BlockSpec(memory_space=pltpu.SMEM),
            pl.BlockSpec((w, D_IN0), lambda n: (n, 0)),
            pl.BlockSpec((w, 2 * D_KV), lambda n: (jnp.maximum(n - 1, 0), O_K // (2 * D_KV))),
            pl.BlockSpec((POOL_HIST, D_POOL), lambda n: (jnp.maximum(n * hist_blocks - 1, 0), 0)),
            pl.BlockSpec((len(POOL_WINDOWS), LANES, LANES), lambda n: (0, 0, 0)),
            pl.BlockSpec((1, D_POOL), lambda n: (0, 0)),
        ],
        out_specs=pl.BlockSpec((w, D_POOL + D_ATTN), lambda n: (n, 0)),
        out_shape=jax.ShapeDtypeStruct((t, D_POOL + D_ATTN), BF16),
        compiler_params=_cparams("parallel"),
        name="mixer0_seq",
    )(sinks, z0, z0, z0, pool_w, pool_scale.reshape(1, D_POOL))


def _pool_step_body(*refs):
    hist_refs = refs[:POOL_BUF]
    z_ref, pw_ref, ps_ref, o_ref = refs[POOL_BUF:]
    for g, win in enumerate(POOL_WINDOWS):
        sl = slice(g * LANES, (g + 1) * LANES)
        u = z_ref[:, sl]
        s = u
        for r in range(POOL_BUF - (win - 1), POOL_BUF):
            s = s + hist_refs[r][:, sl]
        diff = s / float(win) - u
        y = _dot(diff.astype(BF16), pw_ref[g].astype(BF16)) * ps_ref[:, sl]
        o_ref[:, sl] = y.astype(o_ref.dtype)


def _pool_step(z0, hist_t, pool_w, pool_scale):
    b = z0.shape[0]
    in_specs = [pl.BlockSpec((None, b, D_POOL), functools.partial(lambda i, r: (r, 0, 0), r=r))
                for r in range(POOL_BUF)]
    in_specs += [
        pl.BlockSpec((b, D_POOL), lambda i: (0, 0)),
        pl.BlockSpec((len(POOL_WINDOWS), LANES, LANES), lambda i: (0, 0, 0)),
        pl.BlockSpec((1, D_POOL), lambda i: (0, 0)),
    ]
    return pl.pallas_call(
        _pool_step_body,
        grid=(1,),
        in_specs=in_specs,
        out_specs=pl.BlockSpec((b, D_POOL), lambda i: (0, 0)),
        out_shape=jax.ShapeDtypeStruct((b, D_POOL), BF16),
        compiler_params=_cparams("arbitrary"),
        name="pool_step",
    )(*([hist_t] * POOL_BUF), z0, pool_w, pool_scale.reshape(1, D_POOL))


def _attn_step_body(q_ref, kn_ref, vn_ref, knt_ref, vnt_ref, kt_ref, vt_ref, sink_ref, o_ref, kto_ref, vto_ref):
    bs = q_ref.shape[0]
    newest = lax.broadcasted_iota(jnp.int32, (HEAD_DIM, WINDOW), 1) == WINDOW - 1
    pairs = [(b, hk) for b in range(bs) for hk in range(N_KV_HEADS)]
    n = range(len(pairs))
    kt = [kt_ref[b, hk] for b, hk in pairs]
    vt = [vt_ref[b, hk] for b, hk in pairs]
    q = [q_ref[b, hk] * HEAD_DIM ** -0.5 for b, hk in pairs]
    sk = [sink_ref[hk * Q_PER_KV:(hk + 1) * Q_PER_KV, :] for b, hk in pairs]
    s = [_dot(q[i].astype(BF16), kt[i].astype(BF16)) for i in n]
    s_new = [jnp.sum(q[i] * kn_ref[b, hk:hk + 1, :], axis=-1, keepdims=True) for i, (b, hk) in enumerate(pairs)]
    mx = [jnp.maximum(jnp.maximum(jnp.max(s[i], axis=-1, keepdims=True), s_new[i]), sk[i]) for i in n]
    pr = [jnp.exp(s[i] - mx[i]) for i in n]
    pn = [jnp.exp(s_new[i] - mx[i]) for i in n]
    den = [jnp.sum(pr[i], axis=-1, keepdims=True) + pn[i] + jnp.exp(sk[i] - mx[i]) for i in n]
    o = [_dot_nt(pr[i].astype(BF16), vt[i].astype(BF16)) for i in n]
    for i, (b, hk) in enumerate(pairs):
        o_ref[b, hk] = (o[i] + pn[i] * vn_ref[b, hk:hk + 1, :]) / den[i]
        kto_ref[b, hk] = jnp.where(newest, pltpu.roll(knt_ref[hk], WINDOW - 1 - b, 1),
                                   pltpu.roll(kt[i], WINDOW - 1, 1))
        vto_ref[b, hk] = jnp.where(newest, pltpu.roll(vnt_ref[hk], WINDOW - 1 - b, 1),
                                   pltpu.roll(vt[i], WINDOW - 1, 1))


def _attn_step(z0, kt, vt, sinks, *, bs=ATTN_STEP_SEQS):
    b = z0.shape[0]
    assert b % bs == 0 and kt.shape[3] == WINDOW == LANES and bs <= WINDOW
    nblk = b // bs
    q4 = z0[:, D_POOL:O_K].reshape(b, N_KV_HEADS, Q_PER_KV, HEAD_DIM)
    kn = z0[:, O_K:O_V].reshape(b, N_KV_HEADS, HEAD_DIM)
    vn = z0[:, O_V:].reshape(b, N_KV_HEADS, HEAD_DIM)

    def columns(x):
        xt = x.reshape(nblk, bs, N_KV_HEADS, HEAD_DIM).transpose(0, 2, 3, 1)
        return jnp.pad(xt, ((0, 0), (0, 0), (0, 0), (0, WINDOW - bs)))

    cache_spec = pl.BlockSpec((bs, N_KV_HEADS, HEAD_DIM, WINDOW), lambda i: (i, 0, 0, 0))
    col_spec = pl.BlockSpec((None, N_KV_HEADS, HEAD_DIM, WINDOW), lambda i: (i, 0, 0, 0))
    new_spec = pl.BlockSpec((bs, N_KV_HEADS, HEAD_DIM), lambda i: (i, 0, 0))
    q_spec = pl.BlockSpec((bs, N_KV_HEADS, Q_PER_KV, HEAD_DIM), lambda i: (i, 0, 0, 0))
    att, kto, vto = pl.pallas_call(
        _attn_step_body,
        grid=(nblk,),
        in_specs=[q_spec, new_spec, new_spec, col_spec, col_spec, cache_spec, cache_spec,
                  pl.BlockSpec((N_Q_HEADS, 1), lambda i: (0, 0))],
        out_specs=[q_spec, cache_spec, cache_spec],
        out_shape=[jax.ShapeDtypeStruct(q4.shape, F32), jax.ShapeDtypeStruct(kt.shape, F32),
                   jax.ShapeDtypeStruct(vt.shape, F32)],
        compiler_params=_cparams("parallel"),
        name="attn_step",
    )(q4, kn, vn, columns(kn), columns(vn), kt, vt, sinks.reshape(N_Q_HEADS, 1))
    return att.reshape(b, D_ATTN), kto, vto


def _qkvz_body(x_ref, xs_ref, w_ref, cw_ref, p3_ref, p2_ref, p1_ref,
               o_ref, raw_ref, os_ref, raws_ref, w_bf, carry_ref):
    tm = x_ref.shape[0]
    j = pl.program_id(0)
    i = pl.program_id(1)

    @pl.when(i == 0)
    def _():
        _cast_transposed(w_ref, w_bf)
        carry_ref[:, 0:SUBLANES, :] = jnp.zeros((carry_ref.shape[0], SUBLANES, LANES), F32)

    def heads(a):
        return [a[:, h * LANES:(h + 1) * LANES] for h in range(GDN_HEADS)]

    def l2norm(a):
        return a * lax.rsqrt(jnp.sum(a * a, axis=-1, keepdims=True) + EPS)

    def finish(u, taps, out_ref):
        def conv_act():
            t3, t2, t1 = taps()
            c = cw_ref[3:4, :] * u
            c = cw_ref[0:1, :] * t3 + cw_ref[1:2, :] * t2 + cw_ref[2:3, :] * t1 + c
            return _silu(c)

        @pl.when(j == 0)
        def _():
            for h, a in enumerate(heads(conv_act())):
                out_ref[h] = l2norm(a) * GDN_DK ** -0.5

        @pl.when(j == 1)
        def _():
            for h, a in enumerate(heads(conv_act())):
                out_ref[h] = l2norm(a)

        @pl.when(j == 2)
        def _():
            for h, a in enumerate(heads(conv_act())):
                out_ref[h] = a

        @pl.when(j == 3)
        def _():
            for h, a in enumerate(heads(u)):
                out_ref[h] = a

    @pl.when(i > 0)
    def _():
        u = _dot(x_ref[...], w_bf[...])
        _store_slabs(carry_ref, u)

        def taps():
            return tuple(jnp.concatenate([_tap(carry_ref, c, tm, k) for c in range(carry_ref.shape[0])], axis=1)
                         for k in (3, 2, 1))

        finish(u, taps, o_ref)
        raw_ref[...] = _roll_history(carry_ref, tm)

    @pl.when(i == 0)
    def _():
        u = _dot(xs_ref[...], w_bf[...])
        raws_ref[...] = u
        finish(u, lambda: (p3_ref[...], p2_ref[...], p1_ref[...]), os_ref)


def _qkvz(h, hs, w_in_t, conv_w, state_t, *, tm=ROW_TILE):
    m, k = h.shape
    ms = hs.shape[0]
    tm = min(tm, m)
    tn = D_GDN_K
    assert D_GDN_K == D_GDN_V and GDN_DK == LANES and m % tm == 0
    nj, nm = D_GDN_MAIN // tn, m // tm
    in_specs = [
        pl.BlockSpec((tm, k), lambda j, i: (jnp.maximum(i - 1, 0), 0)),
        pl.BlockSpec((ms, k), lambda j, i: (0, 0)),
        pl.BlockSpec((tn, k), lambda j, i: (j, 0)),
        pl.BlockSpec((GDN_CONV, tn), lambda j, i: (0, jnp.minimum(j, 2))),
    ]
    for r in range(GDN_CONV - 1):
        in_specs.append(pl.BlockSpec((None, ms, tn), functools.partial(lambda j, i, r: (r, 0, jnp.minimum(j, 2)), r=r)))
    outs = pl.pallas_call(
        _qkvz_body,
        grid=(nj, nm + 1),
        in_specs=in_specs,
        out_specs=[
            pl.BlockSpec((None, GDN_HEADS, tm, LANES), lambda j, i: (j, 0, jnp.maximum(i - 1, 0), 0)),
            pl.BlockSpec((SUBLANES, tn), lambda j, i: (0, j)),
            pl.BlockSpec((None, GDN_HEADS, ms, LANES), lambda j, i: (j, 0, 0, 0)),
            pl.BlockSpec((ms, tn), lambda j, i: (0, j)),
        ],
        out_shape=[
            jax.ShapeDtypeStruct((nj, GDN_HEADS, m, LANES), F32),
            jax.ShapeDtypeStruct((SUBLANES, D_GDN_MAIN), F32),
            jax.ShapeDtypeStruct((nj, GDN_HEADS, ms, LANES), F32),
            jax.ShapeDtypeStruct((ms, D_GDN_MAIN), F32),
        ],
        scratch_shapes=[pltpu.VMEM((k, tn), BF16), pltpu.VMEM((tn // LANES, SUBLANES + tm, LANES), F32)],
        compiler_params=_cparams("arbitrary", "arbitrary"),
        name="qkvz",
    )(h, hs, w_in_t, conv_w, state_t, state_t, state_t)
    return outs[:2], outs[2:]


def _tail_body(x_ref, xs_ref, wa_ref, wb_ref, cw_ref, alog_ref, dtb_ref, p2_ref, p1_ref,
               ysc_ref, gate_ref, m_ref, yscs_ref, gates_ref, ms_ref, w_bf, carry_ref):
    tm = x_ref.shape[0]
    i = pl.program_id(0)

    @pl.when(i == 0)
    def _():
        n_raw = wb_ref.shape[0]
        step = MXU_COLS
        for c in range(0, 3 * D_SCONV, step):
            r0 = n_raw + c
            if r0 + step <= wa_ref.shape[0]:
                blk = wa_ref[r0:r0 + step, :]
            else:
                blk = jnp.concatenate([wa_ref[r0:, :], wb_ref[...]], axis=0)
            w_bf[:, c:c + step] = blk.T.astype(BF16)
        raw = jnp.concatenate([wa_ref[0:n_raw, :], jnp.zeros((LANES - n_raw, wa_ref.shape[1]), F32)], axis=0)
        w_bf[:, 3 * D_SCONV:] = raw.T.astype(BF16)
        carry_ref[:, 0:SUBLANES, :] = jnp.zeros((carry_ref.shape[0], SUBLANES, LANES), F32)

    def finish(z, mm, taps, y_ref, g_ref):
        conv = cw_ref[0:1, :] * taps[0] + cw_ref[1:2, :] * taps[1] + cw_ref[2:3, :] * mm
        y_ref[...] = (z[:, 0:D_SCONV] * conv).astype(y_ref.dtype)
        raw = z[:, 3 * D_SCONV:]
        lane = lax.broadcasted_iota(jnp.int32, raw.shape, 1)
        beta = jax.nn.sigmoid(raw)
        sp = raw + dtb_ref[...]
        softplus = jnp.maximum(sp, 0.0) + jnp.log1p(jnp.exp(-jnp.abs(sp)))
        g = -jnp.exp(alog_ref[...]) * softplus
        eg = pltpu.roll(jnp.exp(g), LANE_EG - LANE_G, 1)
        g_ref[...] = jnp.where(lane < LANE_G, beta, jnp.where(lane < LANE_EG, g, eg))

    @pl.when(i > 0)
    def _():
        z = _dot(x_ref[...], w_bf[...])
        mm = z[:, D_SCONV:2 * D_SCONV] * z[:, 2 * D_SCONV:3 * D_SCONV]
        _store_slabs(carry_ref, mm)
        taps = tuple(jnp.concatenate([_tap(carry_ref, c, tm, k) for c in range(carry_ref.shape[0])], axis=1)
                     for k in (2, 1))
        finish(z, mm, taps, ysc_ref, gate_ref)
        m_ref[...] = _roll_history(carry_ref, tm)

    @pl.when(i == 0)
    def _():
        z = _dot(xs_ref[...], w_bf[...])
        mm = z[:, D_SCONV:2 * D_SCONV] * z[:, 2 * D_SCONV:3 * D_SCONV]
        ms_ref[...] = mm
        finish(z, mm, (p2_ref[...], p1_ref[...]), yscs_ref, gates_ref)


def _tail(h, hs, w_in_t, conv_w, a_log_row, dt_bias_row, state_t, *, tm=ROW_TILE):
    m, k = h.shape
    ms = hs.shape[0]
    tm = min(tm, m)
    n_raw, n_conv = 2 * GDN_HEADS, 3 * D_SCONV
    assert m % tm == 0 and w_in_t.shape[0] == D_GDN_MAIN + n_raw + n_conv
    assert D_GDN_MAIN % n_conv == 0 and (D_GDN_MAIN + n_conv) % n_raw == 0 and n_raw % SUBLANES == 0
    nm = m // tm
    tile = lambda i: (jnp.maximum(i - 1, 0), 0)
    const = lambda i: (0, 0)
    in_specs = [
        pl.BlockSpec((tm, k), tile),
        pl.BlockSpec((ms, k), const),
        pl.BlockSpec((n_conv, k), lambda i: (D_GDN_MAIN // n_conv, 0)),
        pl.BlockSpec((n_raw, k), lambda i: ((D_GDN_MAIN + n_conv) // n_raw, 0)),
        pl.BlockSpec((SCONV_W, D_SCONV), const),
        pl.BlockSpec((1, LANES), const),
        pl.BlockSpec((1, LANES), const),
        pl.BlockSpec((None, ms, D_SCONV), lambda i: (0, 0, 0)),
        pl.BlockSpec((None, ms, D_SCONV), lambda i: (1, 0, 0)),
    ]
    outs = pl.pallas_call(
        _tail_body,
        grid=(nm + 1,),
        in_specs=in_specs,
        out_specs=[
            pl.BlockSpec((tm, D_SCONV), tile),
            pl.BlockSpec((tm, LANES), tile),
            pl.BlockSpec((SUBLANES, D_SCONV), const),
            pl.BlockSpec((ms, D_SCONV), const),
            pl.BlockSpec((ms, LANES), const),
            pl.BlockSpec((ms, D_SCONV), const),
        ],
        out_shape=[
            jax.ShapeDtypeStruct((m, D_SCONV), BF16),
            jax.ShapeDtypeStruct((m, LANES), F32),
            jax.ShapeDtypeStruct((SUBLANES, D_SCONV), F32),
            jax.ShapeDtypeStruct((ms, D_SCONV), BF16),
            jax.ShapeDtypeStruct((ms, LANES), F32),
            jax.ShapeDtypeStruct((ms, D_SCONV), F32),
        ],
        scratch_shapes=[pltpu.VMEM((k, D_TAIL), BF16),
                        pltpu.VMEM((D_SCONV // LANES, SUBLANES + tm, LANES), F32)],
        compiler_params=_cparams("arbitrary"),
        name="tail",
    )(h, hs, w_in_t, w_in_t, conv_w, a_log_row, dt_bias_row, state_t, state_t)
    return outs[:3], outs[3:]


def _gated_norm(o, zg, nw):
    y = o * lax.rsqrt(jnp.mean(o * o, axis=-1, keepdims=True) + EPS) * nw
    return y * _silu(zg)


def _gdn_body(nc, sb, qkvz_ref, gate_ref, nw_ref, qs_ref, gs_ref, st_ref,
              y_ref, sfin_ref, ys_ref, sto_ref, s_ref, yacc_ref):
    c = GDN_CHUNK
    n = pl.program_id(0)
    step_part = _gdn_step_part(n, sb, qs_ref, gs_ref, nw_ref, st_ref, sto_ref, yacc_ref)

    def tick():
        next(step_part, None)

    @pl.when(n == 0)
    def _():
        s_ref[...] = jnp.zeros_like(s_ref)

    assert 2 * c == LANES and GDN_HEADS % 2 == 0
    r = lax.broadcasted_iota(jnp.int32, (c, LANES), 0)
    lane = lax.broadcasted_iota(jnp.int32, (c, LANES), 1)
    lo = lane < c
    col = jnp.where(lo, lane, lane - c)
    tri = r >= col
    strict = r > col
    rr = lax.broadcasted_iota(jnp.int32, (c, c), 0)
    ones = jnp.where(rr >= lax.broadcasted_iota(jnp.int32, (c, c), 1), 1.0, 0.0).astype(BF16)
    zpad = jnp.zeros((LANES - c, LANES), F32)
    zrows = jnp.zeros((c, LANES), BF16)
    nw = nw_ref[...]
    duos = [(j, m) for j in range(nc) for m in range(GDN_HEADS // 2)]
    nd = range(len(duos))

    def block_diag(a):
        return jnp.concatenate([jnp.where(lo, a, 0.0), jnp.where(lo, 0.0, a)], axis=0).astype(BF16)

    gates, gcs, gcts = [], [], []
    for j in range(nc):
        gate = gate_ref[j * c:(j + 1) * c, :]
        g1 = gate.astype(BF16)
        r1 = gate - g1.astype(F32)
        g2 = r1.astype(BF16)
        g3 = (r1 - g2.astype(F32)).astype(BF16)
        gc = _dot(ones, g1) + _dot(ones, g2) + _dot(ones, g3)
        gates.append(gate)
        gcs.append(gc)
        gcts.append(jnp.concatenate([gc, zpad], axis=0).T)

    def rows(j):
        return slice(j * c, (j + 1) * c)

    def per_head(fn):
        return [[fn(j, 2 * m + e) for e in range(2)] for j, m in duos]

    gcol = per_head(lambda j, h: gcs[j][:, LANE_G + h:LANE_G + h + 1])
    bcol = per_head(lambda j, h: gates[j][:, LANE_BETA + h:LANE_BETA + h + 1])
    grow = [jnp.concatenate([gcts[j][LANE_G + 2 * m:LANE_G + 2 * m + 1, 0:c],
                             gcts[j][LANE_G + 2 * m + 1:LANE_G + 2 * m + 2, 0:c]], axis=1) for j, m in duos]
    decay = [jnp.exp(jnp.where(tri, jnp.where(lo, gcol[i][0], gcol[i][1]) - grow[i], NEG)) for i in nd]
    eg = [[jnp.exp(g) for g in gcol[i]] for i in nd]
    k = per_head(lambda j, h: qkvz_ref[1, h, rows(j), :])
    qd = per_head(lambda j, h: qkvz_ref[0, h, rows(j), :])
    kb = [[k[i][e] * bcol[i][e] for e in range(2)] for i in nd]
    kq = [_dot_nt(jnp.concatenate([kb[i][0], qd[i][0]], axis=0).astype(BF16),
                  jnp.concatenate([k[i][0].astype(BF16), zrows], axis=0))
          + _dot_nt(jnp.concatenate([kb[i][1], qd[i][1]], axis=0).astype(BF16),
                    jnp.concatenate([zrows, k[i][1].astype(BF16)], axis=0)) for i in nd]
    tick()
    x = [jnp.where(strict, -(kq[i][:c] * decay[i]), 0.0) for i in nd]
    intra = [jnp.where(tri, kq[i][c:] * decay[i], 0.0).astype(BF16) for i in nd]
    p = [_dot(x[i].astype(BF16), block_diag(x[i])) for i in nd]
    t_off = x
    n_steps = c.bit_length() - 2
    for step in range(n_steps):
        tick()
        bd = [block_diag(p[i]) for i in nd]
        if step < n_steps - 1:
            pt = [_dot(jnp.concatenate([p[i], t_off[i]], axis=0).astype(BF16), bd[i]) for i in nd]
            t_off = [t_off[i] + p[i] + pt[i][c:] for i in nd]
            p = [pti[:c] for pti in pt]
        else:
            t_off = [t_off[i] + p[i] + _dot(t_off[i].astype(BF16), bd[i]) for i in nd]
    tick()
    rhs = [[jnp.concatenate([qkvz_ref[2, 2 * m + e, rows(j), :] * bcol[i][e], kb[i][e] * eg[i][e]], axis=1)
            for e in range(2)] for i, (j, m) in enumerate(duos)]
    rhs2 = [jnp.concatenate(rhs[i], axis=0).astype(BF16) for i in nd]
    sol = [[rhs[i][0] + _dot(jnp.where(lo, t_off[i], 0.0).astype(BF16), rhs2[i]),
            rhs[i][1] + _dot(jnp.where(lo, 0.0, t_off[i]).astype(BF16), rhs2[i])] for i in nd]
    wq = [[jnp.concatenate([sol[i][e][:, GDN_DV:], qd[i][e] * eg[i][e]], axis=0).astype(BF16) for e in range(2)]
          for i in nd]
    glast = [[g[c - 1:c, :] for g in gcol[i]] for i in nd]
    kdt = [jnp.concatenate([k[i][e] * jnp.exp(glast[i][e] - gcol[i][e]) for e in range(2)], axis=0).T.astype(BF16)
           for i in nd]
    ikd = [jnp.concatenate([intra[i], kdt[i]], axis=0) for i in nd]
    g_tot = [[jnp.exp(g) for g in glast[i]] for i in nd]

    for j in range(nc):
        tick()
        idx = [j * (GDN_HEADS // 2) + m for m in range(GDN_HEADS // 2)]
        s = [s_ref[h] for h in range(GDN_HEADS)]
        ws = [_dot(wq[idx[h // 2]][h % 2], s[h].astype(BF16)) for h in range(GDN_HEADS)]
        v_new = [(sol[idx[h // 2]][h % 2][:, :GDN_DV] - ws[h][:c]).astype(BF16) for h in range(GDN_HEADS)]
        upd = [_dot(ikd[idx[h // 2]], jnp.concatenate([v_new[h], zrows] if h % 2 == 0 else [zrows, v_new[h]], axis=0))
               for h in range(GDN_HEADS)]
        for h in range(GDN_HEADS):
            s_ref[h] = s[h] * g_tot[idx[h // 2]][h % 2] + upd[h][c:]
            o = ws[h][c:] + upd[h][:c]
            zg = qkvz_ref[3, h, rows(j), :]
            y_ref[rows(j), h * GDN_DV:(h + 1) * GDN_DV] = _gated_norm(o, zg, nw).astype(y_ref.dtype)

    for _ in step_part:
        pass

    @pl.when(n == pl.num_programs(0) - 1)
    def _():
        sfin_ref[...] = s_ref[...]
        for h in range(GDN_HEADS):
            ys_ref[:, h * GDN_DV:(h + 1) * GDN_DV] = yacc_ref[h].astype(ys_ref.dtype)


def _gdn_step_part(n, sb, qs_ref, gs_ref, nw_ref, st_ref, sto_ref, yacc_ref):
    nw = nw_ref[...]
    lane = lax.broadcasted_iota(jnp.int32, (1, LANES), 1)
    piece_row = lax.broadcasted_iota(jnp.int32, (2 * SUBLANES, LANES), 0)
    ones = jnp.where(piece_row < 3, 1.0, 0.0).astype(BF16)

    def spread(row):
        p1 = row.astype(BF16).astype(F32)
        r1 = row - p1
        p2 = r1.astype(BF16).astype(F32)
        p3 = r1 - p2
        pieces = jnp.where(piece_row == 0, p1, jnp.where(piece_row == 1, p2, jnp.where(piece_row == 2, p3, 0.0)))
        return lax.dot_general(pieces.astype(BF16), ones, (((0,), (0,)), ((), ())), preferred_element_type=F32)

    def pick(row, l):
        return jnp.sum(jnp.where(lane == l, row, 0.0), axis=1, keepdims=True)

    pairs = [(b, h) for b in range(sb) for h in range(GDN_HEADS)]
    m = range(len(pairs))
    rows = [pl.ds(n * sb + b, 1) for b in range(sb)]
    grow = [gs_ref[rows[b], :] for b in range(sb)]
    beta = [pick(grow[b], LANE_BETA + h) for b, h in pairs]
    eg = [pick(grow[b], LANE_EG + h) for b, h in pairs]
    yield
    qb = [spread(qs_ref[0, h, rows[b], :]) for b, h in pairs]
    yield
    kb = [spread(qs_ref[1, h, rows[b], :]) for b, h in pairs]
    yield
    s = [st_ref[b, h] * eg[i] for i, (b, h) in enumerate(pairs)]
    yield
    kv = [jnp.sum(s[i] * kb[i], axis=0, keepdims=True) for i in m]
    yield
    delta = [(qs_ref[2, h, rows[b], :] - kv[i]) * beta[i] for i, (b, h) in enumerate(pairs)]
    s = [s[i] + kb[i] * delta[i] for i in m]
    yield
    o = [jnp.sum(s[i] * qb[i], axis=0, keepdims=True) for i in m]
    yield
    for i, (b, h) in enumerate(pairs):
        sto_ref[b, h] = s[i]
        yacc_ref[h, rows[b], :] = _gated_norm(o[i], qs_ref[3, h, rows[b], :], nw)


def _gdn(qkvz, gate, qkvz_s, gate_s, norm_w, state, *, nc=GDN_CHUNKS_PER_STEP):
    t, b = qkvz.shape[2], qkvz_s.shape[2]
    c = GDN_CHUNK
    steps = t // (nc * c)
    assert t % (nc * c) == 0 and c & (c - 1) == 0 and b % steps == 0
    sb = b // steps
    state_spec = pl.BlockSpec((sb, GDN_HEADS, GDN_DK, GDN_DV), lambda n: (n, 0, 0, 0))
    const2 = lambda n: (0, 0)
    return pl.pallas_call(
        functools.partial(_gdn_body, nc, sb),
        grid=(steps,),
        in_specs=[
            pl.BlockSpec((4, GDN_HEADS, nc * c, LANES), lambda n: (0, 0, n, 0)),
            pl.BlockSpec((nc * c, LANES), lambda n: (n, 0)),
            pl.BlockSpec((1, GDN_DV), const2),
            pl.BlockSpec((4, GDN_HEADS, b, LANES), lambda n: (0, 0, 0, 0)),
            pl.BlockSpec((b, LANES), const2),
            state_spec,
        ],
        out_specs=[
            pl.BlockSpec((nc * c, D_GDN_V), lambda n: (n, 0)),
            pl.BlockSpec((GDN_HEADS, GDN_DK, GDN_DV), lambda n: (0, 0, 0)),
            pl.BlockSpec((b, D_GDN_V), const2),
            state_spec,
        ],
        out_shape=[
            jax.ShapeDtypeStruct((t, D_GDN_V), BF16),
            jax.ShapeDtypeStruct((GDN_HEADS, GDN_DK, GDN_DV), F32),
            jax.ShapeDtypeStruct((b, D_GDN_V), BF16),
            jax.ShapeDtypeStruct(state.shape, F32),
        ],
        scratch_shapes=[pltpu.VMEM((GDN_HEADS, GDN_DK, GDN_DV), F32), pltpu.VMEM((GDN_HEADS, b, GDN_DV), F32)],
        compiler_params=_cparams("arbitrary"),
        name="gdn",
    )(qkvz, gate, norm_w.reshape(1, GDN_DV), qkvz_s, gate_s, state)


def _gate_param_row(p):
    return jnp.zeros((1, LANES), F32).at[0, LANE_G:LANE_G + GDN_HEADS].set(p.astype(F32))


def _ffn(h, hs, x, xs, state_t, w_up, w_conv, w_down, next_norm, *, final=False, name):
    (hid, keep_a, keep_b), (hid_s, up_a, up_b) = _ffn_up(h, hs, w_up, w_conv, state_t)
    out, out_s = _matmul([hid], [hid_s], w_down, tm=FFN_DOWN_ROW_TILE, residual=x, residual2=xs,
                         norm_w=next_norm, final=final, chunk=FFN_DOWN_STAGE_ROWS, name=name)
    return out, out_s, jnp.concatenate([keep_a, keep_b], axis=1), jnp.concatenate([up_a, up_b], axis=1)


def kernel(x_prompt, x_sample, state_l0_pool, cache_l0_k, cache_l0_v, state_l0_ffn_conv, state_l1_gdn_conv, state_l1_gdn_S, state_l1_sconv, state_l1_ffn_conv, l0_norm_mix, l0_w_in, l0_pool_w, l0_pool_scale, l0_sinks, l0_w_out, l0_norm_ffn, l0_ffn_w_up, l0_ffn_conv, l0_ffn_w_down, l1_norm_mix, l1_w_in, l1_gdn_conv, l1_gdn_A_log, l1_gdn_dt_bias, l1_gdn_norm, l1_sconv_w, l1_w_out, l1_norm_ffn, l1_ffn_w_up, l1_ffn_conv, l1_ffn_w_down, final_norm):
    bp, t, d = x_prompt.shape
    nb, ts = x_sample.shape[:2]
    wb = cache_l0_k.shape[1]
    assert bp == 1 and ts == 1 and d == D_MODEL and wb == WINDOW and t >= WINDOW
    w_in1_t = l1_w_in.T
    a_log_row = _gate_param_row(l1_gdn_A_log)
    dt_bias_row = _gate_param_row(l1_gdn_dt_bias)
    pool_t = state_l0_pool.transpose(1, 0, 2)
    kt, vt = cache_l0_k.transpose(0, 2, 3, 1), cache_l0_v.transpose(0, 2, 3, 1)
    ffn0_t = state_l0_ffn_conv.transpose(1, 0, 2)
    gconv_t = state_l1_gdn_conv.transpose(1, 0, 2)
    sconv_t = state_l1_sconv.transpose(1, 0, 2)
    ffn1_t = state_l1_ffn_conv.transpose(1, 0, 2)
    last = SUBLANES

    xp, xs = x_prompt[0], x_sample[:, 0]
    z0, z0s = _matmul([_rmsnorm(xp, l0_norm_mix, BF16)], [_rmsnorm(xs, l0_norm_mix, BF16)], l0_w_in,
                      name="in_proj0")
    mix = _mixer0_seq(z0, l0_pool_w, l0_pool_scale, l0_sinks)
    y_pool = _pool_step(z0s, pool_t, l0_pool_w, l0_pool_scale)
    att, kt_new, vt_new = _attn_step(z0s, kt, vt, l0_sinks)
    mix_s = jnp.concatenate([y_pool, att.astype(BF16)], axis=1)
    (x1, h1), (x1s, h1s) = _matmul([mix], [mix_s], l0_w_out, residual=xp, residual2=xs,
                                   norm_w=l0_norm_ffn, name="out_proj0")
    (x2, h2), (x2s, h2s), ffn0_rows, ffn0_new = _ffn(
        h1, h1s, x1, x1s, ffn0_t, l0_ffn_w_up, l0_ffn_conv, l0_ffn_w_down, l1_norm_mix, name="ffn_down0")
    (qkvz, raw), (qkvz_s, raw_s) = _qkvz(h2, h2s, w_in1_t, l1_gdn_conv, gconv_t)
    (ysc, gate, mrows), (ysc_s, gate_s, m_s) = _tail(
        h2, h2s, w_in1_t, l1_sconv_w, a_log_row, dt_bias_row, sconv_t)
    ygdn, s_fin, ygdn_s, s_new = _gdn(qkvz, gate, qkvz_s, gate_s, l1_gdn_norm, state_l1_gdn_S)
    (x3, h3), (x3s, h3s) = _matmul([ygdn, ysc], [ygdn_s, ysc_s], l1_w_out, residual=x2,
                                   residual2=x2s, norm_w=l1_norm_ffn, name="out_proj1")
    y_prompt, y_sample, ffn1_rows, ffn1_new = _ffn(
        h3, h3s, x3, x3s, ffn1_t, l1_ffn_w_up, l1_ffn_conv, l1_ffn_w_down, final_norm, final=True,
        name="ffn_down1")

    p_pool = z0[t - POOL_BUF:, :D_POOL][None]
    p_k = z0[t - wb:, O_K:O_V].reshape(1, wb, N_KV_HEADS, HEAD_DIM)
    p_v = z0[t - wb:, O_V:].reshape(1, wb, N_KV_HEADS, HEAD_DIM)
    p_ffn0 = ffn0_rows[last - (FFN_CONV - 1):][None]
    p_gconv = raw[last - (GDN_CONV - 1):, :D_GDN_CONV][None]
    p_sconv = mrows[last - (SCONV_W - 1):][None]
    p_ffn1 = ffn1_rows[last - (FFN_CONV - 1):][None]

    def push(state_t, new_row):
        return jnp.concatenate([state_t[1:], new_row[None]], axis=0).transpose(1, 0, 2)

    s_pool = push(pool_t, z0s[:, :D_POOL])
    s_k, s_v = kt_new.transpose(0, 3, 1, 2), vt_new.transpose(0, 3, 1, 2)
    s_ffn0 = push(ffn0_t, ffn0_new)
    s_gconv = push(gconv_t, raw_s[:, :D_GDN_CONV])
    s_sconv = push(sconv_t, m_s)
    s_ffn1 = push(ffn1_t, ffn1_new)
    return (y_prompt[None], y_sample[:, None], p_pool, s_pool, p_k, s_k, p_v, s_v, p_ffn0, s_ffn0,
            p_gconv, s_gconv, s_fin[None], s_new, p_sconv, s_sconv, p_ffn1, s_ffn1)
```

```python
import functools

import jax
import jax.numpy as jnp
from jax import lax
from jax.experimental import pallas as pl
from jax.experimental.pallas import tpu as pltpu

F32 = jnp.float32
BF16 = jnp.bfloat16
EPS = 1e-6
NEG = -1e30

LANES = 128
SUBLANES = 8
MXU_COLS = 256
VMEM_LIMIT_BYTES = 60 * 1024 * 1024

ROW_TILE = 512
FFN_UP_ROW_TILE = 1024
FFN_UP_COL_TILE = 512
FFN_DOWN_ROW_TILE = 256
STAGE_ROWS = 256
FFN_DOWN_STAGE_ROWS = 512
GDN_CHUNKS_PER_STEP = 4
QKVZ_GROUP_HEADS = 4
ATTN_STEP_SEQS = 8

D_MODEL = 2048
D_POOL = 512
POOL_WINDOWS = (2, 4, 8, 16)
POOL_BUF = max(POOL_WINDOWS) - 1
POOL_HIST = 16
HEAD_DIM = 64
N_Q_HEADS = 24
N_KV_HEADS = 4
Q_PER_KV = N_Q_HEADS // N_KV_HEADS
WINDOW = 128
D_ATTN = N_Q_HEADS * HEAD_DIM
D_KV = N_KV_HEADS * HEAD_DIM
D_IN0 = D_POOL + D_ATTN + 2 * D_KV
O_K = D_POOL + D_ATTN
O_V = O_K + D_KV
GDN_HEADS = 12
GDN_DK = 128
GDN_DV = 128
D_GDN_K = GDN_HEADS * GDN_DK
D_GDN_V = GDN_HEADS * GDN_DV
D_GDN_CONV = 2 * D_GDN_K + D_GDN_V
D_GDN_MAIN = D_GDN_CONV + D_GDN_V
GDN_CONV = 4
GDN_CHUNK = 64
D_SCONV = 512
SCONV_W = 3
D_TAIL = 3 * D_SCONV + LANES
D_FF = 5632
FFN_CONV = 3
LANE_BETA = 0
LANE_G = GDN_HEADS
LANE_EG = 2 * GDN_HEADS


def _cparams(*sem):
    return pltpu.CompilerParams(dimension_semantics=sem, vmem_limit_bytes=VMEM_LIMIT_BYTES)


def _dot(a, b):
    return jnp.dot(a, b, preferred_element_type=F32)


def _dot_nt(a, b):
    return lax.dot_general(a, b, (((1,), (1,)), ((), ())), preferred_element_type=F32)


def _silu(x):
    return x * jax.nn.sigmoid(x)


def _cast_rows(w_ref, wbf_ref, chunk=STAGE_ROWS):
    k = w_ref.shape[0]
    chunk = min(chunk, k)
    assert k % chunk == 0

    def body(c, carry):
        r = pl.multiple_of(c * chunk, chunk)
        wbf_ref[pl.ds(r, chunk), :] = w_ref[pl.ds(r, chunk), :].astype(BF16)
        return carry

    lax.fori_loop(0, k // chunk, body, 0)


def _cast_transposed(wt_ref, wbf_ref):
    n = wt_ref.shape[0]
    step = MXU_COLS if n % MXU_COLS == 0 else LANES
    for c in range(0, n, step):
        wbf_ref[:, c:c + step] = wt_ref[c:c + step, :].T.astype(BF16)


def _store_slabs(slab_ref, u):
    m = u.shape[0]
    for c in range(slab_ref.shape[0]):
        slab_ref[c, SUBLANES:SUBLANES + m, :] = u[:, c * LANES:(c + 1) * LANES]


def _tap(slab_ref, c, m, delay):
    return slab_ref[c, pl.ds(SUBLANES - delay, m), :]


def _roll_history(slab_ref, m):
    last = [slab_ref[c, m:m + SUBLANES, :] for c in range(slab_ref.shape[0])]
    for c, rows in enumerate(last):
        slab_ref[c, 0:SUBLANES, :] = rows
    return jnp.concatenate(last, axis=1)


def _rmsnorm_body(x_ref, w_ref, o_ref):
    x = x_ref[...]
    ms = jnp.mean(x * x, axis=-1, keepdims=True)
    o_ref[...] = (x * lax.rsqrt(ms + EPS) * w_ref[...]).astype(o_ref.dtype)


def _rmsnorm(x, w, out_dtype):
    m, d = x.shape
    tm = min(m, ROW_TILE)
    assert m % tm == 0
    return pl.pallas_call(
        _rmsnorm_body,
        grid=(m // tm,),
        in_specs=[pl.BlockSpec((tm, d), lambda i: (i, 0)), pl.BlockSpec((1, d), lambda i: (0, 0))],
        out_specs=pl.BlockSpec((tm, d), lambda i: (i, 0)),
        out_shape=jax.ShapeDtypeStruct((m, d), out_dtype),
        compiler_params=_cparams("parallel"),
        name="rmsnorm",
    )(x, w.reshape(1, d))


def _mm_body(n_pieces, n_stage, mode, *refs):
    n_out = {"plain": 1, "residual": 2, "final": 1}[mode]
    n_in = n_pieces + (0 if mode == "plain" else 1)
    set_a, set_b = refs[:n_in], refs[n_in:2 * n_in]
    w_ref = refs[2 * n_in]
    pos = 2 * n_in + 1
    nw_ref = None
    if mode != "plain":
        nw_ref = refs[pos]
        pos += 1
    out_a, out_b = refs[pos:pos + n_out], refs[pos + n_out:pos + 2 * n_out]
    wbf_ref = refs[pos + 2 * n_out]
    s = pl.program_id(0)
    last = pl.num_programs(0) - 1
    ck = w_ref.shape[0]

    @pl.when(s < n_stage)
    def _():
        _cast_rows(w_ref, wbf_ref.at[pl.ds(pl.multiple_of(s * ck, ck), ck), :])

    def rows(ins, outs):
        acc, row = None, 0
        for x_ref in ins[:n_pieces]:
            kp = x_ref.shape[1]
            d = _dot(x_ref[...], wbf_ref[row:row + kp, :])
            acc = d if acc is None else acc + d
            row += kp
        if mode == "plain":
            outs[0][...] = acc
            return
        xn = acc + ins[n_pieces][...]
        hn = xn * lax.rsqrt(jnp.mean(xn * xn, axis=-1, keepdims=True) + EPS) * nw_ref[...]
        if mode == "residual":
            outs[0][...] = xn
        outs[-1][...] = hn.astype(outs[-1].dtype)

    @pl.when((s >= n_stage) & (s < last))
    def _():
        rows(set_a, out_a)

    @pl.when(s == last)
    def _():
        rows(set_b, out_b)


def _matmul(xs, xs2, w, *, tm=ROW_TILE, residual=None, residual2=None, norm_w=None, final=False, chunk=STAGE_ROWS,
            name="matmul"):
    m, m2 = xs[0].shape[0], xs2[0].shape[0]
    k, n = w.shape
    tm = min(tm, m)
    assert m % tm == 0 and k % chunk == 0 and sum(x.shape[1] for x in xs) == k
    n_stage, nm = k // chunk, m // tm
    mode = "plain" if residual is None else ("final" if final else "residual")

    def row_tile(s):
        return jnp.clip(s - n_stage, 0, nm - 1)

    tile_a = pl.BlockSpec((tm, n), lambda s: (row_tile(s), 0))
    tile_b = pl.BlockSpec((m2, n), lambda s: (0, 0))
    in_a = [pl.BlockSpec((tm, x.shape[1]), lambda s: (row_tile(s), 0)) for x in xs]
    in_b = [pl.BlockSpec((m2, x.shape[1]), lambda s: (0, 0)) for x in xs2]
    args_a, args_b = list(xs), list(xs2)
    if mode != "plain":
        in_a.append(tile_a)
        in_b.append(tile_b)
        args_a.append(residual)
        args_b.append(residual2)
    in_specs = in_a + in_b + [pl.BlockSpec((chunk, n), lambda s: (jnp.minimum(s, n_stage - 1), 0))]
    args = args_a + args_b + [w]
    if mode != "plain":
        in_specs.append(pl.BlockSpec((1, n), lambda s: (0, 0)))
        args.append(norm_w.reshape(1, n))
    dtypes = [F32, BF16] if mode == "residual" else [F32]
    out_specs = [tile_a] * len(dtypes) + [tile_b] * len(dtypes)
    out_shape = ([jax.ShapeDtypeStruct((m, n), dt) for dt in dtypes]
                 + [jax.ShapeDtypeStruct((m2, n), dt) for dt in dtypes])
    outs = pl.pallas_call(
        functools.partial(_mm_body, len(xs), n_stage, mode),
        grid=(n_stage + nm + 1,),
        in_specs=in_specs,
        out_specs=out_specs,
        out_shape=out_shape,
        scratch_shapes=[pltpu.VMEM((k, n), BF16)],
        compiler_params=_cparams("arbitrary"),
        name=name,
    )(*args)
    half = len(dtypes)
    first, second = outs[:half], outs[half:]
    return (first[0], second[0]) if half == 1 else (tuple(first), tuple(second))


def _ffn_up_body(x_ref, xs_ref, wa_ref, wb_ref, cwa_ref, cwb_ref, a2_ref, a1_ref, b2_ref, b1_ref,
                 hid_ref, upa_ref, upb_ref, hids_ref, upsa_ref, upsb_ref, wa_bf, wb_bf, ca_ref, cb_ref):
    tm = x_ref.shape[0]
    i = pl.program_id(1)

    @pl.when(i == 0)
    def _():
        _cast_rows(wa_ref, wa_bf)
        _cast_rows(wb_ref, wb_bf)
        ca_ref[:, 0:SUBLANES, :] = jnp.zeros((ca_ref.shape[0], SUBLANES, LANES), F32)
        cb_ref[:, 0:SUBLANES, :] = jnp.zeros((cb_ref.shape[0], SUBLANES, LANES), F32)

    def gate(ua, ub, taps_a, taps_b):
        c_a = cwa_ref[0:1, :] * taps_a[0] + cwa_ref[1:2, :] * taps_a[1] + cwa_ref[2:3, :] * ua
        c_b = cwb_ref[0:1, :] * taps_b[0] + cwb_ref[1:2, :] * taps_b[1] + cwb_ref[2:3, :] * ub
        return (_silu(c_a) * c_b).astype(BF16)

    @pl.when(i > 0)
    def _():
        x = x_ref[...]
        _store_slabs(ca_ref, _dot(x, wa_bf[...]))
        _store_slabs(cb_ref, _dot(x, wb_bf[...]))
        for c in range(ca_ref.shape[0]):
            cols = slice(c * LANES, (c + 1) * LANES)
            c_a = (cwa_ref[0:1, cols] * _tap(ca_ref, c, tm, 2) + cwa_ref[1:2, cols] * _tap(ca_ref, c, tm, 1)
                   + cwa_ref[2:3, cols] * _tap(ca_ref, c, tm, 0))
            c_b = (cwb_ref[0:1, cols] * _tap(cb_ref, c, tm, 2) + cwb_ref[1:2, cols] * _tap(cb_ref, c, tm, 1)
                   + cwb_ref[2:3, cols] * _tap(cb_ref, c, tm, 0))
            hid_ref[:, cols] = (_silu(c_a) * c_b).astype(BF16)
        upa_ref[...] = _roll_history(ca_ref, tm)
        upb_ref[...] = _roll_history(cb_ref, tm)

    @pl.when(i == 0)
    def _():
        x = xs_ref[...]
        ua = _dot(x, wa_bf[...])
        ub = _dot(x, wb_bf[...])
        hids_ref[...] = gate(ua, ub, (a2_ref[...], a1_ref[...]), (b2_ref[...], b1_ref[...]))
        upsa_ref[...] = ua
        upsb_ref[...] = ub


def _ffn_up(h, hs, w_up, conv_w, state_t, *, tn=FFN_UP_COL_TILE, tm=FFN_UP_ROW_TILE):
    m, k = h.shape
    ms = hs.shape[0]
    tm = min(tm, m)
    nb, nm = D_FF // tn, m // tm
    assert m % tm == 0 and D_FF % tn == 0
    tile = lambda j, i: (jnp.maximum(i - 1, 0), j)
    in_specs = [
        pl.BlockSpec((tm, k), lambda j, i: (jnp.maximum(i - 1, 0), 0)),
        pl.BlockSpec((ms, k), lambda j, i: (0, 0)),
        pl.BlockSpec((k, tn), lambda j, i: (0, j)),
        pl.BlockSpec((k, tn), lambda j, i: (0, j + nb)),
        pl.BlockSpec((FFN_CONV, tn), lambda j, i: (0, j)),
        pl.BlockSpec((FFN_CONV, tn), lambda j, i: (0, j + nb)),
        pl.BlockSpec((None, ms, tn), lambda j, i: (0, 0, j)),
        pl.BlockSpec((None, ms, tn), lambda j, i: (1, 0, j)),
        pl.BlockSpec((None, ms, tn), lambda j, i: (0, 0, j + nb)),
        pl.BlockSpec((None, ms, tn), lambda j, i: (1, 0, j + nb)),
    ]
    keep_spec = pl.BlockSpec((SUBLANES, tn), lambda j, i: (0, j))
    step_spec = pl.BlockSpec((ms, tn), lambda j, i: (0, j))
    outs = pl.pallas_call(
        _ffn_up_body,
        grid=(nb, nm + 1),
        in_specs=in_specs,
        out_specs=[pl.BlockSpec((tm, tn), tile), keep_spec, keep_spec, step_spec, step_spec, step_spec],
        out_shape=[
            jax.ShapeDtypeStruct((m, D_FF), BF16),
            jax.ShapeDtypeStruct((SUBLANES, D_FF), F32),
            jax.ShapeDtypeStruct((SUBLANES, D_FF), F32),
            jax.ShapeDtypeStruct((ms, D_FF), BF16),
            jax.ShapeDtypeStruct((ms, D_FF), F32),
            jax.ShapeDtypeStruct((ms, D_FF), F32),
        ],
        scratch_shapes=[pltpu.VMEM((k, tn), BF16), pltpu.VMEM((k, tn), BF16),
                        pltpu.VMEM((tn // LANES, SUBLANES + tm, LANES), F32),
                        pltpu.VMEM((tn // LANES, SUBLANES + tm, LANES), F32)],
        compiler_params=_cparams("arbitrary", "arbitrary"),
        name="ffn_up",
    )(h, hs, w_up, w_up, conv_w, conv_w, state_t, state_t, state_t, state_t)
    return outs[:3], outs[3:]


def _half_lane_pair(x, head_in_high, lo_mask):
    if head_in_high:
        hi = jnp.where(lo_mask, 0.0, x)
        return pltpu.roll(hi, HEAD_DIM, 1), hi
    lo = jnp.where(lo_mask, x, 0.0)
    return lo, pltpu.roll(lo, HEAD_DIM, 1)


def _mixer0_seq_body(sink_ref, z_ref, kvp_ref, up_ref, pw_ref, ps_ref, o_ref):
    n = pl.program_id(0)
    first = n == 0
    w = WINDOW
    lo_mask = lax.broadcasted_iota(jnp.int32, (1, LANES), 1) < HEAD_DIM

    hist = jnp.where(first, 0.0, up_ref[...])
    pos = n * w + lax.broadcasted_iota(jnp.int32, (w, 1), 0)
    for g, win in enumerate(POOL_WINDOWS):
        sl = slice(g * LANES, (g + 1) * LANES)
        e = jnp.concatenate([hist[:, sl], z_ref[:, sl]], axis=0)
        s, sh = e, 1
        while sh < win:
            s = s + pltpu.roll(s, sh, 0)
            sh *= 2
        cnt = jnp.minimum(pos + 1, win).astype(F32)
        diff = s[POOL_HIST:] / cnt - e[POOL_HIST:]
        y = _dot(diff.astype(BF16), pw_ref[g].astype(BF16)) * ps_ref[:, sl]
        o_ref[:, sl] = y.astype(o_ref.dtype)

    row = lax.broadcasted_iota(jnp.int32, (w, 2 * w), 0)
    col = lax.broadcasted_iota(jnp.int32, (w, 2 * w), 1)
    valid = (col >= row) & (col <= row + w) & (jnp.logical_not(first) | (col >= w))
    tiles_per_kv = Q_PER_KV * HEAD_DIM // LANES
    for c in range(D_KV // LANES):
        k2 = jnp.concatenate([kvp_ref[:, c * LANES:(c + 1) * LANES],
                              z_ref[:, O_K + c * LANES:O_K + (c + 1) * LANES]], axis=0)
        v2 = jnp.concatenate([kvp_ref[:, D_KV + c * LANES:D_KV + (c + 1) * LANES],
                              z_ref[:, O_V + c * LANES:O_V + (c + 1) * LANES]], axis=0)
        for p in range(2):
            hk = 2 * c + p
            k_lo, k_hi = (t.astype(BF16) for t in _half_lane_pair(k2, p == 1, lo_mask))
            v_lo, v_hi = (t.astype(BF16) for t in _half_lane_pair(v2, p == 1, lo_mask))
            q0 = D_POOL + hk * Q_PER_KV * HEAD_DIM
            qst = jnp.concatenate(
                [z_ref[:, q0 + a * LANES:q0 + (a + 1) * LANES] for a in range(tiles_per_kv)], axis=0)
            qst = (qst * HEAD_DIM ** -0.5).astype(BF16)
            s_lo = _dot_nt(qst, k_lo)
            s_hi = _dot_nt(qst, k_hi)
            for a in range(tiles_per_kv):
                probs, inv = [], []
                for par, s_all in ((0, s_lo), (1, s_hi)):
                    sk = sink_ref[hk * Q_PER_KV + 2 * a + par]
                    s = jnp.where(valid, s_all[a * w:(a + 1) * w], NEG)
                    mx = jnp.maximum(jnp.max(s, axis=-1, keepdims=True), sk)
                    pr = jnp.exp(s - mx)
                    den = jnp.sum(pr, axis=-1, keepdims=True) + jnp.exp(sk - mx)
                    probs.append(pr.astype(BF16))
                    inv.append(1.0 / den)
                o = _dot(probs[0], v_lo) + _dot(probs[1], v_hi)
                o = o * jnp.where(lo_mask, inv[0], inv[1])
                o_ref[:, q0 + a * LANES:q0 + (a + 1) * LANES] = o.astype(o_ref.dtype)


def _mixer0_seq(z0, pool_w, pool_scale, sinks):
    t = z0.shape[0]
    w = WINDOW
    assert t % w == 0
    hist_blocks = w // POOL_HIST
    return pl.pallas_call(
        _mixer0_seq_body,
        grid=(t // w,),
        in_specs=[
            pl.BlockSpec(memory_space=pltpu.SMEM),
            pl.BlockSpec((w, D_IN0), lambda n: (n, 0)),
            pl.BlockSpec((w, 2 * D_KV), lambda n: (jnp.maximum(n - 1, 0), O_K // (2 * D_KV))),
            pl.BlockSpec((POOL_HIST, D_POOL), lambda n: (jnp.maximum(n * hist_blocks - 1, 0), 0)),
            pl.BlockSpec((len(POOL_WINDOWS), LANES, LANES), lambda n: (0, 0, 0)),
            pl.BlockSpec((1, D_POOL), lambda n: (0, 0)),
        ],
        out_specs=pl.BlockSpec((w, D_POOL + D_ATTN), lambda n: (n, 0)),
        out_shape=jax.ShapeDtypeStruct((t, D_POOL + D_ATTN), BF16),
        compiler_params=_cparams("parallel"),
        name="mixer0_seq",
    )(sinks, z0, z0, z0, pool_w, pool_scale.reshape(1, D_POOL))


def _pool_step_body(*refs):
    hist_refs = refs[:POOL_BUF]
    z_ref, pw_ref, ps_ref, o_ref = refs[POOL_BUF:]
    for g, win in enumerate(POOL_WINDOWS):
        sl = slice(g * LANES, (g + 1) * LANES)
        u = z_ref[:, sl]
        s = u
        for r in range(POOL_BUF - (win - 1), POOL_BUF):
            s = s + hist_refs[r][:, sl]
        diff = s / float(win) - u
        y = _dot(diff.astype(BF16), pw_ref[g].astype(BF16)) * ps_ref[:, sl]
        o_ref[:, sl] = y.astype(o_ref.dtype)


def _pool_step(z0, hist_t, pool_w, pool_scale):
    b = z0.shape[0]
    in_specs = [pl.BlockSpec((None, b, D_POOL), functools.partial(lambda i, r: (r, 0, 0), r=r))
                for r in range(POOL_BUF)]
    in_specs += [
        pl.BlockSpec((b, D_POOL), lambda i: (0, 0)),
        pl.BlockSpec((len(POOL_WINDOWS), LANES, LANES), lambda i: (0, 0, 0)),
        pl.BlockSpec((1, D_POOL), lambda i: (0, 0)),
    ]
    return pl.pallas_call(
        _pool_step_body,
        grid=(1,),
        in_specs=in_specs,
        out_specs=pl.BlockSpec((b, D_POOL), lambda i: (0, 0)),
        out_shape=jax.ShapeDtypeStruct((b, D_POOL), BF16),
        compiler_params=_cparams("arbitrary"),
        name="pool_step",
    )(*([hist_t] * POOL_BUF), z0, pool_w, pool_scale.reshape(1, D_POOL))


def _attn_step_body(q_ref, kn_ref, vn_ref, knt_ref, vnt_ref, kt_ref, vt_ref, sink_ref, o_ref, kto_ref, vto_ref):
    bs = q_ref.shape[0]
    newest = lax.broadcasted_iota(jnp.int32, (HEAD_DIM, WINDOW), 1) == WINDOW - 1
    pairs = [(b, hk) for b in range(bs) for hk in range(N_KV_HEADS)]
    n = range(len(pairs))
    kt = [kt_ref[b, hk] for b, hk in pairs]
    vt = [vt_ref[b, hk] for b, hk in pairs]
    q = [q_ref[b, hk] * HEAD_DIM ** -0.5 for b, hk in pairs]
    sk = [sink_ref[hk * Q_PER_KV:(hk + 1) * Q_PER_KV, :] for b, hk in pairs]
    s = [_dot(q[i].astype(BF16), kt[i].astype(BF16)) for i in n]
    s_new = [jnp.sum(q[i] * kn_ref[b, hk:hk + 1, :], axis=-1, keepdims=True) for i, (b, hk) in enumerate(pairs)]
    mx = [jnp.maximum(jnp.maximum(jnp.max(s[i], axis=-1, keepdims=True), s_new[i]), sk[i]) for i in n]
    pr = [jnp.exp(s[i] - mx[i]) for i in n]
    pn = [jnp.exp(s_new[i] - mx[i]) for i in n]
    den = [jnp.sum(pr[i], axis=-1, keepdims=True) + pn[i] + jnp.exp(sk[i] - mx[i]) for i in n]
    o = [_dot_nt(pr[i].astype(BF16), vt[i].astype(BF16)) for i in n]
    for i, (b, hk) in enumerate(pairs):
        o_ref[b, hk] = (o[i] + pn[i] * vn_ref[b, hk:hk + 1, :]) / den[i]
        kto_ref[b, hk] = jnp.where(newest, pltpu.roll(knt_ref[hk], WINDOW - 1 - b, 1),
                                   pltpu.roll(kt[i], WINDOW - 1, 1))
        vto_ref[b, hk] = jnp.where(newest, pltpu.roll(vnt_ref[hk], WINDOW - 1 - b, 1),
                                   pltpu.roll(vt[i], WINDOW - 1, 1))


def _attn_step(z0, kt, vt, sinks, *, bs=ATTN_STEP_SEQS):
    b = z0.shape[0]
    assert b % bs == 0 and kt.shape[3] == WINDOW == LANES and bs <= WINDOW
    nblk = b // bs
    q4 = z0[:, D_POOL:O_K].reshape(b, N_KV_HEADS, Q_PER_KV, HEAD_DIM)
    kn = z0[:, O_K:O_V].reshape(b, N_KV_HEADS, HEAD_DIM)
    vn = z0[:, O_V:].reshape(b, N_KV_HEADS, HEAD_DIM)

    def columns(x):
        xt = x.reshape(nblk, bs, N_KV_HEADS, HEAD_DIM).transpose(0, 2, 3, 1)
        return jnp.pad(xt, ((0, 0), (0, 0), (0, 0), (0, WINDOW - bs)))

    cache_spec = pl.BlockSpec((bs, N_KV_HEADS, HEAD_DIM, WINDOW), lambda i: (i, 0, 0, 0))
    col_spec = pl.BlockSpec((None, N_KV_HEADS, HEAD_DIM, WINDOW), lambda i: (i, 0, 0, 0))
    new_spec = pl.BlockSpec((bs, N_KV_HEADS, HEAD_DIM), lambda i: (i, 0, 0))
    q_spec = pl.BlockSpec((bs, N_KV_HEADS, Q_PER_KV, HEAD_DIM), lambda i: (i, 0, 0, 0))
    att, kto, vto = pl.pallas_call(
        _attn_step_body,
        grid=(nblk,),
        in_specs=[q_spec, new_spec, new_spec, col_spec, col_spec, cache_spec, cache_spec,
                  pl.BlockSpec((N_Q_HEADS, 1), lambda i: (0, 0))],
        out_specs=[q_spec, cache_spec, cache_spec],
        out_shape=[jax.ShapeDtypeStruct(q4.shape, F32), jax.ShapeDtypeStruct(kt.shape, F32),
                   jax.ShapeDtypeStruct(vt.shape, F32)],
        compiler_params=_cparams("parallel"),
        name="attn_step",
    )(q4, kn, vn, columns(kn), columns(vn), kt, vt, sinks.reshape(N_Q_HEADS, 1))
    return att.reshape(b, D_ATTN), kto, vto


def _qkvz_body(x_ref, xs_ref, w_ref, cw_ref, p3_ref, p2_ref, p1_ref,
               o_ref, raw_ref, os_ref, raws_ref, w_bf, carry_ref):
    tm = x_ref.shape[0]
    j = pl.program_id(0)
    i = pl.program_id(1)

    @pl.when(i == 0)
    def _():
        _cast_transposed(w_ref, w_bf)
        carry_ref[:, 0:SUBLANES, :] = jnp.zeros((carry_ref.shape[0], SUBLANES, LANES), F32)

    def heads(a):
        return [a[:, h * LANES:(h + 1) * LANES] for h in range(GDN_HEADS)]

    def l2norm(a):
        return a * lax.rsqrt(jnp.sum(a * a, axis=-1, keepdims=True) + EPS)

    def finish(u, taps, out_ref):
        def conv_act():
            t3, t2, t1 = taps()
            c = cw_ref[3:4, :] * u
            c = cw_ref[0:1, :] * t3 + cw_ref[1:2, :] * t2 + cw_ref[2:3, :] * t1 + c
            return _silu(c)

        @pl.when(j == 0)
        def _():
            for h, a in enumerate(heads(conv_act())):
                out_ref[h] = l2norm(a) * GDN_DK ** -0.5

        @pl.when(j == 1)
        def _():
            for h, a in enumerate(heads(conv_act())):
                out_ref[h] = l2norm(a)

        @pl.when(j == 2)
        def _():
            for h, a in enumerate(heads(conv_act())):
                out_ref[h] = a

        @pl.when(j == 3)
        def _():
            for h, a in enumerate(heads(u)):
                out_ref[h] = a

    @pl.when((i > 0) & (j < 3))
    def _():
        x = x_ref[...]
        norm_scale = jnp.where(j == 0, GDN_DK ** -0.5, 1.0)
        gw = QKVZ_GROUP_HEADS * LANES
        for g0 in range(0, GDN_HEADS, QKVZ_GROUP_HEADS):
            u = _dot(x, w_bf[:, g0 * LANES:g0 * LANES + gw])
            for hh in range(QKVZ_GROUP_HEADS):
                h = g0 + hh
                sl = slice(h * LANES, (h + 1) * LANES)
                carry_ref[h, SUBLANES:SUBLANES + tm, :] = u[:, hh * LANES:(hh + 1) * LANES]
                c = cw_ref[3:4, sl] * _tap(carry_ref, h, tm, 0)
                c = (cw_ref[0:1, sl] * _tap(carry_ref, h, tm, 3) + cw_ref[1:2, sl] * _tap(carry_ref, h, tm, 2)
                     + cw_ref[2:3, sl] * _tap(carry_ref, h, tm, 1) + c)
                a = _silu(c)
                inv = jnp.where(j < 2, lax.rsqrt(jnp.sum(a * a, axis=-1, keepdims=True) + EPS) * norm_scale, 1.0)
                o_ref[h] = a * inv
        raw_ref[...] = _roll_history(carry_ref, tm)

    @pl.when((i > 0) & (j == 3))
    def _():
        u = _dot(x_ref[...], w_bf[...])
        for h, a in enumerate(heads(u)):
            o_ref[h] = a
        raw_ref[...] = u[tm - SUBLANES:]

    @pl.when(i == 0)
    def _():
        u = _dot(xs_ref[...], w_bf[...])
        raws_ref[...] = u
        finish(u, lambda: (p3_ref[...], p2_ref[...], p1_ref[...]), os_ref)


def _qkvz(h, hs, w_in_t, conv_w, state_t, *, tm=ROW_TILE):
    m, k = h.shape
    ms = hs.shape[0]
    tm = min(tm, m)
    tn = D_GDN_K
    assert D_GDN_K == D_GDN_V and GDN_DK == LANES and m % tm == 0
    nj, nm = D_GDN_MAIN // tn, m // tm
    in_specs = [
        pl.BlockSpec((tm, k), lambda j, i: (jnp.maximum(i - 1, 0), 0)),
        pl.BlockSpec((ms, k), lambda j, i: (0, 0)),
        pl.BlockSpec((tn, k), lambda j, i: (j, 0)),
        pl.BlockSpec((GDN_CONV, tn), lambda j, i: (0, jnp.minimum(j, 2))),
    ]
    for r in range(GDN_CONV - 1):
        in_specs.append(pl.BlockSpec((None, ms, tn), functools.partial(lambda j, i, r: (r, 0, jnp.minimum(j, 2)), r=r)))
    outs = pl.pallas_call(
        _qkvz_body,
        grid=(nj, nm + 1),
        in_specs=in_specs,
        out_specs=[
            pl.BlockSpec((None, GDN_HEADS, tm, LANES), lambda j, i: (j, 0, jnp.maximum(i - 1, 0), 0)),
            pl.BlockSpec((SUBLANES, tn), lambda j, i: (0, j)),
            pl.BlockSpec((None, GDN_HEADS, ms, LANES), lambda j, i: (j, 0, 0, 0)),
            pl.BlockSpec((ms, tn), lambda j, i: (0, j)),
        ],
        out_shape=[
            jax.ShapeDtypeStruct((nj, GDN_HEADS, m, LANES), F32),
            jax.ShapeDtypeStruct((SUBLANES, D_GDN_MAIN), F32),
            jax.ShapeDtypeStruct((nj, GDN_HEADS, ms, LANES), F32),
            jax.ShapeDtypeStruct((ms, D_GDN_MAIN), F32),
        ],
        scratch_shapes=[pltpu.VMEM((k, tn), BF16), pltpu.VMEM((tn // LANES, SUBLANES + tm, LANES), F32)],
        compiler_params=_cparams("arbitrary", "arbitrary"),
        name="qkvz",
    )(h, hs, w_in_t, conv_w, state_t, state_t, state_t)
    return outs[:2], outs[2:]


def _tail_body(x_ref, xs_ref, wa_ref, wb_ref, cw_ref, alog_ref, dtb_ref, p2_ref, p1_ref,
               ysc_ref, gate_ref, m_ref, yscs_ref, gates_ref, ms_ref, w_bf, carry_ref):
    tm = x_ref.shape[0]
    i = pl.program_id(0)

    @pl.when(i == 0)
    def _():
        n_raw = wb_ref.shape[0]
        step = MXU_COLS
        for c in range(0, 3 * D_SCONV, step):
            r0 = n_raw + c
            if r0 + step <= wa_ref.shape[0]:
                blk = wa_ref[r0:r0 + step, :]
            else:
                blk = jnp.concatenate([wa_ref[r0:, :], wb_ref[...]], axis=0)
            w_bf[:, c:c + step] = blk.T.astype(BF16)
        raw = jnp.concatenate([wa_ref[0:n_raw, :], jnp.zeros((LANES - n_raw, wa_ref.shape[1]), F32)], axis=0)
        w_bf[:, 3 * D_SCONV:] = raw.T.astype(BF16)
        carry_ref[:, 0:SUBLANES, :] = jnp.zeros((carry_ref.shape[0], SUBLANES, LANES), F32)

    def finish(z, mm, taps, y_ref, g_ref):
        conv = cw_ref[0:1, :] * taps[0] + cw_ref[1:2, :] * taps[1] + cw_ref[2:3, :] * mm
        y_ref[...] = (z[:, 0:D_SCONV] * conv).astype(y_ref.dtype)
        raw = z[:, 3 * D_SCONV:]
        lane = lax.broadcasted_iota(jnp.int32, raw.shape, 1)
        beta = jax.nn.sigmoid(raw)
        sp = raw + dtb_ref[...]
        softplus = jnp.maximum(sp, 0.0) + jnp.log1p(jnp.exp(-jnp.abs(sp)))
        g = -jnp.exp(alog_ref[...]) * softplus
        eg = pltpu.roll(jnp.exp(g), LANE_EG - LANE_G, 1)
        g_ref[...] = jnp.where(lane < LANE_G, beta, jnp.where(lane < LANE_EG, g, eg))

    @pl.when(i > 0)
    def _():
        z = _dot(x_ref[...], w_bf[...])
        mm = z[:, D_SCONV:2 * D_SCONV] * z[:, 2 * D_SCONV:3 * D_SCONV]
        _store_slabs(carry_ref, mm)
        taps = tuple(jnp.concatenate([_tap(carry_ref, c, tm, k) for c in range(carry_ref.shape[0])], axis=1)
                     for k in (2, 1))
        finish(z, mm, taps, ysc_ref, gate_ref)
        m_ref[...] = _roll_history(carry_ref, tm)

    @pl.when(i == 0)
    def _():
        z = _dot(xs_ref[...], w_bf[...])
        mm = z[:, D_SCONV:2 * D_SCONV] * z[:, 2 * D_SCONV:3 * D_SCONV]
        ms_ref[...] = mm
        finish(z, mm, (p2_ref[...], p1_ref[...]), yscs_ref, gates_ref)


def _tail(h, hs, w_in_t, conv_w, a_log_row, dt_bias_row, state_t, *, tm=ROW_TILE):
    m, k = h.shape
    ms = hs.shape[0]
    tm = min(tm, m)
    n_raw, n_conv = 2 * GDN_HEADS, 3 * D_SCONV
    assert m % tm == 0 and w_in_t.shape[0] == D_GDN_MAIN + n_raw + n_conv
    assert D_GDN_MAIN % n_conv == 0 and (D_GDN_MAIN + n_conv) % n_raw == 0 and n_raw % SUBLANES == 0
    nm = m // tm
    tile = lambda i: (jnp.maximum(i - 1, 0), 0)
    const = lambda i: (0, 0)
    in_specs = [
        pl.BlockSpec((tm, k), tile),
        pl.BlockSpec((ms, k), const),
        pl.BlockSpec((n_conv, k), lambda i: (D_GDN_MAIN // n_conv, 0)),
        pl.BlockSpec((n_raw, k), lambda i: ((D_GDN_MAIN + n_conv) // n_raw, 0)),
        pl.BlockSpec((SCONV_W, D_SCONV), const),
        pl.BlockSpec((1, LANES), const),
        pl.BlockSpec((1, LANES), const),
        pl.BlockSpec((None, ms, D_SCONV), lambda i: (0, 0, 0)),
        pl.BlockSpec((None, ms, D_SCONV), lambda i: (1, 0, 0)),
    ]
    outs = pl.pallas_call(
        _tail_body,
        grid=(nm + 1,),
        in_specs=in_specs,
        out_specs=[
            pl.BlockSpec((tm, D_SCONV), tile),
            pl.BlockSpec((tm, LANES), tile),
            pl.BlockSpec((SUBLANES, D_SCONV), const),
            pl.BlockSpec((ms, D_SCONV), const),
            pl.BlockSpec((ms, LANES), const),
            pl.BlockSpec((ms, D_SCONV), const),
        ],
        out_shape=[
            jax.ShapeDtypeStruct((m, D_SCONV), BF16),
            jax.ShapeDtypeStruct((m, LANES), F32),
            jax.ShapeDtypeStruct((SUBLANES, D_SCONV), F32),
            jax.ShapeDtypeStruct((ms, D_SCONV), BF16),
            jax.ShapeDtypeStruct((ms, LANES), F32),
            jax.ShapeDtypeStruct((ms, D_SCONV), F32),
        ],
        scratch_shapes=[pltpu.VMEM((k, D_TAIL), BF16),
                        pltpu.VMEM((D_SCONV // LANES, SUBLANES + tm, LANES), F32)],
        compiler_params=_cparams("arbitrary"),
        name="tail",
    )(h, hs, w_in_t, w_in_t, conv_w, a_log_row, dt_bias_row, state_t, state_t)
    return outs[:3], outs[3:]


def _gated_norm(o, zg, nw):
    y = o * lax.rsqrt(jnp.mean(o * o, axis=-1, keepdims=True) + EPS) * nw
    return y * _silu(zg)


def _gdn_body(nc, sb, qkvz_ref, gate_ref, nw_ref, qs_ref, gs_ref, st_ref,
              y_ref, sfin_ref, ys_ref, sto_ref, s_ref, yacc_ref):
    c = GDN_CHUNK
    n = pl.program_id(0)
    step_part = _gdn_step_part(n, sb, qs_ref, gs_ref, nw_ref, st_ref, sto_ref, yacc_ref)

    def tick():
        next(step_part, None)

    @pl.when(n == 0)
    def _():
        s_ref[...] = jnp.zeros_like(s_ref)

    assert 2 * c == LANES and GDN_HEADS % 2 == 0
    r = lax.broadcasted_iota(jnp.int32, (c, LANES), 0)
    lane = lax.broadcasted_iota(jnp.int32, (c, LANES), 1)
    lo = lane < c
    col = jnp.where(lo, lane, lane - c)
    tri = r >= col
    strict = r > col
    rr = lax.broadcasted_iota(jnp.int32, (c, c), 0)
    ones = jnp.where(rr >= lax.broadcasted_iota(jnp.int32, (c, c), 1), 1.0, 0.0).astype(BF16)
    zpad = jnp.zeros((LANES - c, LANES), F32)
    zrows = jnp.zeros((c, LANES), BF16)
    nw = nw_ref[...]
    duos = [(j, m) for j in range(nc) for m in range(GDN_HEADS // 2)]
    nd = range(len(duos))

    def block_diag(a):
        return jnp.concatenate([jnp.where(lo, a, 0.0), jnp.where(lo, 0.0, a)], axis=0).astype(BF16)

    gates, gcs, gcts = [], [], []
    for j in range(nc):
        gate = gate_ref[j * c:(j + 1) * c, :]
        g1 = gate.astype(BF16)
        r1 = gate - g1.astype(F32)
        g2 = r1.astype(BF16)
        g3 = (r1 - g2.astype(F32)).astype(BF16)
        gc = _dot(ones, g1) + _dot(ones, g2) + _dot(ones, g3)
        gates.append(gate)
        gcs.append(gc)
        gcts.append(jnp.concatenate([gc, zpad], axis=0).T)

    def rows(j):
        return slice(j * c, (j + 1) * c)

    def per_head(fn):
        return [[fn(j, 2 * m + e) for e in range(2)] for j, m in duos]

    gcol = per_head(lambda j, h: gcs[j][:, LANE_G + h:LANE_G + h + 1])
    bcol = per_head(lambda j, h: gates[j][:, LANE_BETA + h:LANE_BETA + h + 1])
    grow = [jnp.concatenate([gcts[j][LANE_G + 2 * m:LANE_G + 2 * m + 1, 0:c],
                             gcts[j][LANE_G + 2 * m + 1:LANE_G + 2 * m + 2, 0:c]], axis=1) for j, m in duos]
    decay = [jnp.exp(jnp.where(tri, jnp.where(lo, gcol[i][0], gcol[i][1]) - grow[i], NEG)) for i in nd]
    eg = [[jnp.exp(g) for g in gcol[i]] for i in nd]
    k = per_head(lambda j, h: qkvz_ref[1, h, rows(j), :])
    qd = per_head(lambda j, h: qkvz_ref[0, h, rows(j), :])
    kb = [[k[i][e] * bcol[i][e] for e in range(2)] for i in nd]
    kq = [_dot_nt(jnp.concatenate([kb[i][0], qd[i][0]], axis=0).astype(BF16),
                  jnp.concatenate([k[i][0].astype(BF16), zrows], axis=0))
          + _dot_nt(jnp.concatenate([kb[i][1], qd[i][1]], axis=0).astype(BF16),
                    jnp.concatenate([zrows, k[i][1].astype(BF16)], axis=0)) for i in nd]
    tick()
    x = [jnp.where(strict, -(kq[i][:c] * decay[i]), 0.0) for i in nd]
    intra = [jnp.where(tri, kq[i][c:] * decay[i], 0.0).astype(BF16) for i in nd]
    p = [_dot(x[i].astype(BF16), block_diag(x[i])) for i in nd]
    t_off = x
    n_steps = c.bit_length() - 2
    for step in range(n_steps):
        tick()
        bd = [block_diag(p[i]) for i in nd]
        if step < n_steps - 1:
            pt = [_dot(jnp.concatenate([p[i], t_off[i]], axis=0).astype(BF16), bd[i]) for i in nd]
            t_off = [t_off[i] + p[i] + pt[i][c:] for i in nd]
            p = [pti[:c] for pti in pt]
        else:
            t_off = [t_off[i] + p[i] + _dot(t_off[i].astype(BF16), bd[i]) for i in nd]
    tick()
    rhs = [[jnp.concatenate([qkvz_ref[2, 2 * m + e, rows(j), :] * bcol[i][e], kb[i][e] * eg[i][e]], axis=1)
            for e in range(2)] for i, (j, m) in enumerate(duos)]
    rhs2 = [jnp.concatenate(rhs[i], axis=0).astype(BF16) for i in nd]
    sol = [[rhs[i][0] + _dot(jnp.where(lo, t_off[i], 0.0).astype(BF16), rhs2[i]),
            rhs[i][1] + _dot(jnp.where(lo, 0.0, t_off[i]).astype(BF16), rhs2[i])] for i in nd]
    wq = [[jnp.concatenate([sol[i][e][:, GDN_DV:], qd[i][e] * eg[i][e]], axis=0).astype(BF16) for e in range(2)]
          for i in nd]
    glast = [[g[c - 1:c, :] for g in gcol[i]] for i in nd]
    kdt = [jnp.concatenate([k[i][e] * jnp.exp(glast[i][e] - gcol[i][e]) for e in range(2)], axis=0).T.astype(BF16)
           for i in nd]
    ikd = [jnp.concatenate([intra[i], kdt[i]], axis=0) for i in nd]
    g_tot = [[jnp.exp(g) for g in glast[i]] for i in nd]

    for j in range(nc):
        tick()
        idx = [j * (GDN_HEADS // 2) + m for m in range(GDN_HEADS // 2)]
        s = [s_ref[h] for h in range(GDN_HEADS)]
        ws = [_dot(wq[idx[h // 2]][h % 2], s[h].astype(BF16)) for h in range(GDN_HEADS)]
        v_new = [(sol[idx[h // 2]][h % 2][:, :GDN_DV] - ws[h][:c]).astype(BF16) for h in range(GDN_HEADS)]
        upd = [_dot(ikd[idx[h // 2]], jnp.concatenate([v_new[h], zrows] if h % 2 == 0 else [zrows, v_new[h]], axis=0))
               for h in range(GDN_HEADS)]
        for h in range(GDN_HEADS):
            s_ref[h] = s[h] * g_tot[idx[h // 2]][h % 2] + upd[h][c:]
            o = ws[h][c:] + upd[h][:c]
            zg = qkvz_ref[3, h, rows(j), :]
            y_ref[rows(j), h * GDN_DV:(h + 1) * GDN_DV] = _gated_norm(o, zg, nw).astype(y_ref.dtype)

    for _ in step_part:
        pass

    @pl.when(n == pl.num_programs(0) - 1)
    def _():
        sfin_ref[...] = s_ref[...]
        for h in range(GDN_HEADS):
            ys_ref[:, h * GDN_DV:(h + 1) * GDN_DV] = yacc_ref[h].astype(ys_ref.dtype)


def _gdn_step_part(n, sb, qs_ref, gs_ref, nw_ref, st_ref, sto_ref, yacc_ref):
    nw = nw_ref[...]
    lane = lax.broadcasted_iota(jnp.int32, (1, LANES), 1)
    piece_row = lax.broadcasted_iota(jnp.int32, (2 * SUBLANES, LANES), 0)
    ones = jnp.where(piece_row < 3, 1.0, 0.0).astype(BF16)

    def spread(row):
        p1 = row.astype(BF16).astype(F32)
        r1 = row - p1
        p2 = r1.astype(BF16).astype(F32)
        p3 = r1 - p2
        pieces = jnp.where(piece_row == 0, p1, jnp.where(piece_row == 1, p2, jnp.where(piece_row == 2, p3, 0.0)))
        return lax.dot_general(pieces.astype(BF16), ones, (((0,), (0,)), ((), ())), preferred_element_type=F32)

    def pick(row, l):
        return jnp.sum(jnp.where(lane == l, row, 0.0), axis=1, keepdims=True)

    pairs = [(b, h) for b in range(sb) for h in range(GDN_HEADS)]
    m = range(len(pairs))
    rows = [pl.ds(n * sb + b, 1) for b in range(sb)]
    grow = [gs_ref[rows[b], :] for b in range(sb)]
    beta = [pick(grow[b], LANE_BETA + h) for b, h in pairs]
    eg = [pick(grow[b], LANE_EG + h) for b, h in pairs]
    yield
    qb = [spread(qs_ref[0, h, rows[b], :]) for b, h in pairs]
    yield
    kb = [spread(qs_ref[1, h, rows[b], :]) for b, h in pairs]
    yield
    s = [st_ref[b, h] * eg[i] for i, (b, h) in enumerate(pairs)]
    yield
    kv = [jnp.sum(s[i] * kb[i], axis=0, keepdims=True) for i in m]
    yield
    delta = [(qs_ref[2, h, rows[b], :] - kv[i]) * beta[i] for i, (b, h) in enumerate(pairs)]
    s = [s[i] + kb[i] * delta[i] for i in m]
    yield
    o = [jnp.sum(s[i] * qb[i], axis=0, keepdims=True) for i in m]
    yield
    for i, (b, h) in enumerate(pairs):
        sto_ref[b, h] = s[i]
        yacc_ref[h, rows[b], :] = _gated_norm(o[i], qs_ref[3, h, rows[b], :], nw)


def _gdn(qkvz, gate, qkvz_s, gate_s, norm_w, state, *, nc=GDN_CHUNKS_PER_STEP):
    t, b = qkvz.shape[2], qkvz_s.shape[2]
    c = GDN_CHUNK
    steps = t // (nc * c)
    assert t % (nc * c) == 0 and c & (c - 1) == 0 and b % steps == 0
    sb = b // steps
    state_spec = pl.BlockSpec((sb, GDN_HEADS, GDN_DK, GDN_DV), lambda n: (n, 0, 0, 0))
    const2 = lambda n: (0, 0)
    return pl.pallas_call(
        functools.partial(_gdn_body, nc, sb),
        grid=(steps,),
        in_specs=[
            pl.BlockSpec((4, GDN_HEADS, nc * c, LANES), lambda n: (0, 0, n, 0)),
            pl.BlockSpec((nc * c, LANES), lambda n: (n, 0)),
            pl.BlockSpec((1, GDN_DV), const2),
            pl.BlockSpec((4, GDN_HEADS, b, LANES), lambda n: (0, 0, 0, 0)),
            pl.BlockSpec((b, LANES), const2),
            state_spec,
        ],
        out_specs=[
            pl.BlockSpec((nc * c, D_GDN_V), lambda n: (n, 0)),
            pl.BlockSpec((GDN_HEADS, GDN_DK, GDN_DV), lambda n: (0, 0, 0)),
            pl.BlockSpec((b, D_GDN_V), const2),
            state_spec,
        ],
        out_shape=[
            jax.ShapeDtypeStruct((t, D_GDN_V), BF16),
            jax.ShapeDtypeStruct((GDN_HEADS, GDN_DK, GDN_DV), F32),
            jax.ShapeDtypeStruct((b, D_GDN_V), BF16),
            jax.ShapeDtypeStruct(state.shape, F32),
        ],
        scratch_shapes=[pltpu.VMEM((GDN_HEADS, GDN_DK, GDN_DV), F32), pltpu.VMEM((GDN_HEADS, b, GDN_DV), F32)],
        compiler_params=_cparams("arbitrary"),
        name="gdn",
    )(qkvz, gate, norm_w.reshape(1, GDN_DV), qkvz_s, gate_s, state)


def _gate_param_row(p):
    return jnp.zeros((1, LANES), F32).at[0, LANE_G:LANE_G + GDN_HEADS].set(p.astype(F32))


def _ffn(h, hs, x, xs, state_t, w_up, w_conv, w_down, next_norm, *, final=False, name):
    (hid, keep_a, keep_b), (hid_s, up_a, up_b) = _ffn_up(h, hs, w_up, w_conv, state_t)
    out, out_s = _matmul([hid], [hid_s], w_down, tm=FFN_DOWN_ROW_TILE, residual=x, residual2=xs,
                         norm_w=next_norm, final=final, chunk=FFN_DOWN_STAGE_ROWS, name=name)
    return out, out_s, jnp.concatenate([keep_a, keep_b], axis=1), jnp.concatenate([up_a, up_b], axis=1)


def kernel(x_prompt, x_sample, state_l0_pool, cache_l0_k, cache_l0_v, state_l0_ffn_conv, state_l1_gdn_conv, state_l1_gdn_S, state_l1_sconv, state_l1_ffn_conv, l0_norm_mix, l0_w_in, l0_pool_w, l0_pool_scale, l0_sinks, l0_w_out, l0_norm_ffn, l0_ffn_w_up, l0_ffn_conv, l0_ffn_w_down, l1_norm_mix, l1_w_in, l1_gdn_conv, l1_gdn_A_log, l1_gdn_dt_bias, l1_gdn_norm, l1_sconv_w, l1_w_out, l1_norm_ffn, l1_ffn_w_up, l1_ffn_conv, l1_ffn_w_down, final_norm):
    bp, t, d = x_prompt.shape
    nb, ts = x_sample.shape[:2]
    wb = cache_l0_k.shape[1]
    assert bp == 1 and ts == 1 and d == D_MODEL and wb == WINDOW and t >= WINDOW
    w_in1_t = l1_w_in.T
    a_log_row = _gate_param_row(l1_gdn_A_log)
    dt_bias_row = _gate_param_row(l1_gdn_dt_bias)
    pool_t = state_l0_pool.transpose(1, 0, 2)
    kt, vt = cache_l0_k.transpose(0, 2, 3, 1), cache_l0_v.transpose(0, 2, 3, 1)
    ffn0_t = state_l0_ffn_conv.transpose(1, 0, 2)
    gconv_t = state_l1_gdn_conv.transpose(1, 0, 2)
    sconv_t = state_l1_sconv.transpose(1, 0, 2)
    ffn1_t = state_l1_ffn_conv.transpose(1, 0, 2)
    last = SUBLANES

    xp, xs = x_prompt[0], x_sample[:, 0]
    z0, z0s = _matmul([_rmsnorm(xp, l0_norm_mix, BF16)], [_rmsnorm(xs, l0_norm_mix, BF16)], l0_w_in,
                      name="in_proj0")
    mix = _mixer0_seq(z0, l0_pool_w, l0_pool_scale, l0_sinks)
    y_pool = _pool_step(z0s, pool_t, l0_pool_w, l0_pool_scale)
    att, kt_new, vt_new = _attn_step(z0s, kt, vt, l0_sinks)
    mix_s = jnp.concatenate([y_pool, att.astype(BF16)], axis=1)
    (x1, h1), (x1s, h1s) = _matmul([mix], [mix_s], l0_w_out, residual=xp, residual2=xs,
                                   norm_w=l0_norm_ffn, name="out_proj0")
    (x2, h2), (x2s, h2s), ffn0_rows, ffn0_new = _ffn(
        h1, h1s, x1, x1s, ffn0_t, l0_ffn_w_up, l0_ffn_conv, l0_ffn_w_down, l1_norm_mix, name="ffn_down0")
    (qkvz, raw), (qkvz_s, raw_s) = _qkvz(h2, h2s, w_in1_t, l1_gdn_conv, gconv_t)
    (ysc, gate, mrows), (ysc_s, gate_s, m_s) = _tail(
        h2, h2s, w_in1_t, l1_sconv_w, a_log_row, dt_bias_row, sconv_t)
    ygdn, s_fin, ygdn_s, s_new = _gdn(qkvz, gate, qkvz_s, gate_s, l1_gdn_norm, state_l1_gdn_S)
    (x3, h3), (x3s, h3s) = _matmul([ygdn, ysc], [ygdn_s, ysc_s], l1_w_out, residual=x2,
                                   residual2=x2s, norm_w=l1_norm_ffn, name="out_proj1")
    y_prompt, y_sample, ffn1_rows, ffn1_new = _ffn(
        h3, h3s, x3, x3s, ffn1_t, l1_ffn_w_up, l1_ffn_conv, l1_ffn_w_down, final_norm, final=True,
        name="ffn_down1")

    p_pool = z0[t - POOL_BUF:, :D_POOL][None]
    p_k = z0[t - wb:, O_K:O_V].reshape(1, wb, N_KV_HEADS, HEAD_DIM)
    p_v = z0[t - wb:, O_V:].reshape(1, wb, N_KV_HEADS, HEAD_DIM)
    p_ffn0 = ffn0_rows[last - (FFN_CONV - 1):][None]
    p_gconv = raw[last - (GDN_CONV - 1):, :D_GDN_CONV][None]
    p_sconv = mrows[last - (SCONV_W - 1):][None]
    p_ffn1 = ffn1_rows[last - (FFN_CONV - 1):][None]

    def push(state_t, new_row):
        return jnp.concatenate([state_t[1:], new_row[None]], axis=0).transpose(1, 0, 2)

    s_pool = push(pool_t, z0s[:, :D_POOL])
    s_k, s_v = kt_new.transpose(0, 3, 1, 2), vt_new.transpose(0, 3, 1, 2)
    s_ffn0 = push(ffn0_t, ffn0_new)
    s_gconv = push(gconv_t, raw_s[:, :D_GDN_CONV])
    s_sconv = push(sconv_t, m_s)
    s_ffn1 = push(ffn1_t, ffn1_new)
    return (y_prompt[None], y_sample[:, None], p_pool, s_pool, p_k, s_k, p_v, s_v, p_ffn0, s_ffn0,
            p_gconv, s_gconv, s_fin[None], s_new, p_sconv, s_sconv, p_ffn1, s_ffn1)
```

```python
import functools

import jax
import jax.numpy as jnp
from jax import lax
from jax.experimental import pallas as pl
from jax.experimental.pallas import tpu as pltpu

F32 = jnp.float32
BF16 = jnp.bfloat16
EPS = 1e-6
NEG = -1e30

LANES = 128
SUBLANES = 8
MXU_COLS = 256
VMEM_LIMIT_BYTES = 60 * 1024 * 1024

ROW_TILE = 512
FFN_UP_ROW_TILE = 1024
FFN_UP_COL_TILE = 512
FFN_DOWN_ROW_TILE = 256
STAGE_ROWS = 256
FFN_DOWN_STAGE_ROWS = 512
GDN_CHUNKS_PER_STEP = 4
QKVZ_GROUP_HEADS = 4
ATTN_STEP_SEQS = 8

D_MODEL = 2048
D_POOL = 512
POOL_WINDOWS = (2, 4, 8, 16)
POOL_BUF = max(POOL_WINDOWS) - 1
POOL_HIST = 16
HEAD_DIM = 64
N_Q_HEADS = 24
N_KV_HEADS = 4
Q_PER_KV = N_Q_HEADS // N_KV_HEADS
WINDOW = 128
D_ATTN = N_Q_HEADS * HEAD_DIM
D_KV = N_KV_HEADS * HEAD_DIM
D_IN0 = D_POOL + D_ATTN + 2 * D_KV
O_K = D_POOL + D_ATTN
O_V = O_K + D_KV
GDN_HEADS = 12
GDN_DK = 128
GDN_DV = 128
D_GDN_K = GDN_HEADS * GDN_DK
D_GDN_V = GDN_HEADS * GDN_DV
D_GDN_CONV = 2 * D_GDN_K + D_GDN_V
D_GDN_MAIN = D_GDN_CONV + D_GDN_V
GDN_CONV = 4
GDN_CHUNK = 64
D_SCONV = 512
SCONV_W = 3
D_TAIL = 3 * D_SCONV + LANES
D_FF = 5632
FFN_CONV = 3
LANE_BETA = 0
LANE_G = GDN_HEADS
LANE_EG = 2 * GDN_HEADS


def _cparams(*sem):
    return pltpu.CompilerParams(dimension_semantics=sem, vmem_limit_bytes=VMEM_LIMIT_BYTES)


def _dot(a, b):
    return jnp.dot(a, b, preferred_element_type=F32)


def _dot_nt(a, b):
    return lax.dot_general(a, b, (((1,), (1,)), ((), ())), preferred_element_type=F32)


def _silu(x):
    return x * jax.nn.sigmoid(x)


def _cast_rows(w_ref, wbf_ref, chunk=STAGE_ROWS):
    k = w_ref.shape[0]
    chunk = min(chunk, k)
    assert k % chunk == 0

    def body(c, carry):
        r = pl.multiple_of(c * chunk, chunk)
        wbf_ref[pl.ds(r, chunk), :] = w_ref[pl.ds(r, chunk), :].astype(BF16)
        return carry

    lax.fori_loop(0, k // chunk, body, 0)


def _cast_transposed(wt_ref, wbf_ref):
    n = wt_ref.shape[0]
    step = MXU_COLS if n % MXU_COLS == 0 else LANES
    for c in range(0, n, step):
        wbf_ref[:, c:c + step] = wt_ref[c:c + step, :].T.astype(BF16)


def _store_slabs(slab_ref, u):
    m = u.shape[0]
    for c in range(slab_ref.shape[0]):
        slab_ref[c, SUBLANES:SUBLANES + m, :] = u[:, c * LANES:(c + 1) * LANES]


def _tap(slab_ref, c, m, delay):
    return slab_ref[c, pl.ds(SUBLANES - delay, m), :]


def _roll_history(slab_ref, m):
    last = [slab_ref[c, m:m + SUBLANES, :] for c in range(slab_ref.shape[0])]
    for c, rows in enumerate(last):
        slab_ref[c, 0:SUBLANES, :] = rows
    return jnp.concatenate(last, axis=1)


def _rmsnorm_body(x_ref, w_ref, o_ref):
    x = x_ref[...]
    ms = jnp.mean(x * x, axis=-1, keepdims=True)
    o_ref[...] = (x * lax.rsqrt(ms + EPS) * w_ref[...]).astype(o_ref.dtype)


def _rmsnorm(x, w, out_dtype):
    m, d = x.shape
    tm = min(m, ROW_TILE)
    assert m % tm == 0
    return pl.pallas_call(
        _rmsnorm_body,
        grid=(m // tm,),
        in_specs=[pl.BlockSpec((tm, d), lambda i: (i, 0)), pl.BlockSpec((1, d), lambda i: (0, 0))],
        out_specs=pl.BlockSpec((tm, d), lambda i: (i, 0)),
        out_shape=jax.ShapeDtypeStruct((m, d), out_dtype),
        compiler_params=_cparams("parallel"),
        name="rmsnorm",
    )(x, w.reshape(1, d))


def _mm_body(n_pieces, n_stage, mode, *refs):
    n_out = {"plain": 1, "residual": 2, "final": 1}[mode]
    n_in = n_pieces + (0 if mode == "plain" else 1)
    set_a, set_b = refs[:n_in], refs[n_in:2 * n_in]
    w_ref = refs[2 * n_in]
    pos = 2 * n_in + 1
    nw_ref = None
    if mode != "plain":
        nw_ref = refs[pos]
        pos += 1
    out_a, out_b = refs[pos:pos + n_out], refs[pos + n_out:pos + 2 * n_out]
    wbf_ref = refs[pos + 2 * n_out]
    s = pl.program_id(0)
    last = pl.num_programs(0) - 1
    ck = w_ref.shape[0]

    @pl.when(s < n_stage)
    def _():
        _cast_rows(w_ref, wbf_ref.at[pl.ds(pl.multiple_of(s * ck, ck), ck), :])

    def rows(ins, outs):
        acc, row = None, 0
        for x_ref in ins[:n_pieces]:
            kp = x_ref.shape[1]
            d = _dot(x_ref[...], wbf_ref[row:row + kp, :])
            acc = d if acc is None else acc + d
            row += kp
        if mode == "plain":
            outs[0][...] = acc
            return
        xn = acc + ins[n_pieces][...]
        hn = xn * lax.rsqrt(jnp.mean(xn * xn, axis=-1, keepdims=True) + EPS) * nw_ref[...]
        if mode == "residual":
            outs[0][...] = xn
        outs[-1][...] = hn.astype(outs[-1].dtype)

    @pl.when((s >= n_stage) & (s < last))
    def _():
        rows(set_a, out_a)

    @pl.when(s == last)
    def _():
        rows(set_b, out_b)


def _matmul(xs, xs2, w, *, tm=ROW_TILE, residual=None, residual2=None, norm_w=None, final=False, chunk=STAGE_ROWS,
            name="matmul"):
    m, m2 = xs[0].shape[0], xs2[0].shape[0]
    k, n = w.shape
    tm = min(tm, m)
    assert m % tm == 0 and k % chunk == 0 and sum(x.shape[1] for x in xs) == k
    n_stage, nm = k // chunk, m // tm
    mode = "plain" if residual is None else ("final" if final else "residual")

    def row_tile(s):
        return jnp.clip(s - n_stage, 0, nm - 1)

    tile_a = pl.BlockSpec((tm, n), lambda s: (row_tile(s), 0))
    tile_b = pl.BlockSpec((m2, n), lambda s: (0, 0))
    in_a = [pl.BlockSpec((tm, x.shape[1]), lambda s: (row_tile(s), 0)) for x in xs]
    in_b = [pl.BlockSpec((m2, x.shape[1]), lambda s: (0, 0)) for x in xs2]
    args_a, args_b = list(xs), list(xs2)
    if mode != "plain":
        in_a.append(tile_a)
        in_b.append(tile_b)
        args_a.append(residual)
        args_b.append(residual2)
    in_specs = in_a + in_b + [pl.BlockSpec((chunk, n), lambda s: (jnp.minimum(s, n_stage - 1), 0))]
    args = args_a + args_b + [w]
    if mode != "plain":
        in_specs.append(pl.BlockSpec((1, n), lambda s: (0, 0)))
        args.append(norm_w.reshape(1, n))
    dtypes = [F32, BF16] if mode == "residual" else [F32]
    out_specs = [tile_a] * len(dtypes) + [tile_b] * len(dtypes)
    out_shape = ([jax.ShapeDtypeStruct((m, n), dt) for dt in dtypes]
                 + [jax.ShapeDtypeStruct((m2, n), dt) for dt in dtypes])
    outs = pl.pallas_call(
        functools.partial(_mm_body, len(xs), n_stage, mode),
        grid=(n_stage + nm + 1,),
        in_specs=in_specs,
        out_specs=out_specs,
        out_shape=out_shape,
        scratch_shapes=[pltpu.VMEM((k, n), BF16)],
        compiler_params=_cparams("arbitrary"),
        name=name,
    )(*args)
    half = len(dtypes)
    first, second = outs[:half], outs[half:]
    return (first[0], second[0]) if half == 1 else (tuple(first), tuple(second))


def _ffn_up_body(x_ref, xs_ref, wa_ref, wb_ref, cwa_ref, cwb_ref, a2_ref, a1_ref, b2_ref, b1_ref,
                 hid_ref, upa_ref, upb_ref, hids_ref, upsa_ref, upsb_ref, wa_bf, wb_bf, ca_ref, cb_ref):
    tm = x_ref.shape[0]
    i = pl.program_id(1)

    @pl.when(i == 0)
    def _():
        _cast_rows(wa_ref, wa_bf)
        _cast_rows(wb_ref, wb_bf)
        ca_ref[:, 0:SUBLANES, :] = jnp.zeros((ca_ref.shape[0], SUBLANES, LANES), F32)
        cb_ref[:, 0:SUBLANES, :] = jnp.zeros((cb_ref.shape[0], SUBLANES, LANES), F32)

    def gate(ua, ub, taps_a, taps_b):
        c_a = cwa_ref[0:1, :] * taps_a[0] + cwa_ref[1:2, :] * taps_a[1] + cwa_ref[2:3, :] * ua
        c_b = cwb_ref[0:1, :] * taps_b[0] + cwb_ref[1:2, :] * taps_b[1] + cwb_ref[2:3, :] * ub
        return (_silu(c_a) * c_b).astype(BF16)

    @pl.when(i > 0)
    def _():
        x = x_ref[...]
        _store_slabs(ca_ref, _dot(x, wa_bf[...]))
        _store_slabs(cb_ref, _dot(x, wb_bf[...]))
        for c in range(ca_ref.shape[0]):
            cols = slice(c * LANES, (c + 1) * LANES)
            c_a = (cwa_ref[0:1, cols] * _tap(ca_ref, c, tm, 2) + cwa_ref[1:2, cols] * _tap(ca_ref, c, tm, 1)
                   + cwa_ref[2:3, cols] * _tap(ca_ref, c, tm, 0))
            c_b = (cwb_ref[0:1, cols] * _tap(cb_ref, c, tm, 2) + cwb_ref[1:2, cols] * _tap(cb_ref, c, tm, 1)
                   + cwb_ref[2:3, cols] * _tap(cb_ref, c, tm, 0))
            hid_ref[:, cols] = (_silu(c_a) * c_b).astype(BF16)
        upa_ref[...] = _roll_history(ca_ref, tm)
        upb_ref[...] = _roll_history(cb_ref, tm)

    @pl.when(i == 0)
    def _():
        x = xs_ref[...]
        ua = _dot(x, wa_bf[...])
        ub = _dot(x, wb_bf[...])
        hids_ref[...] = gate(ua, ub, (a2_ref[...], a1_ref[...]), (b2_ref[...], b1_ref[...]))
        upsa_ref[...] = ua
        upsb_ref[...] = ub


def _ffn_up(h, hs, w_up, conv_w, state_t, *, tn=FFN_UP_COL_TILE, tm=FFN_UP_ROW_TILE):
    m, k = h.shape
    ms = hs.shape[0]
    tm = min(tm, m)
    nb, nm = D_FF // tn, m // tm
    assert m % tm == 0 and D_FF % tn == 0
    tile = lambda j, i: (jnp.maximum(i - 1, 0), j)
    in_specs = [
        pl.BlockSpec((tm, k), lambda j, i: (jnp.maximum(i - 1, 0), 0)),
        pl.BlockSpec((ms, k), lambda j, i: (0, 0)),
        pl.BlockSpec((k, tn), lambda j, i: (0, j)),
        pl.BlockSpec((k, tn), lambda j, i: (0, j + nb)),
        pl.BlockSpec((FFN_CONV, tn), lambda j, i: (0, j)),
        pl.BlockSpec((FFN_CONV, tn), lambda j, i: (0, j + nb)),
        pl.BlockSpec((None, ms, tn), lambda j, i: (0, 0, j)),
        pl.BlockSpec((None, ms, tn), lambda j, i: (1, 0, j)),
        pl.BlockSpec((None, ms, tn), lambda j, i: (0, 0, j + nb)),
        pl.BlockSpec((None, ms, tn), lambda j, i: (1, 0, j + nb)),
    ]
    keep_spec = pl.BlockSpec((SUBLANES, tn), lambda j, i: (0, j))
    step_spec = pl.BlockSpec((ms, tn), lambda j, i: (0, j))
    outs = pl.pallas_call(
        _ffn_up_body,
        grid=(nb, nm + 1),
        in_specs=in_specs,
        out_specs=[pl.BlockSpec((tm, tn), tile), keep_spec, keep_spec, step_spec, step_spec, step_spec],
        out_shape=[
            jax.ShapeDtypeStruct((m, D_FF), BF16),
            jax.ShapeDtypeStruct((SUBLANES, D_FF), F32),
            jax.ShapeDtypeStruct((SUBLANES, D_FF), F32),
            jax.ShapeDtypeStruct((ms, D_FF), BF16),
            jax.ShapeDtypeStruct((ms, D_FF), F32),
            jax.ShapeDtypeStruct((ms, D_FF), F32),
        ],
        scratch_shapes=[pltpu.VMEM((k, tn), BF16), pltpu.VMEM((k, tn), BF16),
                        pltpu.VMEM((tn // LANES, SUBLANES + tm, LANES), F32),
                        pltpu.VMEM((tn // LANES, SUBLANES + tm, LANES), F32)],
        compiler_params=_cparams("arbitrary", "arbitrary"),
        name="ffn_up",
    )(h, hs, w_up, w_up, conv_w, conv_w, state_t, state_t, state_t, state_t)
    return outs[:3], outs[3:]


def _half_lane_pair(x, head_in_high, lo_mask):
    if head_in_high:
        hi = jnp.where(lo_mask, 0.0, x)
        return pltpu.roll(hi, HEAD_DIM, 1), hi
    lo = jnp.where(lo_mask, x, 0.0)
    return lo, pltpu.roll(lo, HEAD_DIM, 1)


def _mixer0_seq_body(sink_ref, z_ref, kvp_ref, up_ref, pw_ref, ps_ref, o_ref):
    n = pl.program_id(0)
    first = n == 0
    w = WINDOW
    lo_mask = lax.broadcasted_iota(jnp.int32, (1, LANES), 1) < HEAD_DIM

    hist = jnp.where(first, 0.0, up_ref[...])
    pos = n * w + lax.broadcasted_iota(jnp.int32, (w, 1), 0)
    for g, win in enumerate(POOL_WINDOWS):
        sl = slice(g * LANES, (g + 1) * LANES)
        e = jnp.concatenate([hist[:, sl], z_ref[:, sl]], axis=0)
        s, sh = e, 1
        while sh < win:
            s = s + pltpu.roll(s, sh, 0)
            sh *= 2
        cnt = jnp.minimum(pos + 1, win).astype(F32)
        diff = s[POOL_HIST:] / cnt - e[POOL_HIST:]
        y = _dot(diff.astype(BF16), pw_ref[g].astype(BF16)) * ps_ref[:, sl]
        o_ref[:, sl] = y.astype(o_ref.dtype)

    row = lax.broadcasted_iota(jnp.int32, (w, 2 * w), 0)
    col = lax.broadcasted_iota(jnp.int32, (w, 2 * w), 1)
    valid = (col >= row) & (col <= row + w) & (jnp.logical_not(first) | (col >= w))
    tiles_per_kv = Q_PER_KV * HEAD_DIM // LANES
    for c in range(D_KV // LANES):
        k2 = jnp.concatenate([kvp_ref[:, c * LANES:(c + 1) * LANES],
                              z_ref[:, O_K + c * LANES:O_K + (c + 1) * LANES]], axis=0)
        v2 = jnp.concatenate([kvp_ref[:, D_KV + c * LANES:D_KV + (c + 1) * LANES],
                              z_ref[:, O_V + c * LANES:O_V + (c + 1) * LANES]], axis=0)
        for p in range(2):
            hk = 2 * c + p
            k_lo, k_hi = (t.astype(BF16) for t in _half_lane_pair(k2, p == 1, lo_mask))
            v_lo, v_hi = (t.astype(BF16) for t in _half_lane_pair(v2, p == 1, lo_mask))
            q0 = D_POOL + hk * Q_PER_KV * HEAD_DIM
            qst = jnp.concatenate(
                [z_ref[:, q0 + a * LANES:q0 + (a + 1) * LANES] for a in range(tiles_per_kv)], axis=0)
            qst = (qst * HEAD_DIM ** -0.5).astype(BF16)
            s_lo = _dot_nt(qst, k_lo)
            s_hi = _dot_nt(qst, k_hi)
            for a in range(tiles_per_kv):
                probs, inv = [], []
                for par, s_all in ((0, s_lo), (1, s_hi)):
                    sk = sink_ref[hk * Q_PER_KV + 2 * a + par]
                    s = jnp.where(valid, s_all[a * w:(a + 1) * w], NEG)
                    mx = jnp.maximum(jnp.max(s, axis=-1, keepdims=True), sk)
                    pr = jnp.exp(s - mx)
                    den = jnp.sum(pr, axis=-1, keepdims=True) + jnp.exp(sk - mx)
                    probs.append(pr.astype(BF16))
                    inv.append(1.0 / den)
                o = _dot(probs[0], v_lo) + _dot(probs[1], v_hi)
                o = o * jnp.where(lo_mask, inv[0], inv[1])
                o_ref[:, q0 + a * LANES:q0 + (a + 1) * LANES] = o.astype(o_ref.dtype)


def _mixer0_seq(z0, pool_w, pool_scale, sinks):
    t = z0.shape[0]
    w = WINDOW
    assert t % w == 0
    hist_blocks = w // POOL_HIST
    return pl.pallas_call(
        _mixer0_seq_body,
        grid=(t // w,),
        in_specs=[
            pl.BlockSpec(memory_space=pltpu.SMEM),
            pl.BlockSpec((w, D_IN0), lambda n: (n, 0)),
            pl.BlockSpec((w, 2 * D_KV), lambda n: (jnp.maximum(n - 1, 0), O_K // (2 * D_KV))),
            pl.BlockSpec((POOL_HIST, D_POOL), lambda n: (jnp.maximum(n * hist_blocks - 1, 0), 0)),
            pl.BlockSpec((len(POOL_WINDOWS), LANES, LANES), lambda n: (0, 0, 0)),
            pl.BlockSpec((1, D_POOL), lambda n: (0, 0)),
        ],
        out_specs=pl.BlockSpec((w, D_POOL + D_ATTN), lambda n: (n, 0)),
        out_shape=jax.ShapeDtypeStruct((t, D_POOL + D_ATTN), BF16),
        compiler_params=_cparams("parallel"),
        name="mixer0_seq",
    )(sinks, z0, z0, z0, pool_w, pool_scale.reshape(1, D_POOL))


def _pool_step_body(*refs):
    hist_refs = refs[:POOL_BUF]
    z_ref, pw_ref, ps_ref, o_ref = refs[POOL_BUF:]
    for g, win in enumerate(POOL_WINDOWS):
        sl = slice(g * LANES, (g + 1) * LANES)
        u = z_ref[:, sl]
        s = u
        for r in range(POOL_BUF - (win - 1), POOL_BUF):
            s = s + hist_refs[r][:, sl]
        diff = s / float(win) - u
        y = _dot(diff.astype(BF16), pw_ref[g].astype(BF16)) * ps_ref[:, sl]
        o_ref[:, sl] = y.astype(o_ref.dtype)


def _pool_step(z0, hist_t, pool_w, pool_scale):
    b = z0.shape[0]
    in_specs = [pl.BlockSpec((None, b, D_POOL), functools.partial(lambda i, r: (r, 0, 0), r=r))
                for r in range(POOL_BUF)]
    in_specs += [
        pl.BlockSpec((b, D_POOL), lambda i: (0, 0)),
        pl.BlockSpec((len(POOL_WINDOWS), LANES, LANES), lambda i: (0, 0, 0)),
        pl.BlockSpec((1, D_POOL), lambda i: (0, 0)),
    ]
    return pl.pallas_call(
        _pool_step_body,
        grid=(1,),
        in_specs=in_specs,
        out_specs=pl.BlockSpec((b, D_POOL), lambda i: (0, 0)),
        out_shape=jax.ShapeDtypeStruct((b, D_POOL), BF16),
        compiler_params=_cparams("arbitrary"),
        name="pool_step",
    )(*([hist_t] * POOL_BUF), z0, pool_w, pool_scale.reshape(1, D_POOL))


def _attn_step_body(q_ref, kn_ref, vn_ref, kt_ref, vt_ref, sink_ref, o_ref, kto_ref, vto_ref):
    bs = q_ref.shape[0]
    newest = lax.broadcasted_iota(jnp.int32, (HEAD_DIM, WINDOW), 1) == WINDOW - 1
    pairs = [(b, hk) for b in range(bs) for hk in range(N_KV_HEADS)]
    n = range(len(pairs))
    kt = [kt_ref[b, hk] for b, hk in pairs]
    vt = [vt_ref[b, hk] for b, hk in pairs]
    q = [q_ref[b, hk] * HEAD_DIM ** -0.5 for b, hk in pairs]
    sk = [sink_ref[hk * Q_PER_KV:(hk + 1) * Q_PER_KV, :] for b, hk in pairs]
    s = [_dot(q[i].astype(BF16), kt[i].astype(BF16)) for i in n]
    s_new = [jnp.sum(q[i] * kn_ref[b, hk:hk + 1, :], axis=-1, keepdims=True) for i, (b, hk) in enumerate(pairs)]
    mx = [jnp.maximum(jnp.maximum(jnp.max(s[i], axis=-1, keepdims=True), s_new[i]), sk[i]) for i in n]
    pr = [jnp.exp(s[i] - mx[i]) for i in n]
    pn = [jnp.exp(s_new[i] - mx[i]) for i in n]
    den = [jnp.sum(pr[i], axis=-1, keepdims=True) + pn[i] + jnp.exp(sk[i] - mx[i]) for i in n]
    o = [_dot_nt(pr[i].astype(BF16), vt[i].astype(BF16)) for i in n]
    piece_row = lax.broadcasted_iota(jnp.int32, (2 * SUBLANES, HEAD_DIM), 0)
    ones = jnp.where(lax.broadcasted_iota(jnp.int32, (2 * SUBLANES, WINDOW), 0) < 3, 1.0, 0.0).astype(BF16)

    def spread(row):
        p1 = row.astype(BF16).astype(F32)
        r1 = row - p1
        p2 = r1.astype(BF16).astype(F32)
        p3 = r1 - p2
        pieces = jnp.where(piece_row == 0, p1, jnp.where(piece_row == 1, p2, jnp.where(piece_row == 2, p3, 0.0)))
        return lax.dot_general(pieces.astype(BF16), ones, (((0,), (0,)), ((), ())), preferred_element_type=F32)

    for i, (b, hk) in enumerate(pairs):
        o_ref[b, hk] = (o[i] + pn[i] * vn_ref[b, hk:hk + 1, :]) / den[i]
        kto_ref[b, hk] = jnp.where(newest, spread(kn_ref[b, hk:hk + 1, :]), pltpu.roll(kt[i], WINDOW - 1, 1))
        vto_ref[b, hk] = jnp.where(newest, spread(vn_ref[b, hk:hk + 1, :]), pltpu.roll(vt[i], WINDOW - 1, 1))


def _attn_step(z0, kt, vt, sinks, *, bs=ATTN_STEP_SEQS):
    b = z0.shape[0]
    assert b % bs == 0 and kt.shape[3] == WINDOW == LANES
    nblk = b // bs
    q4 = z0[:, D_POOL:O_K].reshape(b, N_KV_HEADS, Q_PER_KV, HEAD_DIM)
    kn = z0[:, O_K:O_V].reshape(b, N_KV_HEADS, HEAD_DIM)
    vn = z0[:, O_V:].reshape(b, N_KV_HEADS, HEAD_DIM)

    cache_spec = pl.BlockSpec((bs, N_KV_HEADS, HEAD_DIM, WINDOW), lambda i: (i, 0, 0, 0))
    new_spec = pl.BlockSpec((bs, N_KV_HEADS, HEAD_DIM), lambda i: (i, 0, 0))
    q_spec = pl.BlockSpec((bs, N_KV_HEADS, Q_PER_KV, HEAD_DIM), lambda i: (i, 0, 0, 0))
    att, kto, vto = pl.pallas_call(
        _attn_step_body,
        grid=(nblk,),
        in_specs=[q_spec, new_spec, new_spec, cache_spec, cache_spec,
                  pl.BlockSpec((N_Q_HEADS, 1), lambda i: (0, 0))],
        out_specs=[q_spec, cache_spec, cache_spec],
        out_shape=[jax.ShapeDtypeStruct(q4.shape, F32), jax.ShapeDtypeStruct(kt.shape, F32),
                   jax.ShapeDtypeStruct(vt.shape, F32)],
        compiler_params=_cparams("parallel"),
        name="attn_step",
    )(q4, kn, vn, kt, vt, sinks.reshape(N_Q_HEADS, 1))
    return att.reshape(b, D_ATTN), kto, vto


def _qkvz_body(x_ref, xs_ref, w_ref, cw_ref, p3_ref, p2_ref, p1_ref,
               o_ref, raw_ref, os_ref, raws_ref, w_bf, carry_ref):
    tm = x_ref.shape[0]
    j = pl.program_id(0)
    i = pl.program_id(1)

    @pl.when(i == 0)
    def _():
        _cast_transposed(w_ref, w_bf)
        carry_ref[:, 0:SUBLANES, :] = jnp.zeros((carry_ref.shape[0], SUBLANES, LANES), F32)

    def heads(a):
        return [a[:, h * LANES:(h + 1) * LANES] for h in range(GDN_HEADS)]

    def l2norm(a):
        return a * lax.rsqrt(jnp.sum(a * a, axis=-1, keepdims=True) + EPS)

    def finish(u, taps, out_ref):
        def conv_act():
            t3, t2, t1 = taps()
            c = cw_ref[3:4, :] * u
            c = cw_ref[0:1, :] * t3 + cw_ref[1:2, :] * t2 + cw_ref[2:3, :] * t1 + c
            return _silu(c)

        @pl.when(j == 0)
        def _():
            for h, a in enumerate(heads(conv_act())):
                out_ref[h] = l2norm(a) * GDN_DK ** -0.5

        @pl.when(j == 1)
        def _():
            for h, a in enumerate(heads(conv_act())):
                out_ref[h] = l2norm(a)

        @pl.when(j == 2)
        def _():
            for h, a in enumerate(heads(conv_act())):
                out_ref[h] = a

        @pl.when(j == 3)
        def _():
            for h, a in enumerate(heads(u)):
                out_ref[h] = a

    @pl.when((i > 0) & (j < 3))
    def _():
        x = x_ref[...]
        norm_scale = jnp.where(j == 0, GDN_DK ** -0.5, 1.0)
        gw = QKVZ_GROUP_HEADS * LANES
        for g0 in range(0, GDN_HEADS, QKVZ_GROUP_HEADS):
            u = _dot(x, w_bf[:, g0 * LANES:g0 * LANES + gw])
            for hh in range(QKVZ_GROUP_HEADS):
                h = g0 + hh
                sl = slice(h * LANES, (h + 1) * LANES)
                carry_ref[h, SUBLANES:SUBLANES + tm, :] = u[:, hh * LANES:(hh + 1) * LANES]
                c = cw_ref[3:4, sl] * _tap(carry_ref, h, tm, 0)
                c = (cw_ref[0:1, sl] * _tap(carry_ref, h, tm, 3) + cw_ref[1:2, sl] * _tap(carry_ref, h, tm, 2)
                     + cw_ref[2:3, sl] * _tap(carry_ref, h, tm, 1) + c)
                a = _silu(c)
                inv = jnp.where(j < 2, lax.rsqrt(jnp.sum(a * a, axis=-1, keepdims=True) + EPS) * norm_scale, 1.0)
                o_ref[h] = a * inv
        raw_ref[...] = _roll_history(carry_ref, tm)

    @pl.when((i > 0) & (j == 3))
    def _():
        u = _dot(x_ref[...], w_bf[...])
        for h, a in enumerate(heads(u)):
            o_ref[h] = a
        raw_ref[...] = u[tm - SUBLANES:]

    @pl.when(i == 0)
    def _():
        u = _dot(xs_ref[...], w_bf[...])
        raws_ref[...] = u
        finish(u, lambda: (p3_ref[...], p2_ref[...], p1_ref[...]), os_ref)


def _qkvz(h, hs, w_in_t, conv_w, state_t, *, tm=ROW_TILE):
    m, k = h.shape
    ms = hs.shape[0]
    tm = min(tm, m)
    tn = D_GDN_K
    assert D_GDN_K == D_GDN_V and GDN_DK == LANES and m % tm == 0
    nj, nm = D_GDN_MAIN // tn, m // tm
    in_specs = [
        pl.BlockSpec((tm, k), lambda j, i: (jnp.maximum(i - 1, 0), 0)),
        pl.BlockSpec((ms, k), lambda j, i: (0, 0)),
        pl.BlockSpec((tn, k), lambda j, i: (j, 0)),
        pl.BlockSpec((GDN_CONV, tn), lambda j, i: (0, jnp.minimum(j, 2))),
    ]
    for r in range(GDN_CONV - 1):
        in_specs.append(pl.BlockSpec((None, ms, tn), functools.partial(lambda j, i, r: (r, 0, jnp.minimum(j, 2)), r=r)))
    outs = pl.pallas_call(
        _qkvz_body,
        grid=(nj, nm + 1),
        in_specs=in_specs,
        out_specs=[
            pl.BlockSpec((None, GDN_HEADS, tm, LANES), lambda j, i: (j, 0, jnp.maximum(i - 1, 0), 0)),
            pl.BlockSpec((SUBLANES, tn), lambda j, i: (0, j)),
            pl.BlockSpec((None, GDN_HEADS, ms, LANES), lambda j, i: (j, 0, 0, 0)),
            pl.BlockSpec((ms, tn), lambda j, i: (0, j)),
        ],
        out_shape=[
            jax.ShapeDtypeStruct((nj, GDN_HEADS, m, LANES), F32),
            jax.ShapeDtypeStruct((SUBLANES, D_GDN_MAIN), F32),
            jax.ShapeDtypeStruct((nj, GDN_HEADS, ms, LANES), F32),
            jax.ShapeDtypeStruct((ms, D_GDN_MAIN), F32),
        ],
        scratch_shapes=[pltpu.VMEM((k, tn), BF16), pltpu.VMEM((tn // LANES, SUBLANES + tm, LANES), F32)],
        compiler_params=_cparams("arbitrary", "arbitrary"),
        name="qkvz",
    )(h, hs, w_in_t, conv_w, state_t, state_t, state_t)
    return outs[:2], outs[2:]


def _tail_body(x_ref, xs_ref, wa_ref, wb_ref, cw_ref, alog_ref, dtb_ref, p2_ref, p1_ref,
               ysc_ref, gate_ref, m_ref, yscs_ref, gates_ref, ms_ref, w_bf, carry_ref):
    tm = x_ref.shape[0]
    i = pl.program_id(0)

    @pl.when(i == 0)
    def _():
        n_raw = wb_ref.shape[0]
        step = MXU_COLS
        for c in range(0, 3 * D_SCONV, step):
            r0 = n_raw + c
            if r0 + step <= wa_ref.shape[0]:
                blk = wa_ref[r0:r0 + step, :]
            else:
                blk = jnp.concatenate([wa_ref[r0:, :], wb_ref[...]], axis=0)
            w_bf[:, c:c + step] = blk.T.astype(BF16)
        raw = jnp.concatenate([wa_ref[0:n_raw, :], jnp.zeros((LANES - n_raw, wa_ref.shape[1]), F32)], axis=0)
        w_bf[:, 3 * D_SCONV:] = raw.T.astype(BF16)
        carry_ref[:, 0:SUBLANES, :] = jnp.zeros((carry_ref.shape[0], SUBLANES, LANES), F32)

    def finish(z, mm, taps, y_ref, g_ref):
        conv = cw_ref[0:1, :] * taps[0] + cw_ref[1:2, :] * taps[1] + cw_ref[2:3, :] * mm
        y_ref[...] = (z[:, 0:D_SCONV] * conv).astype(y_ref.dtype)
        raw = z[:, 3 * D_SCONV:]
        lane = lax.broadcasted_iota(jnp.int32, raw.shape, 1)
        beta = jax.nn.sigmoid(raw)
        sp = raw + dtb_ref[...]
        softplus = jnp.maximum(sp, 0.0) + jnp.log1p(jnp.exp(-jnp.abs(sp)))
        g = -jnp.exp(alog_ref[...]) * softplus
        eg = pltpu.roll(jnp.exp(g), LANE_EG - LANE_G, 1)
        g_ref[...] = jnp.where(lane < LANE_G, beta, jnp.where(lane < LANE_EG, g, eg))

    @pl.when(i > 0)
    def _():
        z = _dot(x_ref[...], w_bf[...])
        mm = z[:, D_SCONV:2 * D_SCONV] * z[:, 2 * D_SCONV:3 * D_SCONV]
        _store_slabs(carry_ref, mm)
        taps = tuple(jnp.concatenate([_tap(carry_ref, c, tm, k) for c in range(carry_ref.shape[0])], axis=1)
                     for k in (2, 1))
        finish(z, mm, taps, ysc_ref, gate_ref)
        m_ref[...] = _roll_history(carry_ref, tm)

    @pl.when(i == 0)
    def _():
        z = _dot(xs_ref[...], w_bf[...])
        mm = z[:, D_SCONV:2 * D_SCONV] * z[:, 2 * D_SCONV:3 * D_SCONV]
        ms_ref[...] = mm
        finish(z, mm, (p2_ref[...], p1_ref[...]), yscs_ref, gates_ref)


def _tail(h, hs, w_in_t, conv_w, a_log_row, dt_bias_row, state_t, *, tm=ROW_TILE):
    m, k = h.shape
    ms = hs.shape[0]
    tm = min(tm, m)
    n_raw, n_conv = 2 * GDN_HEADS, 3 * D_SCONV
    assert m % tm == 0 and w_in_t.shape[0] == D_GDN_MAIN + n_raw + n_conv
    assert D_GDN_MAIN % n_conv == 0 and (D_GDN_MAIN + n_conv) % n_raw == 0 and n_raw % SUBLANES == 0
    nm = m // tm
    tile = lambda i: (jnp.maximum(i - 1, 0), 0)
    const = lambda i: (0, 0)
    in_specs = [
        pl.BlockSpec((tm, k), tile),
        pl.BlockSpec((ms, k), const),
        pl.BlockSpec((n_conv, k), lambda i: (D_GDN_MAIN // n_conv, 0)),
        pl.BlockSpec((n_raw, k), lambda i: ((D_GDN_MAIN + n_conv) // n_raw, 0)),
        pl.BlockSpec((SCONV_W, D_SCONV), const),
        pl.BlockSpec((1, LANES), const),
        pl.BlockSpec((1, LANES), const),
        pl.BlockSpec((None, ms, D_SCONV), lambda i: (0, 0, 0)),
        pl.BlockSpec((None, ms, D_SCONV), lambda i: (1, 0, 0)),
    ]
    outs = pl.pallas_call(
        _tail_body,
        grid=(nm + 1,),
        in_specs=in_specs,
        out_specs=[
            pl.BlockSpec((tm, D_SCONV), tile),
            pl.BlockSpec((tm, LANES), tile),
            pl.BlockSpec((SUBLANES, D_SCONV), const),
            pl.BlockSpec((ms, D_SCONV), const),
            pl.BlockSpec((ms, LANES), const),
            pl.BlockSpec((ms, D_SCONV), const),
        ],
        out_shape=[
            jax.ShapeDtypeStruct((m, D_SCONV), BF16),
            jax.ShapeDtypeStruct((m, LANES), F32),
            jax.ShapeDtypeStruct((SUBLANES, D_SCONV), F32),
            jax.ShapeDtypeStruct((ms, D_SCONV), BF16),
            jax.ShapeDtypeStruct((ms, LANES), F32),
            jax.ShapeDtypeStruct((ms, D_SCONV), F32),
        ],
        scratch_shapes=[pltpu.VMEM((k, D_TAIL), BF16),
                        pltpu.VMEM((D_SCONV // LANES, SUBLANES + tm, LANES), F32)],
        compiler_params=_cparams("arbitrary"),
        name="tail",
    )(h, hs, w_in_t, w_in_t, conv_w, a_log_row, dt_bias_row, state_t, state_t)
    return outs[:3], outs[3:]


def _gated_norm(o, zg, nw):
    y = o * lax.rsqrt(jnp.mean(o * o, axis=-1, keepdims=True) + EPS) * nw
    return y * _silu(zg)


def _gdn_body(nc, sb, qkvz_ref, gate_ref, nw_ref, qs_ref, gs_ref, st_ref,
              y_ref, sfin_ref, ys_ref, sto_ref, s_ref, yacc_ref):
    c = GDN_CHUNK
    n = pl.program_id(0)
    step_part = _gdn_step_part(n, sb, qs_ref, gs_ref, nw_ref, st_ref, sto_ref, yacc_ref)

    def tick():
        next(step_part, None)

    @pl.when(n == 0)
    def _():
        s_ref[...] = jnp.zeros_like(s_ref)

    assert 2 * c == LANES and GDN_HEADS % 2 == 0
    r = lax.broadcasted_iota(jnp.int32, (c, LANES), 0)
    lane = lax.broadcasted_iota(jnp.int32, (c, LANES), 1)
    lo = lane < c
    col = jnp.where(lo, lane, lane - c)
    tri = r >= col
    strict = r > col
    rr = lax.broadcasted_iota(jnp.int32, (c, c), 0)
    ones = jnp.where(rr >= lax.broadcasted_iota(jnp.int32, (c, c), 1), 1.0, 0.0).astype(BF16)
    zpad = jnp.zeros((LANES - c, LANES), F32)
    zrows = jnp.zeros((c, LANES), BF16)
    nw = nw_ref[...]
    duos = [(j, m) for j in range(nc) for m in range(GDN_HEADS // 2)]
    nd = range(len(duos))

    def block_diag(a):
        return jnp.concatenate([jnp.where(lo, a, 0.0), jnp.where(lo, 0.0, a)], axis=0).astype(BF16)

    gates, gcs, gcts = [], [], []
    for j in range(nc):
        gate = gate_ref[j * c:(j + 1) * c, :]
        g1 = gate.astype(BF16)
        r1 = gate - g1.astype(F32)
        g2 = r1.astype(BF16)
        g3 = (r1 - g2.astype(F32)).astype(BF16)
        gc = _dot(ones, g1) + _dot(ones, g2) + _dot(ones, g3)
        gates.append(gate)
        gcs.append(gc)
        gcts.append(jnp.concatenate([gc, zpad], axis=0).T)

    def rows(j):
        return slice(j * c, (j + 1) * c)

    def per_head(fn):
        return [[fn(j, 2 * m + e) for e in range(2)] for j, m in duos]

    gcol = per_head(lambda j, h: gcs[j][:, LANE_G + h:LANE_G + h + 1])
    bcol = per_head(lambda j, h: gates[j][:, LANE_BETA + h:LANE_BETA + h + 1])
    grow = [jnp.concatenate([gcts[j][LANE_G + 2 * m:LANE_G + 2 * m + 1, 0:c],
                             gcts[j][LANE_G + 2 * m + 1:LANE_G + 2 * m + 2, 0:c]], axis=1) for j, m in duos]
    decay = [jnp.exp(jnp.where(tri, jnp.where(lo, gcol[i][0], gcol[i][1]) - grow[i], NEG)) for i in nd]
    eg = [[jnp.exp(g) for g in gcol[i]] for i in nd]
    k = per_head(lambda j, h: qkvz_ref[1, h, rows(j), :])
    qd = per_head(lambda j, h: qkvz_ref[0, h, rows(j), :])
    kb = [[k[i][e] * bcol[i][e] for e in range(2)] for i in nd]
    kq = [_dot_nt(jnp.concatenate([kb[i][0], qd[i][0]], axis=0).astype(BF16),
                  jnp.concatenate([k[i][0].astype(BF16), zrows], axis=0))
          + _dot_nt(jnp.concatenate([kb[i][1], qd[i][1]], axis=0).astype(BF16),
                    jnp.concatenate([zrows, k[i][1].astype(BF16)], axis=0)) for i in nd]
    tick()
    x = [jnp.where(strict, -(kq[i][:c] * decay[i]), 0.0) for i in nd]
    intra = [jnp.where(tri, kq[i][c:] * decay[i], 0.0).astype(BF16) for i in nd]
    p = [_dot(x[i].astype(BF16), block_diag(x[i])) for i in nd]
    t_off = x
    n_steps = c.bit_length() - 2
    for step in range(n_steps):
        tick()
        bd = [block_diag(p[i]) for i in nd]
        if step < n_steps - 1:
            pt = [_dot(jnp.concatenate([p[i], t_off[i]], axis=0).astype(BF16), bd[i]) for i in nd]
            t_off = [t_off[i] + p[i] + pt[i][c:] for i in nd]
            p = [pti[:c] for pti in pt]
        else:
            t_off = [t_off[i] + p[i] + _dot(t_off[i].astype(BF16), bd[i]) for i in nd]
    tick()
    rhs = [[jnp.concatenate([qkvz_ref[2, 2 * m + e, rows(j), :] * bcol[i][e], kb[i][e] * eg[i][e]], axis=1)
            for e in range(2)] for i, (j, m) in enumerate(duos)]
    rhs2 = [jnp.concatenate(rhs[i], axis=0).astype(BF16) for i in nd]
    sol = [[rhs[i][0] + _dot(jnp.where(lo, t_off[i], 0.0).astype(BF16), rhs2[i]),
            rhs[i][1] + _dot(jnp.where(lo, 0.0, t_off[i]).astype(BF16), rhs2[i])] for i in nd]
    wq = [[jnp.concatenate([sol[i][e][:, GDN_DV:], qd[i][e] * eg[i][e]], axis=0).astype(BF16) for e in range(2)]
          for i in nd]
    glast = [[g[c - 1:c, :] for g in gcol[i]] for i in nd]
    kdt = [jnp.concatenate([k[i][e] * jnp.exp(glast[i][e] - gcol[i][e]) for e in range(2)], axis=0).T.astype(BF16)
           for i in nd]
    ikd = [jnp.concatenate([intra[i], kdt[i]], axis=0) for i in nd]
    g_tot = [[jnp.exp(g) for g in glast[i]] for i in nd]

    for j in range(nc):
        tick()
        idx = [j * (GDN_HEADS // 2) + m for m in range(GDN_HEADS // 2)]
        s = [s_ref[h] for h in range(GDN_HEADS)]
        ws = [_dot(wq[idx[h // 2]][h % 2], s[h].astype(BF16)) for h in range(GDN_HEADS)]
        v_new = [(sol[idx[h // 2]][h % 2][:, :GDN_DV] - ws[h][:c]).astype(BF16) for h in range(GDN_HEADS)]
        upd = [_dot(ikd[idx[h // 2]], jnp.concatenate([v_new[h], zrows] if h % 2 == 0 else [zrows, v_new[h]], axis=0))
               for h in range(GDN_HEADS)]
        for h in range(GDN_HEADS):
            s_ref[h] = s[h] * g_tot[idx[h // 2]][h % 2] + upd[h][c:]
            o = ws[h][c:] + upd[h][:c]
            zg = qkvz_ref[3, h, rows(j), :]
            y_ref[rows(j), h * GDN_DV:(h + 1) * GDN_DV] = _gated_norm(o, zg, nw).astype(y_ref.dtype)

    for _ in step_part:
        pass

    @pl.when(n == pl.num_programs(0) - 1)
    def _():
        sfin_ref[...] = s_ref[...]
        for h in range(GDN_HEADS):
            ys_ref[:, h * GDN_DV:(h + 1) * GDN_DV] = yacc_ref[h].astype(ys_ref.dtype)


def _gdn_step_part(n, sb, qs_ref, gs_ref, nw_ref, st_ref, sto_ref, yacc_ref):
    nw = nw_ref[...]
    lane = lax.broadcasted_iota(jnp.int32, (1, LANES), 1)
    piece_row = lax.broadcasted_iota(jnp.int32, (2 * SUBLANES, LANES), 0)
    ones = jnp.where(piece_row < 3, 1.0, 0.0).astype(BF16)

    def spread(row):
        p1 = row.astype(BF16).astype(F32)
        r1 = row - p1
        p2 = r1.astype(BF16).astype(F32)
        p3 = r1 - p2
        pieces = jnp.where(piece_row == 0, p1, jnp.where(piece_row == 1, p2, jnp.where(piece_row == 2, p3, 0.0)))
        return lax.dot_general(pieces.astype(BF16), ones, (((0,), (0,)), ((), ())), preferred_element_type=F32)

    def pick(row, l):
        return jnp.sum(jnp.where(lane == l, row, 0.0), axis=1, keepdims=True)

    pairs = [(b, h) for b in range(sb) for h in range(GDN_HEADS)]
    m = range(len(pairs))
    rows = [pl.ds(n * sb + b, 1) for b in range(sb)]
    grow = [gs_ref[rows[b], :] for b in range(sb)]
    beta = [pick(grow[b], LANE_BETA + h) for b, h in pairs]
    eg = [pick(grow[b], LANE_EG + h) for b, h in pairs]
    yield
    qb = [spread(qs_ref[0, h, rows[b], :]) for b, h in pairs]
    yield
    kb = [spread(qs_ref[1, h, rows[b], :]) for b, h in pairs]
    yield
    s = [st_ref[b, h] * eg[i] for i, (b, h) in enumerate(pairs)]
    yield
    kv = [jnp.sum(s[i] * kb[i], axis=0, keepdims=True) for i in m]
    yield
    delta = [(qs_ref[2, h, rows[b], :] - kv[i]) * beta[i] for i, (b, h) in enumerate(pairs)]
    s = [s[i] + kb[i] * delta[i] for i in m]
    yield
    o = [jnp.sum(s[i] * qb[i], axis=0, keepdims=True) for i in m]
    yield
    for i, (b, h) in enumerate(pairs):
        sto_ref[b, h] = s[i]
        yacc_ref[h, rows[b], :] = _gated_norm(o[i], qs_ref[3, h, rows[b], :], nw)


def _gdn(qkvz, gate, qkvz_s, gate_s, norm_w, state, *, nc=GDN_CHUNKS_PER_STEP):
    t, b = qkvz.shape[2], qkvz_s.shape[2]
    c = GDN_CHUNK
    steps = t // (nc * c)
    assert t % (nc * c) == 0 and c & (c - 1) == 0 and b % steps == 0
    sb = b // steps
    state_spec = pl.BlockSpec((sb, GDN_HEADS, GDN_DK, GDN_DV), lambda n: (n, 0, 0, 0))
    const2 = lambda n: (0, 0)
    return pl.pallas_call(
        functools.partial(_gdn_body, nc, sb),
        grid=(steps,),
        in_specs=[
            pl.BlockSpec((4, GDN_HEADS, nc * c, LANES), lambda n: (0, 0, n, 0)),
            pl.BlockSpec((nc * c, LANES), lambda n: (n, 0)),
            pl.BlockSpec((1, GDN_DV), const2),
            pl.BlockSpec((4, GDN_HEADS, b, LANES), lambda n: (0, 0, 0, 0)),
            pl.BlockSpec((b, LANES), const2),
            state_spec,
        ],
        out_specs=[
            pl.BlockSpec((nc * c, D_GDN_V), lambda n: (n, 0)),
            pl.BlockSpec((GDN_HEADS, GDN_DK, GDN_DV), lambda n: (0, 0, 0)),
            pl.BlockSpec((b, D_GDN_V), const2),
            state_spec,
        ],
        out_shape=[
            jax.ShapeDtypeStruct((t, D_GDN_V), BF16),
            jax.ShapeDtypeStruct((GDN_HEADS, GDN_DK, GDN_DV), F32),
            jax.ShapeDtypeStruct((b, D_GDN_V), BF16),
            jax.ShapeDtypeStruct(state.shape, F32),
        ],
        scratch_shapes=[pltpu.VMEM((GDN_HEADS, GDN_DK, GDN_DV), F32), pltpu.VMEM((GDN_HEADS, b, GDN_DV), F32)],
        compiler_params=_cparams("arbitrary"),
        name="gdn",
    )(qkvz, gate, norm_w.reshape(1, GDN_DV), qkvz_s, gate_s, state)


def _gate_param_row(p):
    return jnp.zeros((1, LANES), F32).at[0, LANE_G:LANE_G + GDN_HEADS].set(p.astype(F32))


def _ffn(h, hs, x, xs, state_t, w_up, w_conv, w_down, next_norm, *, final=False, name):
    (hid, keep_a, keep_b), (hid_s, up_a, up_b) = _ffn_up(h, hs, w_up, w_conv, state_t)
    out, out_s = _matmul([hid], [hid_s], w_down, tm=FFN_DOWN_ROW_TILE, residual=x, residual2=xs,
                         norm_w=next_norm, final=final, chunk=FFN_DOWN_STAGE_ROWS, name=name)
    return out, out_s, jnp.concatenate([keep_a, keep_b], axis=1), jnp.concatenate([up_a, up_b], axis=1)


def kernel(x_prompt, x_sample, state_l0_pool, cache_l0_k, cache_l0_v, state_l0_ffn_conv, state_l1_gdn_conv, state_l1_gdn_S, state_l1_sconv, state_l1_ffn_conv, l0_norm_mix, l0_w_in, l0_pool_w, l0_pool_scale, l0_sinks, l0_w_out, l0_norm_ffn, l0_ffn_w_up, l0_ffn_conv, l0_ffn_w_down, l1_norm_mix, l1_w_in, l1_gdn_conv, l1_gdn_A_log, l1_gdn_dt_bias, l1_gdn_norm, l1_sconv_w, l1_w_out, l1_norm_ffn, l1_ffn_w_up, l1_ffn_conv, l1_ffn_w_down, final_norm):
    bp, t, d = x_prompt.shape
    nb, ts = x_sample.shape[:2]
    wb = cache_l0_k.shape[1]
    assert bp == 1 and ts == 1 and d == D_MODEL and wb == WINDOW and t >= WINDOW
    w_in1_t = l1_w_in.T
    a_log_row = _gate_param_row(l1_gdn_A_log)
    dt_bias_row = _gate_param_row(l1_gdn_dt_bias)
    pool_t = state_l0_pool.transpose(1, 0, 2)
    kt, vt = cache_l0_k.transpose(0, 2, 3, 1), cache_l0_v.transpose(0, 2, 3, 1)
    ffn0_t = state_l0_ffn_conv.transpose(1, 0, 2)
    gconv_t = state_l1_gdn_conv.transpose(1, 0, 2)
    sconv_t = state_l1_sconv.transpose(1, 0, 2)
    ffn1_t = state_l1_ffn_conv.transpose(1, 0, 2)
    last = SUBLANES

    xp, xs = x_prompt[0], x_sample[:, 0]
    z0, z0s = _matmul([_rmsnorm(xp, l0_norm_mix, BF16)], [_rmsnorm(xs, l0_norm_mix, BF16)], l0_w_in,
                      name="in_proj0")
    mix = _mixer0_seq(z0, l0_pool_w, l0_pool_scale, l0_sinks)
    y_pool = _pool_step(z0s, pool_t, l0_pool_w, l0_pool_scale)
    att, kt_new, vt_new = _attn_step(z0s, kt, vt, l0_sinks)
    mix_s = jnp.concatenate([y_pool, att.astype(BF16)], axis=1)
    (x1, h1), (x1s, h1s) = _matmul([mix], [mix_s], l0_w_out, residual=xp, residual2=xs,
                                   norm_w=l0_norm_ffn, name="out_proj0")
    (x2, h2), (x2s, h2s), ffn0_rows, ffn0_new = _ffn(
        h1, h1s, x1, x1s, ffn0_t, l0_ffn_w_up, l0_ffn_conv, l0_ffn_w_down, l1_norm_mix, name="ffn_down0")
    (qkvz, raw), (qkvz_s, raw_s) = _qkvz(h2, h2s, w_in1_t, l1_gdn_conv, gconv_t)
    (ysc, gate, mrows), (ysc_s, gate_s, m_s) = _tail(
        h2, h2s, w_in1_t, l1_sconv_w, a_log_row, dt_bias_row, sconv_t)
    ygdn, s_fin, ygdn_s, s_new = _gdn(qkvz, gate, qkvz_s, gate_s, l1_gdn_norm, state_l1_gdn_S)
    (x3, h3), (x3s, h3s) = _matmul([ygdn, ysc], [ygdn_s, ysc_s], l1_w_out, residual=x2,
                                   residual2=x2s, norm_w=l1_norm_ffn, name="out_proj1")
    y_prompt, y_sample, ffn1_rows, ffn1_new = _ffn(
        h3, h3s, x3, x3s, ffn1_t, l1_ffn_w_up, l1_ffn_conv, l1_ffn_w_down, final_norm, final=True,
        name="ffn_down1")

    p_pool = z0[t - POOL_BUF:, :D_POOL][None]
    p_k = z0[t - wb:, O_K:O_V].reshape(1, wb, N_KV_HEADS, HEAD_DIM)
    p_v = z0[t - wb:, O_V:].reshape(1, wb, N_KV_HEADS, HEAD_DIM)
    p_ffn0 = ffn0_rows[last - (FFN_CONV - 1):][None]
    p_gconv = raw[last - (GDN_CONV - 1):, :D_GDN_CONV][None]
    p_sconv = mrows[last - (SCONV_W - 1):][None]
    p_ffn1 = ffn1_rows[last - (FFN_CONV - 1):][None]

    def push(state_t, new_row):
        return jnp.concatenate([state_t[1:], new_row[None]], axis=0).transpose(1, 0, 2)

    s_pool = push(pool_t, z0s[:, :D_POOL])
    s_k, s_v = kt_new.transpose(0, 3, 1, 2), vt_new.transpose(0, 3, 1, 2)
    s_ffn0 = push(ffn0_t, ffn0_new)
    s_gconv = push(gconv_t, raw_s[:, :D_GDN_CONV])
    s_sconv = push(sconv_t, m_s)
    s_ffn1 = push(ffn1_t, ffn1_new)
    return (y_prompt[None], y_sample[:, None], p_pool, s_pool, p_k, s_k, p_v, s_v, p_ffn0, s_ffn0,
            p_gconv, s_gconv, s_fin[None], s_new, p_sconv, s_sconv, p_ffn1, s_ffn1)
```

```python
import functools

import jax
import jax.numpy as jnp
from jax import lax
from jax.experimental import pallas as pl
from jax.experimental.pallas import tpu as pltpu

F32 = jnp.float32
BF16 = jnp.bfloat16
EPS = 1e-6
NEG = -1e30

LANES = 128
SUBLANES = 8
MXU_COLS = 256
VMEM_LIMIT_BYTES = 60 * 1024 * 1024

ROW_TILE = 512
FFN_UP_ROW_TILE = 1024
FFN_UP_COL_TILE = 512
FFN_DOWN_ROW_TILE = 256
STAGE_ROWS = 256
FFN_DOWN_STAGE_ROWS = 512
GDN_CHUNKS_PER_STEP = 4
QKVZ_GROUP_HEADS = 4
ATTN_STEP_SEQS = 8

D_MODEL = 2048
D_POOL = 512
POOL_WINDOWS = (2, 4, 8, 16)
POOL_BUF = max(POOL_WINDOWS) - 1
POOL_HIST = 16
HEAD_DIM = 64
N_Q_HEADS = 24
N_KV_HEADS = 4
Q_PER_KV = N_Q_HEADS // N_KV_HEADS
WINDOW = 128
D_ATTN = N_Q_HEADS * HEAD_DIM
D_KV = N_KV_HEADS * HEAD_DIM
D_IN0 = D_POOL + D_ATTN + 2 * D_KV
O_K = D_POOL + D_ATTN
O_V = O_K + D_KV
GDN_HEADS = 12
GDN_DK = 128
GDN_DV = 128
D_GDN_K = GDN_HEADS * GDN_DK
D_GDN_V = GDN_HEADS * GDN_DV
D_GDN_CONV = 2 * D_GDN_K + D_GDN_V
D_GDN_MAIN = D_GDN_CONV + D_GDN_V
GDN_CONV = 4
GDN_CHUNK = 64
D_SCONV = 512
SCONV_W = 3
D_TAIL = 3 * D_SCONV + LANES
D_FF = 5632
FFN_CONV = 3
LANE_BETA = 0
LANE_G = GDN_HEADS
LANE_EG = 2 * GDN_HEADS


def _cparams(*sem):
    return pltpu.CompilerParams(dimension_semantics=sem, vmem_limit_bytes=VMEM_LIMIT_BYTES)


def _dot(a, b):
    return jnp.dot(a, b, preferred_element_type=F32)


def _dot_nt(a, b):
    return lax.dot_general(a, b, (((1,), (1,)), ((), ())), preferred_element_type=F32)


def _silu(x):
    return x * jax.nn.sigmoid(x)


def _cast_rows(w_ref, wbf_ref, chunk=STAGE_ROWS):
    k = w_ref.shape[0]
    chunk = min(chunk, k)
    assert k % chunk == 0

    def body(c, carry):
        r = pl.multiple_of(c * chunk, chunk)
        wbf_ref[pl.ds(r, chunk), :] = w_ref[pl.ds(r, chunk), :].astype(BF16)
        return carry

    lax.fori_loop(0, k // chunk, body, 0)


def _cast_transposed(wt_ref, wbf_ref):
    n = wt_ref.shape[0]
    step = MXU_COLS if n % MXU_COLS == 0 else LANES
    for c in range(0, n, step):
        wbf_ref[:, c:c + step] = wt_ref[c:c + step, :].T.astype(BF16)


def _store_slabs(slab_ref, u):
    m = u.shape[0]
    for c in range(slab_ref.shape[0]):
        slab_ref[c, SUBLANES:SUBLANES + m, :] = u[:, c * LANES:(c + 1) * LANES]


def _tap(slab_ref, c, m, delay):
    return slab_ref[c, pl.ds(SUBLANES - delay, m), :]


def _roll_history(slab_ref, m):
    last = [slab_ref[c, m:m + SUBLANES, :] for c in range(slab_ref.shape[0])]
    for c, rows in enumerate(last):
        slab_ref[c, 0:SUBLANES, :] = rows
    return jnp.concatenate(last, axis=1)


def _rmsnorm_body(x_ref, w_ref, o_ref):
    x = x_ref[...]
    ms = jnp.mean(x * x, axis=-1, keepdims=True)
    o_ref[...] = (x * lax.rsqrt(ms + EPS) * w_ref[...]).astype(o_ref.dtype)


def _rmsnorm(x, w, out_dtype):
    m, d = x.shape
    tm = min(m, ROW_TILE)
    assert m % tm == 0
    return pl.pallas_call(
        _rmsnorm_body,
        grid=(m // tm,),
        in_specs=[pl.BlockSpec((tm, d), lambda i: (i, 0)), pl.BlockSpec((1, d), lambda i: (0, 0))],
        out_specs=pl.BlockSpec((tm, d), lambda i: (i, 0)),
        out_shape=jax.ShapeDtypeStruct((m, d), out_dtype),
        compiler_params=_cparams("parallel"),
        name="rmsnorm",
    )(x, w.reshape(1, d))


def _mm_body(n_pieces, n_stage, mode, *refs):
    n_out = {"plain": 1, "residual": 2, "final": 1}[mode]
    n_in = n_pieces + (0 if mode == "plain" else 1)
    set_a, set_b = refs[:n_in], refs[n_in:2 * n_in]
    w_ref = refs[2 * n_in]
    pos = 2 * n_in + 1
    nw_ref = None
    if mode != "plain":
        nw_ref = refs[pos]
        pos += 1
    out_a, out_b = refs[pos:pos + n_out], refs[pos + n_out:pos + 2 * n_out]
    wbf_ref = refs[pos + 2 * n_out]
    s = pl.program_id(0)
    last = pl.num_programs(0) - 1
    ck = w_ref.shape[0]

    @pl.when(s < n_stage)
    def _():
        _cast_rows(w_ref, wbf_ref.at[pl.ds(pl.multiple_of(s * ck, ck), ck), :])

    def rows(ins, outs):
        acc, row = None, 0
        for x_ref in ins[:n_pieces]:
            kp = x_ref.shape[1]
            d = _dot(x_ref[...], wbf_ref[row:row + kp, :])
            acc = d if acc is None else acc + d
            row += kp
        if mode == "plain":
            outs[0][...] = acc
            return
        xn = acc + ins[n_pieces][...]
        hn = xn * lax.rsqrt(jnp.mean(xn * xn, axis=-1, keepdims=True) + EPS) * nw_ref[...]
        if mode == "residual":
            outs[0][...] = xn
        outs[-1][...] = hn.astype(outs[-1].dtype)

    @pl.when((s >= n_stage) & (s < last))
    def _():
        rows(set_a, out_a)

    @pl.when(s == last)
    def _():
        rows(set_b, out_b)


def _matmul(xs, xs2, w, *, tm=ROW_TILE, residual=None, residual2=None, norm_w=None, final=False, chunk=STAGE_ROWS,
            name="matmul"):
    m, m2 = xs[0].shape[0], xs2[0].shape[0]
    k, n = w.shape
    tm = min(tm, m)
    assert m % tm == 0 and k % chunk == 0 and sum(x.shape[1] for x in xs) == k
    n_stage, nm = k // chunk, m // tm
    mode = "plain" if residual is None else ("final" if final else "residual")

    def row_tile(s):
        return jnp.clip(s - n_stage, 0, nm - 1)

    tile_a = pl.BlockSpec((tm, n), lambda s: (row_tile(s), 0))
    tile_b = pl.BlockSpec((m2, n), lambda s: (0, 0))
    in_a = [pl.BlockSpec((tm, x.shape[1]), lambda s: (row_tile(s), 0)) for x in xs]
    in_b = [pl.BlockSpec((m2, x.shape[1]), lambda s: (0, 0)) for x in xs2]
    args_a, args_b = list(xs), list(xs2)
    if mode != "plain":
        in_a.append(tile_a)
        in_b.append(tile_b)
        args_a.append(residual)
        args_b.append(residual2)
    in_specs = in_a + in_b + [pl.BlockSpec((chunk, n), lambda s: (jnp.minimum(s, n_stage - 1), 0))]
    args = args_a + args_b + [w]
    if mode != "plain":
        in_specs.append(pl.BlockSpec((1, n), lambda s: (0, 0)))
        args.append(norm_w.reshape(1, n))
    dtypes = [F32, BF16] if mode == "residual" else [F32]
    out_specs = [tile_a] * len(dtypes) + [tile_b] * len(dtypes)
    out_shape = ([jax.ShapeDtypeStruct((m, n), dt) for dt in dtypes]
                 + [jax.ShapeDtypeStruct((m2, n), dt) for dt in dtypes])
    outs = pl.pallas_call(
        functools.partial(_mm_body, len(xs), n_stage, mode),
        grid=(n_stage + nm + 1,),
        in_specs=in_specs,
        out_specs=out_specs,
        out_shape=out_shape,
        scratch_shapes=[pltpu.VMEM((k, n), BF16)],
        compiler_params=_cparams("arbitrary"),
        name=name,
    )(*args)
    half = len(dtypes)
    first, second = outs[:half], outs[half:]
    return (first[0], second[0]) if half == 1 else (tuple(first), tuple(second))


def _ffn_up_body(x_ref, xs_ref, wa_ref, wb_ref, cwa_ref, cwb_ref, a2_ref, a1_ref, b2_ref, b1_ref,
                 hid_ref, upa_ref, upb_ref, hids_ref, upsa_ref, upsb_ref, wa_bf, wb_bf, ca_ref, cb_ref):
    tm = x_ref.shape[0]
    i = pl.program_id(1)

    @pl.when(i == 0)
    def _():
        _cast_rows(wa_ref, wa_bf)
        _cast_rows(wb_ref, wb_bf)
        ca_ref[:, 0:SUBLANES, :] = jnp.zeros((ca_ref.shape[0], SUBLANES, LANES), F32)
        cb_ref[:, 0:SUBLANES, :] = jnp.zeros((cb_ref.shape[0], SUBLANES, LANES), F32)

    def gate(ua, ub, taps_a, taps_b):
        c_a = cwa_ref[0:1, :] * taps_a[0] + cwa_ref[1:2, :] * taps_a[1] + cwa_ref[2:3, :] * ua
        c_b = cwb_ref[0:1, :] * taps_b[0] + cwb_ref[1:2, :] * taps_b[1] + cwb_ref[2:3, :] * ub
        return (_silu(c_a) * c_b).astype(BF16)

    @pl.when(i > 0)
    def _():
        x = x_ref[...]
        _store_slabs(ca_ref, _dot(x, wa_bf[...]))
        _store_slabs(cb_ref, _dot(x, wb_bf[...]))
        for c in range(ca_ref.shape[0]):
            cols = slice(c * LANES, (c + 1) * LANES)
            c_a = (cwa_ref[0:1, cols] * _tap(ca_ref, c, tm, 2) + cwa_ref[1:2, cols] * _tap(ca_ref, c, tm, 1)
                   + cwa_ref[2:3, cols] * _tap(ca_ref, c, tm, 0))
            c_b = (cwb_ref[0:1, cols] * _tap(cb_ref, c, tm, 2) + cwb_ref[1:2, cols] * _tap(cb_ref, c, tm, 1)
                   + cwb_ref[2:3, cols] * _tap(cb_ref, c, tm, 0))
            hid_ref[:, cols] = (_silu(c_a) * c_b).astype(BF16)
        upa_ref[...] = _roll_history(ca_ref, tm)
        upb_ref[...] = _roll_history(cb_ref, tm)

    @pl.when(i == 0)
    def _():
        x = xs_ref[...]
        ua = _dot(x, wa_bf[...])
        ub = _dot(x, wb_bf[...])
        hids_ref[...] = gate(ua, ub, (a2_ref[...], a1_ref[...]), (b2_ref[...], b1_ref[...]))
        upsa_ref[...] = ua
        upsb_ref[...] = ub


def _ffn_up(h, hs, w_up, conv_w, state_t, *, tn=FFN_UP_COL_TILE, tm=FFN_UP_ROW_TILE):
    m, k = h.shape
    ms = hs.shape[0]
    tm = min(tm, m)
    nb, nm = D_FF // tn, m // tm
    assert m % tm == 0 and D_FF % tn == 0
    tile = lambda j, i: (jnp.maximum(i - 1, 0), j)
    in_specs = [
        pl.BlockSpec((tm, k), lambda j, i: (jnp.maximum(i - 1, 0), 0)),
        pl.BlockSpec((ms, k), lambda j, i: (0, 0)),
        pl.BlockSpec((k, tn), lambda j, i: (0, j)),
        pl.BlockSpec((k, tn), lambda j, i: (0, j + nb)),
        pl.BlockSpec((FFN_CONV, tn), lambda j, i: (0, j)),
        pl.BlockSpec((FFN_CONV, tn), lambda j, i: (0, j + nb)),
        pl.BlockSpec((None, ms, tn), lambda j, i: (0, 0, j)),
        pl.BlockSpec((None, ms, tn), lambda j, i: (1, 0, j)),
        pl.BlockSpec((None, ms, tn), lambda j, i: (0, 0, j + nb)),
        pl.BlockSpec((None, ms, tn), lambda j, i: (1, 0, j + nb)),
    ]
    keep_spec = pl.BlockSpec((SUBLANES, tn), lambda j, i: (0, j))
    step_spec = pl.BlockSpec((ms, tn), lambda j, i: (0, j))
    outs = pl.pallas_call(
        _ffn_up_body,
        grid=(nb, nm + 1),
        in_specs=in_specs,
        out_specs=[pl.BlockSpec((tm, tn), tile), keep_spec, keep_spec, step_spec, step_spec, step_spec],
        out_shape=[
            jax.ShapeDtypeStruct((m, D_FF), BF16),
            jax.ShapeDtypeStruct((SUBLANES, D_FF), F32),
            jax.ShapeDtypeStruct((SUBLANES, D_FF), F32),
            jax.ShapeDtypeStruct((ms, D_FF), BF16),
            jax.ShapeDtypeStruct((ms, D_FF), F32),
            jax.ShapeDtypeStruct((ms, D_FF), F32),
        ],
        scratch_shapes=[pltpu.VMEM((k, tn), BF16), pltpu.VMEM((k, tn), BF16),
                        pltpu.VMEM((tn // LANES, SUBLANES + tm, LANES), F32),
                        pltpu.VMEM((tn // LANES, SUBLANES + tm, LANES), F32)],
        compiler_params=_cparams("arbitrary", "arbitrary"),
        name="ffn_up",
    )(h, hs, w_up, w_up, conv_w, conv_w, state_t, state_t, state_t, state_t)
    return outs[:3], outs[3:]


def _half_lane_pair(x, head_in_high, lo_mask):
    if head_in_high:
        hi = jnp.where(lo_mask, 0.0, x)
        return pltpu.roll(hi, HEAD_DIM, 1), hi
    lo = jnp.where(lo_mask, x, 0.0)
    return lo, pltpu.roll(lo, HEAD_DIM, 1)


def _mixer0_seq_body(sink_ref, z_ref, kvp_ref, up_ref, pw_ref, ps_ref, o_ref):
    n = pl.program_id(0)
    first = n == 0
    w = WINDOW
    lo_mask = lax.broadcasted_iota(jnp.int32, (1, LANES), 1) < HEAD_DIM

    hist = jnp.where(first, 0.0, up_ref[...])
    pos = n * w + lax.broadcasted_iota(jnp.int32, (w, 1), 0)
    for g, win in enumerate(POOL_WINDOWS):
        sl = slice(g * LANES, (g + 1) * LANES)
        e = jnp.concatenate([hist[:, sl], z_ref[:, sl]], axis=0)
        s, sh = e, 1
        while sh < win:
            s = s + pltpu.roll(s, sh, 0)
            sh *= 2
        cnt = jnp.minimum(pos + 1, win).astype(F32)
        diff = s[POOL_HIST:] / cnt - e[POOL_HIST:]
        y = _dot(diff.astype(BF16), pw_ref[g].astype(BF16)) * ps_ref[:, sl]
        o_ref[:, sl] = y.astype(o_ref.dtype)

    row = lax.broadcasted_iota(jnp.int32, (w, 2 * w), 0)
    col = lax.broadcasted_iota(jnp.int32, (w, 2 * w), 1)
    valid = (col >= row) & (col <= row + w) & (jnp.logical_not(first) | (col >= w))
    tiles_per_kv = Q_PER_KV * HEAD_DIM // LANES
    for c in range(D_KV // LANES):
        k2 = jnp.concatenate([kvp_ref[:, c * LANES:(c + 1) * LANES],
                              z_ref[:, O_K + c * LANES:O_K + (c + 1) * LANES]], axis=0)
        v2 = jnp.concatenate([kvp_ref[:, D_KV + c * LANES:D_KV + (c + 1) * LANES],
                              z_ref[:, O_V + c * LANES:O_V + (c + 1) * LANES]], axis=0)
        for p in range(2):
            hk = 2 * c + p
            k_lo, k_hi = (t.astype(BF16) for t in _half_lane_pair(k2, p == 1, lo_mask))
            v_lo, v_hi = (t.astype(BF16) for t in _half_lane_pair(v2, p == 1, lo_mask))
            q0 = D_POOL + hk * Q_PER_KV * HEAD_DIM
            qst = jnp.concatenate(
                [z_ref[:, q0 + a * LANES:q0 + (a + 1) * LANES] for a in range(tiles_per_kv)], axis=0)
            qst = (qst * HEAD_DIM ** -0.5).astype(BF16)
            s_lo = _dot_nt(qst, k_lo)
            s_hi = _dot_nt(qst, k_hi)
            for a in range(tiles_per_kv):
                probs, inv = [], []
                for par, s_all in ((0, s_lo), (1, s_hi)):
                    sk = sink_ref[hk * Q_PER_KV + 2 * a + par]
                    s = jnp.where(valid, s_all[a * w:(a + 1) * w], NEG)
                    mx = jnp.maximum(jnp.max(s, axis=-1, keepdims=True), sk)
                    pr = jnp.exp(s - mx)
                    den = jnp.sum(pr, axis=-1, keepdims=True) + jnp.exp(sk - mx)
                    probs.append(pr.astype(BF16))
                    inv.append(1.0 / den)
                o = _dot(probs[0], v_lo) + _dot(probs[1], v_hi)
                o = o * jnp.where(lo_mask, inv[0], inv[1])
                o_ref[:, q0 + a * LANES:q0 + (a + 1) * LANES] = o.astype(o_ref.dtype)


def _mixer0_seq(z0, pool_w, pool_scale, sinks):
    t = z0.shape[0]
    w = WINDOW
    assert t % w == 0
    hist_blocks = w // POOL_HIST
    return pl.pallas_call(
        _mixer0_seq_body,
        grid=(t // w,),
        in_specs=[
            pl.BlockSpec(memory_space=pltpu.SMEM),
            pl.BlockSpec((w, D_IN0), lambda n: (n, 0)),
            pl.BlockSpec((w, 2 * D_KV), lambda n: (jnp.maximum(n - 1, 0), O_K // (2 * D_KV))),
            pl.BlockSpec((POOL_HIST, D_POOL), lambda n: (jnp.maximum(n * hist_blocks - 1, 0), 0)),
            pl.BlockSpec((len(POOL_WINDOWS), LANES, LANES), lambda n: (0, 0, 0)),
            pl.BlockSpec((1, D_POOL), lambda n: (0, 0)),
        ],
        out_specs=pl.BlockSpec((w, D_POOL + D_ATTN), lambda n: (n, 0)),
        out_shape=jax.ShapeDtypeStruct((t, D_POOL + D_ATTN), BF16),
        compiler_params=_cparams("parallel"),
        name="mixer0_seq",
    )(sinks, z0, z0, z0, pool_w, pool_scale.reshape(1, D_POOL))


def _pool_step_body(*refs):
    hist_refs = refs[:POOL_BUF]
    z_ref, pw_ref, ps_ref, o_ref = refs[POOL_BUF:]
    for g, win in enumerate(POOL_WINDOWS):
        sl = slice(g * LANES, (g + 1) * LANES)
        u = z_ref[:, sl]
        s = u
        for r in range(POOL_BUF - (win - 1), POOL_BUF):
            s = s + hist_refs[r][:, sl]
        diff = s / float(win) - u
        y = _dot(diff.astype(BF16), pw_ref[g].astype(BF16)) * ps_ref[:, sl]
        o_ref[:, sl] = y.astype(o_ref.dtype)


def _pool_step(z0, hist_t, pool_w, pool_scale):
    b = z0.shape[0]
    in_specs = [pl.BlockSpec((None, b, D_POOL), functools.partial(lambda i, r: (r, 0, 0), r=r))
                for r in range(POOL_BUF)]
    in_specs += [
        pl.BlockSpec((b, D_POOL), lambda i: (0, 0)),
        pl.BlockSpec((len(POOL_WINDOWS), LANES, LANES), lambda i: (0, 0, 0)),
        pl.BlockSpec((1, D_POOL), lambda i: (0, 0)),
    ]
    return pl.pallas_call(
        _pool_step_body,
        grid=(1,),
        in_specs=in_specs,
        out_specs=pl.BlockSpec((b, D_POOL), lambda i: (0, 0)),
        out_shape=jax.ShapeDtypeStruct((b, D_POOL), BF16),
        compiler_params=_cparams("arbitrary"),
        name="pool_step",
    )(*([hist_t] * POOL_BUF), z0, pool_w, pool_scale.reshape(1, D_POOL))


def _attn_step_body(q_ref, kn_ref, vn_ref, kt_ref, vt_ref, sink_ref, o_ref, kto_ref, vto_ref):
    bs = q_ref.shape[0]
    newest = lax.broadcasted_iota(jnp.int32, (HEAD_DIM, WINDOW), 1) == WINDOW - 1
    pairs = [(b, hk) for b in range(bs) for hk in range(N_KV_HEADS)]
    n = range(len(pairs))
    kt = [kt_ref[b, hk] for b, hk in pairs]
    vt = [vt_ref[b, hk] for b, hk in pairs]
    q = [q_ref[b, hk] * HEAD_DIM ** -0.5 for b, hk in pairs]
    sk = [sink_ref[hk * Q_PER_KV:(hk + 1) * Q_PER_KV, :] for b, hk in pairs]
    s = [_dot(q[i].astype(BF16), kt[i].astype(BF16)) for i in n]
    s_new = [jnp.sum(q[i] * kn_ref[b, hk:hk + 1, :], axis=-1, keepdims=True) for i, (b, hk) in enumerate(pairs)]
    mx = [jnp.maximum(jnp.maximum(jnp.max(s[i], axis=-1, keepdims=True), s_new[i]), sk[i]) for i in n]
    pr = [jnp.exp(s[i] - mx[i]) for i in n]
    pn = [jnp.exp(s_new[i] - mx[i]) for i in n]
    den = [jnp.sum(pr[i], axis=-1, keepdims=True) + pn[i] + jnp.exp(sk[i] - mx[i]) for i in n]
    o = [_dot_nt(pr[i].astype(BF16), vt[i].astype(BF16)) for i in n]
    piece_row = lax.broadcasted_iota(jnp.int32, (2 * SUBLANES, HEAD_DIM), 0)
    ones = jnp.where(lax.broadcasted_iota(jnp.int32, (2 * SUBLANES, WINDOW), 0) < 3, 1.0, 0.0).astype(BF16)

    def spread(row):
        p1 = row.astype(BF16).astype(F32)
        r1 = row - p1
        p2 = r1.astype(BF16).astype(F32)
        p3 = r1 - p2
        pieces = jnp.where(piece_row == 0, p1, jnp.where(piece_row == 1, p2, jnp.where(piece_row == 2, p3, 0.0)))
        return lax.dot_general(pieces.astype(BF16), ones, (((0,), (0,)), ((), ())), preferred_element_type=F32)

    for i, (b, hk) in enumerate(pairs):
        o_ref[b, hk] = (o[i] + pn[i] * vn_ref[b, hk:hk + 1, :]) / den[i]
        kto_ref[b, hk] = jnp.where(newest, spread(kn_ref[b, hk:hk + 1, :]), pltpu.roll(kt[i], WINDOW - 1, 1))
        vto_ref[b, hk] = jnp.where(newest, spread(vn_ref[b, hk:hk + 1, :]), pltpu.roll(vt[i], WINDOW - 1, 1))


def _attn_step(z0, kt, vt, sinks, *, bs=ATTN_STEP_SEQS):
    b = z0.shape[0]
    assert b % bs == 0 and kt.shape[3] == WINDOW == LANES
    nblk = b // bs
    q4 = z0[:, D_POOL:O_K].reshape(b, N_KV_HEADS, Q_PER_KV, HEAD_DIM)
    kn = z0[:, O_K:O_V].reshape(b, N_KV_HEADS, HEAD_DIM)
    vn = z0[:, O_V:].reshape(b, N_KV_HEADS, HEAD_DIM)

    cache_spec = pl.BlockSpec((bs, N_KV_HEADS, HEAD_DIM, WINDOW), lambda i: (i, 0, 0, 0))
    new_spec = pl.BlockSpec((bs, N_KV_HEADS, HEAD_DIM), lambda i: (i, 0, 0))
    q_spec = pl.BlockSpec((bs, N_KV_HEADS, Q_PER_KV, HEAD_DIM), lambda i: (i, 0, 0, 0))
    att, kto, vto = pl.pallas_call(
        _attn_step_body,
        grid=(nblk,),
        in_specs=[q_spec, new_spec, new_spec, cache_spec, cache_spec,
                  pl.BlockSpec((N_Q_HEADS, 1), lambda i: (0, 0))],
        out_specs=[q_spec, cache_spec, cache_spec],
        out_shape=[jax.ShapeDtypeStruct(q4.shape, F32), jax.ShapeDtypeStruct(kt.shape, F32),
                   jax.ShapeDtypeStruct(vt.shape, F32)],
        compiler_params=_cparams("parallel"),
        name="attn_step",
    )(q4, kn, vn, kt, vt, sinks.reshape(N_Q_HEADS, 1))
    return att.reshape(b, D_ATTN), kto, vto


def _qkvz_body(x_ref, xs_ref, w_ref, cw_ref, p3_ref, p2_ref, p1_ref,
               o_ref, raw_ref, os_ref, raws_ref, w_bf, carry_ref):
    tm = x_ref.shape[0]
    j = pl.program_id(0)
    i = pl.program_id(1)

    @pl.when(i == 0)
    def _():
        _cast_transposed(w_ref, w_bf)
        carry_ref[:, 0:SUBLANES, :] = jnp.zeros((carry_ref.shape[0], SUBLANES, LANES), F32)

    def heads(a):
        return [a[:, h * LANES:(h + 1) * LANES] for h in range(GDN_HEADS)]

    def l2norm(a):
        return a * lax.rsqrt(jnp.sum(a * a, axis=-1, keepdims=True) + EPS)

    def finish(u, taps, out_ref):
        def conv_act():
            t3, t2, t1 = taps()
            c = cw_ref[3:4, :] * u
            c = cw_ref[0:1, :] * t3 + cw_ref[1:2, :] * t2 + cw_ref[2:3, :] * t1 + c
            return _silu(c)

        @pl.when(j == 0)
        def _():
            for h, a in enumerate(heads(conv_act())):
                out_ref[h] = l2norm(a) * GDN_DK ** -0.5

        @pl.when(j == 1)
        def _():
            for h, a in enumerate(heads(conv_act())):
                out_ref[h] = l2norm(a)

        @pl.when(j == 2)
        def _():
            for h, a in enumerate(heads(conv_act())):
                out_ref[h] = a

        @pl.when(j == 3)
        def _():
            for h, a in enumerate(heads(u)):
                out_ref[h] = a

    @pl.when((i > 0) & (j < 3))
    def _():
        x = x_ref[...]
        norm_scale = jnp.where(j == 0, GDN_DK ** -0.5, 1.0)
        gw = QKVZ_GROUP_HEADS * LANES
        for g0 in range(0, GDN_HEADS, QKVZ_GROUP_HEADS):
            u = _dot(x, w_bf[:, g0 * LANES:g0 * LANES + gw])
            for hh in range(QKVZ_GROUP_HEADS):
                h = g0 + hh
                sl = slice(h * LANES, (h + 1) * LANES)
                carry_ref[h, SUBLANES:SUBLANES + tm, :] = u[:, hh * LANES:(hh + 1) * LANES]
                c = cw_ref[3:4, sl] * _tap(carry_ref, h, tm, 0)
                c = (cw_ref[0:1, sl] * _tap(carry_ref, h, tm, 3) + cw_ref[1:2, sl] * _tap(carry_ref, h, tm, 2)
                     + cw_ref[2:3, sl] * _tap(carry_ref, h, tm, 1) + c)
                a = _silu(c)
                inv = jnp.where(j < 2, lax.rsqrt(jnp.sum(a * a, axis=-1, keepdims=True) + EPS) * norm_scale, 1.0)
                o_ref[h] = a * inv
        raw_ref[...] = _roll_history(carry_ref, tm)

    @pl.when((i > 0) & (j == 3))
    def _():
        u = _dot(x_ref[...], w_bf[...])
        for h, a in enumerate(heads(u)):
            o_ref[h] = a
        raw_ref[...] = u[tm - SUBLANES:]

    @pl.when(i == 0)
    def _():
        u = _dot(xs_ref[...], w_bf[...])
        raws_ref[...] = u
        finish(u, lambda: (p3_ref[...], p2_ref[...], p1_ref[...]), os_ref)


def _qkvz(h, hs, w_in_t, conv_w, state_t, *, tm=ROW_TILE):
    m, k = h.shape
    ms = hs.shape[0]
    tm = min(tm, m)
    tn = D_GDN_K
    assert D_GDN_K == D_GDN_V and GDN_DK == LANES and m % tm == 0
    nj, nm = D_GDN_MAIN // tn, m // tm
    in_specs = [
        pl.BlockSpec((tm, k), lambda j, i: (jnp.maximum(i - 1, 0), 0)),
        pl.BlockSpec((ms, k), lambda j, i: (0, 0)),
        pl.BlockSpec((tn, k), lambda j, i: (j, 0)),
        pl.BlockSpec((GDN_CONV, tn), lambda j, i: (0, jnp.minimum(j, 2))),
    ]
    for r in range(GDN_CONV - 1):
        in_specs.append(pl.BlockSpec((None, ms, tn), functools.partial(lambda j, i, r: (r, 0, jnp.minimum(j, 2)), r=r)))
    outs = pl.pallas_call(
        _qkvz_body,
        grid=(nj, nm + 1),
        in_specs=in_specs,
        out_specs=[
            pl.BlockSpec((None, GDN_HEADS, tm, LANES), lambda j, i: (j, 0, jnp.maximum(i - 1, 0), 0)),
            pl.BlockSpec((SUBLANES, tn), lambda j, i: (0, j)),
            pl.BlockSpec((None, GDN_HEADS, ms, LANES), lambda j, i: (j, 0, 0, 0)),
            pl.BlockSpec((ms, tn), lambda j, i: (0, j)),
        ],
        out_shape=[
            jax.ShapeDtypeStruct((nj, GDN_HEADS, m, LANES), F32),
            jax.ShapeDtypeStruct((SUBLANES, D_GDN_MAIN), F32),
            jax.ShapeDtypeStruct((nj, GDN_HEADS, ms, LANES), F32),
            jax.ShapeDtypeStruct((ms, D_GDN_MAIN), F32),
        ],
        scratch_shapes=[pltpu.VMEM((k, tn), BF16), pltpu.VMEM((tn // LANES, SUBLANES + tm, LANES), F32)],
        compiler_params=_cparams("arbitrary", "arbitrary"),
        name="qkvz",
    )(h, hs, w_in_t, conv_w, state_t, state_t, state_t)
    return outs[:2], outs[2:]


def _tail_body(x_ref, xs_ref, wa_ref, wb_ref, cw_ref, alog_ref, dtb_ref, p2_ref, p1_ref,
               ysc_ref, gate_ref, m_ref, yscs_ref, gates_ref, ms_ref, w_bf, carry_ref):
    tm = x_ref.shape[0]
    i = pl.program_id(0)

    @pl.when(i == 0)
    def _():
        n_raw = wb_ref.shape[0]
        step = MXU_COLS
        for c in range(0, 3 * D_SCONV, step):
            r0 = n_raw + c
            if r0 + step <= wa_ref.shape[0]:
                blk = wa_ref[r0:r0 + step, :]
            else:
                blk = jnp.concatenate([wa_ref[r0:, :], wb_ref[...]], axis=0)
            w_bf[:, c:c + step] = blk.T.astype(BF16)
        raw = jnp.concatenate([wa_ref[0:n_raw, :], jnp.zeros((LANES - n_raw, wa_ref.shape[1]), F32)], axis=0)
        w_bf[:, 3 * D_SCONV:] = raw.T.astype(BF16)
        carry_ref[:, 0:SUBLANES, :] = jnp.zeros((carry_ref.shape[0], SUBLANES, LANES), F32)

    def finish(z, mm, taps, y_ref, g_ref):
        conv = cw_ref[0:1, :] * taps[0] + cw_ref[1:2, :] * taps[1] + cw_ref[2:3, :] * mm
        y_ref[...] = (z[:, 0:D_SCONV] * conv).astype(y_ref.dtype)
        raw = z[:, 3 * D_SCONV:]
        lane = lax.broadcasted_iota(jnp.int32, raw.shape, 1)
        beta = jax.nn.sigmoid(raw)
        sp = raw + dtb_ref[...]
        softplus = jnp.maximum(sp, 0.0) + jnp.log1p(jnp.exp(-jnp.abs(sp)))
        g = -jnp.exp(alog_ref[...]) * softplus
        eg = pltpu.roll(jnp.exp(g), LANE_EG - LANE_G, 1)
        g_ref[...] = jnp.where(lane < LANE_G, beta, jnp.where(lane < LANE_EG, g, eg))

    @pl.when(i > 0)
    def _():
        z = _dot(x_ref[...], w_bf[...])
        mm = z[:, D_SCONV:2 * D_SCONV] * z[:, 2 * D_SCONV:3 * D_SCONV]
        _store_slabs(carry_ref, mm)
        taps = tuple(jnp.concatenate([_tap(carry_ref, c, tm, k) for c in range(carry_ref.shape[0])], axis=1)
                     for k in (2, 1))
        finish(z, mm, taps, ysc_ref, gate_ref)
        m_ref[...] = _roll_history(carry_ref, tm)

    @pl.when(i == 0)
    def _():
        z = _dot(xs_ref[...], w_bf[...])
        mm = z[:, D_SCONV:2 * D_SCONV] * z[:, 2 * D_SCONV:3 * D_SCONV]
        ms_ref[...] = mm
        finish(z, mm, (p2_ref[...], p1_ref[...]), yscs_ref, gates_ref)


def _tail(h, hs, w_in_t, conv_w, a_log_row, dt_bias_row, state_t, *, tm=ROW_TILE):
    m, k = h.shape
    ms = hs.shape[0]
    tm = min(tm, m)
    n_raw, n_conv = 2 * GDN_HEADS, 3 * D_SCONV
    assert m % tm == 0 and w_in_t.shape[0] == D_GDN_MAIN + n_raw + n_conv
    assert D_GDN_MAIN % n_conv == 0 and (D_GDN_MAIN + n_conv) % n_raw == 0 and n_raw % SUBLANES == 0
    nm = m // tm
    tile = lambda i: (jnp.maximum(i - 1, 0), 0)
    const = lambda i: (0, 0)
    in_specs = [
        pl.BlockSpec((tm, k), tile),
        pl.BlockSpec((ms, k), const),
        pl.BlockSpec((n_conv, k), lambda i: (D_GDN_MAIN // n_conv, 0)),
        pl.BlockSpec((n_raw, k), lambda i: ((D_GDN_MAIN + n_conv) // n_raw, 0)),
        pl.BlockSpec((SCONV_W, D_SCONV), const),
        pl.BlockSpec((1, LANES), const),
        pl.BlockSpec((1, LANES), const),
        pl.BlockSpec((None, ms, D_SCONV), lambda i: (0, 0, 0)),
        pl.BlockSpec((None, ms, D_SCONV), lambda i: (1, 0, 0)),
    ]
    outs = pl.pallas_call(
        _tail_body,
        grid=(nm + 1,),
        in_specs=in_specs,
        out_specs=[
            pl.BlockSpec((tm, D_SCONV), tile),
            pl.BlockSpec((tm, LANES), tile),
            pl.BlockSpec((SUBLANES, D_SCONV), const),
            pl.BlockSpec((ms, D_SCONV), const),
            pl.BlockSpec((ms, LANES), const),
            pl.BlockSpec((ms, D_SCONV), const),
        ],
        out_shape=[
            jax.ShapeDtypeStruct((m, D_SCONV), BF16),
            jax.ShapeDtypeStruct((m, LANES), F32),
            jax.ShapeDtypeStruct((SUBLANES, D_SCONV), F32),
            jax.ShapeDtypeStruct((ms, D_SCONV), BF16),
            jax.ShapeDtypeStruct((ms, LANES), F32),
            jax.ShapeDtypeStruct((ms, D_SCONV), F32),
        ],
        scratch_shapes=[pltpu.VMEM((k, D_TAIL), BF16),
                        pltpu.VMEM((D_SCONV // LANES, SUBLANES + tm, LANES), F32)],
        compiler_params=_cparams("arbitrary"),
        name="tail",
    )(h, hs, w_in_t, w_in_t, conv_w, a_log_row, dt_bias_row, state_t, state_t)
    return outs[:3], outs[3:]


def _gated_norm(o, zg, nw):
    y = o * lax.rsqrt(jnp.mean(o * o, axis=-1, keepdims=True) + EPS) * nw
    return y * _silu(zg)


def _gdn_body(nc, sb, qkvz_ref, gate_ref, nw_ref, qs_ref, gs_ref, st_ref,
              y_ref, sfin_ref, ys_ref, sto_ref, s_ref, yacc_ref):
    c = GDN_CHUNK
    n = pl.program_id(0)

    @pl.when(n == 0)
    def _():
        s_ref[...] = jnp.zeros_like(s_ref)

    assert 2 * c == LANES and GDN_HEADS % 2 == 0
    r = lax.broadcasted_iota(jnp.int32, (c, LANES), 0)
    lane = lax.broadcasted_iota(jnp.int32, (c, LANES), 1)
    lo = lane < c
    col = jnp.where(lo, lane, lane - c)
    tri = r >= col
    strict = r > col
    rr = lax.broadcasted_iota(jnp.int32, (c, c), 0)
    ones = jnp.where(rr >= lax.broadcasted_iota(jnp.int32, (c, c), 1), 1.0, 0.0).astype(BF16)
    zpad = jnp.zeros((LANES - c, LANES), F32)
    zrows = jnp.zeros((c, LANES), BF16)
    nw = nw_ref[...]
    duos = [(j, m) for j in range(nc) for m in range(GDN_HEADS // 2)]
    nd = range(len(duos))

    def block_diag(a):
        return jnp.concatenate([jnp.where(lo, a, 0.0), jnp.where(lo, 0.0, a)], axis=0).astype(BF16)

    gates, gcs, gcts = [], [], []
    for j in range(nc):
        gate = gate_ref[j * c:(j + 1) * c, :]
        g1 = gate.astype(BF16)
        r1 = gate - g1.astype(F32)
        g2 = r1.astype(BF16)
        g3 = (r1 - g2.astype(F32)).astype(BF16)
        gc = _dot(ones, g1) + _dot(ones, g2) + _dot(ones, g3)
        gates.append(gate)
        gcs.append(gc)
        gcts.append(jnp.concatenate([gc, zpad], axis=0).T)

    def rows(j):
        return slice(j * c, (j + 1) * c)

    def per_head(fn):
        return [[fn(j, 2 * m + e) for e in range(2)] for j, m in duos]

    gcol = per_head(lambda j, h: gcs[j][:, LANE_G + h:LANE_G + h + 1])
    bcol = per_head(lambda j, h: gates[j][:, LANE_BETA + h:LANE_BETA + h + 1])
    grow = [jnp.concatenate([gcts[j][LANE_G + 2 * m:LANE_G + 2 * m + 1, 0:c],
                             gcts[j][LANE_G + 2 * m + 1:LANE_G + 2 * m + 2, 0:c]], axis=1) for j, m in duos]
    decay = [jnp.exp(jnp.where(tri, jnp.where(lo, gcol[i][0], gcol[i][1]) - grow[i], NEG)) for i in nd]
    eg = [[jnp.exp(g) for g in gcol[i]] for i in nd]
    k = per_head(lambda j, h: qkvz_ref[1, h, rows(j), :])
    qd = per_head(lambda j, h: qkvz_ref[0, h, rows(j), :])
    kb = [[k[i][e] * bcol[i][e] for e in range(2)] for i in nd]
    kq = [_dot_nt(jnp.concatenate([kb[i][0], qd[i][0]], axis=0).astype(BF16),
                  jnp.concatenate([k[i][0].astype(BF16), zrows], axis=0))
          + _dot_nt(jnp.concatenate([kb[i][1], qd[i][1]], axis=0).astype(BF16),
                    jnp.concatenate([zrows, k[i][1].astype(BF16)], axis=0)) for i in nd]
    x = [jnp.where(strict, -(kq[i][:c] * decay[i]), 0.0) for i in nd]
    intra = [jnp.where(tri, kq[i][c:] * decay[i], 0.0).astype(BF16) for i in nd]
    p = [_dot(x[i].astype(BF16), block_diag(x[i])) for i in nd]
    t_off = x
    n_steps = c.bit_length() - 2
    for step in range(n_steps):
        bd = [block_diag(p[i]) for i in nd]
        if step < n_steps - 1:
            pt = [_dot(jnp.concatenate([p[i], t_off[i]], axis=0).astype(BF16), bd[i]) for i in nd]
            t_off = [t_off[i] + p[i] + pt[i][c:] for i in nd]
            p = [pti[:c] for pti in pt]
        else:
            t_off = [t_off[i] + p[i] + _dot(t_off[i].astype(BF16), bd[i]) for i in nd]
    rhs = [[jnp.concatenate([qkvz_ref[2, 2 * m + e, rows(j), :] * bcol[i][e], kb[i][e] * eg[i][e]], axis=1)
            for e in range(2)] for i, (j, m) in enumerate(duos)]
    rhs2 = [jnp.concatenate(rhs[i], axis=0).astype(BF16) for i in nd]
    sol = [[rhs[i][0] + _dot(jnp.where(lo, t_off[i], 0.0).astype(BF16), rhs2[i]),
            rhs[i][1] + _dot(jnp.where(lo, 0.0, t_off[i]).astype(BF16), rhs2[i])] for i in nd]
    wq = [[jnp.concatenate([sol[i][e][:, GDN_DV:], qd[i][e] * eg[i][e]], axis=0).astype(BF16) for e in range(2)]
          for i in nd]
    glast = [[g[c - 1:c, :] for g in gcol[i]] for i in nd]
    kdt = [jnp.concatenate([k[i][e] * jnp.exp(glast[i][e] - gcol[i][e]) for e in range(2)], axis=0).T.astype(BF16)
           for i in nd]
    ikd = [jnp.concatenate([intra[i], kdt[i]], axis=0) for i in nd]
    g_tot = [[jnp.exp(g) for g in glast[i]] for i in nd]

    for j in range(nc):
        idx = [j * (GDN_HEADS // 2) + m for m in range(GDN_HEADS // 2)]
        s = [s_ref[h] for h in range(GDN_HEADS)]
        ws = [_dot(wq[idx[h // 2]][h % 2], s[h].astype(BF16)) for h in range(GDN_HEADS)]
        v_new = [(sol[idx[h // 2]][h % 2][:, :GDN_DV] - ws[h][:c]).astype(BF16) for h in range(GDN_HEADS)]
        upd = [_dot(ikd[idx[h // 2]], jnp.concatenate([v_new[h], zrows] if h % 2 == 0 else [zrows, v_new[h]], axis=0))
               for h in range(GDN_HEADS)]
        for h in range(GDN_HEADS):
            s_ref[h] = s[h] * g_tot[idx[h // 2]][h % 2] + upd[h][c:]
            o = ws[h][c:] + upd[h][:c]
            zg = qkvz_ref[3, h, rows(j), :]
            y_ref[rows(j), h * GDN_DV:(h + 1) * GDN_DV] = _gated_norm(o, zg, nw).astype(y_ref.dtype)

    _gdn_step_part(n, sb, qs_ref, gs_ref, nw_ref, st_ref, sto_ref, yacc_ref)

    @pl.when(n == pl.num_programs(0) - 1)
    def _():
        sfin_ref[...] = s_ref[...]
        for h in range(GDN_HEADS):
            ys_ref[:, h * GDN_DV:(h + 1) * GDN_DV] = yacc_ref[h].astype(ys_ref.dtype)


def _gdn_step_part(n, sb, qs_ref, gs_ref, nw_ref, st_ref, sto_ref, yacc_ref):
    nw = nw_ref[...]
    lane = lax.broadcasted_iota(jnp.int32, (1, LANES), 1)
    piece_row = lax.broadcasted_iota(jnp.int32, (2 * SUBLANES, LANES), 0)
    ones = jnp.where(piece_row < 3, 1.0, 0.0).astype(BF16)

    def spread(row):
        p1 = row.astype(BF16).astype(F32)
        r1 = row - p1
        p2 = r1.astype(BF16).astype(F32)
        p3 = r1 - p2
        pieces = jnp.where(piece_row == 0, p1, jnp.where(piece_row == 1, p2, jnp.where(piece_row == 2, p3, 0.0)))
        return lax.dot_general(pieces.astype(BF16), ones, (((0,), (0,)), ((), ())), preferred_element_type=F32)

    def pick(row, l):
        return jnp.sum(jnp.where(lane == l, row, 0.0), axis=1, keepdims=True)

    pairs = [(b, h) for b in range(sb) for h in range(GDN_HEADS)]
    m = range(len(pairs))
    rows = [pl.ds(n * sb + b, 1) for b in range(sb)]
    grow = [gs_ref[rows[b], :] for b in range(sb)]
    beta = [pick(grow[b], LANE_BETA + h) for b, h in pairs]
    eg = [pick(grow[b], LANE_EG + h) for b, h in pairs]
    qb = [spread(qs_ref[0, h, rows[b], :]) for b, h in pairs]
    kb = [spread(qs_ref[1, h, rows[b], :]) for b, h in pairs]
    s = [st_ref[b, h] * eg[i] for i, (b, h) in enumerate(pairs)]
    kv = [jnp.sum(s[i] * kb[i], axis=0, keepdims=True) for i in m]
    delta = [(qs_ref[2, h, rows[b], :] - kv[i]) * beta[i] for i, (b, h) in enumerate(pairs)]
    s = [s[i] + kb[i] * delta[i] for i in m]
    o = [jnp.sum(s[i] * qb[i], axis=0, keepdims=True) for i in m]
    for i, (b, h) in enumerate(pairs):
        sto_ref[b, h] = s[i]
        yacc_ref[h, rows[b], :] = _gated_norm(o[i], qs_ref[3, h, rows[b], :], nw)


def _gdn(qkvz, gate, qkvz_s, gate_s, norm_w, state, *, nc=GDN_CHUNKS_PER_STEP):
    t, b = qkvz.shape[2], qkvz_s.shape[2]
    c = GDN_CHUNK
    steps = t // (nc * c)
    assert t % (nc * c) == 0 and c & (c - 1) == 0 and b % steps == 0
    sb = b // steps
    state_spec = pl.BlockSpec((sb, GDN_HEADS, GDN_DK, GDN_DV), lambda n: (n, 0, 0, 0))
    const2 = lambda n: (0, 0)
    return pl.pallas_call(
        functools.partial(_gdn_body, nc, sb),
        grid=(steps,),
        in_specs=[
            pl.BlockSpec((4, GDN_HEADS, nc * c, LANES), lambda n: (0, 0, n, 0)),
            pl.BlockSpec((nc * c, LANES), lambda n: (n, 0)),
            pl.BlockSpec((1, GDN_DV), const2),
            pl.BlockSpec((4, GDN_HEADS, b, LANES), lambda n: (0, 0, 0, 0)),
            pl.BlockSpec((b, LANES), const2),
            state_spec,
        ],
        out_specs=[
            pl.BlockSpec((nc * c, D_GDN_V), lambda n: (n, 0)),
            pl.BlockSpec((GDN_HEADS, GDN_DK, GDN_DV), lambda n: (0, 0, 0)),
            pl.BlockSpec((b, D_GDN_V), const2),
            state_spec,
        ],
        out_shape=[
            jax.ShapeDtypeStruct((t, D_GDN_V), BF16),
            jax.ShapeDtypeStruct((GDN_HEADS, GDN_DK, GDN_DV), F32),
            jax.ShapeDtypeStruct((b, D_GDN_V), BF16),
            jax.ShapeDtypeStruct(state.shape, F32),
        ],
        scratch_shapes=[pltpu.VMEM((GDN_HEADS, GDN_DK, GDN_DV), F32), pltpu.VMEM((GDN_HEADS, b, GDN_DV), F32)],
        compiler_params=_cparams("arbitrary"),
        name="gdn",
    )(qkvz, gate, norm_w.reshape(1, GDN_DV), qkvz_s, gate_s, state)


def _gate_param_row(p):
    return jnp.zeros((1, LANES), F32).at[0, LANE_G:LANE_G + GDN_HEADS].set(p.astype(F32))


def _ffn(h, hs, x, xs, state_t, w_up, w_conv, w_down, next_norm, *, final=False, name):
    (hid, keep_a, keep_b), (hid_s, up_a, up_b) = _ffn_up(h, hs, w_up, w_conv, state_t)
    out, out_s = _matmul([hid], [hid_s], w_down, tm=FFN_DOWN_ROW_TILE, residual=x, residual2=xs,
                         norm_w=next_norm, final=final, chunk=FFN_DOWN_STAGE_ROWS, name=name)
    return out, out_s, jnp.concatenate([keep_a, keep_b], axis=1), jnp.concatenate([up_a, up_b], axis=1)


def kernel(x_prompt, x_sample, state_l0_pool, cache_l0_k, cache_l0_v, state_l0_ffn_conv, state_l1_gdn_conv, state_l1_gdn_S, state_l1_sconv, state_l1_ffn_conv, l0_norm_mix, l0_w_in, l0_pool_w, l0_pool_scale, l0_sinks, l0_w_out, l0_norm_ffn, l0_ffn_w_up, l0_ffn_conv, l0_ffn_w_down, l1_norm_mix, l1_w_in, l1_gdn_conv, l1_gdn_A_log, l1_gdn_dt_bias, l1_gdn_norm, l1_sconv_w, l1_w_out, l1_norm_ffn, l1_ffn_w_up, l1_ffn_conv, l1_ffn_w_down, final_norm):
    bp, t, d = x_prompt.shape
    nb, ts = x_sample.shape[:2]
    wb = cache_l0_k.shape[1]
    assert bp == 1 and ts == 1 and d == D_MODEL and wb == WINDOW and t >= WINDOW
    w_in1_t = l1_w_in.T
    a_log_row = _gate_param_row(l1_gdn_A_log)
    dt_bias_row = _gate_param_row(l1_gdn_dt_bias)
    pool_t = state_l0_pool.transpose(1, 0, 2)
    kt, vt = cache_l0_k.transpose(0, 2, 3, 1), cache_l0_v.transpose(0, 2, 3, 1)
    ffn0_t = state_l0_ffn_conv.transpose(1, 0, 2)
    gconv_t = state_l1_gdn_conv.transpose(1, 0, 2)
    sconv_t = state_l1_sconv.transpose(1, 0, 2)
    ffn1_t = state_l1_ffn_conv.transpose(1, 0, 2)
    last = SUBLANES

    xp, xs = x_prompt[0], x_sample[:, 0]
    z0, z0s = _matmul([_rmsnorm(xp, l0_norm_mix, BF16)], [_rmsnorm(xs, l0_norm_mix, BF16)], l0_w_in,
                      name="in_proj0")
    mix = _mixer0_seq(z0, l0_pool_w, l0_pool_scale, l0_sinks)
    y_pool = _pool_step(z0s, pool_t, l0_pool_w, l0_pool_scale)
    att, kt_new, vt_new = _attn_step(z0s, kt, vt, l0_sinks)
    mix_s = jnp.concatenate([y_pool, att.astype(BF16)], axis=1)
    (x1, h1), (x1s, h1s) = _matmul([mix], [mix_s], l0_w_out, residual=xp, residual2=xs,
                                   norm_w=l0_norm_ffn, name="out_proj0")
    (x2, h2), (x2s, h2s), ffn0_rows, ffn0_new = _ffn(
        h1, h1s, x1, x1s, ffn0_t, l0_ffn_w_up, l0_ffn_conv, l0_ffn_w_down, l1_norm_mix, name="ffn_down0")
    (qkvz, raw), (qkvz_s, raw_s) = _qkvz(h2, h2s, w_in1_t, l1_gdn_conv, gconv_t)
    (ysc, gate, mrows), (ysc_s, gate_s, m_s) = _tail(
        h2, h2s, w_in1_t, l1_sconv_w, a_log_row, dt_bias_row, sconv_t)
    ygdn, s_fin, ygdn_s, s_new = _gdn(qkvz, gate, qkvz_s, gate_s, l1_gdn_norm, state_l1_gdn_S)
    (x3, h3), (x3s, h3s) = _matmul([ygdn, ysc], [ygdn_s, ysc_s], l1_w_out, residual=x2,
                                   residual2=x2s, norm_w=l1_norm_ffn, name="out_proj1")
    y_prompt, y_sample, ffn1_rows, ffn1_new = _ffn(
        h3, h3s, x3, x3s, ffn1_t, l1_ffn_w_up, l1_ffn_conv, l1_ffn_w_down, final_norm, final=True,
        name="ffn_down1")

    p_pool = z0[t - POOL_BUF:, :D_POOL][None]
    p_k = z0[t - wb:, O_K:O_V].reshape(1, wb, N_KV_HEADS, HEAD_DIM)
    p_v = z0[t - wb:, O_V:].reshape(1, wb, N_KV_HEADS, HEAD_DIM)
    p_ffn0 = ffn0_rows[last - (FFN_CONV - 1):][None]
    p_gconv = raw[last - (GDN_CONV - 1):, :D_GDN_CONV][None]
    p_sconv = mrows[last - (SCONV_W - 1):][None]
    p_ffn1 = ffn1_rows[last - (FFN_CONV - 1):][None]

    def push(state_t, new_row):
        return jnp.concatenate([state_t[1:], new_row[None]], axis=0).transpose(1, 0, 2)

    s_pool = push(pool_t, z0s[:, :D_POOL])
    s_k, s_v = kt_new.transpose(0, 3, 1, 2), vt_new.transpose(0, 3, 1, 2)
    s_ffn0 = push(ffn0_t, ffn0_new)
    s_gconv = push(gconv_t, raw_s[:, :D_GDN_CONV])
    s_sconv = push(sconv_t, m_s)
    s_ffn1 = push(ffn1_t, ffn1_new)
    return (y_prompt[None], y_sample[:, None], p_pool, s_pool, p_k, s_k, p_v, s_v, p_ffn0, s_ffn0,
            p_gconv, s_gconv, s_fin[None], s_new, p_sconv, s_sconv, p_ffn1, s_ffn1)
```
